```python
import jax, jax.numpy as jnp
from jax import lax
import numpy as np

D_MODEL = 1024
BATCH = 4
SEQ = 8192
DEPTH = 1
DEC_BATCH = 32
DEC_SEQ = 64
PAST_LEN = 1024

CHUNK = 64
QBLOCK = 128
EPS = 1e-6
SB_HEADS = 8
SB_DIM = 64
SB_WIDTH = SB_HEADS * SB_DIM
DSA_HEADS = 8
DSA_DIM = 64
DSA_WIDTH = DSA_HEADS * DSA_DIM
IDX_HEADS = 8
IDX_DIM = 64
TOPK_MAX = 256
IN_SIZES = (SB_WIDTH, SB_WIDTH, SB_WIDTH, DSA_WIDTH, DSA_WIDTH, DSA_WIDTH,
            IDX_HEADS * IDX_DIM, IDX_DIM, IDX_HEADS)
N_IN = sum(IN_SIZES)
PEER_HEADS = 8
PEER_NKEYS = 128
PEER_EXPERTS = PEER_NKEYS * PEER_NKEYS
PEER_DKEY = 256
PEER_DHALF = PEER_DKEY // 2
PEER_TOPK = 16
PEER_ROWS = 128

kernel_name = 'sb_dsa_peer_streaming_encoder_step'


def _rmsnorm(x, gain):
    x32 = x.astype(jnp.float32)
    y = x32 * lax.rsqrt(jnp.mean(x32 * x32, axis=-1, keepdims=True) + EPS)
    return (y * gain.astype(jnp.float32)).astype(x.dtype)


def _alibi_slopes(n_heads):
    return jnp.asarray([2.0 ** (-8.0 * (h + 1) / n_heads) for h in range(n_heads)], dtype=jnp.float32)


def _split_in(z):
    parts, off = [], 0
    for s in IN_SIZES:
        parts.append(z[..., off:off + s])
        off += s
    return parts


def _to_blocks(a, nblk, blk):
    a = a.reshape((a.shape[0], nblk, blk) + a.shape[2:])
    return jnp.moveaxis(a, 1, 0)


def _from_blocks(o):
    o = jnp.moveaxis(o, 0, 1)
    return o.reshape((o.shape[0], o.shape[1] * o.shape[2]) + o.shape[3:])


def _stick_breaking(q, k, v, q_pos, k_pos):
    Bn, T, H, d = q.shape
    blk = QBLOCK if T % QBLOCK == 0 else T
    nblk = T // blk
    scale = d ** -0.5

    def body(args):
        qb, pb = args
        z = jnp.einsum('bqhd,bkhd->bhqk', qb, k).astype(jnp.float32) * scale
        causal = (k_pos[None, :] < pb[:, None])[None, None]
        log_keep = jnp.where(causal, jax.nn.log_sigmoid(-z), 0.0)
        after = lax.cumsum(log_keep, axis=3, reverse=True) - log_keep
        wts = jnp.where(causal, jnp.exp(jax.nn.log_sigmoid(z) + after), 0.0)
        return jnp.einsum('bhqk,bkhd->bqhd', wts.astype(v.dtype), v)

    out = lax.map(body, (_to_blocks(q, nblk, blk), q_pos.reshape(nblk, blk)))
    return _from_blocks(out)


def _dsa_attention(q, k, v, q_idx, k_idx, w_idx, q_pos, k_pos):
    Bn, T, H, d = q.shape
    S = k.shape[1]
    n_sel = min(TOPK_MAX, S // 4)
    slopes = _alibi_slopes(H)
    blk = QBLOCK if T % QBLOCK == 0 else T
    nblk = T // blk

    def body(args):
        qb, qib, wib, pb = args
        q_chunk = pb // CHUNK
        adm = (k_pos // CHUNK)[None, :] <= q_chunk[:, None]
        logits = jnp.einsum('bqhe,bke->bqhk', qib, k_idx).astype(jnp.float32) * (IDX_DIM ** -0.5)
        score = jnp.einsum('bqh,bqhk->bqk', wib.astype(jnp.float32) * (IDX_HEADS ** -0.5), jax.nn.relu(logits))
        score = jnp.where(adm[None], score, -jnp.inf)
        _, sel = lax.top_k(score, n_sel)
        sel_pos = k_pos[sel]
        valid = (sel_pos // CHUNK) <= q_chunk[None, :, None]
        kg = jax.vmap(lambda kb, ib: kb[ib])(k, sel)
        vg = jax.vmap(lambda vb, ib: vb[ib])(v, sel)
        s = jnp.einsum('bqhd,bqkhd->bhqk', qb, kg).astype(jnp.float32) * (d ** -0.5)
        dist = jnp.abs(pb[None, :, None] - sel_pos).astype(jnp.float32)
        s = s - slopes[None, :, None, None] * dist[:, None]
        s = jnp.where(valid[:, None], s, -jnp.inf)
        p = jax.nn.softmax(s, axis=-1)
        return jnp.einsum('bhqk,bqkhd->bqhd', p.astype(v.dtype), vg)

    out = lax.map(body, (_to_blocks(q, nblk, blk), _to_blocks(q_idx, nblk, blk),
                         _to_blocks(w_idx, nblk, blk), q_pos.reshape(nblk, blk)))
    return _from_blocks(out)


def _peer(h, w_peer_q, peer_subkeys, peer_u, peer_v):
    n = h.shape[0]
    nb = -(-n // PEER_ROWS)
    hp = jnp.pad(h, ((0, nb * PEER_ROWS - n), (0, 0))).reshape(nb, PEER_ROWS, h.shape[1])

    def body(hb):
        q = (hb @ w_peer_q).reshape(PEER_ROWS, PEER_HEADS, 2, PEER_DHALF)
        s = jnp.einsum('rhpc,hpnc->rhpn', q, peer_subkeys).astype(jnp.float32)
        v1, i1 = lax.top_k(s[:, :, 0], PEER_TOPK)
        v2, i2 = lax.top_k(s[:, :, 1], PEER_TOPK)
        cand = (v1[..., :, None] + v2[..., None, :]).reshape(PEER_ROWS, PEER_HEADS, PEER_TOPK * PEER_TOPK)
        cidx = (i1[..., :, None] * PEER_NKEYS + i2[..., None, :]).reshape(PEER_ROWS, PEER_HEADS, PEER_TOPK * PEER_TOPK)
        top, pos = lax.top_k(cand, PEER_TOPK)
        eidx = jnp.take_along_axis(cidx, pos, axis=-1)
        g = jax.nn.softmax(top, axis=-1)
        u = peer_u[eidx]
        act = jax.nn.gelu(jnp.einsum('rhkd,rd->rhk', u, hb).astype(jnp.float32), approximate=False)
        return jnp.einsum('rhk,rhkd->rd', (g * act).astype(hb.dtype), peer_v[eidx])

    out = lax.map(body, hp)
    return out.reshape(nb * PEER_ROWS, -1)[:n]


def _layer(x, c, past, q_offset, params):
    (w_ada, b_ada, g_pre_mix, g_post_mix, g_pre_ffn, g_post_ffn, w_in, w_gate,
     w_branch_a, w_branch_b, w_out, w_peer_q, peer_subkeys, peer_u, peer_v) = params
    Bn, T, _ = x.shape
    mod = (jax.nn.silu(c) @ w_ada + b_ada)[:, None, :]
    sh_m, sc_m, gt_m, sh_f, sc_f, gt_f = jnp.split(mod, 6, axis=-1)

    h = _rmsnorm(x, g_pre_mix) * (1.0 + sc_m) + sh_m
    a_q, a_k, a_v, b_q, b_k, b_v, i_q, i_k, i_w = _split_in(h @ w_in)
    a_q = a_q.reshape(Bn, T, SB_HEADS, SB_DIM)
    a_k = a_k.reshape(Bn, T, SB_HEADS, SB_DIM)
    a_v = a_v.reshape(Bn, T, SB_HEADS, SB_DIM)
    b_q = b_q.reshape(Bn, T, DSA_HEADS, DSA_DIM)
    b_k = b_k.reshape(Bn, T, DSA_HEADS, DSA_DIM)
    b_v = b_v.reshape(Bn, T, DSA_HEADS, DSA_DIM)
    i_q = i_q.reshape(Bn, T, IDX_HEADS, IDX_DIM)

    if past is None:
        ka, va, kb, vb, ki = a_k, a_v, b_k, b_v, i_k
    else:
        p_sbk, p_sbv, p_dk, p_dv, p_ki = past
        ka = jnp.concatenate([p_sbk, a_k], axis=1)
        va = jnp.concatenate([p_sbv, a_v], axis=1)
        kb = jnp.concatenate([p_dk, b_k], axis=1)
        vb = jnp.concatenate([p_dv, b_v], axis=1)
        ki = jnp.concatenate([p_ki, i_k], axis=1)
    S = ka.shape[1]
    k_pos = jnp.arange(S, dtype=jnp.int32)
    q_pos = jnp.arange(T, dtype=jnp.int32) + q_offset

    o_a = _stick_breaking(a_q, ka, va, q_pos, k_pos).reshape(Bn, T, SB_WIDTH) @ w_branch_a
    o_b = _dsa_attention(b_q, kb, vb, i_q, ki, i_w, q_pos, k_pos).reshape(Bn, T, DSA_WIDTH) @ w_branch_b
    g_a, g_b = jnp.split(jax.nn.sigmoid(h @ w_gate), 2, axis=-1)
    mixed = (g_a * o_a + g_b * o_b) @ w_out
    x = x + gt_m * _rmsnorm(mixed, g_post_mix)

    h2 = _rmsnorm(x, g_pre_ffn) * (1.0 + sc_f) + sh_f
    f = _peer(h2.reshape(Bn * T, D_MODEL), w_peer_q, peer_subkeys, peer_u, peer_v).reshape(Bn, T, D_MODEL)
    x = x + gt_f * _rmsnorm(f, g_post_ffn)
    return x, (a_k, a_v, b_k, b_v, i_k)


def setup_inputs(seed: int = 0) -> dict:
    key = jax.random.key(seed)
    ks = jax.random.split(key, 26)
    f32 = jnp.float32
    D = D_MODEL

    def nrm(k, shape, scale):
        return jax.random.normal(k, shape, f32) * scale

    return {
        'x_prompt': nrm(ks[0], (BATCH, SEQ, D), 1.0),
        'x_sample': nrm(ks[1], (DEC_BATCH, DEC_SEQ, D), 1.0),
        'c_prompt': nrm(ks[2], (BATCH, D), 1.0),
        'c_sample': nrm(ks[3], (DEC_BATCH, D), 1.0),
        'cache_sb_k': nrm(ks[4], (DEPTH, DEC_BATCH, PAST_LEN, SB_HEADS, SB_DIM), 1.0),
        'cache_sb_v': nrm(ks[5], (DEPTH, DEC_BATCH, PAST_LEN, SB_HEADS, SB_DIM), 1.0),
        'cache_dsa_k': nrm(ks[6], (DEPTH, DEC_BATCH, PAST_LEN, DSA_HEADS, DSA_DIM), 1.0),
        'cache_dsa_v': nrm(ks[7], (DEPTH, DEC_BATCH, PAST_LEN, DSA_HEADS, DSA_DIM), 1.0),
        'cache_dsa_kidx': nrm(ks[8], (DEPTH, DEC_BATCH, PAST_LEN, IDX_DIM), 1.0),
        'w_ada': nrm(ks[9], (DEPTH, D, 6 * D), 0.1 * D ** -0.5),
        'b_ada': nrm(ks[10], (DEPTH, 6 * D), 0.01),
        'g_pre_mix': 1.0 + nrm(ks[11], (DEPTH, D), 0.05),
        'g_post_mix': 1.0 + nrm(ks[12], (DEPTH, D), 0.05),
        'g_pre_ffn': 1.0 + nrm(ks[13], (DEPTH, D), 0.05),
        'g_post_ffn': 1.0 + nrm(ks[14], (DEPTH, D), 0.05),
        'w_in': nrm(ks[15], (DEPTH, D, N_IN), D ** -0.5),
        'w_gate': nrm(ks[16], (DEPTH, D, 2 * D), D ** -0.5),
        'w_branch_a': nrm(ks[17], (DEPTH, SB_WIDTH, D), SB_WIDTH ** -0.5),
        'w_branch_b': nrm(ks[18], (DEPTH, DSA_WIDTH, D), DSA_WIDTH ** -0.5),
        'w_out': nrm(ks[19], (DEPTH, D, D), D ** -0.5),
        'w_peer_q': nrm(ks[20], (DEPTH, D, PEER_HEADS * PEER_DKEY), D ** -0.5),
        'peer_subkeys': nrm(ks[21], (DEPTH, PEER_HEADS, 2, PEER_NKEYS, PEER_DHALF), PEER_DHALF ** -0.5),
        'peer_u': nrm(ks[22], (DEPTH, PEER_EXPERTS, D), D ** -0.5),
        'peer_v': nrm(ks[23], (DEPTH, PEER_EXPERTS, D), D ** -0.5),
    }


def reference(x_prompt, x_sample, c_prompt, c_sample, cache_sb_k, cache_sb_v, cache_dsa_k,
              cache_dsa_v, cache_dsa_kidx, w_ada, b_ada, g_pre_mix, g_post_mix, g_pre_ffn,
              g_post_ffn, w_in, w_gate, w_branch_a, w_branch_b, w_out, w_peer_q, peer_subkeys,
              peer_u, peer_v):
    xp, xs = x_prompt, x_sample
    rows_p = [[], [], [], [], []]
    rows_s = [[], [], [], [], []]
    past_len = cache_sb_k.shape[2]
    for l in range(DEPTH):
        params = (w_ada[l], b_ada[l], g_pre_mix[l], g_post_mix[l], g_pre_ffn[l], g_post_ffn[l],
                  w_in[l], w_gate[l], w_branch_a[l], w_branch_b[l], w_out[l], w_peer_q[l],
                  peer_subkeys[l], peer_u[l], peer_v[l])
        xp, new_p = _layer(xp, c_prompt, None, 0, params)
        past = (cache_sb_k[l], cache_sb_v[l], cache_dsa_k[l], cache_dsa_v[l], cache_dsa_kidx[l])
        xs, new_s = _layer(xs, c_sample, past, past_len, params)
        for i in range(5):
            rows_p[i].append(new_p[i])
            rows_s[i].append(new_s[i])
    sbk_p, sbv_p, dk_p, dv_p, ki_p = [jnp.stack(r, axis=0) for r in rows_p]
    sbk_s, sbv_s, dk_s, dv_s, ki_s = [jnp.stack(r, axis=0) for r in rows_s]
    return (xp, xs, sbk_p, sbv_p, dk_p, dv_p, ki_p, sbk_s, sbv_s, dk_s, dv_s, ki_s)
```

```python
import functools
import math

import jax
import jax.numpy as jnp
import numpy as np
from jax import lax
from jax.experimental import pallas as pl
from jax.experimental.pallas import tpu as pltpu

F32 = jnp.float32
BF16 = jnp.bfloat16
I32 = jnp.int32

D_MODEL = 1024
CHUNK = 64
EPS = 1e-6
N_HEADS = 8
HEAD_DIM = 64
WIDTH = N_HEADS * HEAD_DIM
N_PAIRS = N_HEADS // 2
LANES = 128
TOPK_MAX = 256
PEER_HEADS = 8
PEER_NKEYS = 128
PEER_DHALF = 128
PEER_TOPK = 16
PEER_SEL = PEER_HEADS * PEER_TOPK
W_IN_COLS = 7 * WIDTH + HEAD_DIM + N_HEADS
W_IN_PAD = 7 * WIDTH + LANES
INT_MIN = -(2 ** 31)
NEG_BIG = -1e30
VMEM_LIMIT = 56 * 1024 * 1024


def _params(sem):
    return pltpu.CompilerParams(dimension_semantics=sem, vmem_limit_bytes=VMEM_LIMIT)


def _dot(a, b):
    return jnp.dot(a, b, preferred_element_type=F32)


def _dot_nt(a, b):
    return lax.dot_general(a, b, (((1,), (1,)), ((), ())), preferred_element_type=F32)


def _rms(x, gain):
    return x * lax.rsqrt(jnp.mean(x * x, axis=-1, keepdims=True) + EPS) * gain


def _mod_kernel(c_ref, w_ref, b_ref, o_ref):
    c = c_ref[...]
    s = c * (1.0 / (1.0 + jnp.exp(-c)))
    o_ref[...] = jnp.dot(s, w_ref[...], preferred_element_type=F32,
                         precision=lax.Precision.HIGHEST) + b_ref[...]


def _modulation(c, w_ada, b_ada):
    n = c.shape[0]
    cols = w_ada.shape[1]
    tn = 1024
    return pl.pallas_call(
        _mod_kernel,
        out_shape=jax.ShapeDtypeStruct((n, cols), F32),
        grid=(cols // tn,),
        in_specs=[pl.BlockSpec((n, D_MODEL), lambda j: (0, 0)),
                  pl.BlockSpec((D_MODEL, tn), lambda j: (0, j)),
                  pl.BlockSpec((1, tn), lambda j: (0, j))],
        out_specs=pl.BlockSpec((n, tn), lambda j: (0, j)),
        compiler_params=_params(("arbitrary",)),
        name="mod",
    )(c, w_ada, b_ada.reshape(1, cols))


def _inproj_kernel(x_ref, mod_ref, g_ref, win_ref, wg_ref,
                   aq_ref, ak_ref, av_ref, bq_ref, bk_ref, bv_ref, iq_ref, ki2_ref,
                   akf_ref, avf_ref, bkf_ref, bvf_ref, ikf_ref, iw_ref, ga_ref, gb_ref):
    x = x_ref[...]
    mod = mod_ref[0]
    h = _rms(x, g_ref[...]) * (1.0 + mod[1:2, :]) + mod[0:1, :]
    hb = h.astype(BF16)
    qscale = HEAD_DIM ** -0.5

    def seg(i):
        return _dot(hb, win_ref[:, i * WIDTH:(i + 1) * WIDTH])

    aq_ref[...] = (seg(0) * qscale).astype(BF16)
    z = seg(1)
    akf_ref[...] = z
    ak_ref[...] = z.astype(BF16)
    z = seg(2)
    avf_ref[...] = z
    av_ref[...] = z.astype(BF16)
    bq_ref[...] = (seg(3) * qscale).astype(BF16)
    z = seg(4)
    bkf_ref[...] = z
    bk_ref[...] = z.astype(BF16)
    z = seg(5)
    bvf_ref[...] = z
    bv_ref[...] = z.astype(BF16)
    iq_ref[...] = (seg(6) * qscale).astype(BF16)
    tail = _dot(hb, win_ref[:, 7 * WIDTH:7 * WIDTH + LANES])
    ik = tail[:, :HEAD_DIM]
    ikf_ref[...] = ik
    ki2_ref[...] = jnp.concatenate([ik, ik], axis=-1).astype(BF16)
    iw_ref[...] = tail[:, HEAD_DIM:HEAD_DIM + N_HEADS] * (N_HEADS ** -0.5)
    zg = _dot(hb, wg_ref[...])
    gate = 1.0 / (1.0 + jnp.exp(-zg))
    ga_ref[...] = gate[:, :D_MODEL]
    gb_ref[...] = gate[:, D_MODEL:]


def _inproj(x2, mod3, g_pre, w_in_b, w_gate_b, tm, tiles_per_seq):
    n = x2.shape[0]
    row = lambda i: (i, 0)
    const = lambda i: (0, 0)
    bspec = lambda w: pl.BlockSpec((tm, w), row)
    outs = ([jax.ShapeDtypeStruct((n, WIDTH), BF16)] * 7
            + [jax.ShapeDtypeStruct((n, LANES), BF16)]
            + [jax.ShapeDtypeStruct((n, WIDTH), F32)] * 4
            + [jax.ShapeDtypeStruct((n, HEAD_DIM), F32),
               jax.ShapeDtypeStruct((n, N_HEADS), F32),
               jax.ShapeDtypeStruct((n, D_MODEL), F32),
               jax.ShapeDtypeStruct((n, D_MODEL), F32)])
    out_specs = ([bspec(WIDTH)] * 7 + [bspec(LANES)] + [bspec(WIDTH)] * 4
                 + [bspec(HEAD_DIM), bspec(N_HEADS), bspec(D_MODEL), bspec(D_MODEL)])
    return pl.pallas_call(
        _inproj_kernel,
        out_shape=outs,
        grid=(n // tm,),
        in_specs=[pl.BlockSpec((tm, D_MODEL), row),
                  pl.BlockSpec((1, 6, D_MODEL), lambda i: (i // tiles_per_seq, 0, 0)),
                  pl.BlockSpec((1, D_MODEL), const),
                  pl.BlockSpec((D_MODEL, W_IN_PAD), const, pipeline_mode=pl.Buffered(1)),
                  pl.BlockSpec((D_MODEL, 2 * D_MODEL), const, pipeline_mode=pl.Buffered(1))],
        out_specs=out_specs,
        compiler_params=_params(("parallel",)),
        name="inproj",
    )(x2, mod3, g_pre, w_in_b, w_gate_b)


def _sb_kernel(tab_ref, q_ref, k_ref, v_ref, tri_ref, o_ref, acc_ref, run_ref, *, tq, tk, q_off):
    s = pl.program_id(2)
    qi = tab_ref[0, s]
    kj = tab_ref[1, s]
    first = tab_ref[2, s]
    last = tab_ref[3, s]
    masked = tab_ref[4, s]

    @pl.when(first == 1)
    def _():
        acc_ref[...] = jnp.zeros_like(acc_ref)
        run_ref[...] = jnp.zeros_like(run_ref)

    lane = lax.broadcasted_iota(I32, (tq, LANES), 1)

    def step(use_mask):
        q = q_ref[0]
        k = k_ref[0]
        v = v_ref[0]
        tri = tri_ref[...]
        if use_mask:
            qpos = q_off + qi * tq + lax.broadcasted_iota(I32, (tq, tk), 0)
            kpos = kj * tk + lax.broadcasted_iota(I32, (tq, tk), 1)
            causal = kpos < qpos
        for e in range(2):
            in_half = (lane >= HEAD_DIM) if e else (lane < HEAD_DIM)
            qe = jnp.where(in_half, q, jnp.zeros_like(q))
            z = _dot_nt(qe, k)
            lk = -(jnp.maximum(z, 0.0) + jnp.log(1.0 + jnp.exp(-jnp.abs(z))))
            if use_mask:
                lk = jnp.where(causal, lk, 0.0)
            hi = lk.astype(BF16)
            r1 = lk - hi.astype(F32)
            mid = r1.astype(BF16)
            lo = (r1 - mid.astype(F32)).astype(BF16)
            run = run_ref[e]
            suf = _dot(hi, tri) + _dot(mid, tri) + _dot(lo, tri) + run[:, 0:1]
            w = jnp.exp(z + suf)
            if use_mask:
                w = jnp.where(causal, w, 0.0)
            acc_ref[e] += _dot(w.astype(BF16), v)
            run_ref[e] = jnp.broadcast_to(suf[:, 0:1], (tq, LANES))

    @pl.when(masked == 1)
    def _():
        step(True)

    @pl.when(masked == 0)
    def _():
        step(False)

    @pl.when(last == 1)
    def _():
        o_ref[0] = jnp.where(lane < HEAD_DIM, acc_ref[0], acc_ref[1]).astype(o_ref.dtype)


def _sb_table(nq, nk, tq, tk, q_off):
    rows = []
    for qi in range(nq):
        q_lo = q_off + qi * tq
        q_hi = q_lo + tq - 1
        kjs = [kj for kj in range(nk) if kj * tk < q_hi]
        if not kjs:
            kjs = [0]
        kjs = kjs[::-1]
        for n, kj in enumerate(kjs):
            masked = int(kj * tk + tk - 1 >= q_lo)
            rows.append((qi, kj, int(n == 0), int(n == len(kjs) - 1), masked))
    return np.asarray(rows, dtype=np.int32).T.copy()


def _sb_attention(q, k, v, tq, tk, q_off):
    b, t, _ = q.shape
    s_len = k.shape[1]
    tab = _sb_table(t // tq, s_len // tk, tq, tk, q_off)
    tri = jnp.asarray(np.tril(np.ones((tk, tk), np.float32)), dtype=BF16)
    grid_spec = pltpu.PrefetchScalarGridSpec(
        num_scalar_prefetch=1,
        grid=(b, N_PAIRS, tab.shape[1]),
        in_specs=[pl.BlockSpec((1, tq, LANES), lambda bi, hp, s, tab: (bi, tab[0, s], hp)),
                  pl.BlockSpec((1, tk, LANES), lambda bi, hp, s, tab: (bi, tab[1, s], hp)),
                  pl.BlockSpec((1, tk, LANES), lambda bi, hp, s, tab: (bi, tab[1, s], hp)),
                  pl.BlockSpec((tk, tk), lambda bi, hp, s, tab: (0, 0))],
        out_specs=pl.BlockSpec((1, tq, LANES), lambda bi, hp, s, tab: (bi, tab[0, s], hp)),
        scratch_shapes=[pltpu.VMEM((2, tq, LANES), F32), pltpu.VMEM((2, tq, LANES), F32)],
    )
    return pl.pallas_call(
        functools.partial(_sb_kernel, tq=tq, tk=tk, q_off=q_off),
        out_shape=jax.ShapeDtypeStruct((b, t, WIDTH), BF16),
        grid_spec=grid_spec,
        compiler_params=_params(("parallel", "parallel", "arbitrary")),
        name="sb_attn",
    )(jnp.asarray(tab), q, k, v, tri)


def _sortable(x):
    bits = pltpu.bitcast(x + 0.0, I32)
    return jnp.where(bits < 0, bits ^ 0x7FFFFFFF, bits)


def _dsa_kernel(q_ref, iq_ref, iw_ref, k_ref, v_ref, ki_ref, o_ref,
                keys_ref, bias_ref, m_ref, l_ref, acc_ref,
                *, tq, tk, q_off, n_kb_max, n_sel, idx_bits):
    qi = pl.program_id(1)
    q_lo = q_off + qi * tq
    adm_end = ((q_lo + tq - 1) // CHUNK + 1) * CHUNK
    n_kb = jnp.minimum((adm_end + tk - 1) // tk, n_kb_max)

    lane = lax.broadcasted_iota(I32, (tq, LANES), 1)
    lo_half = lane < HEAD_DIM
    row_pos = q_lo + lax.broadcasted_iota(I32, (tq, tk), 0)
    col_iota = lax.broadcasted_iota(I32, (tq, tk), 1)

    def split_heads(x_pair):
        zero = jnp.zeros_like(x_pair)
        return jnp.where(lo_half, x_pair, zero), jnp.where(lo_half, zero, x_pair)

    iw = iw_ref[0]

    def score_block(j, carry):
        ki = ki_ref[0, pl.ds(pl.multiple_of(j * tk, tk), tk), :]
        score = jnp.zeros((tq, tk), F32)
        for hp in range(N_PAIRS):
            pair = iq_ref[0, :, hp * LANES:(hp + 1) * LANES]
            for e, qe in enumerate(split_heads(pair)):
                h = 2 * hp + e
                score = score + iw[:, h:h + 1] * jnp.maximum(_dot_nt(qe, ki), 0.0)
        kpos = j * tk + col_iota
        adm = (kpos // CHUNK) <= (row_pos // CHUNK)
        keys_ref[j] = jnp.where(adm, _sortable(score), INT_MIN)
        return carry

    lax.fori_loop(0, n_kb, score_block, 0)

    def count_rows(pred_fn):
        def body(j, acc):
            hit = pred_fn(keys_ref[j], j).astype(I32)
            part = hit[:, 0:LANES]
            for c in range(1, tk // LANES):
                part = part + hit[:, c * LANES:(c + 1) * LANES]
            return acc + part
        acc = lax.fori_loop(0, n_kb, body, jnp.zeros((tq, LANES), I32))
        return jnp.sum(acc, axis=1, keepdims=True)

    def bit_step(i, thr):
        cand = thr + jnp.left_shift(jnp.int32(1), 31 - i)
        cnt = count_rows(lambda kb, j: kb >= cand)
        return jnp.where(cnt >= n_sel, cand, thr)

    thr = lax.fori_loop(0, 32, bit_step, jnp.full((tq, 1), INT_MIN, I32))
    n_gt = count_rows(lambda kb, j: kb > thr)
    n_ge = count_rows(lambda kb, j: kb >= thr)
    need = n_sel - n_gt
    real = thr > INT_MIN
    tie = jnp.max(jnp.where(real & (n_ge > n_sel), 1, 0)) > 0

    def tie_cut():
        def idx_step(i, cut):
            cand = cut + jnp.left_shift(jnp.int32(1), idx_bits - 1 - i)
            cnt = count_rows(lambda kb, j: (kb == thr) & ((j * tk + col_iota) < cand))
            return jnp.where(cnt < need, cand, cut)
        return lax.fori_loop(0, idx_bits, idx_step, jnp.zeros((tq, 1), I32))

    cut = lax.cond(tie, tie_cut, lambda: jnp.full((tq, 1), n_kb_max * tk, I32))

    def bias_block(j, carry):
        kb = keys_ref[j]
        kpos = j * tk + col_iota
        sel = (kb > thr) | ((kb == thr) & real & (kpos <= cut))
        bias_ref[j] = jnp.where(sel, 0.0, NEG_BIG)
        return carry

    lax.fori_loop(0, n_kb, bias_block, 0)

    m_ref[...] = jnp.full_like(m_ref, NEG_BIG)
    l_ref[...] = jnp.zeros_like(l_ref)
    acc_ref[...] = jnp.zeros_like(acc_ref)

    def attn_block(j, carry):
        start = pl.multiple_of(j * tk, tk)
        bias = bias_ref[j]
        dist = jnp.abs(row_pos - (j * tk + col_iota)).astype(F32)
        for hp in range(N_PAIRS):
            cols = slice(hp * LANES, (hp + 1) * LANES)
            kp = k_ref[0, pl.ds(start, tk), cols]
            vp = v_ref[0, pl.ds(start, tk), cols]
            for e, qe in enumerate(split_heads(q_ref[0, :, cols])):
                h = 2 * hp + e
                slope = 2.0 ** (-8.0 * (h + 1) / N_HEADS)
                s = _dot_nt(qe, kp) - slope * dist + bias
                m_old = m_ref[h]
                m_new = jnp.maximum(m_old, jnp.max(s, axis=1, keepdims=True))
                alpha = jnp.exp(m_old - m_new)
                p = jnp.exp(s - m_new[:, 0:1])
                l_ref[h] = alpha * l_ref[h] + jnp.sum(p, axis=1, keepdims=True)
                acc_ref[h] = alpha * acc_ref[h] + _dot(p.astype(BF16), vp)
                m_ref[h] = m_new
        return carry

    lax.fori_loop(0, n_kb, attn_block, 0)

    for hp in range(N_PAIRS):
        o_lo = acc_ref[2 * hp] / l_ref[2 * hp]
        o_hi = acc_ref[2 * hp + 1] / l_ref[2 * hp + 1]
        o_ref[0, :, hp * LANES:(hp + 1) * LANES] = jnp.where(lo_half, o_lo, o_hi).astype(o_ref.dtype)


def _dsa_attention(q, iq, iw, k, v, ki2, tq, tk, q_off, n_sel):
    b, t, _ = q.shape
    s_len = k.shape[1]
    n_kb_max = s_len // tk
    qspec = lambda w: pl.BlockSpec((1, tq, w), lambda bi, i: (bi, i, 0))
    kspec = lambda w: pl.BlockSpec((1, s_len, w), lambda bi, i: (bi, 0, 0), pipeline_mode=pl.Buffered(1))
    return pl.pallas_call(
        functools.partial(_dsa_kernel, tq=tq, tk=tk, q_off=q_off, n_kb_max=n_kb_max, n_sel=n_sel,
                          idx_bits=max(1, int(math.ceil(math.log2(s_len + 1))))),
        out_shape=jax.ShapeDtypeStruct((b, t, WIDTH), BF16),
        grid=(b, t // tq),
        in_specs=[qspec(WIDTH), qspec(WIDTH), qspec(N_HEADS), kspec(WIDTH), kspec(WIDTH), kspec(LANES)],
        out_specs=qspec(WIDTH),
        scratch_shapes=[pltpu.VMEM((n_kb_max, tq, tk), I32),
                        pltpu.VMEM((n_kb_max, tq, tk), F32),
                        pltpu.VMEM((N_HEADS, tq, LANES), F32),
                        pltpu.VMEM((N_HEADS, tq, LANES), F32),
                        pltpu.VMEM((N_HEADS, tq, LANES), F32)],
        compiler_params=_params(("parallel", "arbitrary")),
        name="dsa_attn",
    )(q, iq, iw, k, v, ki2)


def _mix_kernel(oa_ref, ob_ref, ga_ref, gb_ref, x_ref, mod_ref, gpost_ref, gpre_ref,
                wa_ref, wb_ref, wo_ref, wq_ref, x1_ref, h2_ref, pq_ref):
    mod = mod_ref[0]
    mixed = ga_ref[...] * _dot(oa_ref[...], wa_ref[...]) + gb_ref[...] * _dot(ob_ref[...], wb_ref[...])
    y = _dot(mixed.astype(BF16), wo_ref[...])
    x1 = x_ref[...] + mod[2:3, :] * _rms(y, gpost_ref[...])
    x1_ref[...] = x1
    h2 = _rms(x1, gpre_ref[...]) * (1.0 + mod[4:5, :]) + mod[3:4, :]
    h2_ref[...] = h2
    pq_ref[...] = _dot(h2.astype(BF16), wq_ref[...]).astype(BF16)


def _mix(oa, ob, ga, gb, x2, mod3, g_post, g_pre, wa, wb, wo, wq, tm, tiles_per_seq):
    n = x2.shape[0]
    row = lambda i: (i, 0)
    const = lambda i: (0, 0)
    nq = wq.shape[1]
    wspec = lambda r, c: pl.BlockSpec((r, c), const, pipeline_mode=pl.Buffered(1))
    return pl.pallas_call(
        _mix_kernel,
        out_shape=[jax.ShapeDtypeStruct((n, D_MODEL), F32),
                   jax.ShapeDtypeStruct((n, D_MODEL), F32),
                   jax.ShapeDtypeStruct((n, nq), BF16)],
        grid=(n // tm,),
        in_specs=[pl.BlockSpec((tm, WIDTH), row), pl.BlockSpec((tm, WIDTH), row),
                  pl.BlockSpec((tm, D_MODEL), row), pl.BlockSpec((tm, D_MODEL), row),
                  pl.BlockSpec((tm, D_MODEL), row),
                  pl.BlockSpec((1, 6, D_MODEL), lambda i: (i // tiles_per_seq, 0, 0)),
                  pl.BlockSpec((1, D_MODEL), const), pl.BlockSpec((1, D_MODEL), const),
                  wspec(WIDTH, D_MODEL), wspec(WIDTH, D_MODEL), wspec(D_MODEL, D_MODEL),
                  wspec(D_MODEL, nq)],
        out_specs=[pl.BlockSpec((tm, D_MODEL), row), pl.BlockSpec((tm, D_MODEL), row),
                   pl.BlockSpec((tm, nq), row)],
        compiler_params=_params(("parallel",)),
        name="mix",
    )(oa, ob, ga, gb, x2, mod3, g_post, g_pre, wa, wb, wo, wq)


def _top16(s, payload=None):
    n = s.shape[0]
    pos = lax.broadcasted_iota(I32, s.shape, 0)
    vals, picks = [], []
    for _ in range(PEER_TOPK):
        m = jnp.max(s, axis=0, keepdims=True)
        p = jnp.min(jnp.where(s == m, pos, n), axis=0, keepdims=True)
        hit = pos == p
        vals.append(m)
        picks.append(p if payload is None else jnp.sum(jnp.where(hit, payload, 0), axis=0, keepdims=True))
        s = jnp.where(hit, -jnp.inf, s)
    return jnp.concatenate(vals, axis=0), jnp.concatenate(picks, axis=0)


def _peersel_kernel(pq_ref, sk_ref, idx_ref, g_ref):
    pq = pq_ref[...]
    s1 = _dot_nt(sk_ref[0, 0], pq[:, :PEER_DHALF])
    s2 = _dot_nt(sk_ref[0, 1], pq[:, PEER_DHALF:])
    v1, i1 = _top16(s1)
    v2, i2 = _top16(s2)
    cand = jnp.concatenate([v1[i:i + 1] + v2 for i in range(PEER_TOPK)], axis=0)
    cidx = jnp.concatenate([i1[i:i + 1] * PEER_NKEYS + i2 for i in range(PEER_TOPK)], axis=0)
    top, eidx = _top16(cand, cidx)
    ex = jnp.exp(top - top[0:1])
    g_ref[0] = ex / jnp.sum(ex, axis=0, keepdims=True)
    idx_ref[0] = eidx


def _peer_select(pq, subkeys_b, tt):
    n = pq.shape[0]
    return pl.pallas_call(
        _peersel_kernel,
        out_shape=[jax.ShapeDtypeStruct((PEER_HEADS, PEER_TOPK, n), I32),
                   jax.ShapeDtypeStruct((PEER_HEADS, PEER_TOPK, n), F32)],
        grid=(n // tt, PEER_HEADS),
        in_specs=[pl.BlockSpec((tt, 2 * PEER_DHALF), lambda i, h: (i, h)),
                  pl.BlockSpec((1, 2, PEER_NKEYS, PEER_DHALF), lambda i, h: (h, 0, 0, 0))],
        out_specs=[pl.BlockSpec((1, PEER_TOPK, tt), lambda i, h: (h, 0, i)),
                   pl.BlockSpec((1, PEER_TOPK, tt), lambda i, h: (h, 0, i))],
        compiler_params=_params(("parallel", "arbitrary")),
        name="peer_select",
    )(pq, subkeys_b)


def _erf_gelu(a):
    return 0.5 * a * (1.0 + lax.erf(a * (2.0 ** -0.5)))


def _peerffn_kernel(idx_ref, g_ref, h2_ref, x1_ref, mod_ref, gpost_ref, u_hbm, v_hbm, o_ref,
                    ubuf, vbuf, f_ref, sem, *, tg):
    def row_copies(t, slot, k):
        e = idx_ref[t, k]
        return (pltpu.make_async_copy(u_hbm.at[pl.ds(e, 1)], ubuf.at[slot, pl.ds(k, 1)], sem.at[0, slot]),
                pltpu.make_async_copy(v_hbm.at[pl.ds(e, 1)], vbuf.at[slot, pl.ds(k, 1)], sem.at[1, slot]))

    def start_token(t, slot):
        def body(k, carry):
            cu, cv = row_copies(t, slot, k)
            cu.start()
            cv.start()
            return carry
        lax.fori_loop(0, PEER_SEL, body, 0, unroll=8)

    def wait_token(slot):
        pltpu.make_async_copy(u_hbm.at[pl.ds(0, PEER_SEL)], ubuf.at[slot], sem.at[0, slot]).wait()
        pltpu.make_async_copy(v_hbm.at[pl.ds(0, PEER_SEL)], vbuf.at[slot], sem.at[1, slot]).wait()

    start_token(0, 0)

    def token(t, carry):
        slot = lax.rem(t, 2)

        @pl.when(t + 1 < tg)
        def _():
            start_token(t + 1, 1 - slot)

        wait_token(slot)
        hrow = h2_ref[pl.ds(t, 1), :].astype(BF16)
        a = _dot_nt(hrow, ubuf[slot].astype(BF16))
        wgt = (g_ref[pl.ds(t, 1), :] * _erf_gelu(a)).astype(BF16)
        f_ref[pl.ds(t, 1), :] = _dot(wgt, vbuf[slot].astype(BF16))
        return carry

    lax.fori_loop(0, tg, token, 0)
    mod = mod_ref[0]
    o_ref[...] = x1_ref[...] + mod[5:6, :] * _rms(f_ref[...], gpost_ref[...])


def _peer_ffn(idx, g, h2, x1, mod3, g_post, u, v, tg, tiles_per_seq):
    n = h2.shape[0]
    row = lambda i: (i, 0)
    return pl.pallas_call(
        functools.partial(_peerffn_kernel, tg=tg),
        out_shape=jax.ShapeDtypeStruct((n, D_MODEL), F32),
        grid=(n // tg,),
        in_specs=[pl.BlockSpec((tg, PEER_SEL), row, memory_space=pltpu.SMEM),
                  pl.BlockSpec((tg, PEER_SEL), row),
                  pl.BlockSpec((tg, D_MODEL), row),
                  pl.BlockSpec((tg, D_MODEL), row),
                  pl.BlockSpec((1, 6, D_MODEL), lambda i: (i // tiles_per_seq, 0, 0)),
                  pl.BlockSpec((1, D_MODEL), lambda i: (0, 0)),
                  pl.BlockSpec(memory_space=pl.ANY),
                  pl.BlockSpec(memory_space=pl.ANY)],
        out_specs=pl.BlockSpec((tg, D_MODEL), row),
        scratch_shapes=[pltpu.VMEM((2, PEER_SEL, D_MODEL), F32),
                        pltpu.VMEM((2, PEER_SEL, D_MODEL), F32),
                        pltpu.VMEM((tg, D_MODEL), F32),
                        pltpu.SemaphoreType.DMA((2, 2))],
        compiler_params=_params(("arbitrary",)),
        name="peer_ffn",
    )(idx, g, h2, x1, mod3, g_post, u, v)


def _pad_keys(x, s_pad):
    return jnp.pad(x, ((0, 0), (0, s_pad - x.shape[1]), (0, 0)))


def _layer(x, mod3, past, weights, tiles):
    (g_pre_mix, g_post_mix, g_pre_ffn, g_post_ffn, w_in_b, w_gate_b, wa_b, wb_b, wo_b, wq_b,
     subkeys_b, peer_u, peer_v) = weights
    tm, sb_tq, sb_tk, dsa_tq, dsa_tk, tt, tg = tiles
    bsz, t, _ = x.shape
    n = bsz * t
    x2 = x.reshape(n, D_MODEL)
    (aq, ak, av, bq, bk, bv, iq, ki2, akf, avf, bkf, bvf, ikf, iw, ga, gb) = _inproj(
        x2, mod3, g_pre_mix, w_in_b, w_gate_b, tm, t // tm)
    seq = lambda a: a.reshape(bsz, t, a.shape[-1])
    if past is None:
        q_off = 0
        ka, va, kb, vb, ki = seq(ak), seq(av), seq(bk), seq(bv), seq(ki2)
        s_len = t
    else:
        p_sbk, p_sbv, p_dk, p_dv, p_ki = past
        q_off = p_sbk.shape[1]
        s_len = q_off + t
        flat = lambda c: c.reshape(bsz, q_off, WIDTH).astype(BF16)
        ka = jnp.concatenate([flat(p_sbk), seq(ak)], axis=1)
        va = jnp.concatenate([flat(p_sbv), seq(av)], axis=1)
        kb = jnp.concatenate([flat(p_dk), seq(bk)], axis=1)
        vb = jnp.concatenate([flat(p_dv), seq(bv)], axis=1)
        pk = p_ki.astype(BF16)
        ki = jnp.concatenate([jnp.concatenate([pk, pk], axis=-1), seq(ki2)], axis=1)
    s_pad = -(-s_len // max(sb_tk, dsa_tk)) * max(sb_tk, dsa_tk)
    ka, va, kb, vb, ki = (_pad_keys(a, s_pad) for a in (ka, va, kb, vb, ki))
    n_sel = min(TOPK_MAX, s_len // 4)

    oa = _sb_attention(seq(aq), ka, va, sb_tq, sb_tk, q_off)
    ob = _dsa_attention(seq(bq), seq(iq), seq(iw), kb, vb, ki, dsa_tq, dsa_tk, q_off, n_sel)

    x1, h2, pq = _mix(oa.reshape(n, WIDTH), ob.reshape(n, WIDTH), ga, gb, x2, mod3, g_post_mix, g_pre_ffn,
                      wa_b, wb_b, wo_b, wq_b, tm, t // tm)
    eidx, gate = _peer_select(pq, subkeys_b, tt)
    eidx = eidx.reshape(PEER_SEL, n).T
    gate = gate.reshape(PEER_SEL, n).T
    y = _peer_ffn(eidx, gate, h2, x1, mod3, g_post_ffn, peer_u, peer_v, tg, t // tg)
    new = tuple(a.reshape(1, bsz, t, N_HEADS, HEAD_DIM) for a in (akf, avf, bkf, bvf)) + (
        ikf.reshape(1, bsz, t, HEAD_DIM),)
    return y.reshape(bsz, t, D_MODEL), new


def _tiles(t):
    if t % 256 == 0:
        return (256, 256, 256, 256, 256, 128, 64)
    return (t, t, 128, t, 128, 128, t)


def kernel(x_prompt, x_sample, c_prompt, c_sample, cache_sb_k, cache_sb_v, cache_dsa_k, cache_dsa_v, cache_dsa_kidx, w_ada, b_ada, g_pre_mix, g_post_mix, g_pre_ffn, g_post_ffn, w_in, w_gate, w_branch_a, w_branch_b, w_out, w_peer_q, peer_subkeys, peer_u, peer_v):
    assert w_ada.shape[0] == 1, "one layer"
    n_p, n_s = c_prompt.shape[0], c_sample.shape[0]
    c_all = jnp.concatenate([c_prompt, c_sample], axis=0)
    rows = -(-c_all.shape[0] // 8) * 8
    c_all = jnp.pad(c_all, ((0, rows - c_all.shape[0]), (0, 0)))
    mod = _modulation(c_all, w_ada[0], b_ada[0]).reshape(rows, 6, D_MODEL)
    w_in_b = jnp.pad(w_in[0], ((0, 0), (0, W_IN_PAD - W_IN_COLS))).astype(BF16)
    weights = (g_pre_mix, g_post_mix, g_pre_ffn, g_post_ffn, w_in_b, w_gate[0].astype(BF16),
               w_branch_a[0].astype(BF16), w_branch_b[0].astype(BF16), w_out[0].astype(BF16),
               w_peer_q[0].astype(BF16), peer_subkeys[0].astype(BF16), peer_u[0], peer_v[0])
    yp, new_p = _layer(x_prompt, mod[:n_p], None, weights, _tiles(x_prompt.shape[1]))
    past = (cache_sb_k[0], cache_sb_v[0], cache_dsa_k[0], cache_dsa_v[0], cache_dsa_kidx[0])
    ys, new_s = _layer(x_sample, mod[n_p:n_p + n_s], past, weights, _tiles(x_sample.shape[1]))
    return (yp, ys) + new_p + new_s
```

```python
import functools
import math

import jax
import jax.numpy as jnp
import numpy as np
from jax import lax
from jax.experimental import pallas as pl
from jax.experimental.pallas import tpu as pltpu

F32 = jnp.float32
BF16 = jnp.bfloat16
I32 = jnp.int32

D_MODEL = 1024
CHUNK = 64
EPS = 1e-6
N_HEADS = 8
HEAD_DIM = 64
WIDTH = N_HEADS * HEAD_DIM
N_PAIRS = N_HEADS // 2
LANES = 128
N_CHUNK = D_MODEL // LANES
TOPK_MAX = 256
PEER_HEADS = 8
PEER_NKEYS = 128
PEER_DHALF = 128
PEER_TOPK = 16
PEER_SEL = PEER_HEADS * PEER_TOPK
PEER_SLOTS = 4
PEER_AHEAD = 2
W_IN_COLS = 7 * WIDTH + HEAD_DIM + N_HEADS
W_IN_PAD = 7 * WIDTH + LANES
INT_MIN = -(2 ** 31)
NEG_BIG = -1e30
VMEM_LIMIT = 56 * 1024 * 1024


def _params(sem):
    return pltpu.CompilerParams(dimension_semantics=sem, vmem_limit_bytes=VMEM_LIMIT)


def _dot(a, b):
    return jnp.dot(a, b, preferred_element_type=F32)


def _dot_nt(a, b):
    return lax.dot_general(a, b, (((1,), (1,)), ((), ())), preferred_element_type=F32)


def _rms(x, gain):
    return x * lax.rsqrt(jnp.mean(x * x, axis=-1, keepdims=True) + EPS) * gain


def _mod_kernel(c_ref, w_ref, b_ref, o_ref):
    c = c_ref[...]
    s = c * (1.0 / (1.0 + jnp.exp(-c)))
    o_ref[...] = jnp.dot(s, w_ref[...], preferred_element_type=F32,
                         precision=lax.Precision.HIGHEST) + b_ref[...]


def _modulation(c, w_ada, b_ada):
    n = c.shape[0]
    cols = w_ada.shape[1]
    tn = 1024
    return pl.pallas_call(
        _mod_kernel,
        out_shape=jax.ShapeDtypeStruct((n, cols), F32),
        grid=(cols // tn,),
        in_specs=[pl.BlockSpec((n, D_MODEL), lambda j: (0, 0)),
                  pl.BlockSpec((D_MODEL, tn), lambda j: (0, j)),
                  pl.BlockSpec((1, tn), lambda j: (0, j))],
        out_specs=pl.BlockSpec((n, tn), lambda j: (0, j)),
        compiler_params=_params(("arbitrary",)),
        name="mod",
    )(c, w_ada, b_ada.reshape(1, cols))


def _inproj_kernel(x_ref, mod_ref, g_ref, win_ref, wg_ref,
                   aq_ref, ak_ref, av_ref, bq_ref, bk_ref, bv_ref, iq_ref, ki2_ref,
                   akf_ref, avf_ref, bkf_ref, bvf_ref, ikf_ref, iw_ref, ga_ref, gb_ref):
    x = x_ref[...]
    mod = mod_ref[0]
    h = _rms(x, g_ref[...]) * (1.0 + mod[1:2, :]) + mod[0:1, :]
    hb = h.astype(BF16)
    qscale = HEAD_DIM ** -0.5

    def seg(i):
        return _dot(hb, win_ref[:, i * WIDTH:(i + 1) * WIDTH])

    aq_ref[...] = (seg(0) * qscale).astype(BF16)
    z = seg(1)
    akf_ref[...] = z
    ak_ref[...] = z.astype(BF16)
    z = seg(2)
    avf_ref[...] = z
    av_ref[...] = z.astype(BF16)
    bq_ref[...] = (seg(3) * qscale).astype(BF16)
    z = seg(4)
    bkf_ref[...] = z
    bk_ref[...] = z.astype(BF16)
    z = seg(5)
    bvf_ref[...] = z
    bv_ref[...] = z.astype(BF16)
    iq_ref[...] = (seg(6) * qscale).astype(BF16)
    tail = _dot(hb, win_ref[:, 7 * WIDTH:7 * WIDTH + LANES])
    ik = tail[:, :HEAD_DIM]
    ikf_ref[...] = ik
    ki2_ref[...] = jnp.concatenate([ik, ik], axis=-1).astype(BF16)
    iw_ref[...] = tail[:, HEAD_DIM:HEAD_DIM + N_HEADS] * (N_HEADS ** -0.5)
    zg = _dot(hb, wg_ref[...])
    gate = 1.0 / (1.0 + jnp.exp(-zg))
    ga_ref[...] = gate[:, :D_MODEL]
    gb_ref[...] = gate[:, D_MODEL:]


def _inproj(x2, mod3, g_pre, w_in_b, w_gate_b, tm, tiles_per_seq):
    n = x2.shape[0]
    row = lambda i: (i, 0)
    const = lambda i: (0, 0)
    bspec = lambda w: pl.BlockSpec((tm, w), row)
    outs = ([jax.ShapeDtypeStruct((n, WIDTH), BF16)] * 7
            + [jax.ShapeDtypeStruct((n, LANES), BF16)]
            + [jax.ShapeDtypeStruct((n, WIDTH), F32)] * 4
            + [jax.ShapeDtypeStruct((n, HEAD_DIM), F32),
               jax.ShapeDtypeStruct((n, N_HEADS), F32),
               jax.ShapeDtypeStruct((n, D_MODEL), F32),
               jax.ShapeDtypeStruct((n, D_MODEL), F32)])
    out_specs = ([bspec(WIDTH)] * 7 + [bspec(LANES)] + [bspec(WIDTH)] * 4
                 + [bspec(HEAD_DIM), bspec(N_HEADS), bspec(D_MODEL), bspec(D_MODEL)])
    return pl.pallas_call(
        _inproj_kernel,
        out_shape=outs,
        grid=(n // tm,),
        in_specs=[pl.BlockSpec((tm, D_MODEL), row),
                  pl.BlockSpec((1, 6, D_MODEL), lambda i: (i // tiles_per_seq, 0, 0)),
                  pl.BlockSpec((1, D_MODEL), const),
                  pl.BlockSpec((D_MODEL, W_IN_PAD), const, pipeline_mode=pl.Buffered(1)),
                  pl.BlockSpec((D_MODEL, 2 * D_MODEL), const, pipeline_mode=pl.Buffered(1))],
        out_specs=out_specs,
        compiler_params=_params(("parallel",)),
        name="inproj",
    )(x2, mod3, g_pre, w_in_b, w_gate_b)


def _sb_kernel(tab_ref, q_ref, k_ref, v_ref, tri_ref, o_ref, acc_ref, run_ref, *, tq, tk, q_off):
    s = pl.program_id(2)
    qi = tab_ref[0, s]
    kj = tab_ref[1, s]
    first = tab_ref[2, s]
    last = tab_ref[3, s]
    masked = tab_ref[4, s]

    @pl.when(first == 1)
    def _():
        acc_ref[...] = jnp.zeros_like(acc_ref)
        run_ref[...] = jnp.zeros_like(run_ref)

    lane = lax.broadcasted_iota(I32, (tq, LANES), 1)

    def step(use_mask):
        q = q_ref[0]
        k = k_ref[0]
        v = v_ref[0]
        tri = tri_ref[...]
        if use_mask:
            qpos = q_off + qi * tq + lax.broadcasted_iota(I32, (tq, tk), 0)
            kpos = kj * tk + lax.broadcasted_iota(I32, (tq, tk), 1)
            causal = kpos < qpos
        for e in range(2):
            in_half = (lane >= HEAD_DIM) if e else (lane < HEAD_DIM)
            qe = jnp.where(in_half, q, jnp.zeros_like(q))
            z = _dot_nt(qe, k)
            lk = -(jnp.maximum(z, 0.0) + jnp.log(1.0 + jnp.exp(-jnp.abs(z))))
            if use_mask:
                lk = jnp.where(causal, lk, 0.0)
            hi = lk.astype(BF16)
            r1 = lk - hi.astype(F32)
            mid = r1.astype(BF16)
            lo = (r1 - mid.astype(F32)).astype(BF16)
            run = run_ref[e]
            suf = _dot(hi, tri) + _dot(mid, tri) + _dot(lo, tri) + run[:, 0:1]
            w = jnp.exp(z + suf)
            if use_mask:
                w = jnp.where(causal, w, 0.0)
            acc_ref[e] += _dot(w.astype(BF16), v)
            run_ref[e] = jnp.broadcast_to(suf[:, 0:1], (tq, LANES))

    @pl.when(masked == 1)
    def _():
        step(True)

    @pl.when(masked == 0)
    def _():
        step(False)

    @pl.when(last == 1)
    def _():
        o_ref[0] = jnp.where(lane < HEAD_DIM, acc_ref[0], acc_ref[1]).astype(o_ref.dtype)


def _sb_table(nq, nk, tq, tk, q_off):
    rows = []
    for qi in range(nq):
        q_lo = q_off + qi * tq
        q_hi = q_lo + tq - 1
        kjs = [kj for kj in range(nk) if kj * tk < q_hi]
        if not kjs:
            kjs = [0]
        kjs = kjs[::-1]
        for n, kj in enumerate(kjs):
            masked = int(kj * tk + tk - 1 >= q_lo)
            rows.append((qi, kj, int(n == 0), int(n == len(kjs) - 1), masked))
    return np.asarray(rows, dtype=np.int32).T.copy()


def _sb_attention(q, k, v, tq, tk, q_off):
    b, t, _ = q.shape
    s_len = k.shape[1]
    tab = _sb_table(t // tq, s_len // tk, tq, tk, q_off)
    tri = jnp.asarray(np.tril(np.ones((tk, tk), np.float32)), dtype=BF16)
    grid_spec = pltpu.PrefetchScalarGridSpec(
        num_scalar_prefetch=1,
        grid=(b, N_PAIRS, tab.shape[1]),
        in_specs=[pl.BlockSpec((1, tq, LANES), lambda bi, hp, s, tab: (bi, tab[0, s], hp)),
                  pl.BlockSpec((1, tk, LANES), lambda bi, hp, s, tab: (bi, tab[1, s], hp)),
                  pl.BlockSpec((1, tk, LANES), lambda bi, hp, s, tab: (bi, tab[1, s], hp)),
                  pl.BlockSpec((tk, tk), lambda bi, hp, s, tab: (0, 0))],
        out_specs=pl.BlockSpec((1, tq, LANES), lambda bi, hp, s, tab: (bi, tab[0, s], hp)),
        scratch_shapes=[pltpu.VMEM((2, tq, LANES), F32), pltpu.VMEM((2, tq, LANES), F32)],
    )
    return pl.pallas_call(
        functools.partial(_sb_kernel, tq=tq, tk=tk, q_off=q_off),
        out_shape=jax.ShapeDtypeStruct((b, t, WIDTH), BF16),
        grid_spec=grid_spec,
        compiler_params=_params(("parallel", "parallel", "arbitrary")),
        name="sb_attn",
    )(jnp.asarray(tab), q, k, v, tri)


def _sortable(x):
    bits = pltpu.bitcast(x + 0.0, I32)
    return jnp.where(bits < 0, bits ^ 0x7FFFFFFF, bits)


def _dsa_kernel(q_ref, iq_ref, iw_ref, k_ref, v_ref, ki_ref, o_ref,
                keys_ref, bias_ref, m_ref, l_ref, acc_ref,
                *, tq, tk, q_off, n_kb_max, n_sel, idx_bits):
    qi = pl.program_id(1)
    q_lo = q_off + qi * tq
    adm_end = ((q_lo + tq - 1) // CHUNK + 1) * CHUNK
    n_kb = jnp.minimum((adm_end + tk - 1) // tk, n_kb_max)

    lane = lax.broadcasted_iota(I32, (tq, LANES), 1)
    lo_half = lane < HEAD_DIM
    row_pos = q_lo + lax.broadcasted_iota(I32, (tq, tk), 0)
    col_iota = lax.broadcasted_iota(I32, (tq, tk), 1)

    def split_heads(x_pair):
        zero = jnp.zeros_like(x_pair)
        return jnp.where(lo_half, x_pair, zero), jnp.where(lo_half, zero, x_pair)

    iw = iw_ref[0]

    def score_block(j, carry):
        ki = ki_ref[0, pl.ds(pl.multiple_of(j * tk, tk), tk), :]
        score = jnp.zeros((tq, tk), F32)
        for hp in range(N_PAIRS):
            pair = iq_ref[0, :, hp * LANES:(hp + 1) * LANES]
            for e, qe in enumerate(split_heads(pair)):
                h = 2 * hp + e
                score = score + iw[:, h:h + 1] * jnp.maximum(_dot_nt(qe, ki), 0.0)
        kpos = j * tk + col_iota
        adm = (kpos // CHUNK) <= (row_pos // CHUNK)
        keys_ref[j] = jnp.where(adm, _sortable(score), INT_MIN)
        return carry

    lax.fori_loop(0, n_kb, score_block, 0)

    def count_rows(pred_fn):
        def body(j, acc):
            hit = pred_fn(keys_ref[j], j).astype(I32)
            part = hit[:, 0:LANES]
            for c in range(1, tk // LANES):
                part = part + hit[:, c * LANES:(c + 1) * LANES]
            return acc + part
        acc = lax.fori_loop(0, n_kb, body, jnp.zeros((tq, LANES), I32))
        return jnp.sum(acc, axis=1, keepdims=True)

    def bit_step(i, thr):
        cand = thr + jnp.left_shift(jnp.int32(1), 31 - i)
        cnt = count_rows(lambda kb, j: kb >= cand)
        return jnp.where(cnt >= n_sel, cand, thr)

    thr = lax.fori_loop(0, 32, bit_step, jnp.full((tq, 1), INT_MIN, I32))
    n_gt = count_rows(lambda kb, j: kb > thr)
    n_ge = count_rows(lambda kb, j: kb >= thr)
    need = n_sel - n_gt
    real = thr > INT_MIN
    tie = jnp.max(jnp.where(real & (n_ge > n_sel), 1, 0)) > 0

    def tie_cut():
        def idx_step(i, cut):
            cand = cut + jnp.left_shift(jnp.int32(1), idx_bits - 1 - i)
            cnt = count_rows(lambda kb, j: (kb == thr) & ((j * tk + col_iota) < cand))
            return jnp.where(cnt < need, cand, cut)
        return lax.fori_loop(0, idx_bits, idx_step, jnp.zeros((tq, 1), I32))

    cut = lax.cond(tie, tie_cut, lambda: jnp.full((tq, 1), n_kb_max * tk, I32))

    def bias_block(j, carry):
        kb = keys_ref[j]
        kpos = j * tk + col_iota
        sel = (kb > thr) | ((kb == thr) & real & (kpos <= cut))
        bias_ref[j] = jnp.where(sel, 0.0, NEG_BIG)
        return carry

    lax.fori_loop(0, n_kb, bias_block, 0)

    m_ref[...] = jnp.full_like(m_ref, NEG_BIG)
    l_ref[...] = jnp.zeros_like(l_ref)
    acc_ref[...] = jnp.zeros_like(acc_ref)

    def attn_block(j, carry):
        start = pl.multiple_of(j * tk, tk)
        bias = bias_ref[j]
        dist = jnp.abs(row_pos - (j * tk + col_iota)).astype(F32)
        for hp in range(N_PAIRS):
            cols = slice(hp * LANES, (hp + 1) * LANES)
            kp = k_ref[0, pl.ds(start, tk), cols]
            vp = v_ref[0, pl.ds(start, tk), cols]
            for e, qe in enumerate(split_heads(q_ref[0, :, cols])):
                h = 2 * hp + e
                slope = 2.0 ** (-8.0 * (h + 1) / N_HEADS)
                s = _dot_nt(qe, kp) - slope * dist + bias
                m_old = m_ref[h]
                m_new = jnp.maximum(m_old, jnp.max(s, axis=1, keepdims=True))
                alpha = jnp.exp(m_old - m_new)
                p = jnp.exp(s - m_new[:, 0:1])
                l_ref[h] = alpha * l_ref[h] + jnp.sum(p, axis=1, keepdims=True)
                acc_ref[h] = alpha * acc_ref[h] + _dot(p.astype(BF16), vp)
                m_ref[h] = m_new
        return carry

    lax.fori_loop(0, n_kb, attn_block, 0)

    for hp in range(N_PAIRS):
        o_lo = acc_ref[2 * hp] / l_ref[2 * hp]
        o_hi = acc_ref[2 * hp + 1] / l_ref[2 * hp + 1]
        o_ref[0, :, hp * LANES:(hp + 1) * LANES] = jnp.where(lo_half, o_lo, o_hi).astype(o_ref.dtype)


def _dsa_attention(q, iq, iw, k, v, ki2, tq, tk, q_off, n_sel):
    b, t, _ = q.shape
    s_len = k.shape[1]
    n_kb_max = s_len // tk
    qspec = lambda w: pl.BlockSpec((1, tq, w), lambda bi, i: (bi, i, 0))
    kspec = lambda w: pl.BlockSpec((1, s_len, w), lambda bi, i: (bi, 0, 0), pipeline_mode=pl.Buffered(1))
    return pl.pallas_call(
        functools.partial(_dsa_kernel, tq=tq, tk=tk, q_off=q_off, n_kb_max=n_kb_max, n_sel=n_sel,
                          idx_bits=max(1, int(math.ceil(math.log2(s_len + 1))))),
        out_shape=jax.ShapeDtypeStruct((b, t, WIDTH), BF16),
        grid=(b, t // tq),
        in_specs=[qspec(WIDTH), qspec(WIDTH), qspec(N_HEADS), kspec(WIDTH), kspec(WIDTH), kspec(LANES)],
        out_specs=qspec(WIDTH),
        scratch_shapes=[pltpu.VMEM((n_kb_max, tq, tk), I32),
                        pltpu.VMEM((n_kb_max, tq, tk), F32),
                        pltpu.VMEM((N_HEADS, tq, LANES), F32),
                        pltpu.VMEM((N_HEADS, tq, LANES), F32),
                        pltpu.VMEM((N_HEADS, tq, LANES), F32)],
        compiler_params=_params(("parallel", "arbitrary")),
        name="dsa_attn",
    )(q, iq, iw, k, v, ki2)


def _mix_kernel(oa_ref, ob_ref, ga_ref, gb_ref, x_ref, mod_ref, gpost_ref, gpre_ref,
                wa_ref, wb_ref, wo_ref, wq_ref, x1_ref, h2_ref, pq_ref):
    mod = mod_ref[0]
    mixed = ga_ref[...] * _dot(oa_ref[...], wa_ref[...]) + gb_ref[...] * _dot(ob_ref[...], wb_ref[...])
    y = _dot(mixed.astype(BF16), wo_ref[...])
    x1 = x_ref[...] + mod[2:3, :] * _rms(y, gpost_ref[...])
    x1_ref[...] = x1
    h2 = _rms(x1, gpre_ref[...]) * (1.0 + mod[4:5, :]) + mod[3:4, :]
    h2_ref[...] = h2
    pq_ref[...] = _dot(h2.astype(BF16), wq_ref[...]).astype(BF16)


def _mix(oa, ob, ga, gb, x2, mod3, g_post, g_pre, wa, wb, wo, wq, tm, tiles_per_seq):
    n = x2.shape[0]
    row = lambda i: (i, 0)
    const = lambda i: (0, 0)
    nq = wq.shape[1]
    wspec = lambda r, c: pl.BlockSpec((r, c), const, pipeline_mode=pl.Buffered(1))
    return pl.pallas_call(
        _mix_kernel,
        out_shape=[jax.ShapeDtypeStruct((n, D_MODEL), F32),
                   jax.ShapeDtypeStruct((n, D_MODEL), F32),
                   jax.ShapeDtypeStruct((n, nq), BF16)],
        grid=(n // tm,),
        in_specs=[pl.BlockSpec((tm, WIDTH), row), pl.BlockSpec((tm, WIDTH), row),
                  pl.BlockSpec((tm, D_MODEL), row), pl.BlockSpec((tm, D_MODEL), row),
                  pl.BlockSpec((tm, D_MODEL), row),
                  pl.BlockSpec((1, 6, D_MODEL), lambda i: (i // tiles_per_seq, 0, 0)),
                  pl.BlockSpec((1, D_MODEL), const), pl.BlockSpec((1, D_MODEL), const),
                  wspec(WIDTH, D_MODEL), wspec(WIDTH, D_MODEL), wspec(D_MODEL, D_MODEL),
                  wspec(D_MODEL, nq)],
        out_specs=[pl.BlockSpec((tm, D_MODEL), row), pl.BlockSpec((tm, D_MODEL), row),
                   pl.BlockSpec((tm, nq), row)],
        compiler_params=_params(("parallel",)),
        name="mix",
    )(oa, ob, ga, gb, x2, mod3, g_post, g_pre, wa, wb, wo, wq)


def _top16(s, payload=None):
    n = s.shape[0]
    pos = lax.broadcasted_iota(I32, s.shape, 0)
    vals, picks = [], []
    for _ in range(PEER_TOPK):
        m = jnp.max(s, axis=0, keepdims=True)
        p = jnp.min(jnp.where(s == m, pos, n), axis=0, keepdims=True)
        hit = pos == p
        vals.append(m)
        picks.append(p if payload is None else jnp.sum(jnp.where(hit, payload, 0), axis=0, keepdims=True))
        s = jnp.where(hit, -jnp.inf, s)
    return jnp.concatenate(vals, axis=0), jnp.concatenate(picks, axis=0)


def _peersel_kernel(pq_ref, sk_ref, idx_ref, g_ref):
    pq = pq_ref[...]
    s1 = _dot_nt(sk_ref[0, 0], pq[:, :PEER_DHALF])
    s2 = _dot_nt(sk_ref[0, 1], pq[:, PEER_DHALF:])
    v1, i1 = _top16(s1)
    v2, i2 = _top16(s2)
    cand = jnp.concatenate([v1[i:i + 1] + v2 for i in range(PEER_TOPK)], axis=0)
    cidx = jnp.concatenate([i1[i:i + 1] * PEER_NKEYS + i2 for i in range(PEER_TOPK)], axis=0)
    top, eidx = _top16(cand, cidx)
    ex = jnp.exp(top - top[0:1])
    g_ref[0] = ex / jnp.sum(ex, axis=0, keepdims=True)
    idx_ref[0] = eidx


def _peer_select(pq, subkeys_b, tt):
    n = pq.shape[0]
    return pl.pallas_call(
        _peersel_kernel,
        out_shape=[jax.ShapeDtypeStruct((PEER_HEADS, PEER_TOPK, n), I32),
                   jax.ShapeDtypeStruct((PEER_HEADS, PEER_TOPK, n), F32)],
        grid=(n // tt, PEER_HEADS),
        in_specs=[pl.BlockSpec((tt, 2 * PEER_DHALF), lambda i, h: (i, h)),
                  pl.BlockSpec((1, 2, PEER_NKEYS, PEER_DHALF), lambda i, h: (h, 0, 0, 0))],
        out_specs=[pl.BlockSpec((1, PEER_TOPK, tt), lambda i, h: (h, 0, i)),
                   pl.BlockSpec((1, PEER_TOPK, tt), lambda i, h: (h, 0, i))],
        compiler_params=_params(("parallel", "arbitrary")),
        name="peer_select",
    )(pq, subkeys_b)


def _erf_gelu(a):
    return 0.5 * a * (1.0 + lax.erf(a * (2.0 ** -0.5)))


def _peerffn_kernel(idx_ref, gt_ref, h2_ref, x1_ref, mod_ref, gpost_ref, uv_hbm, o_ref,
                    buf0, buf1, buf2, buf3, f_ref, sem, *, tg, seqs_per_tile):
    bufs = (buf0, buf1, buf2, buf3)

    def start_token(t, slot):
        tt = jnp.minimum(t, tg - 1)
        for k in range(PEER_SEL):
            pltpu.make_async_copy(uv_hbm.at[pl.ds(idx_ref[tt, k], 1)], bufs[slot].at[pl.ds(k, 1)],
                                  sem.at[slot]).start(priority=k % 2)

    def wait_token(slot):
        pltpu.make_async_copy(uv_hbm.at[pl.ds(0, PEER_SEL)], bufs[slot], sem.at[slot]).wait()

    lane_tok = lax.broadcasted_iota(I32, (PEER_SEL, tg), 1)
    sub8 = lax.broadcasted_iota(I32, (8, D_MODEL), 0)

    def compute(t, slot):
        buf = bufs[slot]
        t8 = pl.multiple_of((t // 8) * 8, 8)
        h8 = h2_ref[pl.ds(t8, 8), :]
        hrow = jnp.sum(jnp.where(sub8 == t - t8, h8, 0.0), axis=0, keepdims=True)
        acc = None
        for c in range(N_CHUNK):
            term = buf[:, c * LANES:(c + 1) * LANES] * hrow[:, c * LANES:(c + 1) * LANES]
            acc = term if acc is None else acc + term
        a = jnp.sum(acc, axis=1, keepdims=True)
        g = jnp.sum(jnp.where(lane_tok == t, gt_ref[...], 0.0), axis=1, keepdims=True)
        w = g * _erf_gelu(a)
        pieces = []
        for c in range(N_CHUNK):
            vc = buf[:, D_MODEL + c * LANES:D_MODEL + (c + 1) * LANES]
            pieces.append(jnp.sum(vc * w, axis=0, keepdims=True))
        f_ref[pl.ds(t, 1), :] = jnp.concatenate(pieces, axis=1)

    for t in range(PEER_AHEAD):
        start_token(t, t)

    def group(i, carry):
        t0 = PEER_SLOTS * i
        for j in range(PEER_SLOTS):
            wait_token(j)
            start_token(t0 + j + PEER_AHEAD, (j + PEER_AHEAD) % PEER_SLOTS)
            compute(t0 + j, j)
        return carry

    lax.fori_loop(0, tg // PEER_SLOTS, group, 0)
    for t in range(PEER_AHEAD):
        wait_token(t)
    rows = tg // seqs_per_tile
    for s in range(seqs_per_tile):
        sl = slice(s * rows, (s + 1) * rows)
        o_ref[sl, :] = x1_ref[sl, :] + mod_ref[s][5:6, :] * _rms(f_ref[sl, :], gpost_ref[...])


def _peer_ffn(idx, gt, h2, x1, mod3, g_post, uv, tg, seq_len):
    n = h2.shape[0]
    row = lambda i: (i, 0)
    seqs_per_tile = max(1, tg // seq_len)
    tiles_per_seq = max(1, seq_len // tg)
    return pl.pallas_call(
        functools.partial(_peerffn_kernel, tg=tg, seqs_per_tile=seqs_per_tile),
        out_shape=jax.ShapeDtypeStruct((n, D_MODEL), F32),
        grid=(n // tg,),
        in_specs=[pl.BlockSpec((tg, PEER_SEL), row, memory_space=pltpu.SMEM),
                  pl.BlockSpec((PEER_SEL, tg), lambda i: (0, i)),
                  pl.BlockSpec((tg, D_MODEL), row),
                  pl.BlockSpec((tg, D_MODEL), row),
                  pl.BlockSpec((seqs_per_tile, 6, D_MODEL), lambda i: (i // tiles_per_seq, 0, 0)),
                  pl.BlockSpec((1, D_MODEL), lambda i: (0, 0)),
                  pl.BlockSpec(memory_space=pl.ANY)],
        out_specs=pl.BlockSpec((tg, D_MODEL), row),
        scratch_shapes=[pltpu.VMEM((PEER_SEL, 2 * D_MODEL), F32)] * PEER_SLOTS
                       + [pltpu.VMEM((tg, D_MODEL), F32), pltpu.SemaphoreType.DMA((PEER_SLOTS,))],
        compiler_params=_params(("arbitrary",)),
        name="peer_ffn",
    )(idx, gt, h2, x1, mod3, g_post, uv)


def _pad_keys(x, s_pad):
    return jnp.pad(x, ((0, 0), (0, s_pad - x.shape[1]), (0, 0)))


def _layer(x, mod3, past, weights, tiles):
    (g_pre_mix, g_post_mix, g_pre_ffn, g_post_ffn, w_in_b, w_gate_b, wa_b, wb_b, wo_b, wq_b,
     subkeys_b, peer_uv) = weights
    tm, sb_tq, sb_tk, dsa_tq, dsa_tk, tt, tg = tiles
    bsz, t, _ = x.shape
    n = bsz * t
    x2 = x.reshape(n, D_MODEL)
    (aq, ak, av, bq, bk, bv, iq, ki2, akf, avf, bkf, bvf, ikf, iw, ga, gb) = _inproj(
        x2, mod3, g_pre_mix, w_in_b, w_gate_b, tm, t // tm)
    seq = lambda a: a.reshape(bsz, t, a.shape[-1])
    if past is None:
        q_off = 0
        ka, va, kb, vb, ki = seq(ak), seq(av), seq(bk), seq(bv), seq(ki2)
        s_len = t
    else:
        p_sbk, p_sbv, p_dk, p_dv, p_ki = past
        q_off = p_sbk.shape[1]
        s_len = q_off + t
        flat = lambda c: c.reshape(bsz, q_off, WIDTH).astype(BF16)
        ka = jnp.concatenate([flat(p_sbk), seq(ak)], axis=1)
        va = jnp.concatenate([flat(p_sbv), seq(av)], axis=1)
        kb = jnp.concatenate([flat(p_dk), seq(bk)], axis=1)
        vb = jnp.concatenate([flat(p_dv), seq(bv)], axis=1)
        pk = p_ki.astype(BF16)
        ki = jnp.concatenate([jnp.concatenate([pk, pk], axis=-1), seq(ki2)], axis=1)
    s_pad = -(-s_len // max(sb_tk, dsa_tk)) * max(sb_tk, dsa_tk)
    ka, va, kb, vb, ki = (_pad_keys(a, s_pad) for a in (ka, va, kb, vb, ki))
    n_sel = min(TOPK_MAX, s_len // 4)

    oa = _sb_attention(seq(aq), ka, va, sb_tq, sb_tk, q_off)
    ob = _dsa_attention(seq(bq), seq(iq), seq(iw), kb, vb, ki, dsa_tq, dsa_tk, q_off, n_sel)

    x1, h2, pq = _mix(oa.reshape(n, WIDTH), ob.reshape(n, WIDTH), ga, gb, x2, mod3, g_post_mix, g_pre_ffn,
                      wa_b, wb_b, wo_b, wq_b, tm, t // tm)
    eidx, gate = _peer_select(pq, subkeys_b, tt)
    eidx = eidx.reshape(PEER_SEL, n).T
    y = _peer_ffn(eidx, gate.reshape(PEER_SEL, n), h2, x1, mod3, g_post_ffn, peer_uv, tg, t)
    new = tuple(a.reshape(1, bsz, t, N_HEADS, HEAD_DIM) for a in (akf, avf, bkf, bvf)) + (
        ikf.reshape(1, bsz, t, HEAD_DIM),)
    return y.reshape(bsz, t, D_MODEL), new


def _tiles(t):
    if t % 256 == 0:
        return (256, 256, 256, 256, 256, 128, 128)
    return (t, t, 128, t, 128, 128, 128)


def kernel(x_prompt, x_sample, c_prompt, c_sample, cache_sb_k, cache_sb_v, cache_dsa_k, cache_dsa_v, cache_dsa_kidx, w_ada, b_ada, g_pre_mix, g_post_mix, g_pre_ffn, g_post_ffn, w_in, w_gate, w_branch_a, w_branch_b, w_out, w_peer_q, peer_subkeys, peer_u, peer_v):
    assert w_ada.shape[0] == 1, "one layer"
    n_p, n_s = c_prompt.shape[0], c_sample.shape[0]
    c_all = jnp.concatenate([c_prompt, c_sample], axis=0)
    rows = -(-c_all.shape[0] // 8) * 8
    c_all = jnp.pad(c_all, ((0, rows - c_all.shape[0]), (0, 0)))
    mod = _modulation(c_all, w_ada[0], b_ada[0]).reshape(rows, 6, D_MODEL)
    w_in_b = jnp.pad(w_in[0], ((0, 0), (0, W_IN_PAD - W_IN_COLS))).astype(BF16)
    weights = (g_pre_mix, g_post_mix, g_pre_ffn, g_post_ffn, w_in_b, w_gate[0].astype(BF16),
               w_branch_a[0].astype(BF16), w_branch_b[0].astype(BF16), w_out[0].astype(BF16),
               w_peer_q[0].astype(BF16), peer_subkeys[0].astype(BF16),
               jnp.concatenate([peer_u[0], peer_v[0]], axis=1))
    yp, new_p = _layer(x_prompt, mod[:n_p], None, weights, _tiles(x_prompt.shape[1]))
    past = (cache_sb_k[0], cache_sb_v[0], cache_dsa_k[0], cache_dsa_v[0], cache_dsa_kidx[0])
    ys, new_s = _layer(x_sample, mod[n_p:n_p + n_s], past, weights, _tiles(x_sample.shape[1]))
    return (yp, ys) + new_p + new_s
```

```python
import functools
import math

import jax
import jax.numpy as jnp
import numpy as np
from jax import lax
from jax.experimental import pallas as pl
from jax.experimental.pallas import tpu as pltpu

F32 = jnp.float32
BF16 = jnp.bfloat16
I32 = jnp.int32

D_MODEL = 1024
CHUNK = 64
EPS = 1e-6
N_HEADS = 8
HEAD_DIM = 64
WIDTH = N_HEADS * HEAD_DIM
N_PAIRS = N_HEADS // 2
LANES = 128
N_CHUNK = D_MODEL // LANES
TOPK_MAX = 256
PEER_HEADS = 8
PEER_NKEYS = 128
PEER_DHALF = 128
PEER_TOPK = 16
PEER_SEL = PEER_HEADS * PEER_TOPK
PEER_SLOTS = 8
PEER_AHEAD = 6
W_IN_COLS = 7 * WIDTH + HEAD_DIM + N_HEADS
W_IN_PAD = 7 * WIDTH + LANES
INT_MIN = -(2 ** 31)
NEG_BIG = -1e30
VMEM_LIMIT = 56 * 1024 * 1024


def _params(sem):
    return pltpu.CompilerParams(dimension_semantics=sem, vmem_limit_bytes=VMEM_LIMIT)


def _dot(a, b):
    return jnp.dot(a, b, preferred_element_type=F32)


def _dot_nt(a, b):
    return lax.dot_general(a, b, (((1,), (1,)), ((), ())), preferred_element_type=F32)


def _rms(x, gain):
    return x * lax.rsqrt(jnp.mean(x * x, axis=-1, keepdims=True) + EPS) * gain


def _mod_kernel(c_ref, w_ref, b_ref, o_ref):
    c = c_ref[...]
    s = c * (1.0 / (1.0 + jnp.exp(-c)))
    o_ref[...] = jnp.dot(s, w_ref[...], preferred_element_type=F32,
                         precision=lax.Precision.HIGHEST) + b_ref[...]


def _modulation(c, w_ada, b_ada):
    n = c.shape[0]
    cols = w_ada.shape[1]
    tn = 1024
    return pl.pallas_call(
        _mod_kernel,
        out_shape=jax.ShapeDtypeStruct((n, cols), F32),
        grid=(cols // tn,),
        in_specs=[pl.BlockSpec((n, D_MODEL), lambda j: (0, 0)),
                  pl.BlockSpec((D_MODEL, tn), lambda j: (0, j)),
                  pl.BlockSpec((1, tn), lambda j: (0, j))],
        out_specs=pl.BlockSpec((n, tn), lambda j: (0, j)),
        compiler_params=_params(("arbitrary",)),
        name="mod",
    )(c, w_ada, b_ada.reshape(1, cols))


def _inproj_kernel(x_ref, mod_ref, g_ref, win_ref, wg_ref,
                   aq_ref, ak_ref, av_ref, bq_ref, bk_ref, bv_ref, iq_ref, ki2_ref,
                   akf_ref, avf_ref, bkf_ref, bvf_ref, ikf_ref, iw_ref, ga_ref, gb_ref):
    x = x_ref[...]
    mod = mod_ref[0]
    h = _rms(x, g_ref[...]) * (1.0 + mod[1:2, :]) + mod[0:1, :]
    hb = h.astype(BF16)
    qscale = HEAD_DIM ** -0.5

    def seg(i):
        return _dot(hb, win_ref[:, i * WIDTH:(i + 1) * WIDTH])

    aq_ref[...] = (seg(0) * qscale).astype(BF16)
    z = seg(1)
    akf_ref[...] = z
    ak_ref[...] = z.astype(BF16)
    z = seg(2)
    avf_ref[...] = z
    av_ref[...] = z.astype(BF16)
    bq_ref[...] = (seg(3) * qscale).astype(BF16)
    z = seg(4)
    bkf_ref[...] = z
    bk_ref[...] = z.astype(BF16)
    z = seg(5)
    bvf_ref[...] = z
    bv_ref[...] = z.astype(BF16)
    iq_ref[...] = (seg(6) * qscale).astype(BF16)
    tail = _dot(hb, win_ref[:, 7 * WIDTH:7 * WIDTH + LANES])
    ik = tail[:, :HEAD_DIM]
    ikf_ref[...] = ik
    ki2_ref[...] = jnp.concatenate([ik, ik], axis=-1).astype(BF16)
    iw_ref[...] = tail[:, HEAD_DIM:HEAD_DIM + N_HEADS] * (N_HEADS ** -0.5)
    zg = _dot(hb, wg_ref[...])
    gate = 1.0 / (1.0 + jnp.exp(-zg))
    ga_ref[...] = gate[:, :D_MODEL]
    gb_ref[...] = gate[:, D_MODEL:]


def _inproj(x2, mod3, g_pre, w_in_b, w_gate_b, tm, tiles_per_seq):
    n = x2.shape[0]
    row = lambda i: (i, 0)
    const = lambda i: (0, 0)
    bspec = lambda w: pl.BlockSpec((tm, w), row)
    outs = ([jax.ShapeDtypeStruct((n, WIDTH), BF16)] * 7
            + [jax.ShapeDtypeStruct((n, LANES), BF16)]
            + [jax.ShapeDtypeStruct((n, WIDTH), F32)] * 4
            + [jax.ShapeDtypeStruct((n, HEAD_DIM), F32),
               jax.ShapeDtypeStruct((n, N_HEADS), F32),
               jax.ShapeDtypeStruct((n, D_MODEL), F32),
               jax.ShapeDtypeStruct((n, D_MODEL), F32)])
    out_specs = ([bspec(WIDTH)] * 7 + [bspec(LANES)] + [bspec(WIDTH)] * 4
                 + [bspec(HEAD_DIM), bspec(N_HEADS), bspec(D_MODEL), bspec(D_MODEL)])
    return pl.pallas_call(
        _inproj_kernel,
        out_shape=outs,
        grid=(n // tm,),
        in_specs=[pl.BlockSpec((tm, D_MODEL), row),
                  pl.BlockSpec((1, 6, D_MODEL), lambda i: (i // tiles_per_seq, 0, 0)),
                  pl.BlockSpec((1, D_MODEL), const),
                  pl.BlockSpec((D_MODEL, W_IN_PAD), const, pipeline_mode=pl.Buffered(1)),
                  pl.BlockSpec((D_MODEL, 2 * D_MODEL), const, pipeline_mode=pl.Buffered(1))],
        out_specs=out_specs,
        compiler_params=_params(("parallel",)),
        name="inproj",
    )(x2, mod3, g_pre, w_in_b, w_gate_b)


def _sb_kernel(tab_ref, q_ref, k_ref, v_ref, tri_ref, o_ref, acc_ref, run_ref, *, tq, tk, q_off):
    s = pl.program_id(2)
    qi = tab_ref[0, s]
    kj = tab_ref[1, s]
    first = tab_ref[2, s]
    last = tab_ref[3, s]
    masked = tab_ref[4, s]

    @pl.when(first == 1)
    def _():
        acc_ref[...] = jnp.zeros_like(acc_ref)
        run_ref[...] = jnp.zeros_like(run_ref)

    lane = lax.broadcasted_iota(I32, (tq, LANES), 1)

    def step(use_mask):
        q = q_ref[0]
        k = k_ref[0]
        v = v_ref[0]
        tri = tri_ref[...]
        if use_mask:
            qpos = q_off + qi * tq + lax.broadcasted_iota(I32, (tq, tk), 0)
            kpos = kj * tk + lax.broadcasted_iota(I32, (tq, tk), 1)
            causal = kpos < qpos
        for e in range(2):
            in_half = (lane >= HEAD_DIM) if e else (lane < HEAD_DIM)
            qe = jnp.where(in_half, q, jnp.zeros_like(q))
            z = _dot_nt(qe, k)
            lk = -(jnp.maximum(z, 0.0) + jnp.log(1.0 + jnp.exp(-jnp.abs(z))))
            if use_mask:
                lk = jnp.where(causal, lk, 0.0)
            hi = lk.astype(BF16)
            r1 = lk - hi.astype(F32)
            mid = r1.astype(BF16)
            lo = (r1 - mid.astype(F32)).astype(BF16)
            run = run_ref[e]
            suf = _dot(hi, tri) + _dot(mid, tri) + _dot(lo, tri) + run[:, 0:1]
            w = jnp.exp(z + suf)
            if use_mask:
                w = jnp.where(causal, w, 0.0)
            acc_ref[e] += _dot(w.astype(BF16), v)
            run_ref[e] = jnp.broadcast_to(suf[:, 0:1], (tq, LANES))

    @pl.when(masked == 1)
    def _():
        step(True)

    @pl.when(masked == 0)
    def _():
        step(False)

    @pl.when(last == 1)
    def _():
        o_ref[0] = jnp.where(lane < HEAD_DIM, acc_ref[0], acc_ref[1]).astype(o_ref.dtype)


def _sb_table(nq, nk, tq, tk, q_off):
    rows = []
    for qi in range(nq):
        q_lo = q_off + qi * tq
        q_hi = q_lo + tq - 1
        kjs = [kj for kj in range(nk) if kj * tk < q_hi]
        if not kjs:
            kjs = [0]
        kjs = kjs[::-1]
        for n, kj in enumerate(kjs):
            masked = int(kj * tk + tk - 1 >= q_lo)
            rows.append((qi, kj, int(n == 0), int(n == len(kjs) - 1), masked))
    return np.asarray(rows, dtype=np.int32).T.copy()


def _sb_attention(q, k, v, tq, tk, q_off):
    b, t, _ = q.shape
    s_len = k.shape[1]
    tab = _sb_table(t // tq, s_len // tk, tq, tk, q_off)
    tri = jnp.asarray(np.tril(np.ones((tk, tk), np.float32)), dtype=BF16)
    grid_spec = pltpu.PrefetchScalarGridSpec(
        num_scalar_prefetch=1,
        grid=(b, N_PAIRS, tab.shape[1]),
        in_specs=[pl.BlockSpec((1, tq, LANES), lambda bi, hp, s, tab: (bi, tab[0, s], hp)),
                  pl.BlockSpec((1, tk, LANES), lambda bi, hp, s, tab: (bi, tab[1, s], hp)),
                  pl.BlockSpec((1, tk, LANES), lambda bi, hp, s, tab: (bi, tab[1, s], hp)),
                  pl.BlockSpec((tk, tk), lambda bi, hp, s, tab: (0, 0))],
        out_specs=pl.BlockSpec((1, tq, LANES), lambda bi, hp, s, tab: (bi, tab[0, s], hp)),
        scratch_shapes=[pltpu.VMEM((2, tq, LANES), F32), pltpu.VMEM((2, tq, LANES), F32)],
    )
    return pl.pallas_call(
        functools.partial(_sb_kernel, tq=tq, tk=tk, q_off=q_off),
        out_shape=jax.ShapeDtypeStruct((b, t, WIDTH), BF16),
        grid_spec=grid_spec,
        compiler_params=_params(("parallel", "parallel", "arbitrary")),
        name="sb_attn",
    )(jnp.asarray(tab), q, k, v, tri)


def _sortable(x):
    bits = pltpu.bitcast(x + 0.0, I32)
    return jnp.where(bits < 0, bits ^ 0x7FFFFFFF, bits)


def _dsa_kernel(q_ref, iq_ref, iw_ref, k_ref, v_ref, ki_ref, o_ref,
                keys_ref, bias_ref, m_ref, l_ref, acc_ref,
                *, tq, tk, q_off, n_kb_max, n_sel, idx_bits):
    qi = pl.program_id(1)
    q_lo = q_off + qi * tq
    adm_end = ((q_lo + tq - 1) // CHUNK + 1) * CHUNK
    n_kb = jnp.minimum((adm_end + tk - 1) // tk, n_kb_max)

    lane = lax.broadcasted_iota(I32, (tq, LANES), 1)
    lo_half = lane < HEAD_DIM
    row_pos = q_lo + lax.broadcasted_iota(I32, (tq, tk), 0)
    col_iota = lax.broadcasted_iota(I32, (tq, tk), 1)

    def split_heads(x_pair):
        zero = jnp.zeros_like(x_pair)
        return jnp.where(lo_half, x_pair, zero), jnp.where(lo_half, zero, x_pair)

    iw = iw_ref[0]

    def score_block(j, carry):
        ki = ki_ref[0, pl.ds(pl.multiple_of(j * tk, tk), tk), :]
        score = jnp.zeros((tq, tk), F32)
        for hp in range(N_PAIRS):
            pair = iq_ref[0, :, hp * LANES:(hp + 1) * LANES]
            for e, qe in enumerate(split_heads(pair)):
                h = 2 * hp + e
                score = score + iw[:, h:h + 1] * jnp.maximum(_dot_nt(qe, ki), 0.0)
        kpos = j * tk + col_iota
        adm = (kpos // CHUNK) <= (row_pos // CHUNK)
        keys_ref[j] = jnp.where(adm, _sortable(score), INT_MIN)
        return carry

    lax.fori_loop(0, n_kb, score_block, 0)

    def count_rows(pred_fn):
        def body(j, acc):
            hit = pred_fn(keys_ref[j], j).astype(I32)
            part = hit[:, 0:LANES]
            for c in range(1, tk // LANES):
                part = part + hit[:, c * LANES:(c + 1) * LANES]
            return acc + part
        acc = lax.fori_loop(0, n_kb, body, jnp.zeros((tq, LANES), I32))
        return jnp.sum(acc, axis=1, keepdims=True)

    def bit_step(i, thr):
        cand = thr + jnp.left_shift(jnp.int32(1), 31 - i)
        cnt = count_rows(lambda kb, j: kb >= cand)
        return jnp.where(cnt >= n_sel, cand, thr)

    thr = lax.fori_loop(0, 32, bit_step, jnp.full((tq, 1), INT_MIN, I32))
    n_gt = count_rows(lambda kb, j: kb > thr)
    n_ge = count_rows(lambda kb, j: kb >= thr)
    need = n_sel - n_gt
    real = thr > INT_MIN
    tie = jnp.max(jnp.where(real & (n_ge > n_sel), 1, 0)) > 0

    def tie_cut():
        def idx_step(i, cut):
            cand = cut + jnp.left_shift(jnp.int32(1), idx_bits - 1 - i)
            cnt = count_rows(lambda kb, j: (kb == thr) & ((j * tk + col_iota) < cand))
            return jnp.where(cnt < need, cand, cut)
        return lax.fori_loop(0, idx_bits, idx_step, jnp.zeros((tq, 1), I32))

    cut = lax.cond(tie, tie_cut, lambda: jnp.full((tq, 1), n_kb_max * tk, I32))

    def bias_block(j, carry):
        kb = keys_ref[j]
        kpos = j * tk + col_iota
        sel = (kb > thr) | ((kb == thr) & real & (kpos <= cut))
        bias_ref[j] = jnp.where(sel, 0.0, NEG_BIG)
        return carry

    lax.fori_loop(0, n_kb, bias_block, 0)

    m_ref[...] = jnp.full_like(m_ref, NEG_BIG)
    l_ref[...] = jnp.zeros_like(l_ref)
    acc_ref[...] = jnp.zeros_like(acc_ref)

    def attn_block(j, carry):
        start = pl.multiple_of(j * tk, tk)
        bias = bias_ref[j]
        dist = jnp.abs(row_pos - (j * tk + col_iota)).astype(F32)
        for hp in range(N_PAIRS):
            cols = slice(hp * LANES, (hp + 1) * LANES)
            kp = k_ref[0, pl.ds(start, tk), cols]
            vp = v_ref[0, pl.ds(start, tk), cols]
            for e, qe in enumerate(split_heads(q_ref[0, :, cols])):
                h = 2 * hp + e
                slope = 2.0 ** (-8.0 * (h + 1) / N_HEADS)
                s = _dot_nt(qe, kp) - slope * dist + bias
                m_old = m_ref[h]
                m_new = jnp.maximum(m_old, jnp.max(s, axis=1, keepdims=True))
                alpha = jnp.exp(m_old - m_new)
                p = jnp.exp(s - m_new[:, 0:1])
                l_ref[h] = alpha * l_ref[h] + jnp.sum(p, axis=1, keepdims=True)
                acc_ref[h] = alpha * acc_ref[h] + _dot(p.astype(BF16), vp)
                m_ref[h] = m_new
        return carry

    lax.fori_loop(0, n_kb, attn_block, 0)

    for hp in range(N_PAIRS):
        o_lo = acc_ref[2 * hp] / l_ref[2 * hp]
        o_hi = acc_ref[2 * hp + 1] / l_ref[2 * hp + 1]
        o_ref[0, :, hp * LANES:(hp + 1) * LANES] = jnp.where(lo_half, o_lo, o_hi).astype(o_ref.dtype)


def _dsa_attention(q, iq, iw, k, v, ki2, tq, tk, q_off, n_sel):
    b, t, _ = q.shape
    s_len = k.shape[1]
    n_kb_max = s_len // tk
    qspec = lambda w: pl.BlockSpec((1, tq, w), lambda bi, i: (bi, i, 0))
    kspec = lambda w: pl.BlockSpec((1, s_len, w), lambda bi, i: (bi, 0, 0), pipeline_mode=pl.Buffered(1))
    return pl.pallas_call(
        functools.partial(_dsa_kernel, tq=tq, tk=tk, q_off=q_off, n_kb_max=n_kb_max, n_sel=n_sel,
                          idx_bits=max(1, int(math.ceil(math.log2(s_len + 1))))),
        out_shape=jax.ShapeDtypeStruct((b, t, WIDTH), BF16),
        grid=(b, t // tq),
        in_specs=[qspec(WIDTH), qspec(WIDTH), qspec(N_HEADS), kspec(WIDTH), kspec(WIDTH), kspec(LANES)],
        out_specs=qspec(WIDTH),
        scratch_shapes=[pltpu.VMEM((n_kb_max, tq, tk), I32),
                        pltpu.VMEM((n_kb_max, tq, tk), F32),
                        pltpu.VMEM((N_HEADS, tq, LANES), F32),
                        pltpu.VMEM((N_HEADS, tq, LANES), F32),
                        pltpu.VMEM((N_HEADS, tq, LANES), F32)],
        compiler_params=_params(("parallel", "arbitrary")),
        name="dsa_attn",
    )(q, iq, iw, k, v, ki2)


def _mix_kernel(oa_ref, ob_ref, ga_ref, gb_ref, x_ref, mod_ref, gpost_ref, gpre_ref,
                wa_ref, wb_ref, wo_ref, wq_ref, x1_ref, h2_ref, pq_ref):
    mod = mod_ref[0]
    mixed = ga_ref[...] * _dot(oa_ref[...], wa_ref[...]) + gb_ref[...] * _dot(ob_ref[...], wb_ref[...])
    y = _dot(mixed.astype(BF16), wo_ref[...])
    x1 = x_ref[...] + mod[2:3, :] * _rms(y, gpost_ref[...])
    x1_ref[...] = x1
    h2 = _rms(x1, gpre_ref[...]) * (1.0 + mod[4:5, :]) + mod[3:4, :]
    h2_ref[...] = h2
    pq_ref[...] = _dot(h2.astype(BF16), wq_ref[...]).astype(BF16)


def _mix(oa, ob, ga, gb, x2, mod3, g_post, g_pre, wa, wb, wo, wq, tm, tiles_per_seq):
    n = x2.shape[0]
    row = lambda i: (i, 0)
    const = lambda i: (0, 0)
    nq = wq.shape[1]
    wspec = lambda r, c: pl.BlockSpec((r, c), const, pipeline_mode=pl.Buffered(1))
    return pl.pallas_call(
        _mix_kernel,
        out_shape=[jax.ShapeDtypeStruct((n, D_MODEL), F32),
                   jax.ShapeDtypeStruct((n, D_MODEL), F32),
                   jax.ShapeDtypeStruct((n, nq), BF16)],
        grid=(n // tm,),
        in_specs=[pl.BlockSpec((tm, WIDTH), row), pl.BlockSpec((tm, WIDTH), row),
                  pl.BlockSpec((tm, D_MODEL), row), pl.BlockSpec((tm, D_MODEL), row),
                  pl.BlockSpec((tm, D_MODEL), row),
                  pl.BlockSpec((1, 6, D_MODEL), lambda i: (i // tiles_per_seq, 0, 0)),
                  pl.BlockSpec((1, D_MODEL), const), pl.BlockSpec((1, D_MODEL), const),
                  wspec(WIDTH, D_MODEL), wspec(WIDTH, D_MODEL), wspec(D_MODEL, D_MODEL),
                  wspec(D_MODEL, nq)],
        out_specs=[pl.BlockSpec((tm, D_MODEL), row), pl.BlockSpec((tm, D_MODEL), row),
                   pl.BlockSpec((tm, nq), row)],
        compiler_params=_params(("parallel",)),
        name="mix",
    )(oa, ob, ga, gb, x2, mod3, g_post, g_pre, wa, wb, wo, wq)


def _top16(s, payload=None):
    n = s.shape[0]
    pos = lax.broadcasted_iota(I32, s.shape, 0)
    vals, picks = [], []
    for _ in range(PEER_TOPK):
        m = jnp.max(s, axis=0, keepdims=True)
        p = jnp.min(jnp.where(s == m, pos, n), axis=0, keepdims=True)
        hit = pos == p
        vals.append(m)
        picks.append(p if payload is None else jnp.sum(jnp.where(hit, payload, 0), axis=0, keepdims=True))
        s = jnp.where(hit, -jnp.inf, s)
    return jnp.concatenate(vals, axis=0), jnp.concatenate(picks, axis=0)


def _peersel_kernel(pq_ref, sk_ref, idx_ref, g_ref):
    pq = pq_ref[...]
    s1 = _dot_nt(sk_ref[0, 0], pq[:, :PEER_DHALF])
    s2 = _dot_nt(sk_ref[0, 1], pq[:, PEER_DHALF:])
    v1, i1 = _top16(s1)
    v2, i2 = _top16(s2)
    cand = jnp.concatenate([v1[i:i + 1] + v2 for i in range(PEER_TOPK)], axis=0)
    cidx = jnp.concatenate([i1[i:i + 1] * PEER_NKEYS + i2 for i in range(PEER_TOPK)], axis=0)
    top, eidx = _top16(cand, cidx)
    ex = jnp.exp(top - top[0:1])
    g_ref[0] = ex / jnp.sum(ex, axis=0, keepdims=True)
    idx_ref[0] = eidx


def _peer_select(pq, subkeys_b, tt):
    n = pq.shape[0]
    return pl.pallas_call(
        _peersel_kernel,
        out_shape=[jax.ShapeDtypeStruct((PEER_HEADS, PEER_TOPK, n), I32),
                   jax.ShapeDtypeStruct((PEER_HEADS, PEER_TOPK, n), F32)],
        grid=(n // tt, PEER_HEADS),
        in_specs=[pl.BlockSpec((tt, 2 * PEER_DHALF), lambda i, h: (i, h)),
                  pl.BlockSpec((1, 2, PEER_NKEYS, PEER_DHALF), lambda i, h: (h, 0, 0, 0))],
        out_specs=[pl.BlockSpec((1, PEER_TOPK, tt), lambda i, h: (h, 0, i)),
                   pl.BlockSpec((1, PEER_TOPK, tt), lambda i, h: (h, 0, i))],
        compiler_params=_params(("parallel", "arbitrary")),
        name="peer_select",
    )(pq, subkeys_b)


def _erf_gelu(a):
    return 0.5 * a * (1.0 + lax.erf(a * (2.0 ** -0.5)))


def _peerffn_kernel(idx_ref, gt_ref, h2_ref, x1_ref, mod_ref, gpost_ref, uv_hbm, o_ref,
                    *scratch, tg, seqs_per_tile):
    bufs, (f_ref, sem) = scratch[:PEER_SLOTS], scratch[PEER_SLOTS:]

    def start_token(t, slot):
        tt = jnp.minimum(t, tg - 1)
        for k in range(PEER_SEL):
            pltpu.make_async_copy(uv_hbm.at[pl.ds(idx_ref[tt, k], 1)], bufs[slot].at[pl.ds(k, 1)],
                                  sem.at[slot]).start(priority=k % 2)

    def wait_token(slot):
        pltpu.make_async_copy(uv_hbm.at[pl.ds(0, PEER_SEL)], bufs[slot], sem.at[slot]).wait()

    lane_tok = lax.broadcasted_iota(I32, (PEER_SEL, tg), 1)
    sub8 = lax.broadcasted_iota(I32, (8, D_MODEL), 0)

    def compute(t, slot):
        buf = bufs[slot]
        t8 = pl.multiple_of((t // 8) * 8, 8)
        h8 = h2_ref[pl.ds(t8, 8), :]
        hrow = jnp.sum(jnp.where(sub8 == t - t8, h8, 0.0), axis=0, keepdims=True)
        acc = None
        for c in range(N_CHUNK):
            term = buf[:, c * LANES:(c + 1) * LANES] * hrow[:, c * LANES:(c + 1) * LANES]
            acc = term if acc is None else acc + term
        a = jnp.sum(acc, axis=1, keepdims=True)
        g = jnp.sum(jnp.where(lane_tok == t, gt_ref[...], 0.0), axis=1, keepdims=True)
        w = g * _erf_gelu(a)
        pieces = []
        for c in range(N_CHUNK):
            vc = buf[:, D_MODEL + c * LANES:D_MODEL + (c + 1) * LANES]
            pieces.append(jnp.sum(vc * w, axis=0, keepdims=True))
        f_ref[pl.ds(t, 1), :] = jnp.concatenate(pieces, axis=1)

    for t in range(PEER_AHEAD):
        start_token(t, t)

    def group(i, carry):
        t0 = PEER_SLOTS * i
        for j in range(PEER_SLOTS):
            wait_token(j)
            start_token(t0 + j + PEER_AHEAD, (j + PEER_AHEAD) % PEER_SLOTS)
            compute(t0 + j, j)
        return carry

    lax.fori_loop(0, tg // PEER_SLOTS, group, 0)
    for t in range(PEER_AHEAD):
        wait_token(t)
    rows = tg // seqs_per_tile
    for s in range(seqs_per_tile):
        sl = slice(s * rows, (s + 1) * rows)
        o_ref[sl, :] = x1_ref[sl, :] + mod_ref[s][5:6, :] * _rms(f_ref[sl, :], gpost_ref[...])


def _peer_ffn(idx, gt, h2, x1, mod3, g_post, uv, tg, seq_len):
    n = h2.shape[0]
    row = lambda i: (i, 0)
    seqs_per_tile = max(1, tg // seq_len)
    tiles_per_seq = max(1, seq_len // tg)
    return pl.pallas_call(
        functools.partial(_peerffn_kernel, tg=tg, seqs_per_tile=seqs_per_tile),
        out_shape=jax.ShapeDtypeStruct((n, D_MODEL), F32),
        grid=(n // tg,),
        in_specs=[pl.BlockSpec((tg, PEER_SEL), row, memory_space=pltpu.SMEM),
                  pl.BlockSpec((PEER_SEL, tg), lambda i: (0, i)),
                  pl.BlockSpec((tg, D_MODEL), row),
                  pl.BlockSpec((tg, D_MODEL), row),
                  pl.BlockSpec((seqs_per_tile, 6, D_MODEL), lambda i: (i // tiles_per_seq, 0, 0)),
                  pl.BlockSpec((1, D_MODEL), lambda i: (0, 0)),
                  pl.BlockSpec(memory_space=pl.ANY)],
        out_specs=pl.BlockSpec((tg, D_MODEL), row),
        scratch_shapes=[pltpu.VMEM((PEER_SEL, 2 * D_MODEL), F32)] * PEER_SLOTS
                       + [pltpu.VMEM((tg, D_MODEL), F32), pltpu.SemaphoreType.DMA((PEER_SLOTS,))],
        compiler_params=_params(("arbitrary",)),
        name="peer_ffn",
    )(idx, gt, h2, x1, mod3, g_post, uv)


def _pad_keys(x, s_pad):
    return jnp.pad(x, ((0, 0), (0, s_pad - x.shape[1]), (0, 0)))


def _layer(x, mod3, past, weights, tiles):
    (g_pre_mix, g_post_mix, g_pre_ffn, g_post_ffn, w_in_b, w_gate_b, wa_b, wb_b, wo_b, wq_b,
     subkeys_b, peer_uv) = weights
    tm, sb_tq, sb_tk, dsa_tq, dsa_tk, tt, tg = tiles
    bsz, t, _ = x.shape
    n = bsz * t
    x2 = x.reshape(n, D_MODEL)
    (aq, ak, av, bq, bk, bv, iq, ki2, akf, avf, bkf, bvf, ikf, iw, ga, gb) = _inproj(
        x2, mod3, g_pre_mix, w_in_b, w_gate_b, tm, t // tm)
    seq = lambda a: a.reshape(bsz, t, a.shape[-1])
    if past is None:
        q_off = 0
        ka, va, kb, vb, ki = seq(ak), seq(av), seq(bk), seq(bv), seq(ki2)
        s_len = t
    else:
        p_sbk, p_sbv, p_dk, p_dv, p_ki = past
        q_off = p_sbk.shape[1]
        s_len = q_off + t
        flat = lambda c: c.reshape(bsz, q_off, WIDTH).astype(BF16)
        ka = jnp.concatenate([flat(p_sbk), seq(ak)], axis=1)
        va = jnp.concatenate([flat(p_sbv), seq(av)], axis=1)
        kb = jnp.concatenate([flat(p_dk), seq(bk)], axis=1)
        vb = jnp.concatenate([flat(p_dv), seq(bv)], axis=1)
        pk = p_ki.astype(BF16)
        ki = jnp.concatenate([jnp.concatenate([pk, pk], axis=-1), seq(ki2)], axis=1)
    s_pad = -(-s_len // max(sb_tk, dsa_tk)) * max(sb_tk, dsa_tk)
    ka, va, kb, vb, ki = (_pad_keys(a, s_pad) for a in (ka, va, kb, vb, ki))
    n_sel = min(TOPK_MAX, s_len // 4)

    oa = _sb_attention(seq(aq), ka, va, sb_tq, sb_tk, q_off)
    ob = _dsa_attention(seq(bq), seq(iq), seq(iw), kb, vb, ki, dsa_tq, dsa_tk, q_off, n_sel)

    x1, h2, pq = _mix(oa.reshape(n, WIDTH), ob.reshape(n, WIDTH), ga, gb, x2, mod3, g_post_mix, g_pre_ffn,
                      wa_b, wb_b, wo_b, wq_b, tm, t // tm)
    eidx, gate = _peer_select(pq, subkeys_b, tt)
    eidx = eidx.reshape(PEER_SEL, n).T
    y = _peer_ffn(eidx, gate.reshape(PEER_SEL, n), h2, x1, mod3, g_post_ffn, peer_uv, tg, t)
    new = tuple(a.reshape(1, bsz, t, N_HEADS, HEAD_DIM) for a in (akf, avf, bkf, bvf)) + (
        ikf.reshape(1, bsz, t, HEAD_DIM),)
    return y.reshape(bsz, t, D_MODEL), new


def _tiles(t):
    if t % 256 == 0:
        return (256, 256, 256, 256, 256, 128, 128)
    return (t, t, 128, t, 128, 128, 128)


def kernel(x_prompt, x_sample, c_prompt, c_sample, cache_sb_k, cache_sb_v, cache_dsa_k, cache_dsa_v, cache_dsa_kidx, w_ada, b_ada, g_pre_mix, g_post_mix, g_pre_ffn, g_post_ffn, w_in, w_gate, w_branch_a, w_branch_b, w_out, w_peer_q, peer_subkeys, peer_u, peer_v):
    assert w_ada.shape[0] == 1, "one layer"
    n_p, n_s = c_prompt.shape[0], c_sample.shape[0]
    c_all = jnp.concatenate([c_prompt, c_sample], axis=0)
    rows = -(-c_all.shape[0] // 8) * 8
    c_all = jnp.pad(c_all, ((0, rows - c_all.shape[0]), (0, 0)))
    mod = _modulation(c_all, w_ada[0], b_ada[0]).reshape(rows, 6, D_MODEL)
    w_in_b = jnp.pad(w_in[0], ((0, 0), (0, W_IN_PAD - W_IN_COLS))).astype(BF16)
    weights = (g_pre_mix, g_post_mix, g_pre_ffn, g_post_ffn, w_in_b, w_gate[0].astype(BF16),
               w_branch_a[0].astype(BF16), w_branch_b[0].astype(BF16), w_out[0].astype(BF16),
               w_peer_q[0].astype(BF16), peer_subkeys[0].astype(BF16),
               jnp.concatenate([peer_u[0], peer_v[0]], axis=1))
    yp, new_p = _layer(x_prompt, mod[:n_p], None, weights, _tiles(x_prompt.shape[1]))
    past = (cache_sb_k[0], cache_sb_v[0], cache_dsa_k[0], cache_dsa_v[0], cache_dsa_kidx[0])
    ys, new_s = _layer(x_sample, mod[n_p:n_p + n_s], past, weights, _tiles(x_sample.shape[1]))
    return (yp, ys) + new_p + new_s
```

```python
import functools
import math

import jax
import jax.numpy as jnp
import numpy as np
from jax import lax
from jax.experimental import pallas as pl
from jax.experimental.pallas import tpu as pltpu

F32 = jnp.float32
BF16 = jnp.bfloat16
I32 = jnp.int32

D_MODEL = 1024
CHUNK = 64
EPS = 1e-6
N_HEADS = 8
HEAD_DIM = 64
WIDTH = N_HEADS * HEAD_DIM
N_PAIRS = N_HEADS // 2
LANES = 128
N_CHUNK = D_MODEL // LANES
TOPK_MAX = 256
PEER_HEADS = 8
PEER_NKEYS = 128
PEER_DHALF = 128
PEER_TOPK = 16
PEER_SEL = PEER_HEADS * PEER_TOPK
PEER_SLOTS = 8
PEER_AHEAD = 6
W_IN_COLS = 7 * WIDTH + HEAD_DIM + N_HEADS
W_IN_PAD = 7 * WIDTH + LANES
INT_MIN = -(2 ** 31)
NEG_BIG = -1e30
DIST_MASKED = 1e30
LOG2E = 1.4426950408889634
VMEM_LIMIT = 56 * 1024 * 1024


def _params(sem):
    return pltpu.CompilerParams(dimension_semantics=sem, vmem_limit_bytes=VMEM_LIMIT)


def _dot(a, b):
    return jnp.dot(a, b, preferred_element_type=F32)


def _dot_nt(a, b):
    return lax.dot_general(a, b, (((1,), (1,)), ((), ())), preferred_element_type=F32)


def _rms(x, gain):
    return x * lax.rsqrt(jnp.mean(x * x, axis=-1, keepdims=True) + EPS) * gain


def _mod_kernel(c_ref, w_ref, b_ref, o_ref):
    c = c_ref[...]
    s = c * (1.0 / (1.0 + jnp.exp(-c)))
    o_ref[...] = jnp.dot(s, w_ref[...], preferred_element_type=F32,
                         precision=lax.Precision.HIGHEST) + b_ref[...]


def _modulation(c, w_ada, b_ada):
    n = c.shape[0]
    cols = w_ada.shape[1]
    tn = 1024
    return pl.pallas_call(
        _mod_kernel,
        out_shape=jax.ShapeDtypeStruct((n, cols), F32),
        grid=(cols // tn,),
        in_specs=[pl.BlockSpec((n, D_MODEL), lambda j: (0, 0)),
                  pl.BlockSpec((D_MODEL, tn), lambda j: (0, j)),
                  pl.BlockSpec((1, tn), lambda j: (0, j))],
        out_specs=pl.BlockSpec((n, tn), lambda j: (0, j)),
        compiler_params=_params(("arbitrary",)),
        name="mod",
    )(c, w_ada, b_ada.reshape(1, cols))


def _inproj_kernel(x_ref, mod_ref, g_ref, win_ref, wg_ref,
                   aq_ref, ak_ref, av_ref, bq_ref, bk_ref, bv_ref, iq_ref, ki2_ref,
                   akf_ref, avf_ref, bkf_ref, bvf_ref, ikf_ref, iw_ref, ga_ref, gb_ref):
    x = x_ref[...]
    mod = mod_ref[0]
    h = _rms(x, g_ref[...]) * (1.0 + mod[1:2, :]) + mod[0:1, :]
    hb = h.astype(BF16)
    qscale = HEAD_DIM ** -0.5

    def seg(i):
        return _dot(hb, win_ref[:, i * WIDTH:(i + 1) * WIDTH])

    aq_ref[...] = (seg(0) * (qscale * LOG2E)).astype(BF16)
    z = seg(1)
    akf_ref[...] = z
    ak_ref[...] = z.astype(BF16)
    z = seg(2)
    avf_ref[...] = z
    av_ref[...] = z.astype(BF16)
    bq_ref[...] = (seg(3) * (qscale * LOG2E)).astype(BF16)
    z = seg(4)
    bkf_ref[...] = z
    bk_ref[...] = z.astype(BF16)
    z = seg(5)
    bvf_ref[...] = z
    bv_ref[...] = z.astype(BF16)
    iq_ref[...] = (seg(6) * qscale).astype(BF16)
    tail = _dot(hb, win_ref[:, 7 * WIDTH:7 * WIDTH + LANES])
    ik = tail[:, :HEAD_DIM]
    ikf_ref[...] = ik
    ki2_ref[...] = jnp.concatenate([ik, ik], axis=-1).astype(BF16)
    iw_ref[...] = tail[:, HEAD_DIM:HEAD_DIM + N_HEADS] * (N_HEADS ** -0.5)
    zg = _dot(hb, wg_ref[...])
    gate = 1.0 / (1.0 + jnp.exp(-zg))
    ga_ref[...] = gate[:, :D_MODEL]
    gb_ref[...] = gate[:, D_MODEL:]


def _inproj(x2, mod3, g_pre, w_in_b, w_gate_b, tm, tiles_per_seq):
    n = x2.shape[0]
    row = lambda i: (i, 0)
    const = lambda i: (0, 0)
    bspec = lambda w: pl.BlockSpec((tm, w), row)
    outs = ([jax.ShapeDtypeStruct((n, WIDTH), BF16)] * 7
            + [jax.ShapeDtypeStruct((n, LANES), BF16)]
            + [jax.ShapeDtypeStruct((n, WIDTH), F32)] * 4
            + [jax.ShapeDtypeStruct((n, HEAD_DIM), F32),
               jax.ShapeDtypeStruct((n, N_HEADS), F32),
               jax.ShapeDtypeStruct((n, D_MODEL), F32),
               jax.ShapeDtypeStruct((n, D_MODEL), F32)])
    out_specs = ([bspec(WIDTH)] * 7 + [bspec(LANES)] + [bspec(WIDTH)] * 4
                 + [bspec(HEAD_DIM), bspec(N_HEADS), bspec(D_MODEL), bspec(D_MODEL)])
    return pl.pallas_call(
        _inproj_kernel,
        out_shape=outs,
        grid=(n // tm,),
        in_specs=[pl.BlockSpec((tm, D_MODEL), row),
                  pl.BlockSpec((1, 6, D_MODEL), lambda i: (i // tiles_per_seq, 0, 0)),
                  pl.BlockSpec((1, D_MODEL), const),
                  pl.BlockSpec((D_MODEL, W_IN_PAD), const, pipeline_mode=pl.Buffered(1)),
                  pl.BlockSpec((D_MODEL, 2 * D_MODEL), const, pipeline_mode=pl.Buffered(1))],
        out_specs=out_specs,
        compiler_params=_params(("parallel",)),
        name="inproj",
    )(x2, mod3, g_pre, w_in_b, w_gate_b)


def _sb_kernel(tab_ref, q_ref, k_ref, v_ref, tri_ref, o_ref, acc0, acc1, run0, run1, *, tq, tk, q_off):
    s = pl.program_id(2)
    qi = tab_ref[0, s]
    kj = tab_ref[1, s]
    first = tab_ref[2, s]
    last = tab_ref[3, s]
    masked = tab_ref[4, s]

    acc_refs, run_refs = (acc0, acc1), (run0, run1)

    @pl.when(first == 1)
    def _():
        for r in acc_refs + run_refs:
            r[...] = jnp.zeros_like(r)

    lane = lax.broadcasted_iota(I32, (tq, LANES), 1)

    def step(use_mask):
        q = q_ref[0]
        k = k_ref[0]
        v = v_ref[0]
        tri = tri_ref[...]
        if use_mask:
            qpos = q_off + qi * tq + lax.broadcasted_iota(I32, (tq, tk), 0)
            kpos = kj * tk + lax.broadcasted_iota(I32, (tq, tk), 1)
            causal = kpos < qpos
        zero = jnp.zeros_like(q)
        z = [_dot_nt(jnp.where(lane < HEAD_DIM, q, zero), k), _dot_nt(jnp.where(lane < HEAD_DIM, zero, q), k)]
        suf = [None, None]
        for e in range(2):
            sp = jnp.maximum(z[e], 0.0) + jnp.log2(1.0 + jnp.exp2(-jnp.abs(z[e])))
            if use_mask:
                sp = jnp.where(causal, sp, 0.0)
            hi = sp.astype(BF16)
            lo = (sp - hi.astype(F32)).astype(BF16)
            suf[e] = _dot(hi, tri) + _dot(lo, tri) + run_refs[e][:, 0:1]
        for e in range(2):
            w = jnp.exp2(z[e] - suf[e])
            if use_mask:
                w = jnp.where(causal, w, 0.0)
            acc_refs[e][...] += _dot(w.astype(BF16), v)
            run_refs[e][...] = jnp.broadcast_to(suf[e][:, 0:1], (tq, LANES))

    @pl.when(masked == 1)
    def _():
        step(True)

    @pl.when(masked == 0)
    def _():
        step(False)

    @pl.when(last == 1)
    def _():
        o_ref[0] = jnp.where(lane < HEAD_DIM, acc0[...], acc1[...]).astype(o_ref.dtype)


def _sb_table(nq, nk, tq, tk, q_off):
    rows = []
    for qi in range(nq):
        q_lo = q_off + qi * tq
        q_hi = q_lo + tq - 1
        kjs = [kj for kj in range(nk) if kj * tk < q_hi]
        if not kjs:
            kjs = [0]
        kjs = kjs[::-1]
        for n, kj in enumerate(kjs):
            masked = int(kj * tk + tk - 1 >= q_lo)
            rows.append((qi, kj, int(n == 0), int(n == len(kjs) - 1), masked))
    return np.asarray(rows, dtype=np.int32).T.copy()


def _sb_attention(q, k, v, tq, tk, q_off):
    b, t, _ = q.shape
    s_len = k.shape[1]
    tab = _sb_table(t // tq, s_len // tk, tq, tk, q_off)
    tri = jnp.asarray(np.tril(np.ones((tk, tk), np.float32)), dtype=BF16)
    grid_spec = pltpu.PrefetchScalarGridSpec(
        num_scalar_prefetch=1,
        grid=(b, N_PAIRS, tab.shape[1]),
        in_specs=[pl.BlockSpec((1, tq, LANES), lambda bi, hp, s, tab: (bi, tab[0, s], hp)),
                  pl.BlockSpec((1, tk, LANES), lambda bi, hp, s, tab: (bi, tab[1, s], hp)),
                  pl.BlockSpec((1, tk, LANES), lambda bi, hp, s, tab: (bi, tab[1, s], hp)),
                  pl.BlockSpec((tk, tk), lambda bi, hp, s, tab: (0, 0))],
        out_specs=pl.BlockSpec((1, tq, LANES), lambda bi, hp, s, tab: (bi, tab[0, s], hp)),
        scratch_shapes=[pltpu.VMEM((tq, LANES), F32)] * 4,
    )
    return pl.pallas_call(
        functools.partial(_sb_kernel, tq=tq, tk=tk, q_off=q_off),
        out_shape=jax.ShapeDtypeStruct((b, t, WIDTH), BF16),
        grid_spec=grid_spec,
        compiler_params=_params(("parallel", "parallel", "arbitrary")),
        name="sb_attn",
    )(jnp.asarray(tab), q, k, v, tri)


def _sortable(x):
    bits = pltpu.bitcast(x + 0.0, I32)
    return jnp.where(bits < 0, bits ^ 0x7FFFFFFF, bits)


def _alibi_slope2(h):
    return LOG2E * 2.0 ** (-8.0 * (h + 1) / N_HEADS)


def _dsa_kernel(q_ref, iq_ref, iw_ref, k_ref, v_ref, ki_ref, o_ref,
                keys_ref, bias_ref, m_ref, l_ref, acc_ref,
                *, tq, tk, q_off, n_kb_max, n_sel, idx_bits):
    qi = pl.program_id(1)
    q_lo = q_off + qi * tq
    adm_end = ((q_lo + tq - 1) // CHUNK + 1) * CHUNK
    n_kb = jnp.minimum((adm_end + tk - 1) // tk, n_kb_max)

    lane = lax.broadcasted_iota(I32, (tq, LANES), 1)
    lo_half = lane < HEAD_DIM
    row_pos = q_lo + lax.broadcasted_iota(I32, (tq, tk), 0)
    col_iota = lax.broadcasted_iota(I32, (tq, tk), 1)

    def split_heads(x_pair):
        zero = jnp.zeros_like(x_pair)
        return jnp.where(lo_half, x_pair, zero), jnp.where(lo_half, zero, x_pair)

    iw = iw_ref[0]

    def score_block(j, carry):
        ki = ki_ref[0, pl.ds(pl.multiple_of(j * tk, tk), tk), :]
        score = jnp.zeros((tq, tk), F32)
        for hp in range(N_PAIRS):
            pair = iq_ref[0, :, hp * LANES:(hp + 1) * LANES]
            for e, qe in enumerate(split_heads(pair)):
                h = 2 * hp + e
                score = score + iw[:, h:h + 1] * jnp.maximum(_dot_nt(qe, ki), 0.0)
        kpos = j * tk + col_iota
        adm = (kpos // CHUNK) <= (row_pos // CHUNK)
        keys_ref[j] = jnp.where(adm, _sortable(score), INT_MIN)
        return carry

    lax.fori_loop(0, n_kb, score_block, 0)

    def count_rows(pred_fn):
        def body(j, acc):
            hit = pred_fn(keys_ref[j], j).astype(I32)
            part = hit[:, 0:LANES]
            for c in range(1, tk // LANES):
                part = part + hit[:, c * LANES:(c + 1) * LANES]
            return acc + part
        acc = lax.fori_loop(0, n_kb, body, jnp.zeros((tq, LANES), I32))
        return jnp.sum(acc, axis=1, keepdims=True)

    def bit_step(i, thr):
        cand = thr + jnp.left_shift(jnp.int32(1), 31 - i)
        cnt = count_rows(lambda kb, j: kb >= cand)
        return jnp.where(cnt >= n_sel, cand, thr)

    thr = lax.fori_loop(0, 32, bit_step, jnp.full((tq, 1), INT_MIN, I32))
    n_gt = count_rows(lambda kb, j: kb > thr)
    n_ge = count_rows(lambda kb, j: kb >= thr)
    need = n_sel - n_gt
    real = thr > INT_MIN
    tie = jnp.max(jnp.where(real & (n_ge > n_sel), 1, 0)) > 0

    def tie_cut():
        def idx_step(i, cut):
            cand = cut + jnp.left_shift(jnp.int32(1), idx_bits - 1 - i)
            cnt = count_rows(lambda kb, j: (kb == thr) & ((j * tk + col_iota) < cand))
            return jnp.where(cnt < need, cand, cut)
        return lax.fori_loop(0, idx_bits, idx_step, jnp.zeros((tq, 1), I32))

    cut = lax.cond(tie, tie_cut, lambda: jnp.full((tq, 1), n_kb_max * tk, I32))

    def bias_block(j, carry):
        kb = keys_ref[j]
        kpos = j * tk + col_iota
        sel = (kb > thr) | ((kb == thr) & real & (kpos <= cut))
        bias_ref[j] = jnp.where(sel, 0.0, NEG_BIG)
        return carry

    lax.fori_loop(0, n_kb, bias_block, 0)

    m_ref[...] = jnp.full_like(m_ref, NEG_BIG)
    l_ref[...] = jnp.zeros_like(l_ref)
    acc_ref[...] = jnp.zeros_like(acc_ref)

    def attn_block(j, carry):
        start = pl.multiple_of(j * tk, tk)
        bias = bias_ref[j]
        dist = jnp.abs(row_pos - (j * tk + col_iota)).astype(F32)
        for hp in range(N_PAIRS):
            cols = slice(hp * LANES, (hp + 1) * LANES)
            kp = k_ref[0, pl.ds(start, tk), cols]
            vp = v_ref[0, pl.ds(start, tk), cols]
            for e, qe in enumerate(split_heads(q_ref[0, :, cols])):
                h = 2 * hp + e
                s = _dot_nt(qe, kp) - _alibi_slope2(h) * dist + bias
                m_old = m_ref[h]
                m_new = jnp.maximum(m_old, jnp.max(s, axis=1, keepdims=True))
                alpha = jnp.exp2(m_old - m_new)
                p = jnp.exp2(s - m_new[:, 0:1])
                l_ref[h] = alpha * l_ref[h] + jnp.sum(p, axis=1, keepdims=True)
                acc_ref[h] = alpha * acc_ref[h] + _dot(p.astype(BF16), vp)
                m_ref[h] = m_new
        return carry

    lax.fori_loop(0, n_kb, attn_block, 0)

    for hp in range(N_PAIRS):
        o_lo = acc_ref[2 * hp] / l_ref[2 * hp]
        o_hi = acc_ref[2 * hp + 1] / l_ref[2 * hp + 1]
        o_ref[0, :, hp * LANES:(hp + 1) * LANES] = jnp.where(lo_half, o_lo, o_hi).astype(o_ref.dtype)


def _dsa_attention(q, iq, iw, k, v, ki2, tq, tk, q_off, n_sel):
    b, t, _ = q.shape
    s_len = k.shape[1]
    n_kb_max = s_len // tk
    qspec = lambda w: pl.BlockSpec((1, tq, w), lambda bi, i: (bi, i, 0))
    kspec = lambda w: pl.BlockSpec((1, s_len, w), lambda bi, i: (bi, 0, 0), pipeline_mode=pl.Buffered(1))
    return pl.pallas_call(
        functools.partial(_dsa_kernel, tq=tq, tk=tk, q_off=q_off, n_kb_max=n_kb_max, n_sel=n_sel,
                          idx_bits=max(1, int(math.ceil(math.log2(s_len + 1))))),
        out_shape=jax.ShapeDtypeStruct((b, t, WIDTH), BF16),
        grid=(b, t // tq),
        in_specs=[qspec(WIDTH), qspec(WIDTH), qspec(N_HEADS), kspec(WIDTH), kspec(WIDTH), kspec(LANES)],
        out_specs=qspec(WIDTH),
        scratch_shapes=[pltpu.VMEM((n_kb_max, tq, tk), I32),
                        pltpu.VMEM((n_kb_max, tq, tk), F32),
                        pltpu.VMEM((N_HEADS, tq, LANES), F32),
                        pltpu.VMEM((N_HEADS, tq, LANES), F32),
                        pltpu.VMEM((N_HEADS, tq, LANES), F32)],
        compiler_params=_params(("parallel", "arbitrary")),
        name="dsa_attn",
    )(q, iq, iw, k, v, ki2)


def _dsat_kernel(q_ref, iq_ref, iwt_ref, k_ref, vt_ref, ki_ref, o_ref, keys_ref, dm_ref, qt_ref, iqt_ref, *stats,
                 tq, tk, q_off, n_kb_max, n_sel, idx_bits):
    m_refs, l_refs, acc_refs = stats[:N_HEADS], stats[N_HEADS:2 * N_HEADS], stats[2 * N_HEADS:]
    qi = pl.program_id(1)
    q_lo = q_off + qi * tq
    adm_end = ((q_lo + tq - 1) // CHUNK + 1) * CHUNK
    n_kb = jnp.minimum((adm_end + tk - 1) // tk, n_kb_max)

    lo_half = lax.broadcasted_iota(I32, (tq, LANES), 1) < HEAD_DIM
    qpos = q_lo + lax.broadcasted_iota(I32, (tk, tq), 1)
    krow = lax.broadcasted_iota(I32, (tk, tq), 0)

    def heads_of(x_ref):
        out = []
        for hp in range(N_PAIRS):
            pair = x_ref[0, :, hp * LANES:(hp + 1) * LANES]
            zero = jnp.zeros_like(pair)
            out += [jnp.where(lo_half, pair, zero), jnp.where(lo_half, zero, pair)]
        return out

    for src, dst in ((iq_ref, iqt_ref), (q_ref, qt_ref)):
        for h, xe in enumerate(heads_of(src)):
            dst[h] = xe.astype(F32).T.astype(BF16)

    def score_block(j, carry):
        ki = ki_ref[0, pl.ds(pl.multiple_of(j * tk, tk), tk), :]
        score = jnp.zeros((tk, tq), F32)
        for h in range(N_HEADS):
            score = score + iwt_ref[0, h:h + 1, :] * jnp.maximum(_dot(ki, iqt_ref[h]), 0.0)
        adm = ((j * tk + krow) // CHUNK) <= (qpos // CHUNK)
        keys_ref[j] = jnp.where(adm, _sortable(score), INT_MIN)
        return carry

    lax.fori_loop(0, n_kb, score_block, 0)

    def count_cols(pred_fn):
        def body(j, acc):
            hit = pred_fn(keys_ref[j], j).astype(I32)
            return acc + jnp.sum(hit.reshape(tk // 8, 8, tq), axis=0)
        acc = lax.fori_loop(0, n_kb, body, jnp.zeros((8, tq), I32))
        return jnp.sum(acc, axis=0, keepdims=True)

    def bit_step(i, thr):
        cand = thr + jnp.left_shift(jnp.int32(1), 31 - i)
        cnt = count_cols(lambda kb, j: kb >= cand)
        return jnp.where(cnt >= n_sel, cand, thr)

    thr = lax.fori_loop(0, 32, bit_step, jnp.full((1, tq), INT_MIN, I32))
    n_gt = count_cols(lambda kb, j: kb > thr)
    n_ge = count_cols(lambda kb, j: kb >= thr)
    need = n_sel - n_gt
    real = thr > INT_MIN
    tie = jnp.max(jnp.where(real & (n_ge > n_sel), 1, 0)) > 0

    def tie_cut():
        def idx_step(i, cut):
            cand = cut + jnp.left_shift(jnp.int32(1), idx_bits - 1 - i)
            cnt = count_cols(lambda kb, j: (kb == thr) & ((j * tk + krow) < cand))
            return jnp.where(cnt < need, cand, cut)
        return lax.fori_loop(0, idx_bits, idx_step, jnp.zeros((1, tq), I32))

    cut = lax.cond(tie, tie_cut, lambda: jnp.full((1, tq), n_kb_max * tk, I32))

    def mask_block(j, carry):
        kb = keys_ref[j]
        kpos = j * tk + krow
        sel = (kb > thr) | ((kb == thr) & real & (kpos <= cut))
        dm_ref[j] = jnp.where(sel, jnp.abs(qpos - kpos).astype(F32), DIST_MASKED)
        return carry

    lax.fori_loop(0, n_kb, mask_block, 0)

    for h in range(N_HEADS):
        m_refs[h][...] = jnp.full_like(m_refs[h], NEG_BIG)
        l_refs[h][...] = jnp.zeros_like(l_refs[h])
        acc_refs[h][...] = jnp.zeros_like(acc_refs[h])

    def attn_block(j, carry):
        start = pl.multiple_of(j * tk, tk)
        dm = dm_ref[j]

        def qk(h):
            return _dot(k_ref[0, pl.ds(start, tk), (h // 2) * LANES:(h // 2 + 1) * LANES], qt_ref[h])

        ahead = 2
        pending = [qk(h) for h in range(ahead)]
        for hp in range(N_PAIRS):
            for e in range(2):
                h = 2 * hp + e
                s_this = pending.pop(0)
                if h + ahead < N_HEADS:
                    pending.append(qk(h + ahead))
                t2 = s_this - _alibi_slope2(h) * dm
                m_old = m_refs[h][...]
                m_new = jnp.maximum(m_old, jnp.max(t2, axis=0, keepdims=True))
                alpha = jnp.exp2(m_old - m_new)
                p = jnp.exp2(t2 - m_new)
                l_refs[h][...] = alpha * l_refs[h][...] + jnp.sum(p, axis=0, keepdims=True)
                vt = vt_ref[0, j, hp * LANES + e * HEAD_DIM:hp * LANES + (e + 1) * HEAD_DIM, :]
                acc_refs[h][...] = alpha * acc_refs[h][...] + _dot(vt, p.astype(BF16))
                m_refs[h][...] = m_new
        return carry

    lax.fori_loop(0, n_kb, attn_block, 0)

    for hp in range(N_PAIRS):
        ot = jnp.concatenate([acc_refs[2 * hp][...] / l_refs[2 * hp][...],
                              acc_refs[2 * hp + 1][...] / l_refs[2 * hp + 1][...]], axis=0)
        o_ref[0, :, hp * LANES:(hp + 1) * LANES] = ot.T.astype(o_ref.dtype)


def _dsa_attention_t(q, iq, iw, k, v, ki2, tq, tk, q_off, n_sel):
    b, t, _ = q.shape
    s_len = k.shape[1]
    n_kb_max = s_len // tk
    iwt = jnp.swapaxes(iw, 1, 2)
    vt = jnp.swapaxes(v.reshape(b, n_kb_max, tk, WIDTH), 2, 3)
    qspec = lambda w: pl.BlockSpec((1, tq, w), lambda bi, i: (bi, i, 0))
    kspec = lambda w: pl.BlockSpec((1, s_len, w), lambda bi, i: (bi, 0, 0), pipeline_mode=pl.Buffered(1))
    return pl.pallas_call(
        functools.partial(_dsat_kernel, tq=tq, tk=tk, q_off=q_off, n_kb_max=n_kb_max, n_sel=n_sel,
                          idx_bits=max(1, int(math.ceil(math.log2(s_len + 1))))),
        out_shape=jax.ShapeDtypeStruct((b, t, WIDTH), BF16),
        grid=(b, t // tq),
        in_specs=[qspec(WIDTH), qspec(WIDTH),
                  pl.BlockSpec((1, N_HEADS, tq), lambda bi, i: (bi, 0, i)),
                  kspec(WIDTH),
                  pl.BlockSpec((1, n_kb_max, WIDTH, tk), lambda bi, i: (bi, 0, 0, 0), pipeline_mode=pl.Buffered(1)),
                  kspec(LANES)],
        out_specs=qspec(WIDTH),
        scratch_shapes=[pltpu.VMEM((n_kb_max, tk, tq), I32), pltpu.VMEM((n_kb_max, tk, tq), F32),
                        pltpu.VMEM((N_HEADS, LANES, tq), BF16), pltpu.VMEM((N_HEADS, LANES, tq), BF16)]
                       + [pltpu.VMEM((1, tq), F32)] * (2 * N_HEADS) + [pltpu.VMEM((HEAD_DIM, tq), F32)] * N_HEADS,
        compiler_params=_params(("parallel", "arbitrary")),
        name="dsa_attn_t",
    )(q, iq, iwt, k, vt, ki2)


def _mix_kernel(oa_ref, ob_ref, ga_ref, gb_ref, x_ref, mod_ref, gpost_ref, gpre_ref,
                wa_ref, wb_ref, wo_ref, wq_ref, x1_ref, h2_ref, pq_ref):
    mod = mod_ref[0]
    mixed = ga_ref[...] * _dot(oa_ref[...], wa_ref[...]) + gb_ref[...] * _dot(ob_ref[...], wb_ref[...])
    y = _dot(mixed.astype(BF16), wo_ref[...])
    x1 = x_ref[...] + mod[2:3, :] * _rms(y, gpost_ref[...])
    x1_ref[...] = x1
    h2 = _rms(x1, gpre_ref[...]) * (1.0 + mod[4:5, :]) + mod[3:4, :]
    h2_ref[...] = h2
    pq_ref[...] = _dot(h2.astype(BF16), wq_ref[...]).astype(BF16)


def _mix(oa, ob, ga, gb, x2, mod3, g_post, g_pre, wa, wb, wo, wq, tm, tiles_per_seq):
    n = x2.shape[0]
    row = lambda i: (i, 0)
    const = lambda i: (0, 0)
    nq = wq.shape[1]
    wspec = lambda r, c: pl.BlockSpec((r, c), const, pipeline_mode=pl.Buffered(1))
    return pl.pallas_call(
        _mix_kernel,
        out_shape=[jax.ShapeDtypeStruct((n, D_MODEL), F32),
                   jax.ShapeDtypeStruct((n, D_MODEL), F32),
                   jax.ShapeDtypeStruct((n, nq), BF16)],
        grid=(n // tm,),
        in_specs=[pl.BlockSpec((tm, WIDTH), row), pl.BlockSpec((tm, WIDTH), row),
                  pl.BlockSpec((tm, D_MODEL), row), pl.BlockSpec((tm, D_MODEL), row),
                  pl.BlockSpec((tm, D_MODEL), row),
                  pl.BlockSpec((1, 6, D_MODEL), lambda i: (i // tiles_per_seq, 0, 0)),
                  pl.BlockSpec((1, D_MODEL), const), pl.BlockSpec((1, D_MODEL), const),
                  wspec(WIDTH, D_MODEL), wspec(WIDTH, D_MODEL), wspec(D_MODEL, D_MODEL),
                  wspec(D_MODEL, nq)],
        out_specs=[pl.BlockSpec((tm, D_MODEL), row), pl.BlockSpec((tm, D_MODEL), row),
                   pl.BlockSpec((tm, nq), row)],
        compiler_params=_params(("parallel",)),
        name="mix",
    )(oa, ob, ga, gb, x2, mod3, g_post, g_pre, wa, wb, wo, wq)


def _top16(s, payload=None):
    n = s.shape[0]
    pos = lax.broadcasted_iota(I32, s.shape, 0)
    vals, picks = [], []
    for _ in range(PEER_TOPK):
        m = jnp.max(s, axis=0, keepdims=True)
        p = jnp.min(jnp.where(s == m, pos, n), axis=0, keepdims=True)
        hit = pos == p
        vals.append(m)
        picks.append(p if payload is None else jnp.sum(jnp.where(hit, payload, 0), axis=0, keepdims=True))
        s = jnp.where(hit, -jnp.inf, s)
    return jnp.concatenate(vals, axis=0), jnp.concatenate(picks, axis=0)


def _peersel_kernel(pq_ref, sk_ref, idx_ref, g_ref):
    pq = pq_ref[...]
    s1 = _dot_nt(sk_ref[0, 0], pq[:, :PEER_DHALF])
    s2 = _dot_nt(sk_ref[0, 1], pq[:, PEER_DHALF:])
    v1, i1 = _top16(s1)
    v2, i2 = _top16(s2)
    cand = jnp.concatenate([v1[i:i + 1] + v2 for i in range(PEER_TOPK)], axis=0)
    cidx = jnp.concatenate([i1[i:i + 1] * PEER_NKEYS + i2 for i in range(PEER_TOPK)], axis=0)
    top, eidx = _top16(cand, cidx)
    ex = jnp.exp(top - top[0:1])
    g_ref[0] = ex / jnp.sum(ex, axis=0, keepdims=True)
    idx_ref[0] = eidx


def _peer_select(pq, subkeys_b, tt):
    n = pq.shape[0]
    return pl.pallas_call(
        _peersel_kernel,
        out_shape=[jax.ShapeDtypeStruct((PEER_HEADS, PEER_TOPK, n), I32),
                   jax.ShapeDtypeStruct((PEER_HEADS, PEER_TOPK, n), F32)],
        grid=(n // tt, PEER_HEADS),
        in_specs=[pl.BlockSpec((tt, 2 * PEER_DHALF), lambda i, h: (i, h)),
                  pl.BlockSpec((1, 2, PEER_NKEYS, PEER_DHALF), lambda i, h: (h, 0, 0, 0))],
        out_specs=[pl.BlockSpec((1, PEER_TOPK, tt), lambda i, h: (h, 0, i)),
                   pl.BlockSpec((1, PEER_TOPK, tt), lambda i, h: (h, 0, i))],
        compiler_params=_params(("parallel", "arbitrary")),
        name="peer_select",
    )(pq, subkeys_b)


def _erf_gelu(a):
    return 0.5 * a * (1.0 + lax.erf(a * (2.0 ** -0.5)))


def _peerffn_kernel(idx_ref, gt_ref, h2_ref, x1_ref, mod_ref, gpost_ref, uv_hbm, o_ref,
                    *scratch, tg, seqs_per_tile):
    bufs, (f_ref, sem) = scratch[:PEER_SLOTS], scratch[PEER_SLOTS:]

    def start_token(t, slot):
        tt = jnp.minimum(t, tg - 1)
        for k in range(PEER_SEL):
            pltpu.make_async_copy(uv_hbm.at[pl.ds(idx_ref[tt, k], 1)], bufs[slot].at[pl.ds(k, 1)],
                                  sem.at[slot]).start(priority=k % 2)

    def wait_token(slot):
        pltpu.make_async_copy(uv_hbm.at[pl.ds(0, PEER_SEL)], bufs[slot], sem.at[slot]).wait()

    lane_tok = lax.broadcasted_iota(I32, (PEER_SEL, tg), 1)
    sub8 = lax.broadcasted_iota(I32, (8, D_MODEL), 0)

    def compute(t, slot):
        buf = bufs[slot]
        t8 = pl.multiple_of((t // 8) * 8, 8)
        h8 = h2_ref[pl.ds(t8, 8), :]
        hrow = jnp.sum(jnp.where(sub8 == t - t8, h8, 0.0), axis=0, keepdims=True)
        acc = None
        for c in range(N_CHUNK):
            term = buf[:, c * LANES:(c + 1) * LANES] * hrow[:, c * LANES:(c + 1) * LANES]
            acc = term if acc is None else acc + term
        a = jnp.sum(acc, axis=1, keepdims=True)
        g = jnp.sum(jnp.where(lane_tok == t, gt_ref[...], 0.0), axis=1, keepdims=True)
        w = g * _erf_gelu(a)
        pieces = []
        for c in range(N_CHUNK):
            vc = buf[:, D_MODEL + c * LANES:D_MODEL + (c + 1) * LANES]
            pieces.append(jnp.sum(vc * w, axis=0, keepdims=True))
        f_ref[pl.ds(t, 1), :] = jnp.concatenate(pieces, axis=1)

    for t in range(PEER_AHEAD):
        start_token(t, t)

    def group(i, carry):
        t0 = PEER_SLOTS * i
        for j in range(PEER_SLOTS):
            wait_token(j)
            start_token(t0 + j + PEER_AHEAD, (j + PEER_AHEAD) % PEER_SLOTS)
            compute(t0 + j, j)
        return carry

    lax.fori_loop(0, tg // PEER_SLOTS, group, 0)
    for t in range(PEER_AHEAD):
        wait_token(t)
    rows = tg // seqs_per_tile
    for s in range(seqs_per_tile):
        sl = slice(s * rows, (s + 1) * rows)
        o_ref[sl, :] = x1_ref[sl, :] + mod_ref[s][5:6, :] * _rms(f_ref[sl, :], gpost_ref[...])


def _peer_ffn(idx, gt, h2, x1, mod3, g_post, uv, tg, seq_len):
    n = h2.shape[0]
    row = lambda i: (i, 0)
    seqs_per_tile = max(1, tg // seq_len)
    tiles_per_seq = max(1, seq_len // tg)
    return pl.pallas_call(
        functools.partial(_peerffn_kernel, tg=tg, seqs_per_tile=seqs_per_tile),
        out_shape=jax.ShapeDtypeStruct((n, D_MODEL), F32),
        grid=(n // tg,),
        in_specs=[pl.BlockSpec((tg, PEER_SEL), row, memory_space=pltpu.SMEM),
                  pl.BlockSpec((PEER_SEL, tg), lambda i: (0, i)),
                  pl.BlockSpec((tg, D_MODEL), row),
                  pl.BlockSpec((tg, D_MODEL), row),
                  pl.BlockSpec((seqs_per_tile, 6, D_MODEL), lambda i: (i // tiles_per_seq, 0, 0)),
                  pl.BlockSpec((1, D_MODEL), lambda i: (0, 0)),
                  pl.BlockSpec(memory_space=pl.ANY)],
        out_specs=pl.BlockSpec((tg, D_MODEL), row),
        scratch_shapes=[pltpu.VMEM((PEER_SEL, 2 * D_MODEL), F32)] * PEER_SLOTS
                       + [pltpu.VMEM((tg, D_MODEL), F32), pltpu.SemaphoreType.DMA((PEER_SLOTS,))],
        compiler_params=_params(("arbitrary",)),
        name="peer_ffn",
    )(idx, gt, h2, x1, mod3, g_post, uv)


def _pad_keys(x, s_pad):
    return jnp.pad(x, ((0, 0), (0, s_pad - x.shape[1]), (0, 0)))


def _layer(x, mod3, past, weights, tiles):
    (g_pre_mix, g_post_mix, g_pre_ffn, g_post_ffn, w_in_b, w_gate_b, wa_b, wb_b, wo_b, wq_b,
     subkeys_b, peer_uv) = weights
    tm, sb_tq, sb_tk, dsa_tq, dsa_tk, tt, tg = tiles
    bsz, t, _ = x.shape
    n = bsz * t
    x2 = x.reshape(n, D_MODEL)
    (aq, ak, av, bq, bk, bv, iq, ki2, akf, avf, bkf, bvf, ikf, iw, ga, gb) = _inproj(
        x2, mod3, g_pre_mix, w_in_b, w_gate_b, tm, t // tm)
    seq = lambda a: a.reshape(bsz, t, a.shape[-1])
    if past is None:
        q_off = 0
        ka, va, kb, vb, ki = seq(ak), seq(av), seq(bk), seq(bv), seq(ki2)
        s_len = t
    else:
        p_sbk, p_sbv, p_dk, p_dv, p_ki = past
        q_off = p_sbk.shape[1]
        s_len = q_off + t
        flat = lambda c: c.reshape(bsz, q_off, WIDTH).astype(BF16)
        ka = jnp.concatenate([flat(p_sbk), seq(ak)], axis=1)
        va = jnp.concatenate([flat(p_sbv), seq(av)], axis=1)
        kb = jnp.concatenate([flat(p_dk), seq(bk)], axis=1)
        vb = jnp.concatenate([flat(p_dv), seq(bv)], axis=1)
        pk = p_ki.astype(BF16)
        ki = jnp.concatenate([jnp.concatenate([pk, pk], axis=-1), seq(ki2)], axis=1)
    s_pad = -(-s_len // max(sb_tk, dsa_tk)) * max(sb_tk, dsa_tk)
    ka, va, kb, vb, ki = (_pad_keys(a, s_pad) for a in (ka, va, kb, vb, ki))
    n_sel = min(TOPK_MAX, s_len // 4)

    oa = _sb_attention(seq(aq), ka, va, sb_tq, sb_tk, q_off)
    dsa = _dsa_attention_t if dsa_tq % LANES == 0 else _dsa_attention
    ob = dsa(seq(bq), seq(iq), seq(iw), kb, vb, ki, dsa_tq, dsa_tk, q_off, n_sel)

    x1, h2, pq = _mix(oa.reshape(n, WIDTH), ob.reshape(n, WIDTH), ga, gb, x2, mod3, g_post_mix, g_pre_ffn,
                      wa_b, wb_b, wo_b, wq_b, tm, t // tm)
    eidx, gate = _peer_select(pq, subkeys_b, tt)
    eidx = eidx.reshape(PEER_SEL, n).T
    y = _peer_ffn(eidx, gate.reshape(PEER_SEL, n), h2, x1, mod3, g_post_ffn, peer_uv, tg, t)
    new = tuple(a.reshape(1, bsz, t, N_HEADS, HEAD_DIM) for a in (akf, avf, bkf, bvf)) + (
        ikf.reshape(1, bsz, t, HEAD_DIM),)
    return y.reshape(bsz, t, D_MODEL), new


def _tiles(t):
    if t % 256 == 0:
        return (256, 256, 256, 256, 256, 128, 128)
    return (t, t, 128, t, 128, 128, 128)


def kernel(x_prompt, x_sample, c_prompt, c_sample, cache_sb_k, cache_sb_v, cache_dsa_k, cache_dsa_v, cache_dsa_kidx, w_ada, b_ada, g_pre_mix, g_post_mix, g_pre_ffn, g_post_ffn, w_in, w_gate, w_branch_a, w_branch_b, w_out, w_peer_q, peer_subkeys, peer_u, peer_v):
    assert w_ada.shape[0] == 1, "one layer"
    n_p, n_s = c_prompt.shape[0], c_sample.shape[0]
    c_all = jnp.concatenate([c_prompt, c_sample], axis=0)
    rows = -(-c_all.shape[0] // 8) * 8
    c_all = jnp.pad(c_all, ((0, rows - c_all.shape[0]), (0, 0)))
    mod = _modulation(c_all, w_ada[0], b_ada[0]).reshape(rows, 6, D_MODEL)
    w_in_b = jnp.pad(w_in[0], ((0, 0), (0, W_IN_PAD - W_IN_COLS))).astype(BF16)
    weights = (g_pre_mix, g_post_mix, g_pre_ffn, g_post_ffn, w_in_b, w_gate[0].astype(BF16),
               w_branch_a[0].astype(BF16), w_branch_b[0].astype(BF16), w_out[0].astype(BF16),
               w_peer_q[0].astype(BF16), peer_subkeys[0].astype(BF16),
               jnp.concatenate([peer_u[0], peer_v[0]], axis=1))
    yp, new_p = _layer(x_prompt, mod[:n_p], None, weights, _tiles(x_prompt.shape[1]))
    past = (cache_sb_k[0], cache_sb_v[0], cache_dsa_k[0], cache_dsa_v[0], cache_dsa_kidx[0])
    ys, new_s = _layer(x_sample, mod[n_p:n_p + n_s], past, weights, _tiles(x_sample.shape[1]))
    return (yp, ys) + new_p + new_s
```

```python
import functools
import math

import jax
import jax.numpy as jnp
import numpy as np
from jax import lax
from jax.experimental import pallas as pl
from jax.experimental.pallas import tpu as pltpu

F32 = jnp.float32
BF16 = jnp.bfloat16
I32 = jnp.int32

D_MODEL = 1024
CHUNK = 64
EPS = 1e-6
N_HEADS = 8
HEAD_DIM = 64
WIDTH = N_HEADS * HEAD_DIM
N_PAIRS = N_HEADS // 2
LANES = 128
N_CHUNK = D_MODEL // LANES
TOPK_MAX = 256
PEER_HEADS = 8
PEER_NKEYS = 128
PEER_DHALF = 128
PEER_TOPK = 16
PEER_SEL = PEER_HEADS * PEER_TOPK
PEER_SLOTS = 8
PEER_AHEAD = 6
W_IN_COLS = 7 * WIDTH + HEAD_DIM + N_HEADS
W_IN_PAD = 7 * WIDTH + LANES
INT_MIN = -(2 ** 31)
NEG_BIG = -1e30
DIST_MASKED = 1e30
LOG2E = 1.4426950408889634
VMEM_LIMIT = 56 * 1024 * 1024


def _params(sem):
    return pltpu.CompilerParams(dimension_semantics=sem, vmem_limit_bytes=VMEM_LIMIT)


def _dot(a, b):
    return jnp.dot(a, b, preferred_element_type=F32)


def _dot_nt(a, b):
    return lax.dot_general(a, b, (((1,), (1,)), ((), ())), preferred_element_type=F32)


def _rms(x, gain):
    return x * lax.rsqrt(jnp.mean(x * x, axis=-1, keepdims=True) + EPS) * gain


def _mod_kernel(c_ref, w_ref, b_ref, o_ref):
    c = c_ref[...]
    s = c * (1.0 / (1.0 + jnp.exp(-c)))
    o_ref[...] = jnp.dot(s, w_ref[...], preferred_element_type=F32,
                         precision=lax.Precision.HIGHEST) + b_ref[...]


def _modulation(c, w_ada, b_ada):
    n = c.shape[0]
    cols = w_ada.shape[1]
    tn = 1024
    return pl.pallas_call(
        _mod_kernel,
        out_shape=jax.ShapeDtypeStruct((n, cols), F32),
        grid=(cols // tn,),
        in_specs=[pl.BlockSpec((n, D_MODEL), lambda j: (0, 0)),
                  pl.BlockSpec((D_MODEL, tn), lambda j: (0, j)),
                  pl.BlockSpec((1, tn), lambda j: (0, j))],
        out_specs=pl.BlockSpec((n, tn), lambda j: (0, j)),
        compiler_params=_params(("arbitrary",)),
        name="mod",
    )(c, w_ada, b_ada.reshape(1, cols))


def _inproj_kernel(x_ref, mod_ref, g_ref, win_ref, wg_ref,
                   aq_ref, ak_ref, av_ref, bq_ref, bk_ref, bv_ref, iq_ref, ki2_ref,
                   akf_ref, avf_ref, bkf_ref, bvf_ref, ikf_ref, iw_ref, ga_ref, gb_ref):
    x = x_ref[...]
    mod = mod_ref[0]
    h = _rms(x, g_ref[...]) * (1.0 + mod[1:2, :]) + mod[0:1, :]
    hb = h.astype(BF16)
    qscale = HEAD_DIM ** -0.5

    def seg(i):
        return _dot(hb, win_ref[:, i * WIDTH:(i + 1) * WIDTH])

    aq_ref[...] = (seg(0) * (qscale * LOG2E)).astype(BF16)
    z = seg(1)
    akf_ref[...] = z
    ak_ref[...] = z.astype(BF16)
    z = seg(2)
    avf_ref[...] = z
    av_ref[...] = z.astype(BF16)
    bq_ref[...] = (seg(3) * (qscale * LOG2E)).astype(BF16)
    z = seg(4)
    bkf_ref[...] = z
    bk_ref[...] = z.astype(BF16)
    z = seg(5)
    bvf_ref[...] = z
    bv_ref[...] = z.astype(BF16)
    iq_ref[...] = (seg(6) * qscale).astype(BF16)
    tail = _dot(hb, win_ref[:, 7 * WIDTH:7 * WIDTH + LANES])
    ik = tail[:, :HEAD_DIM]
    ikf_ref[...] = ik
    ki2_ref[...] = jnp.concatenate([ik, ik], axis=-1).astype(BF16)
    iw_ref[...] = tail[:, HEAD_DIM:HEAD_DIM + N_HEADS] * (N_HEADS ** -0.5)
    zg = _dot(hb, wg_ref[...])
    gate = 1.0 / (1.0 + jnp.exp(-zg))
    ga_ref[...] = gate[:, :D_MODEL]
    gb_ref[...] = gate[:, D_MODEL:]


def _inproj(x2, mod3, g_pre, w_in_b, w_gate_b, tm, tiles_per_seq):
    n = x2.shape[0]
    row = lambda i: (i, 0)
    const = lambda i: (0, 0)
    bspec = lambda w: pl.BlockSpec((tm, w), row)
    outs = ([jax.ShapeDtypeStruct((n, WIDTH), BF16)] * 7
            + [jax.ShapeDtypeStruct((n, LANES), BF16)]
            + [jax.ShapeDtypeStruct((n, WIDTH), F32)] * 4
            + [jax.ShapeDtypeStruct((n, HEAD_DIM), F32),
               jax.ShapeDtypeStruct((n, N_HEADS), F32),
               jax.ShapeDtypeStruct((n, D_MODEL), F32),
               jax.ShapeDtypeStruct((n, D_MODEL), F32)])
    out_specs = ([bspec(WIDTH)] * 7 + [bspec(LANES)] + [bspec(WIDTH)] * 4
                 + [bspec(HEAD_DIM), bspec(N_HEADS), bspec(D_MODEL), bspec(D_MODEL)])
    return pl.pallas_call(
        _inproj_kernel,
        out_shape=outs,
        grid=(n // tm,),
        in_specs=[pl.BlockSpec((tm, D_MODEL), row),
                  pl.BlockSpec((1, 6, D_MODEL), lambda i: (i // tiles_per_seq, 0, 0)),
                  pl.BlockSpec((1, D_MODEL), const),
                  pl.BlockSpec((D_MODEL, W_IN_PAD), const, pipeline_mode=pl.Buffered(1)),
                  pl.BlockSpec((D_MODEL, 2 * D_MODEL), const, pipeline_mode=pl.Buffered(1))],
        out_specs=out_specs,
        compiler_params=_params(("parallel",)),
        name="inproj",
    )(x2, mod3, g_pre, w_in_b, w_gate_b)


def _sb_kernel(tab_ref, q_ref, k_ref, v_ref, tri_ref, o_ref, acc0, acc1, run0, run1, *, tq, tk, q_off):
    s = pl.program_id(2)
    qi = tab_ref[0, s]
    kj = tab_ref[1, s]
    first = tab_ref[2, s]
    last = tab_ref[3, s]
    masked = tab_ref[4, s]

    acc_refs, run_refs = (acc0, acc1), (run0, run1)

    @pl.when(first == 1)
    def _():
        for r in acc_refs + run_refs:
            r[...] = jnp.zeros_like(r)

    lane = lax.broadcasted_iota(I32, (tq, LANES), 1)

    def step(use_mask):
        q = q_ref[0]
        k = k_ref[0]
        v = v_ref[0]
        tri = tri_ref[...]
        if use_mask:
            qpos = q_off + qi * tq + lax.broadcasted_iota(I32, (tq, tk), 0)
            kpos = kj * tk + lax.broadcasted_iota(I32, (tq, tk), 1)
            causal = kpos < qpos
        zero = jnp.zeros_like(q)
        z = [_dot_nt(jnp.where(lane < HEAD_DIM, q, zero), k), _dot_nt(jnp.where(lane < HEAD_DIM, zero, q), k)]
        suf = [None, None]
        for e in range(2):
            sp = jnp.maximum(z[e], 0.0) + jnp.log2(1.0 + jnp.exp2(-jnp.abs(z[e])))
            if use_mask:
                sp = jnp.where(causal, sp, 0.0)
            hi = sp.astype(BF16)
            lo = (sp - hi.astype(F32)).astype(BF16)
            suf[e] = _dot(hi, tri) + _dot(lo, tri) + run_refs[e][:, 0:1]
        for e in range(2):
            w = jnp.exp2(z[e] - suf[e])
            if use_mask:
                w = jnp.where(causal, w, 0.0)
            acc_refs[e][...] += _dot(w.astype(BF16), v)
            run_refs[e][...] = jnp.broadcast_to(suf[e][:, 0:1], (tq, LANES))

    @pl.when(masked == 1)
    def _():
        step(True)

    @pl.when(masked == 0)
    def _():
        step(False)

    @pl.when(last == 1)
    def _():
        o_ref[0] = jnp.where(lane < HEAD_DIM, acc0[...], acc1[...]).astype(o_ref.dtype)


def _sb_table(nq, nk, tq, tk, q_off):
    rows = []
    for qi in range(nq):
        q_lo = q_off + qi * tq
        q_hi = q_lo + tq - 1
        kjs = [kj for kj in range(nk) if kj * tk < q_hi]
        if not kjs:
            kjs = [0]
        kjs = kjs[::-1]
        for n, kj in enumerate(kjs):
            masked = int(kj * tk + tk - 1 >= q_lo)
            rows.append((qi, kj, int(n == 0), int(n == len(kjs) - 1), masked))
    return np.asarray(rows, dtype=np.int32).T.copy()


def _sb_attention(q, k, v, tq, tk, q_off):
    b, t, _ = q.shape
    s_len = k.shape[1]
    tab = _sb_table(t // tq, s_len // tk, tq, tk, q_off)
    tri = jnp.asarray(np.tril(np.ones((tk, tk), np.float32)), dtype=BF16)
    grid_spec = pltpu.PrefetchScalarGridSpec(
        num_scalar_prefetch=1,
        grid=(b, N_PAIRS, tab.shape[1]),
        in_specs=[pl.BlockSpec((1, tq, LANES), lambda bi, hp, s, tab: (bi, tab[0, s], hp)),
                  pl.BlockSpec((1, tk, LANES), lambda bi, hp, s, tab: (bi, tab[1, s], hp)),
                  pl.BlockSpec((1, tk, LANES), lambda bi, hp, s, tab: (bi, tab[1, s], hp)),
                  pl.BlockSpec((tk, tk), lambda bi, hp, s, tab: (0, 0))],
        out_specs=pl.BlockSpec((1, tq, LANES), lambda bi, hp, s, tab: (bi, tab[0, s], hp)),
        scratch_shapes=[pltpu.VMEM((tq, LANES), F32)] * 4,
    )
    return pl.pallas_call(
        functools.partial(_sb_kernel, tq=tq, tk=tk, q_off=q_off),
        out_shape=jax.ShapeDtypeStruct((b, t, WIDTH), BF16),
        grid_spec=grid_spec,
        compiler_params=_params(("parallel", "parallel", "arbitrary")),
        name="sb_attn",
    )(jnp.asarray(tab), q, k, v, tri)


def _sortable(x):
    bits = pltpu.bitcast(x + 0.0, I32)
    return jnp.where(bits < 0, bits ^ 0x7FFFFFFF, bits)


def _alibi_slope2(h):
    return LOG2E * 2.0 ** (-8.0 * (h + 1) / N_HEADS)


def _dsa_kernel(q_ref, iq_ref, iw_ref, k_ref, v_ref, ki_ref, o_ref,
                keys_ref, bias_ref, m_ref, l_ref, acc_ref,
                *, tq, tk, q_off, n_kb_max, n_sel, idx_bits):
    qi = pl.program_id(1)
    q_lo = q_off + qi * tq
    adm_end = ((q_lo + tq - 1) // CHUNK + 1) * CHUNK
    n_kb = jnp.minimum((adm_end + tk - 1) // tk, n_kb_max)

    lane = lax.broadcasted_iota(I32, (tq, LANES), 1)
    lo_half = lane < HEAD_DIM
    row_pos = q_lo + lax.broadcasted_iota(I32, (tq, tk), 0)
    col_iota = lax.broadcasted_iota(I32, (tq, tk), 1)

    def split_heads(x_pair):
        zero = jnp.zeros_like(x_pair)
        return jnp.where(lo_half, x_pair, zero), jnp.where(lo_half, zero, x_pair)

    iw = iw_ref[0]

    def score_block(j, carry):
        ki = ki_ref[0, pl.ds(pl.multiple_of(j * tk, tk), tk), :]
        score = jnp.zeros((tq, tk), F32)
        for hp in range(N_PAIRS):
            pair = iq_ref[0, :, hp * LANES:(hp + 1) * LANES]
            for e, qe in enumerate(split_heads(pair)):
                h = 2 * hp + e
                score = score + iw[:, h:h + 1] * jnp.maximum(_dot_nt(qe, ki), 0.0)
        kpos = j * tk + col_iota
        adm = (kpos // CHUNK) <= (row_pos // CHUNK)
        keys_ref[j] = jnp.where(adm, _sortable(score), INT_MIN)
        return carry

    lax.fori_loop(0, n_kb, score_block, 0)

    def count_rows(pred_fn):
        def body(j, acc):
            hit = pred_fn(keys_ref[j], j).astype(I32)
            part = hit[:, 0:LANES]
            for c in range(1, tk // LANES):
                part = part + hit[:, c * LANES:(c + 1) * LANES]
            return acc + part
        acc = lax.fori_loop(0, n_kb, body, jnp.zeros((tq, LANES), I32))
        return jnp.sum(acc, axis=1, keepdims=True)

    def bit_step(i, thr):
        cand = thr + jnp.left_shift(jnp.int32(1), 31 - i)
        cnt = count_rows(lambda kb, j: kb >= cand)
        return jnp.where(cnt >= n_sel, cand, thr)

    thr = lax.fori_loop(0, 32, bit_step, jnp.full((tq, 1), INT_MIN, I32))
    n_gt = count_rows(lambda kb, j: kb > thr)
    n_ge = count_rows(lambda kb, j: kb >= thr)
    need = n_sel - n_gt
    real = thr > INT_MIN
    tie = jnp.max(jnp.where(real & (n_ge > n_sel), 1, 0)) > 0

    def tie_cut():
        def idx_step(i, cut):
            cand = cut + jnp.left_shift(jnp.int32(1), idx_bits - 1 - i)
            cnt = count_rows(lambda kb, j: (kb == thr) & ((j * tk + col_iota) < cand))
            return jnp.where(cnt < need, cand, cut)
        return lax.fori_loop(0, idx_bits, idx_step, jnp.zeros((tq, 1), I32))

    cut = lax.cond(tie, tie_cut, lambda: jnp.full((tq, 1), n_kb_max * tk, I32))

    def bias_block(j, carry):
        kb = keys_ref[j]
        kpos = j * tk + col_iota
        sel = (kb > thr) | ((kb == thr) & real & (kpos <= cut))
        bias_ref[j] = jnp.where(sel, 0.0, NEG_BIG)
        return carry

    lax.fori_loop(0, n_kb, bias_block, 0)

    m_ref[...] = jnp.full_like(m_ref, NEG_BIG)
    l_ref[...] = jnp.zeros_like(l_ref)
    acc_ref[...] = jnp.zeros_like(acc_ref)

    def attn_block(j, carry):
        start = pl.multiple_of(j * tk, tk)
        bias = bias_ref[j]
        dist = jnp.abs(row_pos - (j * tk + col_iota)).astype(F32)
        for hp in range(N_PAIRS):
            cols = slice(hp * LANES, (hp + 1) * LANES)
            kp = k_ref[0, pl.ds(start, tk), cols]
            vp = v_ref[0, pl.ds(start, tk), cols]
            for e, qe in enumerate(split_heads(q_ref[0, :, cols])):
                h = 2 * hp + e
                s = _dot_nt(qe, kp) - _alibi_slope2(h) * dist + bias
                m_old = m_ref[h]
                m_new = jnp.maximum(m_old, jnp.max(s, axis=1, keepdims=True))
                alpha = jnp.exp2(m_old - m_new)
                p = jnp.exp2(s - m_new[:, 0:1])
                l_ref[h] = alpha * l_ref[h] + jnp.sum(p, axis=1, keepdims=True)
                acc_ref[h] = alpha * acc_ref[h] + _dot(p.astype(BF16), vp)
                m_ref[h] = m_new
        return carry

    lax.fori_loop(0, n_kb, attn_block, 0)

    for hp in range(N_PAIRS):
        o_lo = acc_ref[2 * hp] / l_ref[2 * hp]
        o_hi = acc_ref[2 * hp + 1] / l_ref[2 * hp + 1]
        o_ref[0, :, hp * LANES:(hp + 1) * LANES] = jnp.where(lo_half, o_lo, o_hi).astype(o_ref.dtype)


def _dsa_attention(q, iq, iw, k, v, ki2, tq, tk, q_off, n_sel):
    b, t, _ = q.shape
    s_len = k.shape[1]
    n_kb_max = s_len // tk
    qspec = lambda w: pl.BlockSpec((1, tq, w), lambda bi, i: (bi, i, 0))
    kspec = lambda w: pl.BlockSpec((1, s_len, w), lambda bi, i: (bi, 0, 0), pipeline_mode=pl.Buffered(1))
    return pl.pallas_call(
        functools.partial(_dsa_kernel, tq=tq, tk=tk, q_off=q_off, n_kb_max=n_kb_max, n_sel=n_sel,
                          idx_bits=max(1, int(math.ceil(math.log2(s_len + 1))))),
        out_shape=jax.ShapeDtypeStruct((b, t, WIDTH), BF16),
        grid=(b, t // tq),
        in_specs=[qspec(WIDTH), qspec(WIDTH), qspec(N_HEADS), kspec(WIDTH), kspec(WIDTH), kspec(LANES)],
        out_specs=qspec(WIDTH),
        scratch_shapes=[pltpu.VMEM((n_kb_max, tq, tk), I32),
                        pltpu.VMEM((n_kb_max, tq, tk), F32),
                        pltpu.VMEM((N_HEADS, tq, LANES), F32),
                        pltpu.VMEM((N_HEADS, tq, LANES), F32),
                        pltpu.VMEM((N_HEADS, tq, LANES), F32)],
        compiler_params=_params(("parallel", "arbitrary")),
        name="dsa_attn",
    )(q, iq, iw, k, v, ki2)


def _dsat_kernel(q_ref, iq_ref, iwt_ref, k_ref, vt_ref, ki_ref, o_ref, keys_ref, dm_ref, qt_ref, iqt_ref, *stats,
                 tq, tk, q_off, n_kb_max, n_sel, idx_bits):
    m_refs, l_refs, acc_refs = stats[:N_HEADS], stats[N_HEADS:2 * N_HEADS], stats[2 * N_HEADS:]
    qi = pl.program_id(1)
    q_lo = q_off + qi * tq
    adm_end = ((q_lo + tq - 1) // CHUNK + 1) * CHUNK
    n_kb = jnp.minimum((adm_end + tk - 1) // tk, n_kb_max)

    lo_half = lax.broadcasted_iota(I32, (tq, LANES), 1) < HEAD_DIM
    qpos = q_lo + lax.broadcasted_iota(I32, (tk, tq), 1)
    krow = lax.broadcasted_iota(I32, (tk, tq), 0)

    def heads_of(x_ref):
        out = []
        for hp in range(N_PAIRS):
            pair = x_ref[0, :, hp * LANES:(hp + 1) * LANES]
            zero = jnp.zeros_like(pair)
            out += [jnp.where(lo_half, pair, zero), jnp.where(lo_half, zero, pair)]
        return out

    for src, dst in ((iq_ref, iqt_ref), (q_ref, qt_ref)):
        for h, xe in enumerate(heads_of(src)):
            dst[h] = xe.astype(F32).T.astype(BF16)

    def score_block(j, carry):
        ki = ki_ref[0, pl.ds(pl.multiple_of(j * tk, tk), tk), :]
        score = jnp.zeros((tk, tq), F32)
        for h in range(N_HEADS):
            score = score + iwt_ref[0, h:h + 1, :] * jnp.maximum(_dot(ki, iqt_ref[h]), 0.0)
        adm = ((j * tk + krow) // CHUNK) <= (qpos // CHUNK)
        keys_ref[j] = jnp.where(adm, _sortable(score), INT_MIN)
        return carry

    lax.fori_loop(0, n_kb, score_block, 0)

    def count_cols(pred_fn):
        def body(j, acc):
            hit = pred_fn(keys_ref[j], j).astype(I32)
            return acc + jnp.sum(hit.reshape(tk // 8, 8, tq), axis=0)
        acc = lax.fori_loop(0, n_kb, body, jnp.zeros((8, tq), I32))
        return jnp.sum(acc, axis=0, keepdims=True)

    def bit_step(i, thr):
        cand = thr + jnp.left_shift(jnp.int32(1), 31 - i)
        cnt = count_cols(lambda kb, j: kb >= cand)
        return jnp.where(cnt >= n_sel, cand, thr)

    thr = lax.fori_loop(0, 32, bit_step, jnp.full((1, tq), INT_MIN, I32))
    n_gt = count_cols(lambda kb, j: kb > thr)
    n_ge = count_cols(lambda kb, j: kb >= thr)
    need = n_sel - n_gt
    real = thr > INT_MIN
    tie = jnp.max(jnp.where(real & (n_ge > n_sel), 1, 0)) > 0

    def tie_cut():
        def idx_step(i, cut):
            cand = cut + jnp.left_shift(jnp.int32(1), idx_bits - 1 - i)
            cnt = count_cols(lambda kb, j: (kb == thr) & ((j * tk + krow) < cand))
            return jnp.where(cnt < need, cand, cut)
        return lax.fori_loop(0, idx_bits, idx_step, jnp.zeros((1, tq), I32))

    cut = lax.cond(tie, tie_cut, lambda: jnp.full((1, tq), n_kb_max * tk, I32))

    def mask_block(j, carry):
        kb = keys_ref[j]
        kpos = j * tk + krow
        sel = (kb > thr) | ((kb == thr) & real & (kpos <= cut))
        dm_ref[j] = jnp.where(sel, jnp.abs(qpos - kpos).astype(F32), DIST_MASKED)
        return carry

    lax.fori_loop(0, n_kb, mask_block, 0)

    for h in range(N_HEADS):
        m_refs[h][...] = jnp.full_like(m_refs[h], NEG_BIG)
        l_refs[h][...] = jnp.zeros_like(l_refs[h])
        acc_refs[h][...] = jnp.zeros_like(acc_refs[h])

    def attn_block(j, carry):
        start = pl.multiple_of(j * tk, tk)
        dm = dm_ref[j]

        def qk(h):
            return _dot(k_ref[0, pl.ds(start, tk), (h // 2) * LANES:(h // 2 + 1) * LANES], qt_ref[h])

        ahead = 2
        pending = [qk(h) for h in range(ahead)]
        for hp in range(N_PAIRS):
            for e in range(2):
                h = 2 * hp + e
                s_this = pending.pop(0)
                if h + ahead < N_HEADS:
                    pending.append(qk(h + ahead))
                t2 = s_this - _alibi_slope2(h) * dm
                m_old = m_refs[h][...]
                m_new = jnp.maximum(m_old, jnp.max(t2, axis=0, keepdims=True))
                alpha = jnp.exp2(m_old - m_new)
                p = jnp.exp2(t2 - m_new)
                l_refs[h][...] = alpha * l_refs[h][...] + jnp.sum(p, axis=0, keepdims=True)
                vt = vt_ref[0, j, hp * LANES + e * HEAD_DIM:hp * LANES + (e + 1) * HEAD_DIM, :]
                acc_refs[h][...] = alpha * acc_refs[h][...] + _dot(vt, p.astype(BF16))
                m_refs[h][...] = m_new
        return carry

    lax.fori_loop(0, n_kb, attn_block, 0)

    for hp in range(N_PAIRS):
        ot = jnp.concatenate([acc_refs[2 * hp][...] / l_refs[2 * hp][...],
                              acc_refs[2 * hp + 1][...] / l_refs[2 * hp + 1][...]], axis=0)
        o_ref[0, :, hp * LANES:(hp + 1) * LANES] = ot.T.astype(o_ref.dtype)


def _dsa_attention_t(q, iq, iw, k, v, ki2, tq, tk, q_off, n_sel):
    b, t, _ = q.shape
    s_len = k.shape[1]
    n_kb_max = s_len // tk
    iwt = jnp.swapaxes(iw, 1, 2)
    vt = jnp.swapaxes(v.reshape(b, n_kb_max, tk, WIDTH), 2, 3)
    qspec = lambda w: pl.BlockSpec((1, tq, w), lambda bi, i: (bi, i, 0))
    kspec = lambda w: pl.BlockSpec((1, s_len, w), lambda bi, i: (bi, 0, 0), pipeline_mode=pl.Buffered(1))
    return pl.pallas_call(
        functools.partial(_dsat_kernel, tq=tq, tk=tk, q_off=q_off, n_kb_max=n_kb_max, n_sel=n_sel,
                          idx_bits=max(1, int(math.ceil(math.log2(s_len + 1))))),
        out_shape=jax.ShapeDtypeStruct((b, t, WIDTH), BF16),
        grid=(b, t // tq),
        in_specs=[qspec(WIDTH), qspec(WIDTH),
                  pl.BlockSpec((1, N_HEADS, tq), lambda bi, i: (bi, 0, i)),
                  kspec(WIDTH),
                  pl.BlockSpec((1, n_kb_max, WIDTH, tk), lambda bi, i: (bi, 0, 0, 0), pipeline_mode=pl.Buffered(1)),
                  kspec(LANES)],
        out_specs=qspec(WIDTH),
        scratch_shapes=[pltpu.VMEM((n_kb_max, tk, tq), I32), pltpu.VMEM((n_kb_max, tk, tq), F32),
                        pltpu.VMEM((N_HEADS, LANES, tq), BF16), pltpu.VMEM((N_HEADS, LANES, tq), BF16)]
                       + [pltpu.VMEM((1, tq), F32)] * (2 * N_HEADS) + [pltpu.VMEM((HEAD_DIM, tq), F32)] * N_HEADS,
        compiler_params=_params(("parallel", "arbitrary")),
        name="dsa_attn_t",
    )(q, iq, iwt, k, vt, ki2)


def _mix_kernel(oa_ref, ob_ref, ga_ref, gb_ref, x_ref, mod_ref, gpost_ref, gpre_ref,
                wa_ref, wb_ref, wo_ref, wq_ref, x1_ref, h2_ref, pq_ref):
    mod = mod_ref[0]
    mixed = ga_ref[...] * _dot(oa_ref[...], wa_ref[...]) + gb_ref[...] * _dot(ob_ref[...], wb_ref[...])
    y = _dot(mixed.astype(BF16), wo_ref[...])
    x1 = x_ref[...] + mod[2:3, :] * _rms(y, gpost_ref[...])
    x1_ref[...] = x1
    h2 = _rms(x1, gpre_ref[...]) * (1.0 + mod[4:5, :]) + mod[3:4, :]
    h2_ref[...] = h2
    pq_ref[...] = _dot(h2.astype(BF16), wq_ref[...]).astype(BF16)


def _mix(oa, ob, ga, gb, x2, mod3, g_post, g_pre, wa, wb, wo, wq, tm, tiles_per_seq):
    n = x2.shape[0]
    row = lambda i: (i, 0)
    const = lambda i: (0, 0)
    nq = wq.shape[1]
    wspec = lambda r, c: pl.BlockSpec((r, c), const, pipeline_mode=pl.Buffered(1))
    return pl.pallas_call(
        _mix_kernel,
        out_shape=[jax.ShapeDtypeStruct((n, D_MODEL), F32),
                   jax.ShapeDtypeStruct((n, D_MODEL), F32),
                   jax.ShapeDtypeStruct((n, nq), BF16)],
        grid=(n // tm,),
        in_specs=[pl.BlockSpec((tm, WIDTH), row), pl.BlockSpec((tm, WIDTH), row),
                  pl.BlockSpec((tm, D_MODEL), row), pl.BlockSpec((tm, D_MODEL), row),
                  pl.BlockSpec((tm, D_MODEL), row),
                  pl.BlockSpec((1, 6, D_MODEL), lambda i: (i // tiles_per_seq, 0, 0)),
                  pl.BlockSpec((1, D_MODEL), const), pl.BlockSpec((1, D_MODEL), const),
                  wspec(WIDTH, D_MODEL), wspec(WIDTH, D_MODEL), wspec(D_MODEL, D_MODEL),
                  wspec(D_MODEL, nq)],
        out_specs=[pl.BlockSpec((tm, D_MODEL), row), pl.BlockSpec((tm, D_MODEL), row),
                   pl.BlockSpec((tm, nq), row)],
        compiler_params=_params(("parallel",)),
        name="mix",
    )(oa, ob, ga, gb, x2, mod3, g_post, g_pre, wa, wb, wo, wq)


def _top16(s, payload=None):
    n = s.shape[0]
    pos = lax.broadcasted_iota(I32, s.shape, 0)
    vals, picks = [], []
    for _ in range(PEER_TOPK):
        m = jnp.max(s, axis=0, keepdims=True)
        p = jnp.min(jnp.where(s == m, pos, n), axis=0, keepdims=True)
        hit = pos == p
        vals.append(m)
        picks.append(p if payload is None else jnp.sum(jnp.where(hit, payload, 0), axis=0, keepdims=True))
        s = jnp.where(hit, -jnp.inf, s)
    return jnp.concatenate(vals, axis=0), jnp.concatenate(picks, axis=0)


def _peersel_kernel(pq_ref, sk_ref, idx_ref, g_ref):
    pq = pq_ref[...]
    s1 = _dot_nt(sk_ref[0, 0], pq[:, :PEER_DHALF])
    s2 = _dot_nt(sk_ref[0, 1], pq[:, PEER_DHALF:])
    v1, i1 = _top16(s1)
    v2, i2 = _top16(s2)
    half = PEER_TOPK // 2
    row8 = lax.broadcasted_iota(I32, (half, pq.shape[0]), 0)
    cand = [v1[0:1] + v2]
    cidx = [i1[0:1] * PEER_NKEYS + i2]
    for i in range(1, half):
        cand.append(jnp.where(row8 < PEER_TOPK // (i + 1), v1[i:i + 1] + v2[0:half], -jnp.inf))
        cidx.append(i1[i:i + 1] * PEER_NKEYS + i2[0:half])
    cand.append(v1[half:] + v2[0:1])
    cidx.append(i1[half:] * PEER_NKEYS + i2[0:1])
    top, eidx = _top16(jnp.concatenate(cand, axis=0), jnp.concatenate(cidx, axis=0))
    ex = jnp.exp(top - top[0:1])
    g_ref[0] = ex / jnp.sum(ex, axis=0, keepdims=True)
    idx_ref[0] = eidx


def _peer_select(pq, subkeys_b, tt):
    n = pq.shape[0]
    return pl.pallas_call(
        _peersel_kernel,
        out_shape=[jax.ShapeDtypeStruct((PEER_HEADS, PEER_TOPK, n), I32),
                   jax.ShapeDtypeStruct((PEER_HEADS, PEER_TOPK, n), F32)],
        grid=(n // tt, PEER_HEADS),
        in_specs=[pl.BlockSpec((tt, 2 * PEER_DHALF), lambda i, h: (i, h)),
                  pl.BlockSpec((1, 2, PEER_NKEYS, PEER_DHALF), lambda i, h: (h, 0, 0, 0))],
        out_specs=[pl.BlockSpec((1, PEER_TOPK, tt), lambda i, h: (h, 0, i)),
                   pl.BlockSpec((1, PEER_TOPK, tt), lambda i, h: (h, 0, i))],
        compiler_params=_params(("parallel", "arbitrary")),
        name="peer_select",
    )(pq, subkeys_b)


def _erf_gelu(a):
    return 0.5 * a * (1.0 + lax.erf(a * (2.0 ** -0.5)))


def _peerffn_kernel(idx_ref, gt_ref, h2_ref, x1_ref, mod_ref, gpost_ref, uv_hbm, o_ref,
                    *scratch, tg, seqs_per_tile):
    bufs, (f_ref, sem) = scratch[:PEER_SLOTS], scratch[PEER_SLOTS:]

    def start_token(t, slot):
        tt = jnp.minimum(t, tg - 1)
        for k in range(PEER_SEL):
            pltpu.make_async_copy(uv_hbm.at[idx_ref[tt, k]], bufs[slot].at[:, pl.ds(k, 1), :],
                                  sem.at[slot]).start(priority=k % 2)

    def wait_token(slot):
        pltpu.make_async_copy(bufs[slot], bufs[slot], sem.at[slot]).wait()

    lane_tok = lax.broadcasted_iota(I32, (PEER_SEL, tg), 1)
    sub8 = lax.broadcasted_iota(I32, (8, D_MODEL), 0)

    def compute(t, slot):
        buf = bufs[slot]
        t8 = pl.multiple_of((t // 8) * 8, 8)
        h8 = h2_ref[pl.ds(t8, 8), :]
        hrow = jnp.sum(jnp.where(sub8 == t - t8, h8, 0.0), axis=0, keepdims=True)
        acc = None
        for c in range(N_CHUNK):
            term = buf[c] * hrow[:, c * LANES:(c + 1) * LANES]
            acc = term if acc is None else acc + term
        a = jnp.sum(acc, axis=1, keepdims=True)
        g = jnp.sum(jnp.where(lane_tok == t, gt_ref[...], 0.0), axis=1, keepdims=True)
        w = g * _erf_gelu(a)
        pieces = []
        for c in range(N_CHUNK):
            vc = buf[N_CHUNK + c]
            pieces.append(jnp.sum(vc * w, axis=0, keepdims=True))
        f_ref[pl.ds(t, 1), :] = jnp.concatenate(pieces, axis=1)

    for t in range(PEER_AHEAD):
        start_token(t, t)

    def group(i, carry):
        t0 = PEER_SLOTS * i
        for j in range(PEER_SLOTS):
            wait_token(j)
            start_token(t0 + j + PEER_AHEAD, (j + PEER_AHEAD) % PEER_SLOTS)
            compute(t0 + j, j)
        return carry

    lax.fori_loop(0, tg // PEER_SLOTS, group, 0)
    for t in range(PEER_AHEAD):
        wait_token(t)
    rows = tg // seqs_per_tile
    for s in range(seqs_per_tile):
        sl = slice(s * rows, (s + 1) * rows)
        o_ref[sl, :] = x1_ref[sl, :] + mod_ref[s][5:6, :] * _rms(f_ref[sl, :], gpost_ref[...])


def _peer_ffn(idx, gt, h2, x1, mod3, g_post, uv, tg, seq_len):
    n = h2.shape[0]
    row = lambda i: (i, 0)
    seqs_per_tile = max(1, tg // seq_len)
    tiles_per_seq = max(1, seq_len // tg)
    return pl.pallas_call(
        functools.partial(_peerffn_kernel, tg=tg, seqs_per_tile=seqs_per_tile),
        out_shape=jax.ShapeDtypeStruct((n, D_MODEL), F32),
        grid=(n // tg,),
        in_specs=[pl.BlockSpec((tg, PEER_SEL), row, memory_space=pltpu.SMEM),
                  pl.BlockSpec((PEER_SEL, tg), lambda i: (0, i)),
                  pl.BlockSpec((tg, D_MODEL), row),
                  pl.BlockSpec((tg, D_MODEL), row),
                  pl.BlockSpec((seqs_per_tile, 6, D_MODEL), lambda i: (i // tiles_per_seq, 0, 0)),
                  pl.BlockSpec((1, D_MODEL), lambda i: (0, 0)),
                  pl.BlockSpec(memory_space=pl.ANY)],
        out_specs=pl.BlockSpec((tg, D_MODEL), row),
        scratch_shapes=[pltpu.VMEM((2 * N_CHUNK, PEER_SEL, LANES), F32)] * PEER_SLOTS
                       + [pltpu.VMEM((tg, D_MODEL), F32), pltpu.SemaphoreType.DMA((PEER_SLOTS,))],
        compiler_params=_params(("arbitrary",)),
        name="peer_ffn",
    )(idx, gt, h2, x1, mod3, g_post, uv)


def _pad_keys(x, s_pad):
    return jnp.pad(x, ((0, 0), (0, s_pad - x.shape[1]), (0, 0)))


def _layer(x, mod3, past, weights, tiles):
    (g_pre_mix, g_post_mix, g_pre_ffn, g_post_ffn, w_in_b, w_gate_b, wa_b, wb_b, wo_b, wq_b,
     subkeys_b, peer_uv) = weights
    tm, sb_tq, sb_tk, dsa_tq, dsa_tk, tt, tg = tiles
    bsz, t, _ = x.shape
    n = bsz * t
    x2 = x.reshape(n, D_MODEL)
    (aq, ak, av, bq, bk, bv, iq, ki2, akf, avf, bkf, bvf, ikf, iw, ga, gb) = _inproj(
        x2, mod3, g_pre_mix, w_in_b, w_gate_b, tm, t // tm)
    seq = lambda a: a.reshape(bsz, t, a.shape[-1])
    if past is None:
        q_off = 0
        ka, va, kb, vb, ki = seq(ak), seq(av), seq(bk), seq(bv), seq(ki2)
        s_len = t
    else:
        p_sbk, p_sbv, p_dk, p_dv, p_ki = past
        q_off = p_sbk.shape[1]
        s_len = q_off + t
        flat = lambda c: c.reshape(bsz, q_off, WIDTH).astype(BF16)
        ka = jnp.concatenate([flat(p_sbk), seq(ak)], axis=1)
        va = jnp.concatenate([flat(p_sbv), seq(av)], axis=1)
        kb = jnp.concatenate([flat(p_dk), seq(bk)], axis=1)
        vb = jnp.concatenate([flat(p_dv), seq(bv)], axis=1)
        pk = p_ki.astype(BF16)
        ki = jnp.concatenate([jnp.concatenate([pk, pk], axis=-1), seq(ki2)], axis=1)
    s_pad = -(-s_len // max(sb_tk, dsa_tk)) * max(sb_tk, dsa_tk)
    ka, va, kb, vb, ki = (_pad_keys(a, s_pad) for a in (ka, va, kb, vb, ki))
    n_sel = min(TOPK_MAX, s_len // 4)

    oa = _sb_attention(seq(aq), ka, va, sb_tq, sb_tk, q_off)
    dsa = _dsa_attention_t if dsa_tq % LANES == 0 else _dsa_attention
    ob = dsa(seq(bq), seq(iq), seq(iw), kb, vb, ki, dsa_tq, dsa_tk, q_off, n_sel)

    x1, h2, pq = _mix(oa.reshape(n, WIDTH), ob.reshape(n, WIDTH), ga, gb, x2, mod3, g_post_mix, g_pre_ffn,
                      wa_b, wb_b, wo_b, wq_b, tm, t // tm)
    eidx, gate = _peer_select(pq, subkeys_b, tt)
    eidx = eidx.reshape(PEER_SEL, n).T
    y = _peer_ffn(eidx, gate.reshape(PEER_SEL, n), h2, x1, mod3, g_post_ffn, peer_uv, tg, t)
    new = tuple(a.reshape(1, bsz, t, N_HEADS, HEAD_DIM) for a in (akf, avf, bkf, bvf)) + (
        ikf.reshape(1, bsz, t, HEAD_DIM),)
    return y.reshape(bsz, t, D_MODEL), new


def _tiles(t):
    if t % 256 == 0:
        return (256, 256, 256, 256, 256, 128, 128)
    return (t, t, 128, t, 128, 128, 128)


def kernel(x_prompt, x_sample, c_prompt, c_sample, cache_sb_k, cache_sb_v, cache_dsa_k, cache_dsa_v, cache_dsa_kidx, w_ada, b_ada, g_pre_mix, g_post_mix, g_pre_ffn, g_post_ffn, w_in, w_gate, w_branch_a, w_branch_b, w_out, w_peer_q, peer_subkeys, peer_u, peer_v):
    assert w_ada.shape[0] == 1, "one layer"
    n_p, n_s = c_prompt.shape[0], c_sample.shape[0]
    c_all = jnp.concatenate([c_prompt, c_sample], axis=0)
    rows = -(-c_all.shape[0] // 8) * 8
    c_all = jnp.pad(c_all, ((0, rows - c_all.shape[0]), (0, 0)))
    mod = _modulation(c_all, w_ada[0], b_ada[0]).reshape(rows, 6, D_MODEL)
    w_in_b = jnp.pad(w_in[0], ((0, 0), (0, W_IN_PAD - W_IN_COLS))).astype(BF16)
    weights = (g_pre_mix, g_post_mix, g_pre_ffn, g_post_ffn, w_in_b, w_gate[0].astype(BF16),
               w_branch_a[0].astype(BF16), w_branch_b[0].astype(BF16), w_out[0].astype(BF16),
               w_peer_q[0].astype(BF16), peer_subkeys[0].astype(BF16),
               jnp.concatenate([peer_u[0], peer_v[0]], axis=1).reshape(-1, 2 * N_CHUNK, 1, LANES))
    yp, new_p = _layer(x_prompt, mod[:n_p], None, weights, _tiles(x_prompt.shape[1]))
    past = (cache_sb_k[0], cache_sb_v[0], cache_dsa_k[0], cache_dsa_v[0], cache_dsa_kidx[0])
    ys, new_s = _layer(x_sample, mod[n_p:n_p + n_s], past, weights, _tiles(x_sample.shape[1]))
    return (yp, ys) + new_p + new_s
```

```python
import functools
import math

import jax
import jax.numpy as jnp
import numpy as np
from jax import lax
from jax.experimental import pallas as pl
from jax.experimental.pallas import tpu as pltpu

F32 = jnp.float32
BF16 = jnp.bfloat16
I32 = jnp.int32

D_MODEL = 1024
CHUNK = 64
EPS = 1e-6
N_HEADS = 8
HEAD_DIM = 64
WIDTH = N_HEADS * HEAD_DIM
N_PAIRS = N_HEADS // 2
SB_PAIRS_PER_STEP = 4
LANES = 128
N_CHUNK = D_MODEL // LANES
TOPK_MAX = 256
PEER_HEADS = 8
PEER_NKEYS = 128
PEER_DHALF = 128
PEER_TOPK = 16
PEER_SEL = PEER_HEADS * PEER_TOPK
PEER_SLOTS = 8
PEER_AHEAD = 6
W_IN_COLS = 7 * WIDTH + HEAD_DIM + N_HEADS
W_IN_PAD = 7 * WIDTH + LANES
INT_MIN = -(2 ** 31)
NEG_BIG = -1e30
DIST_MASKED = 1e30
LOG2E = 1.4426950408889634
VMEM_LIMIT = 56 * 1024 * 1024


def _params(sem):
    return pltpu.CompilerParams(dimension_semantics=sem, vmem_limit_bytes=VMEM_LIMIT)


def _dot(a, b):
    return jnp.dot(a, b, preferred_element_type=F32)


def _dot_nt(a, b):
    return lax.dot_general(a, b, (((1,), (1,)), ((), ())), preferred_element_type=F32)


def _rms(x, gain):
    return x * lax.rsqrt(jnp.mean(x * x, axis=-1, keepdims=True) + EPS) * gain


def _mod_kernel(c_ref, w_ref, b_ref, o_ref):
    c = c_ref[...]
    s = c * (1.0 / (1.0 + jnp.exp(-c)))
    o_ref[...] = jnp.dot(s, w_ref[...], preferred_element_type=F32,
                         precision=lax.Precision.HIGHEST) + b_ref[...]


def _modulation(c, w_ada, b_ada):
    n = c.shape[0]
    cols = w_ada.shape[1]
    tn = 1024
    return pl.pallas_call(
        _mod_kernel,
        out_shape=jax.ShapeDtypeStruct((n, cols), F32),
        grid=(cols // tn,),
        in_specs=[pl.BlockSpec((n, D_MODEL), lambda j: (0, 0)),
                  pl.BlockSpec((D_MODEL, tn), lambda j: (0, j)),
                  pl.BlockSpec((1, tn), lambda j: (0, j))],
        out_specs=pl.BlockSpec((n, tn), lambda j: (0, j)),
        compiler_params=_params(("arbitrary",)),
        name="mod",
    )(c, w_ada, b_ada.reshape(1, cols))


def _inproj_kernel(x_ref, mod_ref, g_ref, win_ref, wg_ref,
                   aq_ref, ak_ref, av_ref, bq_ref, bk_ref, bv_ref, iq_ref, ki2_ref,
                   akf_ref, avf_ref, bkf_ref, bvf_ref, ikf_ref, iw_ref, ga_ref, gb_ref):
    x = x_ref[...]
    mod = mod_ref[0]
    h = _rms(x, g_ref[...]) * (1.0 + mod[1:2, :]) + mod[0:1, :]
    hb = h.astype(BF16)
    qscale = HEAD_DIM ** -0.5

    def seg(i):
        return _dot(hb, win_ref[:, i * WIDTH:(i + 1) * WIDTH])

    aq_ref[...] = (seg(0) * (qscale * LOG2E)).astype(BF16)
    z = seg(1)
    akf_ref[...] = z
    ak_ref[...] = z.astype(BF16)
    z = seg(2)
    avf_ref[...] = z
    av_ref[...] = z.astype(BF16)
    bq_ref[...] = (seg(3) * (qscale * LOG2E)).astype(BF16)
    z = seg(4)
    bkf_ref[...] = z
    bk_ref[...] = z.astype(BF16)
    z = seg(5)
    bvf_ref[...] = z
    bv_ref[...] = z.astype(BF16)
    iq_ref[...] = (seg(6) * qscale).astype(BF16)
    tail = _dot(hb, win_ref[:, 7 * WIDTH:7 * WIDTH + LANES])
    ik = tail[:, :HEAD_DIM]
    ikf_ref[...] = ik
    ki2_ref[...] = jnp.concatenate([ik, ik], axis=-1).astype(BF16)
    iw_ref[...] = tail[:, HEAD_DIM:HEAD_DIM + N_HEADS] * (N_HEADS ** -0.5)
    zg = _dot(hb, wg_ref[...])
    gate = 1.0 / (1.0 + jnp.exp(-zg))
    ga_ref[...] = gate[:, :D_MODEL]
    gb_ref[...] = gate[:, D_MODEL:]


def _inproj(x2, mod3, g_pre, w_in_b, w_gate_b, tm, tiles_per_seq):
    n = x2.shape[0]
    row = lambda i: (i, 0)
    const = lambda i: (0, 0)
    bspec = lambda w: pl.BlockSpec((tm, w), row)
    outs = ([jax.ShapeDtypeStruct((n, WIDTH), BF16)] * 7
            + [jax.ShapeDtypeStruct((n, LANES), BF16)]
            + [jax.ShapeDtypeStruct((n, WIDTH), F32)] * 4
            + [jax.ShapeDtypeStruct((n, HEAD_DIM), F32),
               jax.ShapeDtypeStruct((n, N_HEADS), F32),
               jax.ShapeDtypeStruct((n, D_MODEL), F32),
               jax.ShapeDtypeStruct((n, D_MODEL), F32)])
    out_specs = ([bspec(WIDTH)] * 7 + [bspec(LANES)] + [bspec(WIDTH)] * 4
                 + [bspec(HEAD_DIM), bspec(N_HEADS), bspec(D_MODEL), bspec(D_MODEL)])
    return pl.pallas_call(
        _inproj_kernel,
        out_shape=outs,
        grid=(n // tm,),
        in_specs=[pl.BlockSpec((tm, D_MODEL), row),
                  pl.BlockSpec((1, 6, D_MODEL), lambda i: (i // tiles_per_seq, 0, 0)),
                  pl.BlockSpec((1, D_MODEL), const),
                  pl.BlockSpec((D_MODEL, W_IN_PAD), const, pipeline_mode=pl.Buffered(1)),
                  pl.BlockSpec((D_MODEL, 2 * D_MODEL), const, pipeline_mode=pl.Buffered(1))],
        out_specs=out_specs,
        compiler_params=_params(("parallel",)),
        name="inproj",
    )(x2, mod3, g_pre, w_in_b, w_gate_b)


def _sb_kernel(tab_ref, q_ref, k_ref, v_ref, tri_ref, o_ref, *scratch, tq, tk, q_off, pairs):
    s = pl.program_id(2)
    qi = tab_ref[0, s]
    kj = tab_ref[1, s]
    first = tab_ref[2, s]
    last = tab_ref[3, s]
    masked = tab_ref[4, s]
    n_heads = 2 * pairs
    acc_refs, run_refs = scratch[:n_heads], scratch[n_heads:]

    @pl.when(first == 1)
    def _():
        for r in scratch:
            r[...] = jnp.zeros_like(r)

    lo_lanes = lax.broadcasted_iota(I32, (tq, LANES), 1) < HEAD_DIM

    def step(use_mask):
        tri = tri_ref[...]
        if use_mask:
            qpos = q_off + qi * tq + lax.broadcasted_iota(I32, (tq, tk), 0)
            kpos = kj * tk + lax.broadcasted_iota(I32, (tq, tk), 1)
            causal = kpos < qpos
        z = []
        for p in range(pairs):
            q = q_ref[0, :, p * LANES:(p + 1) * LANES]
            k = k_ref[0, :, p * LANES:(p + 1) * LANES]
            zero = jnp.zeros_like(q)
            z += [_dot_nt(jnp.where(lo_lanes, q, zero), k), _dot_nt(jnp.where(lo_lanes, zero, q), k)]
        suf = []
        for h in range(n_heads):
            sp = jnp.maximum(z[h], 0.0) + jnp.log2(1.0 + jnp.exp2(-jnp.abs(z[h])))
            if use_mask:
                sp = jnp.where(causal, sp, 0.0)
            hi = sp.astype(BF16)
            lo = (sp - hi.astype(F32)).astype(BF16)
            suf.append(_dot(hi, tri) + _dot(lo, tri) + run_refs[h][:, 0:1])
        for h in range(n_heads):
            w = jnp.exp2(z[h] - suf[h])
            if use_mask:
                w = jnp.where(causal, w, 0.0)
            p = h // 2
            acc_refs[h][...] += _dot(w.astype(BF16), v_ref[0, :, p * LANES:(p + 1) * LANES])
            run_refs[h][...] = jnp.broadcast_to(suf[h][:, 0:1], (tq, LANES))

    @pl.when(masked == 1)
    def _():
        step(True)

    @pl.when(masked == 0)
    def _():
        step(False)

    @pl.when(last == 1)
    def _():
        for p in range(pairs):
            o_ref[0, :, p * LANES:(p + 1) * LANES] = jnp.where(
                lo_lanes, acc_refs[2 * p][...], acc_refs[2 * p + 1][...]).astype(o_ref.dtype)


def _sb_table(nq, nk, tq, tk, q_off):
    rows = []
    for qi in range(nq):
        q_lo = q_off + qi * tq
        q_hi = q_lo + tq - 1
        kjs = [kj for kj in range(nk) if kj * tk < q_hi]
        if not kjs:
            kjs = [0]
        kjs = kjs[::-1]
        for n, kj in enumerate(kjs):
            masked = int(kj * tk + tk - 1 >= q_lo)
            rows.append((qi, kj, int(n == 0), int(n == len(kjs) - 1), masked))
    return np.asarray(rows, dtype=np.int32).T.copy()


def _sb_attention(q, k, v, tq, tk, q_off):
    b, t, _ = q.shape
    s_len = k.shape[1]
    tab = _sb_table(t // tq, s_len // tk, tq, tk, q_off)
    tri = jnp.asarray(np.tril(np.ones((tk, tk), np.float32)), dtype=BF16)
    pairs = SB_PAIRS_PER_STEP
    width = pairs * LANES
    grid_spec = pltpu.PrefetchScalarGridSpec(
        num_scalar_prefetch=1,
        grid=(b, N_PAIRS // pairs, tab.shape[1]),
        in_specs=[pl.BlockSpec((1, tq, width), lambda bi, hp, s, tab: (bi, tab[0, s], hp)),
                  pl.BlockSpec((1, tk, width), lambda bi, hp, s, tab: (bi, tab[1, s], hp)),
                  pl.BlockSpec((1, tk, width), lambda bi, hp, s, tab: (bi, tab[1, s], hp)),
                  pl.BlockSpec((tk, tk), lambda bi, hp, s, tab: (0, 0))],
        out_specs=pl.BlockSpec((1, tq, width), lambda bi, hp, s, tab: (bi, tab[0, s], hp)),
        scratch_shapes=[pltpu.VMEM((tq, LANES), F32)] * (4 * pairs),
    )
    return pl.pallas_call(
        functools.partial(_sb_kernel, tq=tq, tk=tk, q_off=q_off, pairs=pairs),
        out_shape=jax.ShapeDtypeStruct((b, t, WIDTH), BF16),
        grid_spec=grid_spec,
        compiler_params=_params(("parallel", "parallel", "arbitrary")),
        name="sb_attn",
    )(jnp.asarray(tab), q, k, v, tri)


def _sortable(x):
    bits = pltpu.bitcast(x + 0.0, I32)
    return jnp.where(bits < 0, bits ^ 0x7FFFFFFF, bits)


def _alibi_slope2(h):
    return LOG2E * 2.0 ** (-8.0 * (h + 1) / N_HEADS)


def _dsa_kernel(q_ref, iq_ref, iw_ref, k_ref, v_ref, ki_ref, o_ref,
                keys_ref, bias_ref, m_ref, l_ref, acc_ref,
                *, tq, tk, q_off, n_kb_max, n_sel, idx_bits):
    qi = pl.program_id(1)
    q_lo = q_off + qi * tq
    adm_end = ((q_lo + tq - 1) // CHUNK + 1) * CHUNK
    n_kb = jnp.minimum((adm_end + tk - 1) // tk, n_kb_max)

    lane = lax.broadcasted_iota(I32, (tq, LANES), 1)
    lo_half = lane < HEAD_DIM
    row_pos = q_lo + lax.broadcasted_iota(I32, (tq, tk), 0)
    col_iota = lax.broadcasted_iota(I32, (tq, tk), 1)

    def split_heads(x_pair):
        zero = jnp.zeros_like(x_pair)
        return jnp.where(lo_half, x_pair, zero), jnp.where(lo_half, zero, x_pair)

    iw = iw_ref[0]

    def score_block(j, carry):
        ki = ki_ref[0, pl.ds(pl.multiple_of(j * tk, tk), tk), :]
        score = jnp.zeros((tq, tk), F32)
        for hp in range(N_PAIRS):
            pair = iq_ref[0, :, hp * LANES:(hp + 1) * LANES]
            for e, qe in enumerate(split_heads(pair)):
                h = 2 * hp + e
                score = score + iw[:, h:h + 1] * jnp.maximum(_dot_nt(qe, ki), 0.0)
        kpos = j * tk + col_iota
        adm = (kpos // CHUNK) <= (row_pos // CHUNK)
        keys_ref[j] = jnp.where(adm, _sortable(score), INT_MIN)
        return carry

    lax.fori_loop(0, n_kb, score_block, 0)

    def count_rows(pred_fn):
        def body(j, acc):
            hit = pred_fn(keys_ref[j], j).astype(I32)
            part = hit[:, 0:LANES]
            for c in range(1, tk // LANES):
                part = part + hit[:, c * LANES:(c + 1) * LANES]
            return acc + part
        acc = lax.fori_loop(0, n_kb, body, jnp.zeros((tq, LANES), I32))
        return jnp.sum(acc, axis=1, keepdims=True)

    def bit_step(i, thr):
        cand = thr + jnp.left_shift(jnp.int32(1), 31 - i)
        cnt = count_rows(lambda kb, j: kb >= cand)
        return jnp.where(cnt >= n_sel, cand, thr)

    thr = lax.fori_loop(0, 32, bit_step, jnp.full((tq, 1), INT_MIN, I32))
    n_gt = count_rows(lambda kb, j: kb > thr)
    n_ge = count_rows(lambda kb, j: kb >= thr)
    need = n_sel - n_gt
    real = thr > INT_MIN
    tie = jnp.max(jnp.where(real & (n_ge > n_sel), 1, 0)) > 0

    def tie_cut():
        def idx_step(i, cut):
            cand = cut + jnp.left_shift(jnp.int32(1), idx_bits - 1 - i)
            cnt = count_rows(lambda kb, j: (kb == thr) & ((j * tk + col_iota) < cand))
            return jnp.where(cnt < need, cand, cut)
        return lax.fori_loop(0, idx_bits, idx_step, jnp.zeros((tq, 1), I32))

    cut = lax.cond(tie, tie_cut, lambda: jnp.full((tq, 1), n_kb_max * tk, I32))

    def bias_block(j, carry):
        kb = keys_ref[j]
        kpos = j * tk + col_iota
        sel = (kb > thr) | ((kb == thr) & real & (kpos <= cut))
        bias_ref[j] = jnp.where(sel, 0.0, NEG_BIG)
        return carry

    lax.fori_loop(0, n_kb, bias_block, 0)

    m_ref[...] = jnp.full_like(m_ref, NEG_BIG)
    l_ref[...] = jnp.zeros_like(l_ref)
    acc_ref[...] = jnp.zeros_like(acc_ref)

    def attn_block(j, carry):
        start = pl.multiple_of(j * tk, tk)
        bias = bias_ref[j]
        dist = jnp.abs(row_pos - (j * tk + col_iota)).astype(F32)
        for hp in range(N_PAIRS):
            cols = slice(hp * LANES, (hp + 1) * LANES)
            kp = k_ref[0, pl.ds(start, tk), cols]
            vp = v_ref[0, pl.ds(start, tk), cols]
            for e, qe in enumerate(split_heads(q_ref[0, :, cols])):
                h = 2 * hp + e
                s = _dot_nt(qe, kp) - _alibi_slope2(h) * dist + bias
                m_old = m_ref[h]
                m_new = jnp.maximum(m_old, jnp.max(s, axis=1, keepdims=True))
                alpha = jnp.exp2(m_old - m_new)
                p = jnp.exp2(s - m_new[:, 0:1])
                l_ref[h] = alpha * l_ref[h] + jnp.sum(p, axis=1, keepdims=True)
                acc_ref[h] = alpha * acc_ref[h] + _dot(p.astype(BF16), vp)
                m_ref[h] = m_new
        return carry

    lax.fori_loop(0, n_kb, attn_block, 0)

    for hp in range(N_PAIRS):
        o_lo = acc_ref[2 * hp] / l_ref[2 * hp]
        o_hi = acc_ref[2 * hp + 1] / l_ref[2 * hp + 1]
        o_ref[0, :, hp * LANES:(hp + 1) * LANES] = jnp.where(lo_half, o_lo, o_hi).astype(o_ref.dtype)


def _dsa_attention(q, iq, iw, k, v, ki2, tq, tk, q_off, n_sel):
    b, t, _ = q.shape
    s_len = k.shape[1]
    n_kb_max = s_len // tk
    qspec = lambda w: pl.BlockSpec((1, tq, w), lambda bi, i: (bi, i, 0))
    kspec = lambda w: pl.BlockSpec((1, s_len, w), lambda bi, i: (bi, 0, 0), pipeline_mode=pl.Buffered(1))
    return pl.pallas_call(
        functools.partial(_dsa_kernel, tq=tq, tk=tk, q_off=q_off, n_kb_max=n_kb_max, n_sel=n_sel,
                          idx_bits=max(1, int(math.ceil(math.log2(s_len + 1))))),
        out_shape=jax.ShapeDtypeStruct((b, t, WIDTH), BF16),
        grid=(b, t // tq),
        in_specs=[qspec(WIDTH), qspec(WIDTH), qspec(N_HEADS), kspec(WIDTH), kspec(WIDTH), kspec(LANES)],
        out_specs=qspec(WIDTH),
        scratch_shapes=[pltpu.VMEM((n_kb_max, tq, tk), I32),
                        pltpu.VMEM((n_kb_max, tq, tk), F32),
                        pltpu.VMEM((N_HEADS, tq, LANES), F32),
                        pltpu.VMEM((N_HEADS, tq, LANES), F32),
                        pltpu.VMEM((N_HEADS, tq, LANES), F32)],
        compiler_params=_params(("parallel", "arbitrary")),
        name="dsa_attn",
    )(q, iq, iw, k, v, ki2)


def _dsat_kernel(q_ref, iq_ref, iwt_ref, k_ref, vt_ref, ki_ref, o_ref, keys_ref, dm_ref, qt_ref, iqt_ref, *stats,
                 tq, tk, q_off, n_kb_max, n_sel, idx_bits):
    m_refs, l_refs, acc_refs = stats[:N_HEADS], stats[N_HEADS:2 * N_HEADS], stats[2 * N_HEADS:]
    qi = pl.program_id(1)
    q_lo = q_off + qi * tq
    adm_end = ((q_lo + tq - 1) // CHUNK + 1) * CHUNK
    n_kb = jnp.minimum((adm_end + tk - 1) // tk, n_kb_max)

    lo_half = lax.broadcasted_iota(I32, (tq, LANES), 1) < HEAD_DIM
    qpos = q_lo + lax.broadcasted_iota(I32, (tk, tq), 1)
    krow = lax.broadcasted_iota(I32, (tk, tq), 0)

    def heads_of(x_ref):
        out = []
        for hp in range(N_PAIRS):
            pair = x_ref[0, :, hp * LANES:(hp + 1) * LANES]
            zero = jnp.zeros_like(pair)
            out += [jnp.where(lo_half, pair, zero), jnp.where(lo_half, zero, pair)]
        return out

    for src, dst in ((iq_ref, iqt_ref), (q_ref, qt_ref)):
        for h, xe in enumerate(heads_of(src)):
            dst[h] = xe.astype(F32).T.astype(BF16)

    def score_block(j, carry):
        ki = ki_ref[0, pl.ds(pl.multiple_of(j * tk, tk), tk), :]
        score = jnp.zeros((tk, tq), F32)
        for h in range(N_HEADS):
            score = score + iwt_ref[0, h:h + 1, :] * jnp.maximum(_dot(ki, iqt_ref[h]), 0.0)
        adm = ((j * tk + krow) // CHUNK) <= (qpos // CHUNK)
        keys_ref[j] = jnp.where(adm, _sortable(score), INT_MIN)
        return carry

    lax.fori_loop(0, n_kb, score_block, 0)

    def count_cols(pred_fn):
        def body(j, acc):
            hit = pred_fn(keys_ref[j], j).astype(I32)
            return acc + jnp.sum(hit.reshape(tk // 8, 8, tq), axis=0)
        acc = lax.fori_loop(0, n_kb, body, jnp.zeros((8, tq), I32))
        return jnp.sum(acc, axis=0, keepdims=True)

    def bit_step(i, thr):
        cand = thr + jnp.left_shift(jnp.int32(1), 31 - i)
        cnt = count_cols(lambda kb, j: kb >= cand)
        return jnp.where(cnt >= n_sel, cand, thr)

    thr = lax.fori_loop(0, 32, bit_step, jnp.full((1, tq), INT_MIN, I32))
    n_gt = count_cols(lambda kb, j: kb > thr)
    n_ge = count_cols(lambda kb, j: kb >= thr)
    need = n_sel - n_gt
    real = thr > INT_MIN
    tie = jnp.max(jnp.where(real & (n_ge > n_sel), 1, 0)) > 0

    def tie_cut():
        def idx_step(i, cut):
            cand = cut + jnp.left_shift(jnp.int32(1), idx_bits - 1 - i)
            cnt = count_cols(lambda kb, j: (kb == thr) & ((j * tk + krow) < cand))
            return jnp.where(cnt < need, cand, cut)
        return lax.fori_loop(0, idx_bits, idx_step, jnp.zeros((1, tq), I32))

    cut = lax.cond(tie, tie_cut, lambda: jnp.full((1, tq), n_kb_max * tk, I32))

    def mask_block(j, carry):
        kb = keys_ref[j]
        kpos = j * tk + krow
        sel = (kb > thr) | ((kb == thr) & real & (kpos <= cut))
        dm_ref[j] = jnp.where(sel, jnp.abs(qpos - kpos).astype(F32), DIST_MASKED)
        return carry

    lax.fori_loop(0, n_kb, mask_block, 0)

    for h in range(N_HEADS):
        m_refs[h][...] = jnp.full_like(m_refs[h], NEG_BIG)
        l_refs[h][...] = jnp.zeros_like(l_refs[h])
        acc_refs[h][...] = jnp.zeros_like(acc_refs[h])

    def attn_block(j, carry):
        start = pl.multiple_of(j * tk, tk)
        dm = dm_ref[j]

        def qk(h):
            return _dot(k_ref[0, pl.ds(start, tk), (h // 2) * LANES:(h // 2 + 1) * LANES], qt_ref[h])

        ahead = 2
        pending = [qk(h) for h in range(ahead)]
        for hp in range(N_PAIRS):
            for e in range(2):
                h = 2 * hp + e
                s_this = pending.pop(0)
                if h + ahead < N_HEADS:
                    pending.append(qk(h + ahead))
                t2 = s_this - _alibi_slope2(h) * dm
                m_old = m_refs[h][...]
                m_new = jnp.maximum(m_old, jnp.max(t2, axis=0, keepdims=True))
                alpha = jnp.exp2(m_old - m_new)
                p = jnp.exp2(t2 - m_new)
                l_refs[h][...] = alpha * l_refs[h][...] + jnp.sum(p, axis=0, keepdims=True)
                vt = vt_ref[0, j, hp * LANES + e * HEAD_DIM:hp * LANES + (e + 1) * HEAD_DIM, :]
                acc_refs[h][...] = alpha * acc_refs[h][...] + _dot(vt, p.astype(BF16))
                m_refs[h][...] = m_new
        return carry

    lax.fori_loop(0, n_kb, attn_block, 0)

    for hp in range(N_PAIRS):
        ot = jnp.concatenate([acc_refs[2 * hp][...] / l_refs[2 * hp][...],
                              acc_refs[2 * hp + 1][...] / l_refs[2 * hp + 1][...]], axis=0)
        o_ref[0, :, hp * LANES:(hp + 1) * LANES] = ot.T.astype(o_ref.dtype)


def _dsa_attention_t(q, iq, iw, k, v, ki2, tq, tk, q_off, n_sel):
    b, t, _ = q.shape
    s_len = k.shape[1]
    n_kb_max = s_len // tk
    iwt = jnp.swapaxes(iw, 1, 2)
    vt = jnp.swapaxes(v.reshape(b, n_kb_max, tk, WIDTH), 2, 3)
    qspec = lambda w: pl.BlockSpec((1, tq, w), lambda bi, i: (bi, i, 0))
    kspec = lambda w: pl.BlockSpec((1, s_len, w), lambda bi, i: (bi, 0, 0), pipeline_mode=pl.Buffered(1))
    return pl.pallas_call(
        functools.partial(_dsat_kernel, tq=tq, tk=tk, q_off=q_off, n_kb_max=n_kb_max, n_sel=n_sel,
                          idx_bits=max(1, int(math.ceil(math.log2(s_len + 1))))),
        out_shape=jax.ShapeDtypeStruct((b, t, WIDTH), BF16),
        grid=(b, t // tq),
        in_specs=[qspec(WIDTH), qspec(WIDTH),
                  pl.BlockSpec((1, N_HEADS, tq), lambda bi, i: (bi, 0, i)),
                  kspec(WIDTH),
                  pl.BlockSpec((1, n_kb_max, WIDTH, tk), lambda bi, i: (bi, 0, 0, 0), pipeline_mode=pl.Buffered(1)),
                  kspec(LANES)],
        out_specs=qspec(WIDTH),
        scratch_shapes=[pltpu.VMEM((n_kb_max, tk, tq), I32), pltpu.VMEM((n_kb_max, tk, tq), F32),
                        pltpu.VMEM((N_HEADS, LANES, tq), BF16), pltpu.VMEM((N_HEADS, LANES, tq), BF16)]
                       + [pltpu.VMEM((1, tq), F32)] * (2 * N_HEADS) + [pltpu.VMEM((HEAD_DIM, tq), F32)] * N_HEADS,
        compiler_params=_params(("parallel", "arbitrary")),
        name="dsa_attn_t",
    )(q, iq, iwt, k, vt, ki2)


def _mix_kernel(oa_ref, ob_ref, ga_ref, gb_ref, x_ref, mod_ref, gpost_ref, gpre_ref,
                wa_ref, wb_ref, wo_ref, wq_ref, x1_ref, h2_ref, pq_ref):
    mod = mod_ref[0]
    mixed = ga_ref[...] * _dot(oa_ref[...], wa_ref[...]) + gb_ref[...] * _dot(ob_ref[...], wb_ref[...])
    y = _dot(mixed.astype(BF16), wo_ref[...])
    x1 = x_ref[...] + mod[2:3, :] * _rms(y, gpost_ref[...])
    x1_ref[...] = x1
    h2 = _rms(x1, gpre_ref[...]) * (1.0 + mod[4:5, :]) + mod[3:4, :]
    h2_ref[...] = h2
    pq_ref[...] = _dot(h2.astype(BF16), wq_ref[...]).astype(BF16)


def _mix(oa, ob, ga, gb, x2, mod3, g_post, g_pre, wa, wb, wo, wq, tm, tiles_per_seq):
    n = x2.shape[0]
    row = lambda i: (i, 0)
    const = lambda i: (0, 0)
    nq = wq.shape[1]
    wspec = lambda r, c: pl.BlockSpec((r, c), const, pipeline_mode=pl.Buffered(1))
    return pl.pallas_call(
        _mix_kernel,
        out_shape=[jax.ShapeDtypeStruct((n, D_MODEL), F32),
                   jax.ShapeDtypeStruct((n, D_MODEL), F32),
                   jax.ShapeDtypeStruct((n, nq), BF16)],
        grid=(n // tm,),
        in_specs=[pl.BlockSpec((tm, WIDTH), row), pl.BlockSpec((tm, WIDTH), row),
                  pl.BlockSpec((tm, D_MODEL), row), pl.BlockSpec((tm, D_MODEL), row),
                  pl.BlockSpec((tm, D_MODEL), row),
                  pl.BlockSpec((1, 6, D_MODEL), lambda i: (i // tiles_per_seq, 0, 0)),
                  pl.BlockSpec((1, D_MODEL), const), pl.BlockSpec((1, D_MODEL), const),
                  wspec(WIDTH, D_MODEL), wspec(WIDTH, D_MODEL), wspec(D_MODEL, D_MODEL),
                  wspec(D_MODEL, nq)],
        out_specs=[pl.BlockSpec((tm, D_MODEL), row), pl.BlockSpec((tm, D_MODEL), row),
                   pl.BlockSpec((tm, nq), row)],
        compiler_params=_params(("parallel",)),
        name="mix",
    )(oa, ob, ga, gb, x2, mod3, g_post, g_pre, wa, wb, wo, wq)


def _top16(s, payload=None):
    n = s.shape[0]
    pos = lax.broadcasted_iota(I32, s.shape, 0)
    vals, picks = [], []
    for _ in range(PEER_TOPK):
        m = jnp.max(s, axis=0, keepdims=True)
        p = jnp.min(jnp.where(s == m, pos, n), axis=0, keepdims=True)
        hit = pos == p
        vals.append(m)
        picks.append(p if payload is None else jnp.sum(jnp.where(hit, payload, 0), axis=0, keepdims=True))
        s = jnp.where(hit, -jnp.inf, s)
    return jnp.concatenate(vals, axis=0), jnp.concatenate(picks, axis=0)


def _peersel_kernel(pq_ref, sk_ref, idx_ref, g_ref):
    pq = pq_ref[...]
    s1 = _dot_nt(sk_ref[0, 0], pq[:, :PEER_DHALF])
    s2 = _dot_nt(sk_ref[0, 1], pq[:, PEER_DHALF:])
    v1, i1 = _top16(s1)
    v2, i2 = _top16(s2)
    half = PEER_TOPK // 2
    row8 = lax.broadcasted_iota(I32, (half, pq.shape[0]), 0)
    cand = [v1[0:1] + v2]
    cidx = [i1[0:1] * PEER_NKEYS + i2]
    for i in range(1, half):
        cand.append(jnp.where(row8 < PEER_TOPK // (i + 1), v1[i:i + 1] + v2[0:half], -jnp.inf))
        cidx.append(i1[i:i + 1] * PEER_NKEYS + i2[0:half])
    cand.append(v1[half:] + v2[0:1])
    cidx.append(i1[half:] * PEER_NKEYS + i2[0:1])
    top, eidx = _top16(jnp.concatenate(cand, axis=0), jnp.concatenate(cidx, axis=0))
    ex = jnp.exp(top - top[0:1])
    g_ref[0] = ex / jnp.sum(ex, axis=0, keepdims=True)
    idx_ref[0] = eidx


def _peer_select(pq, subkeys_b, tt):
    n = pq.shape[0]
    return pl.pallas_call(
        _peersel_kernel,
        out_shape=[jax.ShapeDtypeStruct((PEER_HEADS, PEER_TOPK, n), I32),
                   jax.ShapeDtypeStruct((PEER_HEADS, PEER_TOPK, n), F32)],
        grid=(n // tt, PEER_HEADS),
        in_specs=[pl.BlockSpec((tt, 2 * PEER_DHALF), lambda i, h: (i, h)),
                  pl.BlockSpec((1, 2, PEER_NKEYS, PEER_DHALF), lambda i, h: (h, 0, 0, 0))],
        out_specs=[pl.BlockSpec((1, PEER_TOPK, tt), lambda i, h: (h, 0, i)),
                   pl.BlockSpec((1, PEER_TOPK, tt), lambda i, h: (h, 0, i))],
        compiler_params=_params(("parallel", "arbitrary")),
        name="peer_select",
    )(pq, subkeys_b)


def _erf_gelu(a):
    return 0.5 * a * (1.0 + lax.erf(a * (2.0 ** -0.5)))


def _peerffn_kernel(idx_ref, gt_ref, h2_ref, x1_ref, mod_ref, gpost_ref, uv_hbm, o_ref,
                    *scratch, tg, seqs_per_tile):
    bufs, (f_ref, sem) = scratch[:PEER_SLOTS], scratch[PEER_SLOTS:]

    def start_token(t, slot):
        tt = jnp.minimum(t, tg - 1)
        for k in range(PEER_SEL):
            pltpu.make_async_copy(uv_hbm.at[idx_ref[tt, k]], bufs[slot].at[:, pl.ds(k, 1), :],
                                  sem.at[slot]).start(priority=k % 2)

    def wait_token(slot):
        pltpu.make_async_copy(bufs[slot], bufs[slot], sem.at[slot]).wait()

    lane_tok = lax.broadcasted_iota(I32, (PEER_SEL, tg), 1)
    sub8 = lax.broadcasted_iota(I32, (8, D_MODEL), 0)

    def compute(t, slot):
        buf = bufs[slot]
        t8 = pl.multiple_of((t // 8) * 8, 8)
        h8 = h2_ref[pl.ds(t8, 8), :]
        hrow = jnp.sum(jnp.where(sub8 == t - t8, h8, 0.0), axis=0, keepdims=True)
        acc = None
        for c in range(N_CHUNK):
            term = buf[c] * hrow[:, c * LANES:(c + 1) * LANES]
            acc = term if acc is None else acc + term
        a = jnp.sum(acc, axis=1, keepdims=True)
        g = jnp.sum(jnp.where(lane_tok == t, gt_ref[...], 0.0), axis=1, keepdims=True)
        w = g * _erf_gelu(a)
        pieces = []
        for c in range(N_CHUNK):
            vc = buf[N_CHUNK + c]
            pieces.append(jnp.sum(vc * w, axis=0, keepdims=True))
        f_ref[pl.ds(t, 1), :] = jnp.concatenate(pieces, axis=1)

    for t in range(PEER_AHEAD):
        start_token(t, t)

    def group(i, carry):
        t0 = PEER_SLOTS * i
        for j in range(PEER_SLOTS):
            wait_token(j)
            start_token(t0 + j + PEER_AHEAD, (j + PEER_AHEAD) % PEER_SLOTS)
            compute(t0 + j, j)
        return carry

    lax.fori_loop(0, tg // PEER_SLOTS, group, 0)
    for t in range(PEER_AHEAD):
        wait_token(t)
    rows = tg // seqs_per_tile
    for s in range(seqs_per_tile):
        sl = slice(s * rows, (s + 1) * rows)
        o_ref[sl, :] = x1_ref[sl, :] + mod_ref[s][5:6, :] * _rms(f_ref[sl, :], gpost_ref[...])


def _peer_ffn(idx, gt, h2, x1, mod3, g_post, uv, tg, seq_len):
    n = h2.shape[0]
    row = lambda i: (i, 0)
    seqs_per_tile = max(1, tg // seq_len)
    tiles_per_seq = max(1, seq_len // tg)
    return pl.pallas_call(
        functools.partial(_peerffn_kernel, tg=tg, seqs_per_tile=seqs_per_tile),
        out_shape=jax.ShapeDtypeStruct((n, D_MODEL), F32),
        grid=(n // tg,),
        in_specs=[pl.BlockSpec((tg, PEER_SEL), row, memory_space=pltpu.SMEM),
                  pl.BlockSpec((PEER_SEL, tg), lambda i: (0, i)),
                  pl.BlockSpec((tg, D_MODEL), row),
                  pl.BlockSpec((tg, D_MODEL), row),
                  pl.BlockSpec((seqs_per_tile, 6, D_MODEL), lambda i: (i // tiles_per_seq, 0, 0)),
                  pl.BlockSpec((1, D_MODEL), lambda i: (0, 0)),
                  pl.BlockSpec(memory_space=pl.ANY)],
        out_specs=pl.BlockSpec((tg, D_MODEL), row),
        scratch_shapes=[pltpu.VMEM((2 * N_CHUNK, PEER_SEL, LANES), F32)] * PEER_SLOTS
                       + [pltpu.VMEM((tg, D_MODEL), F32), pltpu.SemaphoreType.DMA((PEER_SLOTS,))],
        compiler_params=_params(("arbitrary",)),
        name="peer_ffn",
    )(idx, gt, h2, x1, mod3, g_post, uv)


def _pad_keys(x, s_pad):
    return jnp.pad(x, ((0, 0), (0, s_pad - x.shape[1]), (0, 0)))


def _layer(x, mod3, past, weights, tiles):
    (g_pre_mix, g_post_mix, g_pre_ffn, g_post_ffn, w_in_b, w_gate_b, wa_b, wb_b, wo_b, wq_b,
     subkeys_b, peer_uv) = weights
    tm, sb_tq, sb_tk, dsa_tq, dsa_tk, tt, tg = tiles
    bsz, t, _ = x.shape
    n = bsz * t
    x2 = x.reshape(n, D_MODEL)
    (aq, ak, av, bq, bk, bv, iq, ki2, akf, avf, bkf, bvf, ikf, iw, ga, gb) = _inproj(
        x2, mod3, g_pre_mix, w_in_b, w_gate_b, tm, t // tm)
    seq = lambda a: a.reshape(bsz, t, a.shape[-1])
    if past is None:
        q_off = 0
        ka, va, kb, vb, ki = seq(ak), seq(av), seq(bk), seq(bv), seq(ki2)
        s_len = t
    else:
        p_sbk, p_sbv, p_dk, p_dv, p_ki = past
        q_off = p_sbk.shape[1]
        s_len = q_off + t
        flat = lambda c: c.reshape(bsz, q_off, WIDTH).astype(BF16)
        ka = jnp.concatenate([flat(p_sbk), seq(ak)], axis=1)
        va = jnp.concatenate([flat(p_sbv), seq(av)], axis=1)
        kb = jnp.concatenate([flat(p_dk), seq(bk)], axis=1)
        vb = jnp.concatenate([flat(p_dv), seq(bv)], axis=1)
        pk = p_ki.astype(BF16)
        ki = jnp.concatenate([jnp.concatenate([pk, pk], axis=-1), seq(ki2)], axis=1)
    s_pad = -(-s_len // max(sb_tk, dsa_tk)) * max(sb_tk, dsa_tk)
    ka, va, kb, vb, ki = (_pad_keys(a, s_pad) for a in (ka, va, kb, vb, ki))
    n_sel = min(TOPK_MAX, s_len // 4)

    oa = _sb_attention(seq(aq), ka, va, sb_tq, sb_tk, q_off)
    dsa = _dsa_attention_t if dsa_tq % LANES == 0 else _dsa_attention
    ob = dsa(seq(bq), seq(iq), seq(iw), kb, vb, ki, dsa_tq, dsa_tk, q_off, n_sel)

    x1, h2, pq = _mix(oa.reshape(n, WIDTH), ob.reshape(n, WIDTH), ga, gb, x2, mod3, g_post_mix, g_pre_ffn,
                      wa_b, wb_b, wo_b, wq_b, tm, t // tm)
    eidx, gate = _peer_select(pq, subkeys_b, tt)
    eidx = eidx.reshape(PEER_SEL, n).T
    y = _peer_ffn(eidx, gate.reshape(PEER_SEL, n), h2, x1, mod3, g_post_ffn, peer_uv, tg, t)
    new = tuple(a.reshape(1, bsz, t, N_HEADS, HEAD_DIM) for a in (akf, avf, bkf, bvf)) + (
        ikf.reshape(1, bsz, t, HEAD_DIM),)
    return y.reshape(bsz, t, D_MODEL), new


def _tiles(t):
    if t % 256 == 0:
        return (256, 256, 256, 256, 256, 128, 128)
    return (t, t, 128, t, 128, 128, 128)


def kernel(x_prompt, x_sample, c_prompt, c_sample, cache_sb_k, cache_sb_v, cache_dsa_k, cache_dsa_v, cache_dsa_kidx, w_ada, b_ada, g_pre_mix, g_post_mix, g_pre_ffn, g_post_ffn, w_in, w_gate, w_branch_a, w_branch_b, w_out, w_peer_q, peer_subkeys, peer_u, peer_v):
    assert w_ada.shape[0] == 1, "one layer"
    n_p, n_s = c_prompt.shape[0], c_sample.shape[0]
    c_all = jnp.concatenate([c_prompt, c_sample], axis=0)
    rows = -(-c_all.shape[0] // 8) * 8
    c_all = jnp.pad(c_all, ((0, rows - c_all.shape[0]), (0, 0)))
    mod = _modulation(c_all, w_ada[0], b_ada[0]).reshape(rows, 6, D_MODEL)
    w_in_b = jnp.pad(w_in[0], ((0, 0), (0, W_IN_PAD - W_IN_COLS))).astype(BF16)
    weights = (g_pre_mix, g_post_mix, g_pre_ffn, g_post_ffn, w_in_b, w_gate[0].astype(BF16),
               w_branch_a[0].astype(BF16), w_branch_b[0].astype(BF16), w_out[0].astype(BF16),
               w_peer_q[0].astype(BF16), peer_subkeys[0].astype(BF16),
               jnp.concatenate([peer_u[0], peer_v[0]], axis=1).reshape(-1, 2 * N_CHUNK, 1, LANES))
    yp, new_p = _layer(x_prompt, mod[:n_p], None, weights, _tiles(x_prompt.shape[1]))
    past = (cache_sb_k[0], cache_sb_v[0], cache_dsa_k[0], cache_dsa_v[0], cache_dsa_kidx[0])
    ys, new_s = _layer(x_sample, mod[n_p:n_p + n_s], past, weights, _tiles(x_sample.shape[1]))
    return (yp, ys) + new_p + new_s
```

```python
import functools
import math

import jax
import jax.numpy as jnp
import numpy as np
from jax import lax
from jax.experimental import pallas as pl
from jax.experimental.pallas import tpu as pltpu

F32 = jnp.float32
BF16 = jnp.bfloat16
I32 = jnp.int32

D_MODEL = 1024
CHUNK = 64
EPS = 1e-6
N_HEADS = 8
HEAD_DIM = 64
WIDTH = N_HEADS * HEAD_DIM
N_PAIRS = N_HEADS // 2
SB_PAIRS_PER_STEP = 4
LANES = 128
N_CHUNK = D_MODEL // LANES
TOPK_MAX = 256
PEER_HEADS = 8
PEER_NKEYS = 128
PEER_DHALF = 128
PEER_TOPK = 16
PEER_SEL = PEER_HEADS * PEER_TOPK
PEER_SELECT_HEADS = 4
PEER_SLOTS = 8
PEER_AHEAD = 6
W_IN_COLS = 7 * WIDTH + HEAD_DIM + N_HEADS
W_IN_PAD = 7 * WIDTH + LANES
INT_MIN = -(2 ** 31)
NEG_BIG = -1e30
DIST_MASKED = 1e30
LOG2E = 1.4426950408889634
VMEM_LIMIT = 56 * 1024 * 1024


def _params(sem):
    return pltpu.CompilerParams(dimension_semantics=sem, vmem_limit_bytes=VMEM_LIMIT)


def _dot(a, b):
    return jnp.dot(a, b, preferred_element_type=F32)


def _dot_nt(a, b):
    return lax.dot_general(a, b, (((1,), (1,)), ((), ())), preferred_element_type=F32)


def _rms(x, gain):
    return x * lax.rsqrt(jnp.mean(x * x, axis=-1, keepdims=True) + EPS) * gain


def _mod_kernel(c_ref, w_ref, b_ref, o_ref):
    c = c_ref[...]
    s = c * (1.0 / (1.0 + jnp.exp(-c)))
    o_ref[...] = jnp.dot(s, w_ref[...], preferred_element_type=F32,
                         precision=lax.Precision.HIGHEST) + b_ref[...]


def _modulation(c, w_ada, b_ada):
    n = c.shape[0]
    cols = w_ada.shape[1]
    tn = 1024
    return pl.pallas_call(
        _mod_kernel,
        out_shape=jax.ShapeDtypeStruct((n, cols), F32),
        grid=(cols // tn,),
        in_specs=[pl.BlockSpec((n, D_MODEL), lambda j: (0, 0)),
                  pl.BlockSpec((D_MODEL, tn), lambda j: (0, j)),
                  pl.BlockSpec((1, tn), lambda j: (0, j))],
        out_specs=pl.BlockSpec((n, tn), lambda j: (0, j)),
        compiler_params=_params(("arbitrary",)),
        name="mod",
    )(c, w_ada, b_ada.reshape(1, cols))


def _inproj_kernel(x_ref, mod_ref, g_ref, win_ref, wg_ref,
                   aq_ref, ak_ref, av_ref, bq_ref, bk_ref, bv_ref, iq_ref, ki2_ref,
                   akf_ref, avf_ref, bkf_ref, bvf_ref, ikf_ref, iw_ref, ga_ref, gb_ref):
    x = x_ref[...]
    mod = mod_ref[0]
    h = _rms(x, g_ref[...]) * (1.0 + mod[1:2, :]) + mod[0:1, :]
    hb = h.astype(BF16)
    qscale = HEAD_DIM ** -0.5

    def seg(i):
        return _dot(hb, win_ref[:, i * WIDTH:(i + 1) * WIDTH])

    aq_ref[...] = (seg(0) * (qscale * LOG2E)).astype(BF16)
    z = seg(1)
    akf_ref[...] = z
    ak_ref[...] = z.astype(BF16)
    z = seg(2)
    avf_ref[...] = z
    av_ref[...] = z.astype(BF16)
    bq_ref[...] = (seg(3) * (qscale * LOG2E)).astype(BF16)
    z = seg(4)
    bkf_ref[...] = z
    bk_ref[...] = z.astype(BF16)
    z = seg(5)
    bvf_ref[...] = z
    bv_ref[...] = z.astype(BF16)
    iq_ref[...] = (seg(6) * qscale).astype(BF16)
    tail = _dot(hb, win_ref[:, 7 * WIDTH:7 * WIDTH + LANES])
    ik = tail[:, :HEAD_DIM]
    ikf_ref[...] = ik
    ki2_ref[...] = jnp.concatenate([ik, ik], axis=-1).astype(BF16)
    iw_ref[...] = tail[:, HEAD_DIM:HEAD_DIM + N_HEADS] * (N_HEADS ** -0.5)
    zg = _dot(hb, wg_ref[...])
    gate = 1.0 / (1.0 + jnp.exp(-zg))
    ga_ref[...] = gate[:, :D_MODEL]
    gb_ref[...] = gate[:, D_MODEL:]


def _inproj(x2, mod3, g_pre, w_in_b, w_gate_b, tm, tiles_per_seq):
    n = x2.shape[0]
    row = lambda i: (i, 0)
    const = lambda i: (0, 0)
    bspec = lambda w: pl.BlockSpec((tm, w), row)
    outs = ([jax.ShapeDtypeStruct((n, WIDTH), BF16)] * 7
            + [jax.ShapeDtypeStruct((n, LANES), BF16)]
            + [jax.ShapeDtypeStruct((n, WIDTH), F32)] * 4
            + [jax.ShapeDtypeStruct((n, HEAD_DIM), F32),
               jax.ShapeDtypeStruct((n, N_HEADS), F32),
               jax.ShapeDtypeStruct((n, D_MODEL), F32),
               jax.ShapeDtypeStruct((n, D_MODEL), F32)])
    out_specs = ([bspec(WIDTH)] * 7 + [bspec(LANES)] + [bspec(WIDTH)] * 4
                 + [bspec(HEAD_DIM), bspec(N_HEADS), bspec(D_MODEL), bspec(D_MODEL)])
    return pl.pallas_call(
        _inproj_kernel,
        out_shape=outs,
        grid=(n // tm,),
        in_specs=[pl.BlockSpec((tm, D_MODEL), row),
                  pl.BlockSpec((1, 6, D_MODEL), lambda i: (i // tiles_per_seq, 0, 0)),
                  pl.BlockSpec((1, D_MODEL), const),
                  pl.BlockSpec((D_MODEL, W_IN_PAD), const, pipeline_mode=pl.Buffered(1)),
                  pl.BlockSpec((D_MODEL, 2 * D_MODEL), const, pipeline_mode=pl.Buffered(1))],
        out_specs=out_specs,
        compiler_params=_params(("parallel",)),
        name="inproj",
    )(x2, mod3, g_pre, w_in_b, w_gate_b)


def _sb_kernel(tab_ref, q_ref, k_ref, v_ref, tri_ref, o_ref, *scratch, tq, tk, q_off, pairs):
    s = pl.program_id(2)
    qi = tab_ref[0, s]
    kj = tab_ref[1, s]
    first = tab_ref[2, s]
    last = tab_ref[3, s]
    masked = tab_ref[4, s]
    n_heads = 2 * pairs
    acc_refs, run_refs = scratch[:n_heads], scratch[n_heads:]

    @pl.when(first == 1)
    def _():
        for r in scratch:
            r[...] = jnp.zeros_like(r)

    lo_lanes = lax.broadcasted_iota(I32, (tq, LANES), 1) < HEAD_DIM

    def step(use_mask):
        tri = tri_ref[...]
        if use_mask:
            qpos = q_off + qi * tq + lax.broadcasted_iota(I32, (tq, tk), 0)
            kpos = kj * tk + lax.broadcasted_iota(I32, (tq, tk), 1)
            causal = kpos < qpos
        z = []
        for p in range(pairs):
            q = q_ref[0, :, p * LANES:(p + 1) * LANES]
            k = k_ref[0, :, p * LANES:(p + 1) * LANES]
            zero = jnp.zeros_like(q)
            z += [_dot_nt(jnp.where(lo_lanes, q, zero), k), _dot_nt(jnp.where(lo_lanes, zero, q), k)]
        suf = []
        for h in range(n_heads):
            sp = jnp.maximum(z[h], 0.0) + jnp.log2(1.0 + jnp.exp2(-jnp.abs(z[h])))
            if use_mask:
                sp = jnp.where(causal, sp, 0.0)
            hi = sp.astype(BF16)
            lo = (sp - hi.astype(F32)).astype(BF16)
            suf.append(_dot(hi, tri) + _dot(lo, tri) + run_refs[h][:, 0:1])
        for h in range(n_heads):
            w = jnp.exp2(z[h] - suf[h])
            if use_mask:
                w = jnp.where(causal, w, 0.0)
            p = h // 2
            acc_refs[h][...] += _dot(w.astype(BF16), v_ref[0, :, p * LANES:(p + 1) * LANES])
            run_refs[h][...] = jnp.broadcast_to(suf[h][:, 0:1], (tq, LANES))

    @pl.when(masked == 1)
    def _():
        step(True)

    @pl.when(masked == 0)
    def _():
        step(False)

    @pl.when(last == 1)
    def _():
        for p in range(pairs):
            o_ref[0, :, p * LANES:(p + 1) * LANES] = jnp.where(
                lo_lanes, acc_refs[2 * p][...], acc_refs[2 * p + 1][...]).astype(o_ref.dtype)


def _sb_table(nq, nk, tq, tk, q_off):
    rows = []
    for qi in range(nq):
        q_lo = q_off + qi * tq
        q_hi = q_lo + tq - 1
        kjs = [kj for kj in range(nk) if kj * tk < q_hi]
        if not kjs:
            kjs = [0]
        kjs = kjs[::-1]
        for n, kj in enumerate(kjs):
            masked = int(kj * tk + tk - 1 >= q_lo)
            rows.append((qi, kj, int(n == 0), int(n == len(kjs) - 1), masked))
    return np.asarray(rows, dtype=np.int32).T.copy()


def _sb_attention(q, k, v, tq, tk, q_off):
    b, t, _ = q.shape
    s_len = k.shape[1]
    tab = _sb_table(t // tq, s_len // tk, tq, tk, q_off)
    tri = jnp.asarray(np.tril(np.ones((tk, tk), np.float32)), dtype=BF16)
    pairs = SB_PAIRS_PER_STEP
    width = pairs * LANES
    grid_spec = pltpu.PrefetchScalarGridSpec(
        num_scalar_prefetch=1,
        grid=(b, N_PAIRS // pairs, tab.shape[1]),
        in_specs=[pl.BlockSpec((1, tq, width), lambda bi, hp, s, tab: (bi, tab[0, s], hp)),
                  pl.BlockSpec((1, tk, width), lambda bi, hp, s, tab: (bi, tab[1, s], hp)),
                  pl.BlockSpec((1, tk, width), lambda bi, hp, s, tab: (bi, tab[1, s], hp)),
                  pl.BlockSpec((tk, tk), lambda bi, hp, s, tab: (0, 0))],
        out_specs=pl.BlockSpec((1, tq, width), lambda bi, hp, s, tab: (bi, tab[0, s], hp)),
        scratch_shapes=[pltpu.VMEM((tq, LANES), F32)] * (4 * pairs),
    )
    return pl.pallas_call(
        functools.partial(_sb_kernel, tq=tq, tk=tk, q_off=q_off, pairs=pairs),
        out_shape=jax.ShapeDtypeStruct((b, t, WIDTH), BF16),
        grid_spec=grid_spec,
        compiler_params=_params(("parallel", "parallel", "arbitrary")),
        name="sb_attn",
    )(jnp.asarray(tab), q, k, v, tri)


def _sortable(x):
    bits = pltpu.bitcast(x + 0.0, I32)
    return jnp.where(bits < 0, bits ^ 0x7FFFFFFF, bits)


def _alibi_slope2(h):
    return LOG2E * 2.0 ** (-8.0 * (h + 1) / N_HEADS)


def _dsa_kernel(q_ref, iq_ref, iwt_ref, k_ref, vt_ref, ki_ref, o_ref, keys_ref, dm_ref, qt_ref, iqt_ref, *stats,
                 tq, tk, q_off, n_kb_max, n_sel, idx_bits):
    m_refs, l_refs, acc_refs = stats[:N_HEADS], stats[N_HEADS:2 * N_HEADS], stats[2 * N_HEADS:]
    qi = pl.program_id(1)
    q_lo = q_off + qi * tq
    adm_end = ((q_lo + tq - 1) // CHUNK + 1) * CHUNK
    n_kb = jnp.minimum((adm_end + tk - 1) // tk, n_kb_max)

    lo_half = lax.broadcasted_iota(I32, (tq, LANES), 1) < HEAD_DIM
    qpos = q_lo + lax.broadcasted_iota(I32, (tk, tq), 1)
    krow = lax.broadcasted_iota(I32, (tk, tq), 0)

    def heads_of(x_ref):
        out = []
        for hp in range(N_PAIRS):
            pair = x_ref[0, :, hp * LANES:(hp + 1) * LANES]
            zero = jnp.zeros_like(pair)
            out += [jnp.where(lo_half, pair, zero), jnp.where(lo_half, zero, pair)]
        return out

    for src, dst in ((iq_ref, iqt_ref), (q_ref, qt_ref)):
        for h, xe in enumerate(heads_of(src)):
            dst[h] = xe.astype(F32).T.astype(BF16)

    def score_block(j, carry):
        ki = ki_ref[0, pl.ds(pl.multiple_of(j * tk, tk), tk), :]
        score = jnp.zeros((tk, tq), F32)
        for h in range(N_HEADS):
            score = score + iwt_ref[0, h:h + 1, :] * jnp.maximum(_dot(ki, iqt_ref[h]), 0.0)
        adm = ((j * tk + krow) // CHUNK) <= (qpos // CHUNK)
        keys_ref[j] = jnp.where(adm, _sortable(score), INT_MIN)
        return carry

    lax.fori_loop(0, n_kb, score_block, 0)

    def count_cols(pred_fn):
        def body(j, acc):
            hit = pred_fn(keys_ref[j], j).astype(I32)
            return acc + jnp.sum(hit.reshape(tk // 8, 8, tq), axis=0)
        acc = lax.fori_loop(0, n_kb, body, jnp.zeros((8, tq), I32))
        return jnp.sum(acc, axis=0, keepdims=True)

    def bit_step(i, thr):
        cand = thr + jnp.left_shift(jnp.int32(1), 31 - i)
        cnt = count_cols(lambda kb, j: kb >= cand)
        return jnp.where(cnt >= n_sel, cand, thr)

    thr = lax.fori_loop(0, 32, bit_step, jnp.full((1, tq), INT_MIN, I32))
    n_gt = count_cols(lambda kb, j: kb > thr)
    n_ge = count_cols(lambda kb, j: kb >= thr)
    need = n_sel - n_gt
    real = thr > INT_MIN
    tie = jnp.max(jnp.where(real & (n_ge > n_sel), 1, 0)) > 0

    def tie_cut():
        def idx_step(i, cut):
            cand = cut + jnp.left_shift(jnp.int32(1), idx_bits - 1 - i)
            cnt = count_cols(lambda kb, j: (kb == thr) & ((j * tk + krow) < cand))
            return jnp.where(cnt < need, cand, cut)
        return lax.fori_loop(0, idx_bits, idx_step, jnp.zeros((1, tq), I32))

    cut = lax.cond(tie, tie_cut, lambda: jnp.full((1, tq), n_kb_max * tk, I32))

    def mask_block(j, carry):
        kb = keys_ref[j]
        kpos = j * tk + krow
        sel = (kb > thr) | ((kb == thr) & real & (kpos <= cut))
        dm_ref[j] = jnp.where(sel, jnp.abs(qpos - kpos).astype(F32), DIST_MASKED)
        return carry

    lax.fori_loop(0, n_kb, mask_block, 0)

    for h in range(N_HEADS):
        m_refs[h][...] = jnp.full_like(m_refs[h], NEG_BIG)
        l_refs[h][...] = jnp.zeros_like(l_refs[h])
        acc_refs[h][...] = jnp.zeros_like(acc_refs[h])

    def attn_block(j, carry):
        start = pl.multiple_of(j * tk, tk)
        dm = dm_ref[j]

        def qk(h):
            return _dot(k_ref[0, pl.ds(start, tk), (h // 2) * LANES:(h // 2 + 1) * LANES], qt_ref[h])

        ahead = 4
        pending = [qk(h) for h in range(ahead)]
        for hp in range(N_PAIRS):
            for e in range(2):
                h = 2 * hp + e
                s_this = pending.pop(0)
                if h + ahead < N_HEADS:
                    pending.append(qk(h + ahead))
                t2 = s_this - _alibi_slope2(h) * dm
                m_old = m_refs[h][...]
                m_new = jnp.maximum(m_old, jnp.max(t2, axis=0, keepdims=True))
                alpha = jnp.exp2(m_old - m_new)
                p = jnp.exp2(t2 - m_new)
                l_refs[h][...] = alpha * l_refs[h][...] + jnp.sum(p, axis=0, keepdims=True)
                vt = vt_ref[0, j, hp * LANES + e * HEAD_DIM:hp * LANES + (e + 1) * HEAD_DIM, :]
                acc_refs[h][...] = alpha * acc_refs[h][...] + _dot(vt, p.astype(BF16))
                m_refs[h][...] = m_new
        return carry

    lax.fori_loop(0, n_kb, attn_block, 0)

    for hp in range(N_PAIRS):
        ot = jnp.concatenate([acc_refs[2 * hp][...] / l_refs[2 * hp][...],
                              acc_refs[2 * hp + 1][...] / l_refs[2 * hp + 1][...]], axis=0)
        o_ref[0, :, hp * LANES:(hp + 1) * LANES] = ot.T.astype(o_ref.dtype)


def _dsa_attention(q, iq, iw, k, v, ki2, tq, tk, q_off, n_sel):
    b, t, _ = q.shape
    s_len = k.shape[1]
    n_kb_max = s_len // tk
    iwt = jnp.swapaxes(iw, 1, 2)
    vt = jnp.swapaxes(v.reshape(b, n_kb_max, tk, WIDTH), 2, 3)
    qspec = lambda w: pl.BlockSpec((1, tq, w), lambda bi, i: (bi, i, 0))
    kspec = lambda w: pl.BlockSpec((1, s_len, w), lambda bi, i: (bi, 0, 0), pipeline_mode=pl.Buffered(1))
    return pl.pallas_call(
        functools.partial(_dsa_kernel, tq=tq, tk=tk, q_off=q_off, n_kb_max=n_kb_max, n_sel=n_sel,
                          idx_bits=max(1, int(math.ceil(math.log2(s_len + 1))))),
        out_shape=jax.ShapeDtypeStruct((b, t, WIDTH), BF16),
        grid=(b, t // tq),
        in_specs=[qspec(WIDTH), qspec(WIDTH),
                  pl.BlockSpec((1, N_HEADS, tq), lambda bi, i: (bi, 0, i)),
                  kspec(WIDTH),
                  pl.BlockSpec((1, n_kb_max, WIDTH, tk), lambda bi, i: (bi, 0, 0, 0), pipeline_mode=pl.Buffered(1)),
                  kspec(LANES)],
        out_specs=qspec(WIDTH),
        scratch_shapes=[pltpu.VMEM((n_kb_max, tk, tq), I32), pltpu.VMEM((n_kb_max, tk, tq), F32),
                        pltpu.VMEM((N_HEADS, LANES, tq), BF16), pltpu.VMEM((N_HEADS, LANES, tq), BF16)]
                       + [pltpu.VMEM((1, tq), F32)] * (2 * N_HEADS) + [pltpu.VMEM((HEAD_DIM, tq), F32)] * N_HEADS,
        compiler_params=_params(("parallel", "arbitrary")),
        name="dsa_attn",
    )(q, iq, iwt, k, vt, ki2)


def _mix_kernel(oa_ref, ob_ref, ga_ref, gb_ref, x_ref, mod_ref, gpost_ref, gpre_ref,
                wa_ref, wb_ref, wo_ref, wq_ref, x1_ref, h2_ref, pq_ref):
    mod = mod_ref[0]
    mixed = ga_ref[...] * _dot(oa_ref[...], wa_ref[...]) + gb_ref[...] * _dot(ob_ref[...], wb_ref[...])
    y = _dot(mixed.astype(BF16), wo_ref[...])
    x1 = x_ref[...] + mod[2:3, :] * _rms(y, gpost_ref[...])
    x1_ref[...] = x1
    h2 = _rms(x1, gpre_ref[...]) * (1.0 + mod[4:5, :]) + mod[3:4, :]
    h2_ref[...] = h2
    pq_ref[...] = _dot(h2.astype(BF16), wq_ref[...]).astype(BF16)


def _mix(oa, ob, ga, gb, x2, mod3, g_post, g_pre, wa, wb, wo, wq, tm, tiles_per_seq):
    n = x2.shape[0]
    row = lambda i: (i, 0)
    const = lambda i: (0, 0)
    nq = wq.shape[1]
    wspec = lambda r, c: pl.BlockSpec((r, c), const, pipeline_mode=pl.Buffered(1))
    return pl.pallas_call(
        _mix_kernel,
        out_shape=[jax.ShapeDtypeStruct((n, D_MODEL), F32),
                   jax.ShapeDtypeStruct((n, D_MODEL), F32),
                   jax.ShapeDtypeStruct((n, nq), BF16)],
        grid=(n // tm,),
        in_specs=[pl.BlockSpec((tm, WIDTH), row), pl.BlockSpec((tm, WIDTH), row),
                  pl.BlockSpec((tm, D_MODEL), row), pl.BlockSpec((tm, D_MODEL), row),
                  pl.BlockSpec((tm, D_MODEL), row),
                  pl.BlockSpec((1, 6, D_MODEL), lambda i: (i // tiles_per_seq, 0, 0)),
                  pl.BlockSpec((1, D_MODEL), const), pl.BlockSpec((1, D_MODEL), const),
                  wspec(WIDTH, D_MODEL), wspec(WIDTH, D_MODEL), wspec(D_MODEL, D_MODEL),
                  wspec(D_MODEL, nq)],
        out_specs=[pl.BlockSpec((tm, D_MODEL), row), pl.BlockSpec((tm, D_MODEL), row),
                   pl.BlockSpec((tm, nq), row)],
        compiler_params=_params(("parallel",)),
        name="mix",
    )(oa, ob, ga, gb, x2, mod3, g_post, g_pre, wa, wb, wo, wq)


def _top16_many(arrays, payloads=None):
    state = list(arrays)
    pos = [lax.broadcasted_iota(I32, a.shape, 0) for a in arrays]
    vals = [[] for _ in arrays]
    picks = [[] for _ in arrays]
    for _ in range(PEER_TOPK):
        for c, s in enumerate(state):
            m = jnp.max(s, axis=0, keepdims=True)
            p = jnp.min(jnp.where(s == m, pos[c], s.shape[0]), axis=0, keepdims=True)
            hit = pos[c] == p
            vals[c].append(m)
            picks[c].append(p if payloads is None
                            else jnp.sum(jnp.where(hit, payloads[c], 0), axis=0, keepdims=True))
            state[c] = jnp.where(hit, -jnp.inf, s)
    return [(jnp.concatenate(v, axis=0), jnp.concatenate(p, axis=0)) for v, p in zip(vals, picks)]


def _peersel_kernel(pq_ref, sk_ref, idx_ref, g_ref, *, heads):
    tt = pq_ref.shape[0]
    scores = []
    for h in range(heads):
        for half in range(2):
            lo = (2 * h + half) * PEER_DHALF
            scores.append(_dot_nt(sk_ref[h, half], pq_ref[:, lo:lo + PEER_DHALF]))
    tops = _top16_many(scores)
    half_k = PEER_TOPK // 2
    row8 = lax.broadcasted_iota(I32, (half_k, tt), 0)
    cands, cidxs = [], []
    for h in range(heads):
        (v1, i1), (v2, i2) = tops[2 * h], tops[2 * h + 1]
        cand = [v1[0:1] + v2]
        cidx = [i1[0:1] * PEER_NKEYS + i2]
        for i in range(1, half_k):
            cand.append(jnp.where(row8 < PEER_TOPK // (i + 1), v1[i:i + 1] + v2[0:half_k], -jnp.inf))
            cidx.append(i1[i:i + 1] * PEER_NKEYS + i2[0:half_k])
        cand.append(v1[half_k:] + v2[0:1])
        cidx.append(i1[half_k:] * PEER_NKEYS + i2[0:1])
        cands.append(jnp.concatenate(cand, axis=0))
        cidxs.append(jnp.concatenate(cidx, axis=0))
    for h, (top, eidx) in enumerate(_top16_many(cands, cidxs)):
        ex = jnp.exp(top - top[0:1])
        g_ref[h] = ex / jnp.sum(ex, axis=0, keepdims=True)
        idx_ref[h] = eidx


def _peer_select(pq, subkeys_b, tt):
    n = pq.shape[0]
    heads = PEER_SELECT_HEADS
    return pl.pallas_call(
        functools.partial(_peersel_kernel, heads=heads),
        out_shape=[jax.ShapeDtypeStruct((PEER_HEADS, PEER_TOPK, n), I32),
                   jax.ShapeDtypeStruct((PEER_HEADS, PEER_TOPK, n), F32)],
        grid=(n // tt, PEER_HEADS // heads),
        in_specs=[pl.BlockSpec((tt, heads * 2 * PEER_DHALF), lambda i, h: (i, h)),
                  pl.BlockSpec((heads, 2, PEER_NKEYS, PEER_DHALF), lambda i, h: (h, 0, 0, 0))],
        out_specs=[pl.BlockSpec((heads, PEER_TOPK, tt), lambda i, h: (h, 0, i)),
                   pl.BlockSpec((heads, PEER_TOPK, tt), lambda i, h: (h, 0, i))],
        compiler_params=_params(("parallel", "arbitrary")),
        name="peer_select",
    )(pq, subkeys_b)


def _erf_gelu(a):
    return 0.5 * a * (1.0 + lax.erf(a * (2.0 ** -0.5)))


def _peerffn_kernel(idx_ref, gt_ref, h2_ref, x1_ref, mod_ref, gpost_ref, uv_hbm, o_ref,
                    *scratch, tg, seqs_per_tile):
    bufs, (f_ref, sem) = scratch[:PEER_SLOTS], scratch[PEER_SLOTS:]

    def start_token(t, slot):
        tt = jnp.minimum(t, tg - 1)
        for k in range(PEER_SEL):
            pltpu.make_async_copy(uv_hbm.at[idx_ref[tt, k]], bufs[slot].at[:, pl.ds(k, 1), :],
                                  sem.at[slot]).start(priority=k % 2)

    def wait_token(slot):
        pltpu.make_async_copy(bufs[slot], bufs[slot], sem.at[slot]).wait()

    lane_tok = lax.broadcasted_iota(I32, (PEER_SEL, tg), 1)
    sub8 = lax.broadcasted_iota(I32, (8, D_MODEL), 0)

    def compute(t, slot):
        buf = bufs[slot]
        t8 = pl.multiple_of((t // 8) * 8, 8)
        h8 = h2_ref[pl.ds(t8, 8), :]
        hrow = jnp.sum(jnp.where(sub8 == t - t8, h8, 0.0), axis=0, keepdims=True)
        acc = None
        for c in range(N_CHUNK):
            term = buf[c] * hrow[:, c * LANES:(c + 1) * LANES]
            acc = term if acc is None else acc + term
        a = jnp.sum(acc, axis=1, keepdims=True)
        g = jnp.sum(jnp.where(lane_tok == t, gt_ref[...], 0.0), axis=1, keepdims=True)
        w = g * _erf_gelu(a)
        pieces = []
        for c in range(N_CHUNK):
            vc = buf[N_CHUNK + c]
            pieces.append(jnp.sum(vc * w, axis=0, keepdims=True))
        f_ref[pl.ds(t, 1), :] = jnp.concatenate(pieces, axis=1)

    for t in range(PEER_AHEAD):
        start_token(t, t)

    def group(i, carry):
        t0 = PEER_SLOTS * i
        for j in range(PEER_SLOTS):
            wait_token(j)
            start_token(t0 + j + PEER_AHEAD, (j + PEER_AHEAD) % PEER_SLOTS)
            compute(t0 + j, j)
        return carry

    lax.fori_loop(0, tg // PEER_SLOTS, group, 0)
    for t in range(PEER_AHEAD):
        wait_token(t)
    rows = tg // seqs_per_tile
    for s in range(seqs_per_tile):
        sl = slice(s * rows, (s + 1) * rows)
        o_ref[sl, :] = x1_ref[sl, :] + mod_ref[s][5:6, :] * _rms(f_ref[sl, :], gpost_ref[...])


def _peer_ffn(idx, gt, h2, x1, mod3, g_post, uv, tg, seq_len):
    n = h2.shape[0]
    row = lambda i: (i, 0)
    seqs_per_tile = max(1, tg // seq_len)
    tiles_per_seq = max(1, seq_len // tg)
    return pl.pallas_call(
        functools.partial(_peerffn_kernel, tg=tg, seqs_per_tile=seqs_per_tile),
        out_shape=jax.ShapeDtypeStruct((n, D_MODEL), F32),
        grid=(n // tg,),
        in_specs=[pl.BlockSpec((tg, PEER_SEL), row, memory_space=pltpu.SMEM),
                  pl.BlockSpec((PEER_SEL, tg), lambda i: (0, i)),
                  pl.BlockSpec((tg, D_MODEL), row),
                  pl.BlockSpec((tg, D_MODEL), row),
                  pl.BlockSpec((seqs_per_tile, 6, D_MODEL), lambda i: (i // tiles_per_seq, 0, 0)),
                  pl.BlockSpec((1, D_MODEL), lambda i: (0, 0)),
                  pl.BlockSpec(memory_space=pl.ANY)],
        out_specs=pl.BlockSpec((tg, D_MODEL), row),
        scratch_shapes=[pltpu.VMEM((2 * N_CHUNK, PEER_SEL, LANES), F32)] * PEER_SLOTS
                       + [pltpu.VMEM((tg, D_MODEL), F32), pltpu.SemaphoreType.DMA((PEER_SLOTS,))],
        compiler_params=_params(("arbitrary",)),
        name="peer_ffn",
    )(idx, gt, h2, x1, mod3, g_post, uv)


def _pad_keys(x, s_pad):
    return jnp.pad(x, ((0, 0), (0, s_pad - x.shape[1]), (0, 0)))


def _layer(x, mod3, past, weights, tiles):
    (g_pre_mix, g_post_mix, g_pre_ffn, g_post_ffn, w_in_b, w_gate_b, wa_b, wb_b, wo_b, wq_b,
     subkeys_b, peer_uv) = weights
    tm, sb_tq, sb_tk, dsa_tq, dsa_tk, tt, tg = tiles
    bsz, t, _ = x.shape
    n = bsz * t
    x2 = x.reshape(n, D_MODEL)
    (aq, ak, av, bq, bk, bv, iq, ki2, akf, avf, bkf, bvf, ikf, iw, ga, gb) = _inproj(
        x2, mod3, g_pre_mix, w_in_b, w_gate_b, tm, t // tm)
    seq = lambda a: a.reshape(bsz, t, a.shape[-1])
    if past is None:
        q_off = 0
        ka, va, kb, vb, ki = seq(ak), seq(av), seq(bk), seq(bv), seq(ki2)
        s_len = t
    else:
        p_sbk, p_sbv, p_dk, p_dv, p_ki = past
        q_off = p_sbk.shape[1]
        s_len = q_off + t
        flat = lambda c: c.reshape(bsz, q_off, WIDTH).astype(BF16)
        ka = jnp.concatenate([flat(p_sbk), seq(ak)], axis=1)
        va = jnp.concatenate([flat(p_sbv), seq(av)], axis=1)
        kb = jnp.concatenate([flat(p_dk), seq(bk)], axis=1)
        vb = jnp.concatenate([flat(p_dv), seq(bv)], axis=1)
        pk = p_ki.astype(BF16)
        ki = jnp.concatenate([jnp.concatenate([pk, pk], axis=-1), seq(ki2)], axis=1)
    s_pad = -(-s_len // max(sb_tk, dsa_tk)) * max(sb_tk, dsa_tk)
    ka, va, kb, vb, ki = (_pad_keys(a, s_pad) for a in (ka, va, kb, vb, ki))
    n_sel = min(TOPK_MAX, s_len // 4)

    oa = _sb_attention(seq(aq), ka, va, sb_tq, sb_tk, q_off)
    ob = _dsa_attention(seq(bq), seq(iq), seq(iw), kb, vb, ki, dsa_tq, dsa_tk, q_off, n_sel)

    x1, h2, pq = _mix(oa.reshape(n, WIDTH), ob.reshape(n, WIDTH), ga, gb, x2, mod3, g_post_mix, g_pre_ffn,
                      wa_b, wb_b, wo_b, wq_b, tm, t // tm)
    eidx, gate = _peer_select(pq, subkeys_b, tt)
    eidx = eidx.reshape(PEER_SEL, n).T
    y = _peer_ffn(eidx, gate.reshape(PEER_SEL, n), h2, x1, mod3, g_post_ffn, peer_uv, tg, t)
    new = tuple(a.reshape(1, bsz, t, N_HEADS, HEAD_DIM) for a in (akf, avf, bkf, bvf)) + (
        ikf.reshape(1, bsz, t, HEAD_DIM),)
    return y.reshape(bsz, t, D_MODEL), new


def _tiles(t):
    if t % 256 == 0:
        return (256, 256, 256, 256, 256, 128, 128)
    return (t, t, 128, t, 128, 128, 128)


def kernel(x_prompt, x_sample, c_prompt, c_sample, cache_sb_k, cache_sb_v, cache_dsa_k, cache_dsa_v, cache_dsa_kidx, w_ada, b_ada, g_pre_mix, g_post_mix, g_pre_ffn, g_post_ffn, w_in, w_gate, w_branch_a, w_branch_b, w_out, w_peer_q, peer_subkeys, peer_u, peer_v):
    assert w_ada.shape[0] == 1, "one layer"
    n_p, n_s = c_prompt.shape[0], c_sample.shape[0]
    c_all = jnp.concatenate([c_prompt, c_sample], axis=0)
    rows = -(-c_all.shape[0] // 8) * 8
    c_all = jnp.pad(c_all, ((0, rows - c_all.shape[0]), (0, 0)))
    mod = _modulation(c_all, w_ada[0], b_ada[0]).reshape(rows, 6, D_MODEL)
    w_in_b = jnp.pad(w_in[0], ((0, 0), (0, W_IN_PAD - W_IN_COLS))).astype(BF16)
    weights = (g_pre_mix, g_post_mix, g_pre_ffn, g_post_ffn, w_in_b, w_gate[0].astype(BF16),
               w_branch_a[0].astype(BF16), w_branch_b[0].astype(BF16), w_out[0].astype(BF16),
               w_peer_q[0].astype(BF16), peer_subkeys[0].astype(BF16),
               jnp.concatenate([peer_u[0], peer_v[0]], axis=1).reshape(-1, 2 * N_CHUNK, 1, LANES))
    yp, new_p = _layer(x_prompt, mod[:n_p], None, weights, _tiles(x_prompt.shape[1]))
    past = (cache_sb_k[0], cache_sb_v[0], cache_dsa_k[0], cache_dsa_v[0], cache_dsa_kidx[0])
    ys, new_s = _layer(x_sample, mod[n_p:n_p + n_s], past, weights, _tiles(x_sample.shape[1]))
    return (yp, ys) + new_p + new_s
```

```python
import functools
import math

import jax
import jax.numpy as jnp
import numpy as np
from jax import lax
from jax.experimental import pallas as pl
from jax.experimental.pallas import tpu as pltpu

F32 = jnp.float32
BF16 = jnp.bfloat16
I32 = jnp.int32

D_MODEL = 1024
CHUNK = 64
EPS = 1e-6
N_HEADS = 8
HEAD_DIM = 64
WIDTH = N_HEADS * HEAD_DIM
N_PAIRS = N_HEADS // 2
SB_PAIRS_PER_STEP = 4
LANES = 128
N_CHUNK = D_MODEL // LANES
TOPK_MAX = 256
PEER_HEADS = 8
PEER_NKEYS = 128
PEER_DHALF = 128
PEER_TOPK = 16
PEER_SEL = PEER_HEADS * PEER_TOPK
PEER_SELECT_HEADS = 4
PEER_SLOTS = 16
PEER_AHEAD = 14
W_IN_COLS = 7 * WIDTH + HEAD_DIM + N_HEADS
W_IN_PAD = 7 * WIDTH + LANES
INT_MIN = -(2 ** 31)
NEG_BIG = -1e30
DIST_MASKED = 1e30
LOG2E = 1.4426950408889634
VMEM_LIMIT = 56 * 1024 * 1024


def _params(sem):
    return pltpu.CompilerParams(dimension_semantics=sem, vmem_limit_bytes=VMEM_LIMIT)


def _dot(a, b):
    return jnp.dot(a, b, preferred_element_type=F32)


def _dot_nt(a, b):
    return lax.dot_general(a, b, (((1,), (1,)), ((), ())), preferred_element_type=F32)


def _rms(x, gain):
    return x * lax.rsqrt(jnp.mean(x * x, axis=-1, keepdims=True) + EPS) * gain


def _mod_kernel(c_ref, w_ref, b_ref, o_ref):
    c = c_ref[...]
    s = c * (1.0 / (1.0 + jnp.exp(-c)))
    o_ref[...] = jnp.dot(s, w_ref[...], preferred_element_type=F32,
                         precision=lax.Precision.HIGHEST) + b_ref[...]


def _modulation(c, w_ada, b_ada):
    n = c.shape[0]
    cols = w_ada.shape[1]
    tn = 1024
    return pl.pallas_call(
        _mod_kernel,
        out_shape=jax.ShapeDtypeStruct((n, cols), F32),
        grid=(cols // tn,),
        in_specs=[pl.BlockSpec((n, D_MODEL), lambda j: (0, 0)),
                  pl.BlockSpec((D_MODEL, tn), lambda j: (0, j)),
                  pl.BlockSpec((1, tn), lambda j: (0, j))],
        out_specs=pl.BlockSpec((n, tn), lambda j: (0, j)),
        compiler_params=_params(("arbitrary",)),
        name="mod",
    )(c, w_ada, b_ada.reshape(1, cols))


def _inproj_kernel(x_ref, mod_ref, g_ref, win_ref, wg_ref,
                   aq_ref, ak_ref, av_ref, bq_ref, bk_ref, bv_ref, iq_ref, ki2_ref,
                   akf_ref, avf_ref, bkf_ref, bvf_ref, ikf_ref, iw_ref, ga_ref, gb_ref):
    x = x_ref[...]
    mod = mod_ref[0]
    h = _rms(x, g_ref[...]) * (1.0 + mod[1:2, :]) + mod[0:1, :]
    hb = h.astype(BF16)
    qscale = HEAD_DIM ** -0.5

    def seg(i):
        return _dot(hb, win_ref[:, i * WIDTH:(i + 1) * WIDTH])

    aq_ref[...] = (seg(0) * (qscale * LOG2E)).astype(BF16)
    z = seg(1)
    akf_ref[...] = z
    ak_ref[...] = z.astype(BF16)
    z = seg(2)
    avf_ref[...] = z
    av_ref[...] = z.astype(BF16)
    bq_ref[...] = (seg(3) * (qscale * LOG2E)).astype(BF16)
    z = seg(4)
    bkf_ref[...] = z
    bk_ref[...] = z.astype(BF16)
    z = seg(5)
    bvf_ref[...] = z
    bv_ref[...] = z.astype(BF16)
    iq_ref[...] = (seg(6) * qscale).astype(BF16)
    tail = _dot(hb, win_ref[:, 7 * WIDTH:7 * WIDTH + LANES])
    ik = tail[:, :HEAD_DIM]
    ikf_ref[...] = ik
    ki2_ref[...] = jnp.concatenate([ik, ik], axis=-1).astype(BF16)
    iw_ref[...] = tail[:, HEAD_DIM:HEAD_DIM + N_HEADS] * (N_HEADS ** -0.5)
    zg = _dot(hb, wg_ref[...])
    gate = 1.0 / (1.0 + jnp.exp(-zg))
    ga_ref[...] = gate[:, :D_MODEL]
    gb_ref[...] = gate[:, D_MODEL:]


def _inproj(x2, mod3, g_pre, w_in_b, w_gate_b, tm, tiles_per_seq):
    n = x2.shape[0]
    row = lambda i: (i, 0)
    const = lambda i: (0, 0)
    bspec = lambda w: pl.BlockSpec((tm, w), row)
    outs = ([jax.ShapeDtypeStruct((n, WIDTH), BF16)] * 7
            + [jax.ShapeDtypeStruct((n, LANES), BF16)]
            + [jax.ShapeDtypeStruct((n, WIDTH), F32)] * 4
            + [jax.ShapeDtypeStruct((n, HEAD_DIM), F32),
               jax.ShapeDtypeStruct((n, N_HEADS), F32),
               jax.ShapeDtypeStruct((n, D_MODEL), F32),
               jax.ShapeDtypeStruct((n, D_MODEL), F32)])
    out_specs = ([bspec(WIDTH)] * 7 + [bspec(LANES)] + [bspec(WIDTH)] * 4
                 + [bspec(HEAD_DIM), bspec(N_HEADS), bspec(D_MODEL), bspec(D_MODEL)])
    return pl.pallas_call(
        _inproj_kernel,
        out_shape=outs,
        grid=(n // tm,),
        in_specs=[pl.BlockSpec((tm, D_MODEL), row),
                  pl.BlockSpec((1, 6, D_MODEL), lambda i: (i // tiles_per_seq, 0, 0)),
                  pl.BlockSpec((1, D_MODEL), const),
                  pl.BlockSpec((D_MODEL, W_IN_PAD), const, pipeline_mode=pl.Buffered(1)),
                  pl.BlockSpec((D_MODEL, 2 * D_MODEL), const, pipeline_mode=pl.Buffered(1))],
        out_specs=out_specs,
        compiler_params=_params(("parallel",)),
        name="inproj",
    )(x2, mod3, g_pre, w_in_b, w_gate_b)


def _sb_kernel(tab_ref, q_ref, k_ref, v_ref, tri_ref, o_ref, *scratch, tq, tk, q_off, pairs):
    s = pl.program_id(2)
    qi = tab_ref[0, s]
    kj = tab_ref[1, s]
    first = tab_ref[2, s]
    last = tab_ref[3, s]
    masked = tab_ref[4, s]
    n_heads = 2 * pairs
    acc_refs, run_refs = scratch[:n_heads], scratch[n_heads:]

    @pl.when(first == 1)
    def _():
        for r in scratch:
            r[...] = jnp.zeros_like(r)

    lo_lanes = lax.broadcasted_iota(I32, (tq, LANES), 1) < HEAD_DIM

    def step(use_mask):
        tri = tri_ref[...]
        if use_mask:
            qpos = q_off + qi * tq + lax.broadcasted_iota(I32, (tq, tk), 0)
            kpos = kj * tk + lax.broadcasted_iota(I32, (tq, tk), 1)
            causal = kpos < qpos
        z = []
        for p in range(pairs):
            q = q_ref[0, :, p * LANES:(p + 1) * LANES]
            k = k_ref[0, :, p * LANES:(p + 1) * LANES]
            zero = jnp.zeros_like(q)
            z += [_dot_nt(jnp.where(lo_lanes, q, zero), k), _dot_nt(jnp.where(lo_lanes, zero, q), k)]
        suf = []
        for h in range(n_heads):
            sp = jnp.maximum(z[h], 0.0) + jnp.log2(1.0 + jnp.exp2(-jnp.abs(z[h])))
            if use_mask:
                sp = jnp.where(causal, sp, 0.0)
            hi = sp.astype(BF16)
            lo = (sp - hi.astype(F32)).astype(BF16)
            suf.append(_dot(hi, tri) + _dot(lo, tri) + run_refs[h][:, 0:1])
        for h in range(n_heads):
            w = jnp.exp2(z[h] - suf[h])
            if use_mask:
                w = jnp.where(causal, w, 0.0)
            p = h // 2
            acc_refs[h][...] += _dot(w.astype(BF16), v_ref[0, :, p * LANES:(p + 1) * LANES])
            run_refs[h][...] = jnp.broadcast_to(suf[h][:, 0:1], (tq, LANES))

    @pl.when(masked == 1)
    def _():
        step(True)

    @pl.when(masked == 0)
    def _():
        step(False)

    @pl.when(last == 1)
    def _():
        for p in range(pairs):
            o_ref[0, :, p * LANES:(p + 1) * LANES] = jnp.where(
                lo_lanes, acc_refs[2 * p][...], acc_refs[2 * p + 1][...]).astype(o_ref.dtype)


def _sb_table(nq, nk, tq, tk, q_off):
    rows = []
    for qi in range(nq):
        q_lo = q_off + qi * tq
        q_hi = q_lo + tq - 1
        kjs = [kj for kj in range(nk) if kj * tk < q_hi]
        if not kjs:
            kjs = [0]
        kjs = kjs[::-1]
        for n, kj in enumerate(kjs):
            masked = int(kj * tk + tk - 1 >= q_lo)
            rows.append((qi, kj, int(n == 0), int(n == len(kjs) - 1), masked))
    return np.asarray(rows, dtype=np.int32).T.copy()


def _sb_attention(q, k, v, tq, tk, q_off):
    b, t, _ = q.shape
    s_len = k.shape[1]
    tab = _sb_table(t // tq, s_len // tk, tq, tk, q_off)
    tri = jnp.asarray(np.tril(np.ones((tk, tk), np.float32)), dtype=BF16)
    pairs = SB_PAIRS_PER_STEP
    width = pairs * LANES
    grid_spec = pltpu.PrefetchScalarGridSpec(
        num_scalar_prefetch=1,
        grid=(b, N_PAIRS // pairs, tab.shape[1]),
        in_specs=[pl.BlockSpec((1, tq, width), lambda bi, hp, s, tab: (bi, tab[0, s], hp)),
                  pl.BlockSpec((1, tk, width), lambda bi, hp, s, tab: (bi, tab[1, s], hp)),
                  pl.BlockSpec((1, tk, width), lambda bi, hp, s, tab: (bi, tab[1, s], hp)),
                  pl.BlockSpec((tk, tk), lambda bi, hp, s, tab: (0, 0))],
        out_specs=pl.BlockSpec((1, tq, width), lambda bi, hp, s, tab: (bi, tab[0, s], hp)),
        scratch_shapes=[pltpu.VMEM((tq, LANES), F32)] * (4 * pairs),
    )
    return pl.pallas_call(
        functools.partial(_sb_kernel, tq=tq, tk=tk, q_off=q_off, pairs=pairs),
        out_shape=jax.ShapeDtypeStruct((b, t, WIDTH), BF16),
        grid_spec=grid_spec,
        compiler_params=_params(("parallel", "parallel", "arbitrary")),
        name="sb_attn",
    )(jnp.asarray(tab), q, k, v, tri)


def _sortable(x):
    bits = pltpu.bitcast(x + 0.0, I32)
    return jnp.where(bits < 0, bits ^ 0x7FFFFFFF, bits)


def _alibi_slope2(h):
    return LOG2E * 2.0 ** (-8.0 * (h + 1) / N_HEADS)


def _dsa_kernel(q_ref, iq_ref, iwt_ref, k_ref, vt_ref, ki_ref, o_ref, keys_ref, dm_ref, qt_ref, iqt_ref, *stats,
                 tq, tk, q_off, n_kb_max, n_sel, idx_bits):
    m_refs, l_refs, acc_refs = stats[:N_HEADS], stats[N_HEADS:2 * N_HEADS], stats[2 * N_HEADS:]
    qi = pl.program_id(1)
    q_lo = q_off + qi * tq
    adm_end = ((q_lo + tq - 1) // CHUNK + 1) * CHUNK
    n_kb = jnp.minimum((adm_end + tk - 1) // tk, n_kb_max)

    lo_half = lax.broadcasted_iota(I32, (tq, LANES), 1) < HEAD_DIM
    qpos = q_lo + lax.broadcasted_iota(I32, (tk, tq), 1)
    krow = lax.broadcasted_iota(I32, (tk, tq), 0)

    def heads_of(x_ref):
        out = []
        for hp in range(N_PAIRS):
            pair = x_ref[0, :, hp * LANES:(hp + 1) * LANES]
            zero = jnp.zeros_like(pair)
            out += [jnp.where(lo_half, pair, zero), jnp.where(lo_half, zero, pair)]
        return out

    for src, dst in ((iq_ref, iqt_ref), (q_ref, qt_ref)):
        for h, xe in enumerate(heads_of(src)):
            dst[h] = xe.astype(F32).T.astype(BF16)

    def score_block(j, carry):
        ki = ki_ref[0, pl.ds(pl.multiple_of(j * tk, tk), tk), :]
        score = jnp.zeros((tk, tq), F32)
        for h in range(N_HEADS):
            score = score + iwt_ref[0, h:h + 1, :] * jnp.maximum(_dot(ki, iqt_ref[h]), 0.0)
        adm = ((j * tk + krow) // CHUNK) <= (qpos // CHUNK)
        keys_ref[j] = jnp.where(adm, _sortable(score), INT_MIN)
        return carry

    lax.fori_loop(0, n_kb, score_block, 0)

    def count_cols(pred_fn):
        def body(j, acc):
            hit = pred_fn(keys_ref[j], j).astype(I32)
            return acc + jnp.sum(hit.reshape(tk // 8, 8, tq), axis=0)
        acc = lax.fori_loop(0, n_kb, body, jnp.zeros((8, tq), I32))
        return jnp.sum(acc, axis=0, keepdims=True)

    def bit_step(i, thr):
        cand = thr + jnp.left_shift(jnp.int32(1), 31 - i)
        cnt = count_cols(lambda kb, j: kb >= cand)
        return jnp.where(cnt >= n_sel, cand, thr)

    thr = lax.fori_loop(0, 32, bit_step, jnp.full((1, tq), INT_MIN, I32))
    n_gt = count_cols(lambda kb, j: kb > thr)
    n_ge = count_cols(lambda kb, j: kb >= thr)
    need = n_sel - n_gt
    real = thr > INT_MIN
    tie = jnp.max(jnp.where(real & (n_ge > n_sel), 1, 0)) > 0

    def tie_cut():
        def idx_step(i, cut):
            cand = cut + jnp.left_shift(jnp.int32(1), idx_bits - 1 - i)
            cnt = count_cols(lambda kb, j: (kb == thr) & ((j * tk + krow) < cand))
            return jnp.where(cnt < need, cand, cut)
        return lax.fori_loop(0, idx_bits, idx_step, jnp.zeros((1, tq), I32))

    cut = lax.cond(tie, tie_cut, lambda: jnp.full((1, tq), n_kb_max * tk, I32))

    def mask_block(j, carry):
        kb = keys_ref[j]
        kpos = j * tk + krow
        sel = (kb > thr) | ((kb == thr) & real & (kpos <= cut))
        dm_ref[j] = jnp.where(sel, jnp.abs(qpos - kpos).astype(F32), DIST_MASKED)
        return carry

    lax.fori_loop(0, n_kb, mask_block, 0)

    for h in range(N_HEADS):
        m_refs[h][...] = jnp.full_like(m_refs[h], NEG_BIG)
        l_refs[h][...] = jnp.zeros_like(l_refs[h])
        acc_refs[h][...] = jnp.zeros_like(acc_refs[h])

    def attn_block(j, carry):
        start = pl.multiple_of(j * tk, tk)
        dm = dm_ref[j]

        def qk(h):
            return _dot(k_ref[0, pl.ds(start, tk), (h // 2) * LANES:(h // 2 + 1) * LANES], qt_ref[h])

        ahead = 4
        pending = [qk(h) for h in range(ahead)]
        for hp in range(N_PAIRS):
            for e in range(2):
                h = 2 * hp + e
                s_this = pending.pop(0)
                if h + ahead < N_HEADS:
                    pending.append(qk(h + ahead))
                t2 = s_this - _alibi_slope2(h) * dm
                m_old = m_refs[h][...]
                m_new = jnp.maximum(m_old, jnp.max(t2, axis=0, keepdims=True))
                alpha = jnp.exp2(m_old - m_new)
                p = jnp.exp2(t2 - m_new)
                l_refs[h][...] = alpha * l_refs[h][...] + jnp.sum(p, axis=0, keepdims=True)
                vt = vt_ref[0, j, hp * LANES + e * HEAD_DIM:hp * LANES + (e + 1) * HEAD_DIM, :]
                acc_refs[h][...] = alpha * acc_refs[h][...] + _dot(vt, p.astype(BF16))
                m_refs[h][...] = m_new
        return carry

    lax.fori_loop(0, n_kb, attn_block, 0)

    for hp in range(N_PAIRS):
        ot = jnp.concatenate([acc_refs[2 * hp][...] / l_refs[2 * hp][...],
                              acc_refs[2 * hp + 1][...] / l_refs[2 * hp + 1][...]], axis=0)
        o_ref[0, :, hp * LANES:(hp + 1) * LANES] = ot.T.astype(o_ref.dtype)


def _dsa_attention(q, iq, iw, k, v, ki2, tq, tk, q_off, n_sel):
    b, t, _ = q.shape
    s_len = k.shape[1]
    n_kb_max = s_len // tk
    iwt = jnp.swapaxes(iw, 1, 2)
    vt = jnp.swapaxes(v.reshape(b, n_kb_max, tk, WIDTH), 2, 3)
    qspec = lambda w: pl.BlockSpec((1, tq, w), lambda bi, i: (bi, i, 0))
    kspec = lambda w: pl.BlockSpec((1, s_len, w), lambda bi, i: (bi, 0, 0), pipeline_mode=pl.Buffered(1))
    return pl.pallas_call(
        functools.partial(_dsa_kernel, tq=tq, tk=tk, q_off=q_off, n_kb_max=n_kb_max, n_sel=n_sel,
                          idx_bits=max(1, int(math.ceil(math.log2(s_len + 1))))),
        out_shape=jax.ShapeDtypeStruct((b, t, WIDTH), BF16),
        grid=(b, t // tq),
        in_specs=[qspec(WIDTH), qspec(WIDTH),
                  pl.BlockSpec((1, N_HEADS, tq), lambda bi, i: (bi, 0, i)),
                  kspec(WIDTH),
                  pl.BlockSpec((1, n_kb_max, WIDTH, tk), lambda bi, i: (bi, 0, 0, 0), pipeline_mode=pl.Buffered(1)),
                  kspec(LANES)],
        out_specs=qspec(WIDTH),
        scratch_shapes=[pltpu.VMEM((n_kb_max, tk, tq), I32), pltpu.VMEM((n_kb_max, tk, tq), F32),
                        pltpu.VMEM((N_HEADS, LANES, tq), BF16), pltpu.VMEM((N_HEADS, LANES, tq), BF16)]
                       + [pltpu.VMEM((1, tq), F32)] * (2 * N_HEADS) + [pltpu.VMEM((HEAD_DIM, tq), F32)] * N_HEADS,
        compiler_params=_params(("parallel", "arbitrary")),
        name="dsa_attn",
    )(q, iq, iwt, k, vt, ki2)


def _mix_kernel(oa_ref, ob_ref, ga_ref, gb_ref, x_ref, mod_ref, gpost_ref, gpre_ref,
                wa_ref, wb_ref, wo_ref, wq_ref, x1_ref, h2_ref, pq_ref):
    mod = mod_ref[0]
    mixed = ga_ref[...] * _dot(oa_ref[...], wa_ref[...]) + gb_ref[...] * _dot(ob_ref[...], wb_ref[...])
    y = _dot(mixed.astype(BF16), wo_ref[...])
    x1 = x_ref[...] + mod[2:3, :] * _rms(y, gpost_ref[...])
    x1_ref[...] = x1
    h2 = _rms(x1, gpre_ref[...]) * (1.0 + mod[4:5, :]) + mod[3:4, :]
    h2_ref[...] = h2
    pq_ref[...] = _dot(h2.astype(BF16), wq_ref[...]).astype(BF16)


def _mix(oa, ob, ga, gb, x2, mod3, g_post, g_pre, wa, wb, wo, wq, tm, tiles_per_seq):
    n = x2.shape[0]
    row = lambda i: (i, 0)
    const = lambda i: (0, 0)
    nq = wq.shape[1]
    wspec = lambda r, c: pl.BlockSpec((r, c), const, pipeline_mode=pl.Buffered(1))
    return pl.pallas_call(
        _mix_kernel,
        out_shape=[jax.ShapeDtypeStruct((n, D_MODEL), F32),
                   jax.ShapeDtypeStruct((n, D_MODEL), F32),
                   jax.ShapeDtypeStruct((n, nq), BF16)],
        grid=(n // tm,),
        in_specs=[pl.BlockSpec((tm, WIDTH), row), pl.BlockSpec((tm, WIDTH), row),
                  pl.BlockSpec((tm, D_MODEL), row), pl.BlockSpec((tm, D_MODEL), row),
                  pl.BlockSpec((tm, D_MODEL), row),
                  pl.BlockSpec((1, 6, D_MODEL), lambda i: (i // tiles_per_seq, 0, 0)),
                  pl.BlockSpec((1, D_MODEL), const), pl.BlockSpec((1, D_MODEL), const),
                  wspec(WIDTH, D_MODEL), wspec(WIDTH, D_MODEL), wspec(D_MODEL, D_MODEL),
                  wspec(D_MODEL, nq)],
        out_specs=[pl.BlockSpec((tm, D_MODEL), row), pl.BlockSpec((tm, D_MODEL), row),
                   pl.BlockSpec((tm, nq), row)],
        compiler_params=_params(("parallel",)),
        name="mix",
    )(oa, ob, ga, gb, x2, mod3, g_post, g_pre, wa, wb, wo, wq)


def _top16_many(arrays, payloads=None):
    state = list(arrays)
    pos = [lax.broadcasted_iota(I32, a.shape, 0) for a in arrays]
    vals = [[] for _ in arrays]
    picks = [[] for _ in arrays]
    for _ in range(PEER_TOPK):
        for c, s in enumerate(state):
            m = jnp.max(s, axis=0, keepdims=True)
            p = jnp.min(jnp.where(s == m, pos[c], s.shape[0]), axis=0, keepdims=True)
            hit = pos[c] == p
            vals[c].append(m)
            picks[c].append(p if payloads is None
                            else jnp.sum(jnp.where(hit, payloads[c], 0), axis=0, keepdims=True))
            state[c] = jnp.where(hit, -jnp.inf, s)
    return [(jnp.concatenate(v, axis=0), jnp.concatenate(p, axis=0)) for v, p in zip(vals, picks)]


def _peersel_kernel(pq_ref, sk_ref, idx_ref, g_ref, *, heads):
    tt = pq_ref.shape[0]
    scores = []
    for h in range(heads):
        for half in range(2):
            lo = (2 * h + half) * PEER_DHALF
            scores.append(_dot_nt(sk_ref[h, half], pq_ref[:, lo:lo + PEER_DHALF]))
    tops = _top16_many(scores)
    half_k = PEER_TOPK // 2
    row8 = lax.broadcasted_iota(I32, (half_k, tt), 0)
    cands, cidxs = [], []
    for h in range(heads):
        (v1, i1), (v2, i2) = tops[2 * h], tops[2 * h + 1]
        cand = [v1[0:1] + v2]
        cidx = [i1[0:1] * PEER_NKEYS + i2]
        for i in range(1, half_k):
            cand.append(jnp.where(row8 < PEER_TOPK // (i + 1), v1[i:i + 1] + v2[0:half_k], -jnp.inf))
            cidx.append(i1[i:i + 1] * PEER_NKEYS + i2[0:half_k])
        cand.append(v1[half_k:] + v2[0:1])
        cidx.append(i1[half_k:] * PEER_NKEYS + i2[0:1])
        cands.append(jnp.concatenate(cand, axis=0))
        cidxs.append(jnp.concatenate(cidx, axis=0))
    for h, (top, eidx) in enumerate(_top16_many(cands, cidxs)):
        ex = jnp.exp(top - top[0:1])
        g_ref[h] = ex / jnp.sum(ex, axis=0, keepdims=True)
        idx_ref[h] = eidx


def _peer_select(pq, subkeys_b, tt):
    n = pq.shape[0]
    heads = PEER_SELECT_HEADS
    return pl.pallas_call(
        functools.partial(_peersel_kernel, heads=heads),
        out_shape=[jax.ShapeDtypeStruct((PEER_HEADS, PEER_TOPK, n), I32),
                   jax.ShapeDtypeStruct((PEER_HEADS, PEER_TOPK, n), F32)],
        grid=(n // tt, PEER_HEADS // heads),
        in_specs=[pl.BlockSpec((tt, heads * 2 * PEER_DHALF), lambda i, h: (i, h)),
                  pl.BlockSpec((heads, 2, PEER_NKEYS, PEER_DHALF), lambda i, h: (h, 0, 0, 0))],
        out_specs=[pl.BlockSpec((heads, PEER_TOPK, tt), lambda i, h: (h, 0, i)),
                   pl.BlockSpec((heads, PEER_TOPK, tt), lambda i, h: (h, 0, i))],
        compiler_params=_params(("parallel", "arbitrary")),
        name="peer_select",
    )(pq, subkeys_b)


def _erf_gelu(a):
    return 0.5 * a * (1.0 + lax.erf(a * (2.0 ** -0.5)))


def _peerffn_kernel(idx_ref, gt_ref, h2_ref, x1_ref, mod_ref, gpost_ref, uv_hbm, o_ref,
                    *scratch, tg, seqs_per_tile):
    bufs, (f_ref, sem) = scratch[:PEER_SLOTS], scratch[PEER_SLOTS:]

    def start_token(t, slot):
        tt = jnp.minimum(t, tg - 1)
        for k in range(PEER_SEL):
            pltpu.make_async_copy(uv_hbm.at[idx_ref[tt, k]], bufs[slot].at[:, pl.ds(k, 1), :],
                                  sem.at[slot]).start(priority=k % 2)

    def wait_token(slot):
        pltpu.make_async_copy(bufs[slot], bufs[slot], sem.at[slot]).wait()

    lane_tok = lax.broadcasted_iota(I32, (PEER_SEL, tg), 1)
    sub8 = lax.broadcasted_iota(I32, (8, D_MODEL), 0)

    def compute(t, slot):
        buf = bufs[slot]
        t8 = pl.multiple_of((t // 8) * 8, 8)
        h8 = h2_ref[pl.ds(t8, 8), :]
        hrow = jnp.sum(jnp.where(sub8 == t - t8, h8, 0.0), axis=0, keepdims=True)
        acc = None
        for c in range(N_CHUNK):
            term = buf[c] * hrow[:, c * LANES:(c + 1) * LANES]
            acc = term if acc is None else acc + term
        a = jnp.sum(acc, axis=1, keepdims=True)
        g = jnp.sum(jnp.where(lane_tok == t, gt_ref[...], 0.0), axis=1, keepdims=True)
        w = g * _erf_gelu(a)
        pieces = []
        for c in range(N_CHUNK):
            vc = buf[N_CHUNK + c]
            pieces.append(jnp.sum(vc * w, axis=0, keepdims=True))
        f_ref[pl.ds(t, 1), :] = jnp.concatenate(pieces, axis=1)

    for t in range(PEER_AHEAD):
        start_token(t, t)

    def group(i, carry):
        t0 = PEER_SLOTS * i
        for j in range(PEER_SLOTS):
            wait_token(j)
            start_token(t0 + j + PEER_AHEAD, (j + PEER_AHEAD) % PEER_SLOTS)
            compute(t0 + j, j)
        return carry

    lax.fori_loop(0, tg // PEER_SLOTS, group, 0)
    for t in range(PEER_AHEAD):
        wait_token(t)
    rows = tg // seqs_per_tile
    for s in range(seqs_per_tile):
        sl = slice(s * rows, (s + 1) * rows)
        o_ref[sl, :] = x1_ref[sl, :] + mod_ref[s][5:6, :] * _rms(f_ref[sl, :], gpost_ref[...])


def _peer_ffn(idx, gt, h2, x1, mod3, g_post, uv, tg, seq_len):
    n = h2.shape[0]
    row = lambda i: (i, 0)
    seqs_per_tile = max(1, tg // seq_len)
    tiles_per_seq = max(1, seq_len // tg)
    return pl.pallas_call(
        functools.partial(_peerffn_kernel, tg=tg, seqs_per_tile=seqs_per_tile),
        out_shape=jax.ShapeDtypeStruct((n, D_MODEL), F32),
        grid=(n // tg,),
        in_specs=[pl.BlockSpec((tg, PEER_SEL), row, memory_space=pltpu.SMEM),
                  pl.BlockSpec((PEER_SEL, tg), lambda i: (0, i)),
                  pl.BlockSpec((tg, D_MODEL), row),
                  pl.BlockSpec((tg, D_MODEL), row),
                  pl.BlockSpec((seqs_per_tile, 6, D_MODEL), lambda i: (i // tiles_per_seq, 0, 0)),
                  pl.BlockSpec((1, D_MODEL), lambda i: (0, 0)),
                  pl.BlockSpec(memory_space=pl.ANY)],
        out_specs=pl.BlockSpec((tg, D_MODEL), row),
        scratch_shapes=[pltpu.VMEM((2 * N_CHUNK, PEER_SEL, LANES), F32)] * PEER_SLOTS
                       + [pltpu.VMEM((tg, D_MODEL), F32), pltpu.SemaphoreType.DMA((PEER_SLOTS,))],
        compiler_params=_params(("arbitrary",)),
        name="peer_ffn",
    )(idx, gt, h2, x1, mod3, g_post, uv)


def _pad_keys(x, s_pad):
    return jnp.pad(x, ((0, 0), (0, s_pad - x.shape[1]), (0, 0)))


def _layer(x, mod3, past, weights, tiles):
    (g_pre_mix, g_post_mix, g_pre_ffn, g_post_ffn, w_in_b, w_gate_b, wa_b, wb_b, wo_b, wq_b,
     subkeys_b, peer_uv) = weights
    tm, sb_tq, sb_tk, dsa_tq, dsa_tk, tt, tg = tiles
    bsz, t, _ = x.shape
    n = bsz * t
    x2 = x.reshape(n, D_MODEL)
    (aq, ak, av, bq, bk, bv, iq, ki2, akf, avf, bkf, bvf, ikf, iw, ga, gb) = _inproj(
        x2, mod3, g_pre_mix, w_in_b, w_gate_b, tm, t // tm)
    seq = lambda a: a.reshape(bsz, t, a.shape[-1])
    if past is None:
        q_off = 0
        ka, va, kb, vb, ki = seq(ak), seq(av), seq(bk), seq(bv), seq(ki2)
        s_len = t
    else:
        p_sbk, p_sbv, p_dk, p_dv, p_ki = past
        q_off = p_sbk.shape[1]
        s_len = q_off + t
        flat = lambda c: c.reshape(bsz, q_off, WIDTH).astype(BF16)
        ka = jnp.concatenate([flat(p_sbk), seq(ak)], axis=1)
        va = jnp.concatenate([flat(p_sbv), seq(av)], axis=1)
        kb = jnp.concatenate([flat(p_dk), seq(bk)], axis=1)
        vb = jnp.concatenate([flat(p_dv), seq(bv)], axis=1)
        pk = p_ki.astype(BF16)
        ki = jnp.concatenate([jnp.concatenate([pk, pk], axis=-1), seq(ki2)], axis=1)
    s_pad = -(-s_len // max(sb_tk, dsa_tk)) * max(sb_tk, dsa_tk)
    ka, va, kb, vb, ki = (_pad_keys(a, s_pad) for a in (ka, va, kb, vb, ki))
    n_sel = min(TOPK_MAX, s_len // 4)

    oa = _sb_attention(seq(aq), ka, va, sb_tq, sb_tk, q_off)
    ob = _dsa_attention(seq(bq), seq(iq), seq(iw), kb, vb, ki, dsa_tq, dsa_tk, q_off, n_sel)

    x1, h2, pq = _mix(oa.reshape(n, WIDTH), ob.reshape(n, WIDTH), ga, gb, x2, mod3, g_post_mix, g_pre_ffn,
                      wa_b, wb_b, wo_b, wq_b, tm, t // tm)
    eidx, gate = _peer_select(pq, subkeys_b, tt)
    eidx = eidx.reshape(PEER_SEL, n).T
    y = _peer_ffn(eidx, gate.reshape(PEER_SEL, n), h2, x1, mod3, g_post_ffn, peer_uv, tg, t)
    new = tuple(a.reshape(1, bsz, t, N_HEADS, HEAD_DIM) for a in (akf, avf, bkf, bvf)) + (
        ikf.reshape(1, bsz, t, HEAD_DIM),)
    return y.reshape(bsz, t, D_MODEL), new


def _tiles(t):
    if t % 256 == 0:
        return (256, 256, 256, 256, 256, 128, 128)
    return (t, t, 128, t, 128, 128, 128)


def kernel(x_prompt, x_sample, c_prompt, c_sample, cache_sb_k, cache_sb_v, cache_dsa_k, cache_dsa_v, cache_dsa_kidx, w_ada, b_ada, g_pre_mix, g_post_mix, g_pre_ffn, g_post_ffn, w_in, w_gate, w_branch_a, w_branch_b, w_out, w_peer_q, peer_subkeys, peer_u, peer_v):
    assert w_ada.shape[0] == 1, "one layer"
    n_p, n_s = c_prompt.shape[0], c_sample.shape[0]
    c_all = jnp.concatenate([c_prompt, c_sample], axis=0)
    rows = -(-c_all.shape[0] // 8) * 8
    c_all = jnp.pad(c_all, ((0, rows - c_all.shape[0]), (0, 0)))
    mod = _modulation(c_all, w_ada[0], b_ada[0]).reshape(rows, 6, D_MODEL)
    w_in_b = jnp.pad(w_in[0], ((0, 0), (0, W_IN_PAD - W_IN_COLS))).astype(BF16)
    weights = (g_pre_mix, g_post_mix, g_pre_ffn, g_post_ffn, w_in_b, w_gate[0].astype(BF16),
               w_branch_a[0].astype(BF16), w_branch_b[0].astype(BF16), w_out[0].astype(BF16),
               w_peer_q[0].astype(BF16), peer_subkeys[0].astype(BF16),
               jnp.concatenate([peer_u[0], peer_v[0]], axis=1).reshape(-1, 2 * N_CHUNK, 1, LANES))
    yp, new_p = _layer(x_prompt, mod[:n_p], None, weights, _tiles(x_prompt.shape[1]))
    past = (cache_sb_k[0], cache_sb_v[0], cache_dsa_k[0], cache_dsa_v[0], cache_dsa_kidx[0])
    ys, new_s = _layer(x_sample, mod[n_p:n_p + n_s], past, weights, _tiles(x_sample.shape[1]))
    return (yp, ys) + new_p + new_s
```

```python
import functools
import math

import jax
import jax.numpy as jnp
import numpy as np
from jax import lax
from jax.experimental import pallas as pl
from jax.experimental.pallas import tpu as pltpu

F32 = jnp.float32
BF16 = jnp.bfloat16
I32 = jnp.int32

D_MODEL = 1024
CHUNK = 64
EPS = 1e-6
N_HEADS = 8
HEAD_DIM = 64
WIDTH = N_HEADS * HEAD_DIM
N_PAIRS = N_HEADS // 2
SB_PAIRS_PER_STEP = 4
LANES = 128
N_CHUNK = D_MODEL // LANES
TOPK_MAX = 256
PEER_HEADS = 8
PEER_NKEYS = 128
PEER_DHALF = 128
PEER_TOPK = 16
PEER_SEL = PEER_HEADS * PEER_TOPK
PEER_SELECT_HEADS = 4
PEER_SLOTS = 8
PEER_AHEAD = 6
W_IN_COLS = 7 * WIDTH + HEAD_DIM + N_HEADS
W_IN_PAD = 7 * WIDTH + LANES
INT_MIN = -(2 ** 31)
NEG_BIG = -1e30
DIST_MASKED = 1e30
LOG2E = 1.4426950408889634
VMEM_LIMIT = 56 * 1024 * 1024


def _params(sem):
    return pltpu.CompilerParams(dimension_semantics=sem, vmem_limit_bytes=VMEM_LIMIT)


def _dot(a, b):
    return jnp.dot(a, b, preferred_element_type=F32)


def _dot_nt(a, b):
    return lax.dot_general(a, b, (((1,), (1,)), ((), ())), preferred_element_type=F32)


def _rms(x, gain):
    return x * lax.rsqrt(jnp.mean(x * x, axis=-1, keepdims=True) + EPS) * gain


def _mod_kernel(c_ref, w_ref, b_ref, o_ref):
    c = c_ref[...]
    s = c * (1.0 / (1.0 + jnp.exp(-c)))
    o_ref[...] = jnp.dot(s, w_ref[...], preferred_element_type=F32,
                         precision=lax.Precision.HIGHEST) + b_ref[...]


def _modulation(c, w_ada, b_ada):
    n = c.shape[0]
    cols = w_ada.shape[1]
    tn = 1024
    return pl.pallas_call(
        _mod_kernel,
        out_shape=jax.ShapeDtypeStruct((n, cols), F32),
        grid=(cols // tn,),
        in_specs=[pl.BlockSpec((n, D_MODEL), lambda j: (0, 0)),
                  pl.BlockSpec((D_MODEL, tn), lambda j: (0, j)),
                  pl.BlockSpec((1, tn), lambda j: (0, j))],
        out_specs=pl.BlockSpec((n, tn), lambda j: (0, j)),
        compiler_params=_params(("arbitrary",)),
        name="mod",
    )(c, w_ada, b_ada.reshape(1, cols))


def _inproj_kernel(x_ref, mod_ref, g_ref, win_ref, wg_ref,
                   aq_ref, ak_ref, av_ref, bq_ref, bk_ref, bv_ref, iq_ref, ki2_ref,
                   akf_ref, avf_ref, bkf_ref, bvf_ref, ikf_ref, iw_ref, ga_ref, gb_ref):
    x = x_ref[...]
    mod = mod_ref[0]
    h = _rms(x, g_ref[...]) * (1.0 + mod[1:2, :]) + mod[0:1, :]
    hb = h.astype(BF16)
    qscale = HEAD_DIM ** -0.5

    def seg(i):
        return _dot(hb, win_ref[:, i * WIDTH:(i + 1) * WIDTH])

    aq_ref[...] = (seg(0) * (qscale * LOG2E)).astype(BF16)
    z = seg(1)
    akf_ref[...] = z
    ak_ref[...] = z.astype(BF16)
    z = seg(2)
    avf_ref[...] = z
    av_ref[...] = z.astype(BF16)
    bq_ref[...] = (seg(3) * (qscale * LOG2E)).astype(BF16)
    z = seg(4)
    bkf_ref[...] = z
    bk_ref[...] = z.astype(BF16)
    z = seg(5)
    bvf_ref[...] = z
    bv_ref[...] = z.astype(BF16)
    iq_ref[...] = (seg(6) * qscale).astype(BF16)
    tail = _dot(hb, win_ref[:, 7 * WIDTH:7 * WIDTH + LANES])
    ik = tail[:, :HEAD_DIM]
    ikf_ref[...] = ik
    ki2_ref[...] = jnp.concatenate([ik, ik], axis=-1).astype(BF16)
    iw_ref[...] = tail[:, HEAD_DIM:HEAD_DIM + N_HEADS] * (N_HEADS ** -0.5)
    zg = _dot(hb, wg_ref[...])
    gate = 1.0 / (1.0 + jnp.exp(-zg))
    ga_ref[...] = gate[:, :D_MODEL]
    gb_ref[...] = gate[:, D_MODEL:]


def _inproj(x2, mod3, g_pre, w_in_b, w_gate_b, tm, tiles_per_seq):
    n = x2.shape[0]
    row = lambda i: (i, 0)
    const = lambda i: (0, 0)
    bspec = lambda w: pl.BlockSpec((tm, w), row)
    outs = ([jax.ShapeDtypeStruct((n, WIDTH), BF16)] * 7
            + [jax.ShapeDtypeStruct((n, LANES), BF16)]
            + [jax.ShapeDtypeStruct((n, WIDTH), F32)] * 4
            + [jax.ShapeDtypeStruct((n, HEAD_DIM), F32),
               jax.ShapeDtypeStruct((n, N_HEADS), F32),
               jax.ShapeDtypeStruct((n, D_MODEL), F32),
               jax.ShapeDtypeStruct((n, D_MODEL), F32)])
    out_specs = ([bspec(WIDTH)] * 7 + [bspec(LANES)] + [bspec(WIDTH)] * 4
                 + [bspec(HEAD_DIM), bspec(N_HEADS), bspec(D_MODEL), bspec(D_MODEL)])
    return pl.pallas_call(
        _inproj_kernel,
        out_shape=outs,
        grid=(n // tm,),
        in_specs=[pl.BlockSpec((tm, D_MODEL), row),
                  pl.BlockSpec((1, 6, D_MODEL), lambda i: (i // tiles_per_seq, 0, 0)),
                  pl.BlockSpec((1, D_MODEL), const),
                  pl.BlockSpec((D_MODEL, W_IN_PAD), const, pipeline_mode=pl.Buffered(1)),
                  pl.BlockSpec((D_MODEL, 2 * D_MODEL), const, pipeline_mode=pl.Buffered(1))],
        out_specs=out_specs,
        compiler_params=_params(("parallel",)),
        name="inproj",
    )(x2, mod3, g_pre, w_in_b, w_gate_b)


def _sb_kernel(tab_ref, q_ref, k_ref, v_ref, tri_ref, o_ref, *scratch, tq, tk, q_off, pairs):
    s = pl.program_id(2)
    qi = tab_ref[0, s]
    kj = tab_ref[1, s]
    first = tab_ref[2, s]
    last = tab_ref[3, s]
    masked = tab_ref[4, s]
    n_heads = 2 * pairs
    acc_refs, run_refs = scratch[:n_heads], scratch[n_heads:]

    @pl.when(first == 1)
    def _():
        for r in scratch:
            r[...] = jnp.zeros_like(r)

    lo_lanes = lax.broadcasted_iota(I32, (tq, LANES), 1) < HEAD_DIM

    def step(use_mask):
        tri = tri_ref[...]
        if use_mask:
            qpos = q_off + qi * tq + lax.broadcasted_iota(I32, (tq, tk), 0)
            kpos = kj * tk + lax.broadcasted_iota(I32, (tq, tk), 1)
            causal = kpos < qpos
        z = []
        for p in range(pairs):
            q = q_ref[0, :, p * LANES:(p + 1) * LANES]
            k = k_ref[0, :, p * LANES:(p + 1) * LANES]
            zero = jnp.zeros_like(q)
            z += [_dot_nt(jnp.where(lo_lanes, q, zero), k), _dot_nt(jnp.where(lo_lanes, zero, q), k)]
        suf = []
        for h in range(n_heads):
            sp = jnp.maximum(z[h], 0.0) + jnp.log2(1.0 + jnp.exp2(-jnp.abs(z[h])))
            if use_mask:
                sp = jnp.where(causal, sp, 0.0)
            hi = sp.astype(BF16)
            lo = (sp - hi.astype(F32)).astype(BF16)
            suf.append(_dot(hi, tri) + _dot(lo, tri) + run_refs[h][:, 0:1])
        for h in range(n_heads):
            w = jnp.exp2(z[h] - suf[h])
            if use_mask:
                w = jnp.where(causal, w, 0.0)
            p = h // 2
            acc_refs[h][...] += _dot(w.astype(BF16), v_ref[0, :, p * LANES:(p + 1) * LANES])
            run_refs[h][...] = jnp.broadcast_to(suf[h][:, 0:1], (tq, LANES))

    @pl.when(masked == 1)
    def _():
        step(True)

    @pl.when(masked == 0)
    def _():
        step(False)

    @pl.when(last == 1)
    def _():
        for p in range(pairs):
            o_ref[0, :, p * LANES:(p + 1) * LANES] = jnp.where(
                lo_lanes, acc_refs[2 * p][...], acc_refs[2 * p + 1][...]).astype(o_ref.dtype)


def _sb_table(nq, nk, tq, tk, q_off):
    rows = []
    for qi in range(nq):
        q_lo = q_off + qi * tq
        q_hi = q_lo + tq - 1
        kjs = [kj for kj in range(nk) if kj * tk < q_hi]
        if not kjs:
            kjs = [0]
        kjs = kjs[::-1]
        for n, kj in enumerate(kjs):
            masked = int(kj * tk + tk - 1 >= q_lo)
            rows.append((qi, kj, int(n == 0), int(n == len(kjs) - 1), masked))
    return np.asarray(rows, dtype=np.int32).T.copy()


def _sb_attention(q, k, v, tq, tk, q_off):
    b, t, _ = q.shape
    s_len = k.shape[1]
    tab = _sb_table(t // tq, s_len // tk, tq, tk, q_off)
    tri = jnp.asarray(np.tril(np.ones((tk, tk), np.float32)), dtype=BF16)
    pairs = SB_PAIRS_PER_STEP
    width = pairs * LANES
    grid_spec = pltpu.PrefetchScalarGridSpec(
        num_scalar_prefetch=1,
        grid=(b, N_PAIRS // pairs, tab.shape[1]),
        in_specs=[pl.BlockSpec((1, tq, width), lambda bi, hp, s, tab: (bi, tab[0, s], hp)),
                  pl.BlockSpec((1, tk, width), lambda bi, hp, s, tab: (bi, tab[1, s], hp)),
                  pl.BlockSpec((1, tk, width), lambda bi, hp, s, tab: (bi, tab[1, s], hp)),
                  pl.BlockSpec((tk, tk), lambda bi, hp, s, tab: (0, 0))],
        out_specs=pl.BlockSpec((1, tq, width), lambda bi, hp, s, tab: (bi, tab[0, s], hp)),
        scratch_shapes=[pltpu.VMEM((tq, LANES), F32)] * (4 * pairs),
    )
    return pl.pallas_call(
        functools.partial(_sb_kernel, tq=tq, tk=tk, q_off=q_off, pairs=pairs),
        out_shape=jax.ShapeDtypeStruct((b, t, WIDTH), BF16),
        grid_spec=grid_spec,
        compiler_params=_params(("parallel", "parallel", "arbitrary")),
        name="sb_attn",
    )(jnp.asarray(tab), q, k, v, tri)


def _sortable(x):
    bits = pltpu.bitcast(x + 0.0, I32)
    return jnp.where(bits < 0, bits ^ 0x7FFFFFFF, bits)


def _alibi_slope2(h):
    return LOG2E * 2.0 ** (-8.0 * (h + 1) / N_HEADS)


def _dsa_kernel(q_ref, iq_ref, iwt_ref, k_ref, vt_ref, ki_ref, o_ref, keys_ref, dm_ref, qt_ref, iqt_ref, *stats,
                 tq, tk, q_off, n_kb_max, n_sel, idx_bits):
    m_refs, l_refs, acc_refs = stats[:N_HEADS], stats[N_HEADS:2 * N_HEADS], stats[2 * N_HEADS:]
    qi = pl.program_id(1)
    q_lo = q_off + qi * tq
    adm_end = ((q_lo + tq - 1) // CHUNK + 1) * CHUNK
    n_kb = jnp.minimum((adm_end + tk - 1) // tk, n_kb_max)

    lo_half = lax.broadcasted_iota(I32, (tq, LANES), 1) < HEAD_DIM
    qpos = q_lo + lax.broadcasted_iota(I32, (tk, tq), 1)
    krow = lax.broadcasted_iota(I32, (tk, tq), 0)

    def heads_of(x_ref):
        out = []
        for hp in range(N_PAIRS):
            pair = x_ref[0, :, hp * LANES:(hp + 1) * LANES]
            zero = jnp.zeros_like(pair)
            out += [jnp.where(lo_half, pair, zero), jnp.where(lo_half, zero, pair)]
        return out

    for src, dst in ((iq_ref, iqt_ref), (q_ref, qt_ref)):
        for h, xe in enumerate(heads_of(src)):
            dst[h] = xe.astype(F32).T.astype(BF16)

    def score_block(j, carry):
        ki = ki_ref[0, pl.ds(pl.multiple_of(j * tk, tk), tk), :]
        score = jnp.zeros((tk, tq), F32)
        for h in range(N_HEADS):
            score = score + iwt_ref[0, h:h + 1, :] * jnp.maximum(_dot(ki, iqt_ref[h]), 0.0)
        adm = ((j * tk + krow) // CHUNK) <= (qpos // CHUNK)
        keys_ref[j] = jnp.where(adm, _sortable(score), INT_MIN)
        return carry

    lax.fori_loop(0, n_kb, score_block, 0)

    def count_cols(pred_fn):
        def body(j, acc):
            hit = pred_fn(keys_ref[j], j).astype(I32)
            return acc + jnp.sum(hit.reshape(tk // 8, 8, tq), axis=0)
        acc = lax.fori_loop(0, n_kb, body, jnp.zeros((8, tq), I32))
        return jnp.sum(acc, axis=0, keepdims=True)

    def bit_step(i, thr):
        cand = thr + jnp.left_shift(jnp.int32(1), 31 - i)
        cnt = count_cols(lambda kb, j: kb >= cand)
        return jnp.where(cnt >= n_sel, cand, thr)

    thr = lax.fori_loop(0, 32, bit_step, jnp.full((1, tq), INT_MIN, I32))
    n_gt = count_cols(lambda kb, j: kb > thr)
    n_ge = count_cols(lambda kb, j: kb >= thr)
    need = n_sel - n_gt
    real = thr > INT_MIN
    tie = jnp.max(jnp.where(real & (n_ge > n_sel), 1, 0)) > 0

    def tie_cut():
        def idx_step(i, cut):
            cand = cut + jnp.left_shift(jnp.int32(1), idx_bits - 1 - i)
            cnt = count_cols(lambda kb, j: (kb == thr) & ((j * tk + krow) < cand))
            return jnp.where(cnt < need, cand, cut)
        return lax.fori_loop(0, idx_bits, idx_step, jnp.zeros((1, tq), I32))

    cut = lax.cond(tie, tie_cut, lambda: jnp.full((1, tq), n_kb_max * tk, I32))

    def mask_block(j, carry):
        kb = keys_ref[j]
        kpos = j * tk + krow
        sel = (kb > thr) | ((kb == thr) & real & (kpos <= cut))
        dm_ref[j] = jnp.where(sel, jnp.abs(qpos - kpos).astype(F32), DIST_MASKED)
        return carry

    lax.fori_loop(0, n_kb, mask_block, 0)

    for h in range(N_HEADS):
        m_refs[h][...] = jnp.full_like(m_refs[h], NEG_BIG)
        l_refs[h][...] = jnp.zeros_like(l_refs[h])
        acc_refs[h][...] = jnp.zeros_like(acc_refs[h])

    def attn_block(j, carry):
        start = pl.multiple_of(j * tk, tk)
        dm = dm_ref[j]

        def qk(h):
            return _dot(k_ref[0, pl.ds(start, tk), (h // 2) * LANES:(h // 2 + 1) * LANES], qt_ref[h])

        ahead = 4
        pending = [qk(h) for h in range(ahead)]
        for hp in range(N_PAIRS):
            for e in range(2):
                h = 2 * hp + e
                s_this = pending.pop(0)
                if h + ahead < N_HEADS:
                    pending.append(qk(h + ahead))
                t2 = s_this - _alibi_slope2(h) * dm
                m_old = m_refs[h][...]
                m_new = jnp.maximum(m_old, jnp.max(t2, axis=0, keepdims=True))
                alpha = jnp.exp2(m_old - m_new)
                p = jnp.exp2(t2 - m_new)
                l_refs[h][...] = alpha * l_refs[h][...] + jnp.sum(p, axis=0, keepdims=True)
                vt = vt_ref[0, j, hp * LANES + e * HEAD_DIM:hp * LANES + (e + 1) * HEAD_DIM, :]
                acc_refs[h][...] = alpha * acc_refs[h][...] + _dot(vt, p.astype(BF16))
                m_refs[h][...] = m_new
        return carry

    lax.fori_loop(0, n_kb, attn_block, 0)

    for hp in range(N_PAIRS):
        ot = jnp.concatenate([acc_refs[2 * hp][...] / l_refs[2 * hp][...],
                              acc_refs[2 * hp + 1][...] / l_refs[2 * hp + 1][...]], axis=0)
        o_ref[0, :, hp * LANES:(hp + 1) * LANES] = ot.T.astype(o_ref.dtype)


def _dsa_attention(q, iq, iw, k, v, ki2, tq, tk, q_off, n_sel):
    b, t, _ = q.shape
    s_len = k.shape[1]
    n_kb_max = s_len // tk
    iwt = jnp.swapaxes(iw, 1, 2)
    vt = jnp.swapaxes(v.reshape(b, n_kb_max, tk, WIDTH), 2, 3)
    qspec = lambda w: pl.BlockSpec((1, tq, w), lambda bi, i: (bi, i, 0))
    kspec = lambda w: pl.BlockSpec((1, s_len, w), lambda bi, i: (bi, 0, 0), pipeline_mode=pl.Buffered(1))
    return pl.pallas_call(
        functools.partial(_dsa_kernel, tq=tq, tk=tk, q_off=q_off, n_kb_max=n_kb_max, n_sel=n_sel,
                          idx_bits=max(1, int(math.ceil(math.log2(s_len + 1))))),
        out_shape=jax.ShapeDtypeStruct((b, t, WIDTH), BF16),
        grid=(b, t // tq),
        in_specs=[qspec(WIDTH), qspec(WIDTH),
                  pl.BlockSpec((1, N_HEADS, tq), lambda bi, i: (bi, 0, i)),
                  kspec(WIDTH),
                  pl.BlockSpec((1, n_kb_max, WIDTH, tk), lambda bi, i: (bi, 0, 0, 0), pipeline_mode=pl.Buffered(1)),
                  kspec(LANES)],
        out_specs=qspec(WIDTH),
        scratch_shapes=[pltpu.VMEM((n_kb_max, tk, tq), I32), pltpu.VMEM((n_kb_max, tk, tq), F32),
                        pltpu.VMEM((N_HEADS, LANES, tq), BF16), pltpu.VMEM((N_HEADS, LANES, tq), BF16)]
                       + [pltpu.VMEM((1, tq), F32)] * (2 * N_HEADS) + [pltpu.VMEM((HEAD_DIM, tq), F32)] * N_HEADS,
        compiler_params=_params(("parallel", "arbitrary")),
        name="dsa_attn",
    )(q, iq, iwt, k, vt, ki2)


def _mix_kernel(oa_ref, ob_ref, ga_ref, gb_ref, x_ref, mod_ref, gpost_ref, gpre_ref,
                wa_ref, wb_ref, wo_ref, wq_ref, x1_ref, h2_ref, pq_ref):
    mod = mod_ref[0]
    mixed = ga_ref[...] * _dot(oa_ref[...], wa_ref[...]) + gb_ref[...] * _dot(ob_ref[...], wb_ref[...])
    y = _dot(mixed.astype(BF16), wo_ref[...])
    x1 = x_ref[...] + mod[2:3, :] * _rms(y, gpost_ref[...])
    x1_ref[...] = x1
    h2 = _rms(x1, gpre_ref[...]) * (1.0 + mod[4:5, :]) + mod[3:4, :]
    h2_ref[...] = h2
    pq_ref[...] = _dot(h2.astype(BF16), wq_ref[...]).astype(BF16)


def _mix(oa, ob, ga, gb, x2, mod3, g_post, g_pre, wa, wb, wo, wq, tm, tiles_per_seq):
    n = x2.shape[0]
    row = lambda i: (i, 0)
    const = lambda i: (0, 0)
    nq = wq.shape[1]
    wspec = lambda r, c: pl.BlockSpec((r, c), const, pipeline_mode=pl.Buffered(1))
    return pl.pallas_call(
        _mix_kernel,
        out_shape=[jax.ShapeDtypeStruct((n, D_MODEL), F32),
                   jax.ShapeDtypeStruct((n, D_MODEL), F32),
                   jax.ShapeDtypeStruct((n, nq), BF16)],
        grid=(n // tm,),
        in_specs=[pl.BlockSpec((tm, WIDTH), row), pl.BlockSpec((tm, WIDTH), row),
                  pl.BlockSpec((tm, D_MODEL), row), pl.BlockSpec((tm, D_MODEL), row),
                  pl.BlockSpec((tm, D_MODEL), row),
                  pl.BlockSpec((1, 6, D_MODEL), lambda i: (i // tiles_per_seq, 0, 0)),
                  pl.BlockSpec((1, D_MODEL), const), pl.BlockSpec((1, D_MODEL), const),
                  wspec(WIDTH, D_MODEL), wspec(WIDTH, D_MODEL), wspec(D_MODEL, D_MODEL),
                  wspec(D_MODEL, nq)],
        out_specs=[pl.BlockSpec((tm, D_MODEL), row), pl.BlockSpec((tm, D_MODEL), row),
                   pl.BlockSpec((tm, nq), row)],
        compiler_params=_params(("parallel",)),
        name="mix",
    )(oa, ob, ga, gb, x2, mod3, g_post, g_pre, wa, wb, wo, wq)


def _top16_many(arrays, payloads=None):
    state = list(arrays)
    pos = [lax.broadcasted_iota(I32, a.shape, 0) for a in arrays]
    vals = [[] for _ in arrays]
    picks = [[] for _ in arrays]
    for _ in range(PEER_TOPK):
        for c, s in enumerate(state):
            m = jnp.max(s, axis=0, keepdims=True)
            p = jnp.min(jnp.where(s == m, pos[c], s.shape[0]), axis=0, keepdims=True)
            hit = pos[c] == p
            vals[c].append(m)
            picks[c].append(p if payloads is None
                            else jnp.sum(jnp.where(hit, payloads[c], 0), axis=0, keepdims=True))
            state[c] = jnp.where(hit, -jnp.inf, s)
    return [(jnp.concatenate(v, axis=0), jnp.concatenate(p, axis=0)) for v, p in zip(vals, picks)]


def _peersel_kernel(pq_ref, sk_ref, idx_ref, g_ref, *, heads):
    tt = pq_ref.shape[0]
    scores = []
    for h in range(heads):
        for half in range(2):
            lo = (2 * h + half) * PEER_DHALF
            scores.append(_dot_nt(sk_ref[h, half], pq_ref[:, lo:lo + PEER_DHALF]))
    tops = _top16_many(scores)
    half_k = PEER_TOPK // 2
    row8 = lax.broadcasted_iota(I32, (half_k, tt), 0)
    cands, cidxs = [], []
    for h in range(heads):
        (v1, i1), (v2, i2) = tops[2 * h], tops[2 * h + 1]
        cand = [v1[0:1] + v2]
        cidx = [i1[0:1] * PEER_NKEYS + i2]
        for i in range(1, half_k):
            cand.append(jnp.where(row8 < PEER_TOPK // (i + 1), v1[i:i + 1] + v2[0:half_k], -jnp.inf))
            cidx.append(i1[i:i + 1] * PEER_NKEYS + i2[0:half_k])
        cand.append(v1[half_k:] + v2[0:1])
        cidx.append(i1[half_k:] * PEER_NKEYS + i2[0:1])
        cands.append(jnp.concatenate(cand, axis=0))
        cidxs.append(jnp.concatenate(cidx, axis=0))
    for h, (top, eidx) in enumerate(_top16_many(cands, cidxs)):
        ex = jnp.exp(top - top[0:1])
        g_ref[h] = ex / jnp.sum(ex, axis=0, keepdims=True)
        idx_ref[h] = eidx


def _peer_select(pq, subkeys_b, tt):
    n = pq.shape[0]
    heads = PEER_SELECT_HEADS
    return pl.pallas_call(
        functools.partial(_peersel_kernel, heads=heads),
        out_shape=[jax.ShapeDtypeStruct((PEER_HEADS, PEER_TOPK, n), I32),
                   jax.ShapeDtypeStruct((PEER_HEADS, PEER_TOPK, n), F32)],
        grid=(n // tt, PEER_HEADS // heads),
        in_specs=[pl.BlockSpec((tt, heads * 2 * PEER_DHALF), lambda i, h: (i, h)),
                  pl.BlockSpec((heads, 2, PEER_NKEYS, PEER_DHALF), lambda i, h: (h, 0, 0, 0))],
        out_specs=[pl.BlockSpec((heads, PEER_TOPK, tt), lambda i, h: (h, 0, i)),
                   pl.BlockSpec((heads, PEER_TOPK, tt), lambda i, h: (h, 0, i))],
        compiler_params=_params(("parallel", "arbitrary")),
        name="peer_select",
    )(pq, subkeys_b)


def _erf_gelu(a):
    return 0.5 * a * (1.0 + lax.erf(a * (2.0 ** -0.5)))


def _peerffn_kernel(idx_ref, idx_next_ref, gt_ref, h2_ref, x1_ref, mod_ref, gpost_ref, uv_hbm, o_ref,
                    *scratch, tg, seqs_per_tile):
    bufs, (f_ref, sem) = scratch[:PEER_SLOTS], scratch[PEER_SLOTS:]
    step, last_step = pl.program_id(0), pl.num_programs(0) - 1

    def start_token(t, slot, src=None):
        src = idx_ref if src is None else src
        for k in range(PEER_SEL):
            pltpu.make_async_copy(uv_hbm.at[src[t, k]], bufs[slot].at[:, pl.ds(k, 1), :],
                                  sem.at[slot]).start(priority=k % 2)

    def wait_token(slot):
        pltpu.make_async_copy(bufs[slot], bufs[slot], sem.at[slot]).wait()

    lane_tok = lax.broadcasted_iota(I32, (PEER_SEL, tg), 1)
    sub8 = lax.broadcasted_iota(I32, (8, D_MODEL), 0)

    def compute(t, slot):
        buf = bufs[slot]
        t8 = pl.multiple_of((t // 8) * 8, 8)
        h8 = h2_ref[pl.ds(t8, 8), :]
        hrow = jnp.sum(jnp.where(sub8 == t - t8, h8, 0.0), axis=0, keepdims=True)
        acc = None
        for c in range(N_CHUNK):
            term = buf[c] * hrow[:, c * LANES:(c + 1) * LANES]
            acc = term if acc is None else acc + term
        a = jnp.sum(acc, axis=1, keepdims=True)
        g = jnp.sum(jnp.where(lane_tok == t, gt_ref[...], 0.0), axis=1, keepdims=True)
        w = g * _erf_gelu(a)
        pieces = []
        for c in range(N_CHUNK):
            vc = buf[N_CHUNK + c]
            pieces.append(jnp.sum(vc * w, axis=0, keepdims=True))
        f_ref[pl.ds(t, 1), :] = jnp.concatenate(pieces, axis=1)

    @pl.when(step == 0)
    def _():
        for t in range(PEER_AHEAD):
            start_token(t, t)

    def group(i, carry):
        t0 = PEER_SLOTS * i
        for j in range(PEER_SLOTS):
            wait_token(j)
            start_token(t0 + j + PEER_AHEAD, (j + PEER_AHEAD) % PEER_SLOTS)
            compute(t0 + j, j)
        return carry

    n_groups = tg // PEER_SLOTS
    lax.fori_loop(0, n_groups - 1, group, 0)
    t0 = tg - PEER_SLOTS
    for j in range(PEER_SLOTS):
        wait_token(j)
        ahead = j + PEER_AHEAD
        if ahead < PEER_SLOTS:
            start_token(t0 + ahead, ahead)
        else:
            start_token(ahead - PEER_SLOTS, ahead - PEER_SLOTS, idx_next_ref)
        compute(t0 + j, j)

    @pl.when(step == last_step)
    def _():
        for t in range(PEER_AHEAD):
            wait_token(t)

    rows = tg // seqs_per_tile
    for s in range(seqs_per_tile):
        sl = slice(s * rows, (s + 1) * rows)
        o_ref[sl, :] = x1_ref[sl, :] + mod_ref[s][5:6, :] * _rms(f_ref[sl, :], gpost_ref[...])


def _peer_ffn(idx, gt, h2, x1, mod3, g_post, uv, tg, seq_len):
    n = h2.shape[0]
    row = lambda i: (i, 0)
    seqs_per_tile = max(1, tg // seq_len)
    tiles_per_seq = max(1, seq_len // tg)
    return pl.pallas_call(
        functools.partial(_peerffn_kernel, tg=tg, seqs_per_tile=seqs_per_tile),
        out_shape=jax.ShapeDtypeStruct((n, D_MODEL), F32),
        grid=(n // tg,),
        in_specs=[pl.BlockSpec((tg, PEER_SEL), row, memory_space=pltpu.SMEM),
                  pl.BlockSpec((tg, PEER_SEL), lambda i: (jnp.minimum(i + 1, n // tg - 1), 0), memory_space=pltpu.SMEM),
                  pl.BlockSpec((PEER_SEL, tg), lambda i: (0, i)),
                  pl.BlockSpec((tg, D_MODEL), row),
                  pl.BlockSpec((tg, D_MODEL), row),
                  pl.BlockSpec((seqs_per_tile, 6, D_MODEL), lambda i: (i // tiles_per_seq, 0, 0)),
                  pl.BlockSpec((1, D_MODEL), lambda i: (0, 0)),
                  pl.BlockSpec(memory_space=pl.ANY)],
        out_specs=pl.BlockSpec((tg, D_MODEL), row),
        scratch_shapes=[pltpu.VMEM((2 * N_CHUNK, PEER_SEL, LANES), F32)] * PEER_SLOTS
                       + [pltpu.VMEM((tg, D_MODEL), F32), pltpu.SemaphoreType.DMA((PEER_SLOTS,))],
        compiler_params=_params(("arbitrary",)),
        name="peer_ffn",
    )(idx, idx, gt, h2, x1, mod3, g_post, uv)


def _pad_keys(x, s_pad):
    return jnp.pad(x, ((0, 0), (0, s_pad - x.shape[1]), (0, 0)))


def _layer(x, mod3, past, weights, tiles):
    (g_pre_mix, g_post_mix, g_pre_ffn, g_post_ffn, w_in_b, w_gate_b, wa_b, wb_b, wo_b, wq_b,
     subkeys_b, peer_uv) = weights
    tm, sb_tq, sb_tk, dsa_tq, dsa_tk, tt, tg = tiles
    bsz, t, _ = x.shape
    n = bsz * t
    x2 = x.reshape(n, D_MODEL)
    (aq, ak, av, bq, bk, bv, iq, ki2, akf, avf, bkf, bvf, ikf, iw, ga, gb) = _inproj(
        x2, mod3, g_pre_mix, w_in_b, w_gate_b, tm, t // tm)
    seq = lambda a: a.reshape(bsz, t, a.shape[-1])
    if past is None:
        q_off = 0
        ka, va, kb, vb, ki = seq(ak), seq(av), seq(bk), seq(bv), seq(ki2)
        s_len = t
    else:
        p_sbk, p_sbv, p_dk, p_dv, p_ki = past
        q_off = p_sbk.shape[1]
        s_len = q_off + t
        flat = lambda c: c.reshape(bsz, q_off, WIDTH).astype(BF16)
        ka = jnp.concatenate([flat(p_sbk), seq(ak)], axis=1)
        va = jnp.concatenate([flat(p_sbv), seq(av)], axis=1)
        kb = jnp.concatenate([flat(p_dk), seq(bk)], axis=1)
        vb = jnp.concatenate([flat(p_dv), seq(bv)], axis=1)
        pk = p_ki.astype(BF16)
        ki = jnp.concatenate([jnp.concatenate([pk, pk], axis=-1), seq(ki2)], axis=1)
    s_pad = -(-s_len // max(sb_tk, dsa_tk)) * max(sb_tk, dsa_tk)
    ka, va, kb, vb, ki = (_pad_keys(a, s_pad) for a in (ka, va, kb, vb, ki))
    n_sel = min(TOPK_MAX, s_len // 4)

    oa = _sb_attention(seq(aq), ka, va, sb_tq, sb_tk, q_off)
    ob = _dsa_attention(seq(bq), seq(iq), seq(iw), kb, vb, ki, dsa_tq, dsa_tk, q_off, n_sel)

    x1, h2, pq = _mix(oa.reshape(n, WIDTH), ob.reshape(n, WIDTH), ga, gb, x2, mod3, g_post_mix, g_pre_ffn,
                      wa_b, wb_b, wo_b, wq_b, tm, t // tm)
    eidx, gate = _peer_select(pq, subkeys_b, tt)
    eidx = eidx.reshape(PEER_SEL, n).T
    y = _peer_ffn(eidx, gate.reshape(PEER_SEL, n), h2, x1, mod3, g_post_ffn, peer_uv, tg, t)
    new = tuple(a.reshape(1, bsz, t, N_HEADS, HEAD_DIM) for a in (akf, avf, bkf, bvf)) + (
        ikf.reshape(1, bsz, t, HEAD_DIM),)
    return y.reshape(bsz, t, D_MODEL), new


def _tiles(t):
    if t % 256 == 0:
        return (256, 256, 256, 256, 256, 128, 128)
    return (t, t, 128, t, 128, 128, 128)


def kernel(x_prompt, x_sample, c_prompt, c_sample, cache_sb_k, cache_sb_v, cache_dsa_k, cache_dsa_v, cache_dsa_kidx, w_ada, b_ada, g_pre_mix, g_post_mix, g_pre_ffn, g_post_ffn, w_in, w_gate, w_branch_a, w_branch_b, w_out, w_peer_q, peer_subkeys, peer_u, peer_v):
    assert w_ada.shape[0] == 1, "one layer"
    n_p, n_s = c_prompt.shape[0], c_sample.shape[0]
    c_all = jnp.concatenate([c_prompt, c_sample], axis=0)
    rows = -(-c_all.shape[0] // 8) * 8
    c_all = jnp.pad(c_all, ((0, rows - c_all.shape[0]), (0, 0)))
    mod = _modulation(c_all, w_ada[0], b_ada[0]).reshape(rows, 6, D_MODEL)
    w_in_b = jnp.pad(w_in[0], ((0, 0), (0, W_IN_PAD - W_IN_COLS))).astype(BF16)
    weights = (g_pre_mix, g_post_mix, g_pre_ffn, g_post_ffn, w_in_b, w_gate[0].astype(BF16),
               w_branch_a[0].astype(BF16), w_branch_b[0].astype(BF16), w_out[0].astype(BF16),
               w_peer_q[0].astype(BF16), peer_subkeys[0].astype(BF16),
               jnp.concatenate([peer_u[0].reshape(-1, N_CHUNK, 1, LANES),
                                peer_v[0].reshape(-1, N_CHUNK, 1, LANES)], axis=1))
    yp, new_p = _layer(x_prompt, mod[:n_p], None, weights, _tiles(x_prompt.shape[1]))
    past = (cache_sb_k[0], cache_sb_v[0], cache_dsa_k[0], cache_dsa_v[0], cache_dsa_kidx[0])
    ys, new_s = _layer(x_sample, mod[n_p:n_p + n_s], past, weights, _tiles(x_sample.shape[1]))
    return (yp, ys) + new_p + new_s
```

```python
import functools
import math

import jax
import jax.numpy as jnp
import numpy as np
from jax import lax
from jax.experimental import pallas as pl
from jax.experimental.pallas import tpu as pltpu

F32 = jnp.float32
BF16 = jnp.bfloat16
I32 = jnp.int32
I16 = jnp.int16

D_MODEL = 1024
CHUNK = 64
EPS = 1e-6
N_HEADS = 8
HEAD_DIM = 64
WIDTH = N_HEADS * HEAD_DIM
N_PAIRS = N_HEADS // 2
SB_PAIRS_PER_STEP = 4
LANES = 128
N_CHUNK = D_MODEL // LANES
TOPK_MAX = 256
PEER_HEADS = 8
PEER_NKEYS = 128
PEER_DHALF = 128
PEER_TOPK = 16
PEER_SEL = PEER_HEADS * PEER_TOPK
PEER_SELECT_HEADS = 4
PEER_SLOTS = 8
PEER_AHEAD = 6
W_IN_COLS = 7 * WIDTH + HEAD_DIM + N_HEADS
W_IN_PAD = 7 * WIDTH + LANES
INT_MIN = -(2 ** 31)
HALF_BIAS = 2 ** 15
NEG_BIG = -1e30
DIST_MASKED = 1e30
LOG2E = 1.4426950408889634
VMEM_LIMIT = 56 * 1024 * 1024


def _params(sem):
    return pltpu.CompilerParams(dimension_semantics=sem, vmem_limit_bytes=VMEM_LIMIT)


def _dot(a, b):
    return jnp.dot(a, b, preferred_element_type=F32)


def _dot_nt(a, b):
    return lax.dot_general(a, b, (((1,), (1,)), ((), ())), preferred_element_type=F32)


def _rms(x, gain):
    return x * lax.rsqrt(jnp.mean(x * x, axis=-1, keepdims=True) + EPS) * gain


def _mod_kernel(c_ref, w_ref, b_ref, o_ref):
    c = c_ref[...]
    s = c * (1.0 / (1.0 + jnp.exp(-c)))
    o_ref[...] = jnp.dot(s, w_ref[...], preferred_element_type=F32,
                         precision=lax.Precision.HIGHEST) + b_ref[...]


def _modulation(c, w_ada, b_ada):
    n = c.shape[0]
    cols = w_ada.shape[1]
    tn = 1024
    return pl.pallas_call(
        _mod_kernel,
        out_shape=jax.ShapeDtypeStruct((n, cols), F32),
        grid=(cols // tn,),
        in_specs=[pl.BlockSpec((n, D_MODEL), lambda j: (0, 0)),
                  pl.BlockSpec((D_MODEL, tn), lambda j: (0, j)),
                  pl.BlockSpec((1, tn), lambda j: (0, j))],
        out_specs=pl.BlockSpec((n, tn), lambda j: (0, j)),
        compiler_params=_params(("arbitrary",)),
        name="mod",
    )(c, w_ada, b_ada.reshape(1, cols))


def _inproj_kernel(x_ref, mod_ref, g_ref, win_ref, wg_ref,
                   aq_ref, ak_ref, av_ref, bq_ref, bk_ref, bv_ref, iq_ref, ki2_ref,
                   akf_ref, avf_ref, bkf_ref, bvf_ref, ikf_ref, iw_ref, ga_ref, gb_ref):
    x = x_ref[...]
    mod = mod_ref[0]
    h = _rms(x, g_ref[...]) * (1.0 + mod[1:2, :]) + mod[0:1, :]
    hb = h.astype(BF16)
    qscale = HEAD_DIM ** -0.5

    def seg(i):
        return _dot(hb, win_ref[:, i * WIDTH:(i + 1) * WIDTH])

    aq_ref[...] = (seg(0) * (qscale * LOG2E)).astype(BF16)
    z = seg(1)
    akf_ref[...] = z
    ak_ref[...] = z.astype(BF16)
    z = seg(2)
    avf_ref[...] = z
    av_ref[...] = z.astype(BF16)
    bq_ref[...] = (seg(3) * (qscale * LOG2E)).astype(BF16)
    z = seg(4)
    bkf_ref[...] = z
    bk_ref[...] = z.astype(BF16)
    z = seg(5)
    bvf_ref[...] = z
    bv_ref[...] = z.astype(BF16)
    iq_ref[...] = (seg(6) * qscale).astype(BF16)
    tail = _dot(hb, win_ref[:, 7 * WIDTH:7 * WIDTH + LANES])
    ik = tail[:, :HEAD_DIM]
    ikf_ref[...] = ik
    ki2_ref[...] = jnp.concatenate([ik, ik], axis=-1).astype(BF16)
    iw_ref[...] = tail[:, HEAD_DIM:HEAD_DIM + N_HEADS] * (N_HEADS ** -0.5)
    zg = _dot(hb, wg_ref[...])
    gate = 1.0 / (1.0 + jnp.exp(-zg))
    ga_ref[...] = gate[:, :D_MODEL]
    gb_ref[...] = gate[:, D_MODEL:]


def _inproj(x2, mod3, g_pre, w_in_b, w_gate_b, tm, tiles_per_seq):
    n = x2.shape[0]
    row = lambda i: (i, 0)
    const = lambda i: (0, 0)
    bspec = lambda w: pl.BlockSpec((tm, w), row)
    outs = ([jax.ShapeDtypeStruct((n, WIDTH), BF16)] * 7
            + [jax.ShapeDtypeStruct((n, LANES), BF16)]
            + [jax.ShapeDtypeStruct((n, WIDTH), F32)] * 4
            + [jax.ShapeDtypeStruct((n, HEAD_DIM), F32),
               jax.ShapeDtypeStruct((n, N_HEADS), F32),
               jax.ShapeDtypeStruct((n, D_MODEL), F32),
               jax.ShapeDtypeStruct((n, D_MODEL), F32)])
    out_specs = ([bspec(WIDTH)] * 7 + [bspec(LANES)] + [bspec(WIDTH)] * 4
                 + [bspec(HEAD_DIM), bspec(N_HEADS), bspec(D_MODEL), bspec(D_MODEL)])
    return pl.pallas_call(
        _inproj_kernel,
        out_shape=outs,
        grid=(n // tm,),
        in_specs=[pl.BlockSpec((tm, D_MODEL), row),
                  pl.BlockSpec((1, 6, D_MODEL), lambda i: (i // tiles_per_seq, 0, 0)),
                  pl.BlockSpec((1, D_MODEL), const),
                  pl.BlockSpec((D_MODEL, W_IN_PAD), const, pipeline_mode=pl.Buffered(1)),
                  pl.BlockSpec((D_MODEL, 2 * D_MODEL), const, pipeline_mode=pl.Buffered(1))],
        out_specs=out_specs,
        compiler_params=_params(("parallel",)),
        name="inproj",
    )(x2, mod3, g_pre, w_in_b, w_gate_b)


def _sb_kernel(tab_ref, q_ref, k_ref, v_ref, tri_ref, o_ref, *scratch, tq, tk, q_off, pairs):
    s = pl.program_id(2)
    qi = tab_ref[0, s]
    kj = tab_ref[1, s]
    first = tab_ref[2, s]
    last = tab_ref[3, s]
    masked = tab_ref[4, s]
    n_heads = 2 * pairs
    acc_refs, run_refs = scratch[:n_heads], scratch[n_heads:]

    @pl.when(first == 1)
    def _():
        for r in scratch:
            r[...] = jnp.zeros_like(r)

    lo_lanes = lax.broadcasted_iota(I32, (tq, LANES), 1) < HEAD_DIM

    def step(use_mask):
        tri = tri_ref[...]
        if use_mask:
            qpos = q_off + qi * tq + lax.broadcasted_iota(I32, (tq, tk), 0)
            kpos = kj * tk + lax.broadcasted_iota(I32, (tq, tk), 1)
            causal = kpos < qpos
        z = []
        for p in range(pairs):
            q = q_ref[0, :, p * LANES:(p + 1) * LANES]
            k = k_ref[0, :, p * LANES:(p + 1) * LANES]
            zero = jnp.zeros_like(q)
            z += [_dot_nt(jnp.where(lo_lanes, q, zero), k), _dot_nt(jnp.where(lo_lanes, zero, q), k)]
        suf = []
        for h in range(n_heads):
            sp = jnp.maximum(z[h], 0.0) + jnp.log2(1.0 + jnp.exp2(-jnp.abs(z[h])))
            if use_mask:
                sp = jnp.where(causal, sp, 0.0)
            hi = sp.astype(BF16)
            lo = (sp - hi.astype(F32)).astype(BF16)
            suf.append(_dot(hi, tri) + _dot(lo, tri) + run_refs[h][:, 0:1])
        for h in range(n_heads):
            w = jnp.exp2(z[h] - suf[h])
            if use_mask:
                w = jnp.where(causal, w, 0.0)
            p = h // 2
            acc_refs[h][...] += _dot(w.astype(BF16), v_ref[0, :, p * LANES:(p + 1) * LANES])
            run_refs[h][...] = jnp.broadcast_to(suf[h][:, 0:1], (tq, LANES))

    @pl.when(masked == 1)
    def _():
        step(True)

    @pl.when(masked == 0)
    def _():
        step(False)

    @pl.when(last == 1)
    def _():
        for p in range(pairs):
            o_ref[0, :, p * LANES:(p + 1) * LANES] = jnp.where(
                lo_lanes, acc_refs[2 * p][...], acc_refs[2 * p + 1][...]).astype(o_ref.dtype)


def _sb_table(nq, nk, tq, tk, q_off):
    rows = []
    for qi in range(nq):
        q_lo = q_off + qi * tq
        q_hi = q_lo + tq - 1
        kjs = [kj for kj in range(nk) if kj * tk < q_hi]
        if not kjs:
            kjs = [0]
        kjs = kjs[::-1]
        for n, kj in enumerate(kjs):
            masked = int(kj * tk + tk - 1 >= q_lo)
            rows.append((qi, kj, int(n == 0), int(n == len(kjs) - 1), masked))
    return np.asarray(rows, dtype=np.int32).T.copy()


def _sb_attention(q, k, v, tq, tk, q_off):
    b, t, _ = q.shape
    s_len = k.shape[1]
    tab = _sb_table(t // tq, s_len // tk, tq, tk, q_off)
    tri = jnp.asarray(np.tril(np.ones((tk, tk), np.float32)), dtype=BF16)
    pairs = SB_PAIRS_PER_STEP
    width = pairs * LANES
    grid_spec = pltpu.PrefetchScalarGridSpec(
        num_scalar_prefetch=1,
        grid=(b, N_PAIRS // pairs, tab.shape[1]),
        in_specs=[pl.BlockSpec((1, tq, width), lambda bi, hp, s, tab: (bi, tab[0, s], hp)),
                  pl.BlockSpec((1, tk, width), lambda bi, hp, s, tab: (bi, tab[1, s], hp)),
                  pl.BlockSpec((1, tk, width), lambda bi, hp, s, tab: (bi, tab[1, s], hp)),
                  pl.BlockSpec((tk, tk), lambda bi, hp, s, tab: (0, 0))],
        out_specs=pl.BlockSpec((1, tq, width), lambda bi, hp, s, tab: (bi, tab[0, s], hp)),
        scratch_shapes=[pltpu.VMEM((tq, LANES), F32)] * (4 * pairs),
    )
    return pl.pallas_call(
        functools.partial(_sb_kernel, tq=tq, tk=tk, q_off=q_off, pairs=pairs),
        out_shape=jax.ShapeDtypeStruct((b, t, WIDTH), BF16),
        grid_spec=grid_spec,
        compiler_params=_params(("parallel", "parallel", "arbitrary")),
        name="sb_attn",
    )(jnp.asarray(tab), q, k, v, tri)


def _sortable(x):
    bits = pltpu.bitcast(x + 0.0, I32)
    return jnp.where(bits < 0, bits ^ 0x7FFFFFFF, bits)


def _alibi_slope2(h):
    return LOG2E * 2.0 ** (-8.0 * (h + 1) / N_HEADS)


def _dsa_kernel(q_ref, iq_ref, iwt_ref, k_ref, vt_ref, ki_ref, o_ref, keys_ref, dm_ref, hi_ref, lo_ref, qt_ref, iqt_ref,
                *stats,
                 tq, tk, q_off, n_kb_max, n_sel, idx_bits):
    m_refs, l_refs, acc_refs = stats[:N_HEADS], stats[N_HEADS:2 * N_HEADS], stats[2 * N_HEADS:]
    qi = pl.program_id(1)
    q_lo = q_off + qi * tq
    adm_end = ((q_lo + tq - 1) // CHUNK + 1) * CHUNK
    n_kb = jnp.minimum((adm_end + tk - 1) // tk, n_kb_max)

    lo_half = lax.broadcasted_iota(I32, (tq, LANES), 1) < HEAD_DIM
    qpos = q_lo + lax.broadcasted_iota(I32, (tk, tq), 1)
    krow = lax.broadcasted_iota(I32, (tk, tq), 0)

    def heads_of(x_ref):
        out = []
        for hp in range(N_PAIRS):
            pair = x_ref[0, :, hp * LANES:(hp + 1) * LANES]
            zero = jnp.zeros_like(pair)
            out += [jnp.where(lo_half, pair, zero), jnp.where(lo_half, zero, pair)]
        return out

    for src, dst in ((iq_ref, iqt_ref), (q_ref, qt_ref)):
        for h, xe in enumerate(heads_of(src)):
            dst[h] = xe.astype(F32).T.astype(BF16)

    def score_block(j, carry):
        ki = ki_ref[0, pl.ds(pl.multiple_of(j * tk, tk), tk), :]
        score = jnp.zeros((tk, tq), F32)
        for h in range(N_HEADS):
            score = score + iwt_ref[0, h:h + 1, :] * jnp.maximum(_dot(ki, iqt_ref[h]), 0.0)
        adm = ((j * tk + krow) // CHUNK) <= (qpos // CHUNK)
        key = jnp.where(adm, _sortable(score), INT_MIN)
        keys_ref[j] = key
        hi_ref[j] = (key >> 16).astype(I16)
        lo_ref[j] = ((key & 0xFFFF) - HALF_BIAS).astype(I16)
        return carry

    lax.fori_loop(0, n_kb, score_block, 0)

    def count_cols(pred_fn):
        def body(j, acc):
            hit = pred_fn(keys_ref[j], j).astype(I32)
            return acc + jnp.sum(hit.reshape(tk // 8, 8, tq), axis=0)
        acc = lax.fori_loop(0, n_kb, body, jnp.zeros((8, tq), I32))
        return jnp.sum(acc, axis=0, keepdims=True)

    def count_half(ref, pred_fn):
        def body(j, acc):
            hit = jnp.where(pred_fn(ref[j]), jnp.int16(1), jnp.int16(0))
            for r in range(tk // 16):
                acc = acc + hit[r * 16:(r + 1) * 16, :]
            return acc
        acc = lax.fori_loop(0, n_kb, body, jnp.zeros((16, tq), I16))
        return jnp.sum(acc.astype(I32), axis=0, keepdims=True)

    def search_half(ref, target):
        def bit_step(i, t):
            cand = t + jnp.left_shift(jnp.int32(1), 15 - i)
            cand16 = cand.astype(I16)
            cnt = count_half(ref, lambda kb: kb >= cand16)
            return jnp.where(cnt >= target, cand, t)
        return lax.fori_loop(0, 16, bit_step, jnp.full((1, tq), -HALF_BIAS, I32))

    t_hi = search_half(hi_ref, n_sel)
    t_hi16 = t_hi.astype(I16)
    n_above = count_half(hi_ref, lambda kb: kb > t_hi16)

    def keep_group(j, carry):
        lo_ref[j] = jnp.where(hi_ref[j] == t_hi16, lo_ref[j], jnp.int16(-HALF_BIAS))
        return carry

    lax.fori_loop(0, n_kb, keep_group, 0)
    t_lo = search_half(lo_ref, n_sel - n_above)
    thr = t_hi * (2 * HALF_BIAS) + (t_lo + HALF_BIAS)
    n_gt = count_cols(lambda kb, j: kb > thr)
    n_ge = count_cols(lambda kb, j: kb >= thr)
    need = n_sel - n_gt
    real = thr > INT_MIN
    tie = jnp.max(jnp.where(real & (n_ge > n_sel), 1, 0)) > 0

    def tie_cut():
        def idx_step(i, cut):
            cand = cut + jnp.left_shift(jnp.int32(1), idx_bits - 1 - i)
            cnt = count_cols(lambda kb, j: (kb == thr) & ((j * tk + krow) < cand))
            return jnp.where(cnt < need, cand, cut)
        return lax.fori_loop(0, idx_bits, idx_step, jnp.zeros((1, tq), I32))

    cut = lax.cond(tie, tie_cut, lambda: jnp.full((1, tq), n_kb_max * tk, I32))

    def mask_block(j, carry):
        kb = keys_ref[j]
        kpos = j * tk + krow
        sel = (kb > thr) | ((kb == thr) & real & (kpos <= cut))
        dm_ref[j] = jnp.where(sel, jnp.abs(qpos - kpos).astype(F32), DIST_MASKED)
        return carry

    lax.fori_loop(0, n_kb, mask_block, 0)

    for h in range(N_HEADS):
        m_refs[h][...] = jnp.full_like(m_refs[h], NEG_BIG)
        l_refs[h][...] = jnp.zeros_like(l_refs[h])
        acc_refs[h][...] = jnp.zeros_like(acc_refs[h])

    def attn_block(j, carry):
        start = pl.multiple_of(j * tk, tk)
        dm = dm_ref[j]

        def qk(h):
            return _dot(k_ref[0, pl.ds(start, tk), (h // 2) * LANES:(h // 2 + 1) * LANES], qt_ref[h])

        ahead = 4
        pending = [qk(h) for h in range(ahead)]
        for hp in range(N_PAIRS):
            for e in range(2):
                h = 2 * hp + e
                s_this = pending.pop(0)
                if h + ahead < N_HEADS:
                    pending.append(qk(h + ahead))
                t2 = s_this - _alibi_slope2(h) * dm
                m_old = m_refs[h][...]
                m_new = jnp.maximum(m_old, jnp.max(t2, axis=0, keepdims=True))
                alpha = jnp.exp2(m_old - m_new)
                p = jnp.exp2(t2 - m_new)
                l_refs[h][...] = alpha * l_refs[h][...] + jnp.sum(p, axis=0, keepdims=True)
                vt = vt_ref[0, j, hp * LANES + e * HEAD_DIM:hp * LANES + (e + 1) * HEAD_DIM, :]
                acc_refs[h][...] = alpha * acc_refs[h][...] + _dot(vt, p.astype(BF16))
                m_refs[h][...] = m_new
        return carry

    lax.fori_loop(0, n_kb, attn_block, 0)

    for hp in range(N_PAIRS):
        ot = jnp.concatenate([acc_refs[2 * hp][...] / l_refs[2 * hp][...],
                              acc_refs[2 * hp + 1][...] / l_refs[2 * hp + 1][...]], axis=0)
        o_ref[0, :, hp * LANES:(hp + 1) * LANES] = ot.T.astype(o_ref.dtype)


def _dsa_attention(q, iq, iw, k, v, ki2, tq, tk, q_off, n_sel):
    b, t, _ = q.shape
    s_len = k.shape[1]
    n_kb_max = s_len // tk
    iwt = jnp.swapaxes(iw, 1, 2)
    vt = jnp.swapaxes(v.reshape(b, n_kb_max, tk, WIDTH), 2, 3)
    qspec = lambda w: pl.BlockSpec((1, tq, w), lambda bi, i: (bi, i, 0))
    kspec = lambda w: pl.BlockSpec((1, s_len, w), lambda bi, i: (bi, 0, 0), pipeline_mode=pl.Buffered(1))
    return pl.pallas_call(
        functools.partial(_dsa_kernel, tq=tq, tk=tk, q_off=q_off, n_kb_max=n_kb_max, n_sel=n_sel,
                          idx_bits=max(1, int(math.ceil(math.log2(s_len + 1))))),
        out_shape=jax.ShapeDtypeStruct((b, t, WIDTH), BF16),
        grid=(b, t // tq),
        in_specs=[qspec(WIDTH), qspec(WIDTH),
                  pl.BlockSpec((1, N_HEADS, tq), lambda bi, i: (bi, 0, i)),
                  kspec(WIDTH),
                  pl.BlockSpec((1, n_kb_max, WIDTH, tk), lambda bi, i: (bi, 0, 0, 0), pipeline_mode=pl.Buffered(1)),
                  kspec(LANES)],
        out_specs=qspec(WIDTH),
        scratch_shapes=[pltpu.VMEM((n_kb_max, tk, tq), I32), pltpu.VMEM((n_kb_max, tk, tq), F32),
                        pltpu.VMEM((n_kb_max, tk, tq), I16), pltpu.VMEM((n_kb_max, tk, tq), I16),
                        pltpu.VMEM((N_HEADS, LANES, tq), BF16), pltpu.VMEM((N_HEADS, LANES, tq), BF16)]
                       + [pltpu.VMEM((1, tq), F32)] * (2 * N_HEADS) + [pltpu.VMEM((HEAD_DIM, tq), F32)] * N_HEADS,
        compiler_params=_params(("parallel", "arbitrary")),
        name="dsa_attn",
    )(q, iq, iwt, k, vt, ki2)


def _mix_kernel(oa_ref, ob_ref, ga_ref, gb_ref, x_ref, mod_ref, gpost_ref, gpre_ref,
                wa_ref, wb_ref, wo_ref, wq_ref, x1_ref, h2_ref, pq_ref):
    mod = mod_ref[0]
    mixed = ga_ref[...] * _dot(oa_ref[...], wa_ref[...]) + gb_ref[...] * _dot(ob_ref[...], wb_ref[...])
    y = _dot(mixed.astype(BF16), wo_ref[...])
    x1 = x_ref[...] + mod[2:3, :] * _rms(y, gpost_ref[...])
    x1_ref[...] = x1
    h2 = _rms(x1, gpre_ref[...]) * (1.0 + mod[4:5, :]) + mod[3:4, :]
    h2_ref[...] = h2
    pq_ref[...] = _dot(h2.astype(BF16), wq_ref[...]).astype(BF16)


def _mix(oa, ob, ga, gb, x2, mod3, g_post, g_pre, wa, wb, wo, wq, tm, tiles_per_seq):
    n = x2.shape[0]
    row = lambda i: (i, 0)
    const = lambda i: (0, 0)
    nq = wq.shape[1]
    wspec = lambda r, c: pl.BlockSpec((r, c), const, pipeline_mode=pl.Buffered(1))
    return pl.pallas_call(
        _mix_kernel,
        out_shape=[jax.ShapeDtypeStruct((n, D_MODEL), F32),
                   jax.ShapeDtypeStruct((n, D_MODEL), F32),
                   jax.ShapeDtypeStruct((n, nq), BF16)],
        grid=(n // tm,),
        in_specs=[pl.BlockSpec((tm, WIDTH), row), pl.BlockSpec((tm, WIDTH), row),
                  pl.BlockSpec((tm, D_MODEL), row), pl.BlockSpec((tm, D_MODEL), row),
                  pl.BlockSpec((tm, D_MODEL), row),
                  pl.BlockSpec((1, 6, D_MODEL), lambda i: (i // tiles_per_seq, 0, 0)),
                  pl.BlockSpec((1, D_MODEL), const), pl.BlockSpec((1, D_MODEL), const),
                  wspec(WIDTH, D_MODEL), wspec(WIDTH, D_MODEL), wspec(D_MODEL, D_MODEL),
                  wspec(D_MODEL, nq)],
        out_specs=[pl.BlockSpec((tm, D_MODEL), row), pl.BlockSpec((tm, D_MODEL), row),
                   pl.BlockSpec((tm, nq), row)],
        compiler_params=_params(("parallel",)),
        name="mix",
    )(oa, ob, ga, gb, x2, mod3, g_post, g_pre, wa, wb, wo, wq)


def _top16_many(arrays, payloads=None):
    state = list(arrays)
    pos = [lax.broadcasted_iota(I32, a.shape, 0) for a in arrays]
    vals = [[] for _ in arrays]
    picks = [[] for _ in arrays]
    for _ in range(PEER_TOPK):
        for c, s in enumerate(state):
            m = jnp.max(s, axis=0, keepdims=True)
            p = jnp.min(jnp.where(s == m, pos[c], s.shape[0]), axis=0, keepdims=True)
            hit = pos[c] == p
            vals[c].append(m)
            picks[c].append(p if payloads is None
                            else jnp.sum(jnp.where(hit, payloads[c], 0), axis=0, keepdims=True))
            state[c] = jnp.where(hit, -jnp.inf, s)
    return [(jnp.concatenate(v, axis=0), jnp.concatenate(p, axis=0)) for v, p in zip(vals, picks)]


def _peersel_kernel(pq_ref, sk_ref, idx_ref, g_ref, *, heads):
    tt = pq_ref.shape[0]
    scores = []
    for h in range(heads):
        for half in range(2):
            lo = (2 * h + half) * PEER_DHALF
            scores.append(_dot_nt(sk_ref[h, half], pq_ref[:, lo:lo + PEER_DHALF]))
    tops = _top16_many(scores)
    half_k = PEER_TOPK // 2
    row8 = lax.broadcasted_iota(I32, (half_k, tt), 0)
    cands, cidxs = [], []
    for h in range(heads):
        (v1, i1), (v2, i2) = tops[2 * h], tops[2 * h + 1]
        cand = [v1[0:1] + v2]
        cidx = [i1[0:1] * PEER_NKEYS + i2]
        for i in range(1, half_k):
            cand.append(jnp.where(row8 < PEER_TOPK // (i + 1), v1[i:i + 1] + v2[0:half_k], -jnp.inf))
            cidx.append(i1[i:i + 1] * PEER_NKEYS + i2[0:half_k])
        cand.append(v1[half_k:] + v2[0:1])
        cidx.append(i1[half_k:] * PEER_NKEYS + i2[0:1])
        cands.append(jnp.concatenate(cand, axis=0))
        cidxs.append(jnp.concatenate(cidx, axis=0))
    for h, (top, eidx) in enumerate(_top16_many(cands, cidxs)):
        ex = jnp.exp(top - top[0:1])
        g_ref[h] = ex / jnp.sum(ex, axis=0, keepdims=True)
        idx_ref[h] = eidx


def _peer_select(pq, subkeys_b, tt):
    n = pq.shape[0]
    heads = PEER_SELECT_HEADS
    return pl.pallas_call(
        functools.partial(_peersel_kernel, heads=heads),
        out_shape=[jax.ShapeDtypeStruct((PEER_HEADS, PEER_TOPK, n), I32),
                   jax.ShapeDtypeStruct((PEER_HEADS, PEER_TOPK, n), F32)],
        grid=(n // tt, PEER_HEADS // heads),
        in_specs=[pl.BlockSpec((tt, heads * 2 * PEER_DHALF), lambda i, h: (i, h)),
                  pl.BlockSpec((heads, 2, PEER_NKEYS, PEER_DHALF), lambda i, h: (h, 0, 0, 0))],
        out_specs=[pl.BlockSpec((heads, PEER_TOPK, tt), lambda i, h: (h, 0, i)),
                   pl.BlockSpec((heads, PEER_TOPK, tt), lambda i, h: (h, 0, i))],
        compiler_params=_params(("parallel", "arbitrary")),
        name="peer_select",
    )(pq, subkeys_b)


def _erf_gelu(a):
    return 0.5 * a * (1.0 + lax.erf(a * (2.0 ** -0.5)))


def _peerffn_kernel(idx_ref, idx_next_ref, gt_ref, h2_ref, x1_ref, mod_ref, gpost_ref, uv_hbm, o_ref,
                    *scratch, tg, seqs_per_tile):
    bufs, (f_ref, sem) = scratch[:PEER_SLOTS], scratch[PEER_SLOTS:]
    step, last_step = pl.program_id(0), pl.num_programs(0) - 1

    def start_token(t, slot, src=None):
        src = idx_ref if src is None else src
        for k in range(PEER_SEL):
            pltpu.make_async_copy(uv_hbm.at[src[t, k]], bufs[slot].at[:, pl.ds(k, 1), :],
                                  sem.at[slot]).start(priority=k % 2)

    def wait_token(slot):
        pltpu.make_async_copy(bufs[slot], bufs[slot], sem.at[slot]).wait()

    lane_tok = lax.broadcasted_iota(I32, (PEER_SEL, tg), 1)
    sub8 = lax.broadcasted_iota(I32, (8, D_MODEL), 0)

    def compute(t, slot):
        buf = bufs[slot]
        t8 = pl.multiple_of((t // 8) * 8, 8)
        h8 = h2_ref[pl.ds(t8, 8), :]
        hrow = jnp.sum(jnp.where(sub8 == t - t8, h8, 0.0), axis=0, keepdims=True)
        acc = None
        for c in range(N_CHUNK):
            term = buf[c] * hrow[:, c * LANES:(c + 1) * LANES]
            acc = term if acc is None else acc + term
        a = jnp.sum(acc, axis=1, keepdims=True)
        g = jnp.sum(jnp.where(lane_tok == t, gt_ref[...], 0.0), axis=1, keepdims=True)
        w = g * _erf_gelu(a)
        pieces = []
        for c in range(N_CHUNK):
            vc = buf[N_CHUNK + c]
            pieces.append(jnp.sum(vc * w, axis=0, keepdims=True))
        f_ref[pl.ds(t, 1), :] = jnp.concatenate(pieces, axis=1)

    @pl.when(step == 0)
    def _():
        for t in range(PEER_AHEAD):
            start_token(t, t)

    def group(i, carry):
        t0 = PEER_SLOTS * i
        for j in range(PEER_SLOTS):
            wait_token(j)
            start_token(t0 + j + PEER_AHEAD, (j + PEER_AHEAD) % PEER_SLOTS)
            compute(t0 + j, j)
        return carry

    n_groups = tg // PEER_SLOTS
    lax.fori_loop(0, n_groups - 1, group, 0)
    t0 = tg - PEER_SLOTS
    for j in range(PEER_SLOTS):
        wait_token(j)
        ahead = j + PEER_AHEAD
        if ahead < PEER_SLOTS:
            start_token(t0 + ahead, ahead)
        else:
            start_token(ahead - PEER_SLOTS, ahead - PEER_SLOTS, idx_next_ref)
        compute(t0 + j, j)

    @pl.when(step == last_step)
    def _():
        for t in range(PEER_AHEAD):
            wait_token(t)

    rows = tg // seqs_per_tile
    for s in range(seqs_per_tile):
        sl = slice(s * rows, (s + 1) * rows)
        o_ref[sl, :] = x1_ref[sl, :] + mod_ref[s][5:6, :] * _rms(f_ref[sl, :], gpost_ref[...])


def _peer_ffn(idx, gt, h2, x1, mod3, g_post, uv, tg, seq_len):
    n = h2.shape[0]
    row = lambda i: (i, 0)
    seqs_per_tile = max(1, tg // seq_len)
    tiles_per_seq = max(1, seq_len // tg)
    return pl.pallas_call(
        functools.partial(_peerffn_kernel, tg=tg, seqs_per_tile=seqs_per_tile),
        out_shape=jax.ShapeDtypeStruct((n, D_MODEL), F32),
        grid=(n // tg,),
        in_specs=[pl.BlockSpec((tg, PEER_SEL), row, memory_space=pltpu.SMEM),
                  pl.BlockSpec((tg, PEER_SEL), lambda i: (jnp.minimum(i + 1, n // tg - 1), 0), memory_space=pltpu.SMEM),
                  pl.BlockSpec((PEER_SEL, tg), lambda i: (0, i)),
                  pl.BlockSpec((tg, D_MODEL), row),
                  pl.BlockSpec((tg, D_MODEL), row),
                  pl.BlockSpec((seqs_per_tile, 6, D_MODEL), lambda i: (i // tiles_per_seq, 0, 0)),
                  pl.BlockSpec((1, D_MODEL), lambda i: (0, 0)),
                  pl.BlockSpec(memory_space=pl.ANY)],
        out_specs=pl.BlockSpec((tg, D_MODEL), row),
        scratch_shapes=[pltpu.VMEM((2 * N_CHUNK, PEER_SEL, LANES), F32)] * PEER_SLOTS
                       + [pltpu.VMEM((tg, D_MODEL), F32), pltpu.SemaphoreType.DMA((PEER_SLOTS,))],
        compiler_params=_params(("arbitrary",)),
        name="peer_ffn",
    )(idx, idx, gt, h2, x1, mod3, g_post, uv)


def _pad_keys(x, s_pad):
    return jnp.pad(x, ((0, 0), (0, s_pad - x.shape[1]), (0, 0)))


def _layer(x, mod3, past, weights, tiles):
    (g_pre_mix, g_post_mix, g_pre_ffn, g_post_ffn, w_in_b, w_gate_b, wa_b, wb_b, wo_b, wq_b,
     subkeys_b, peer_uv) = weights
    tm, sb_tq, sb_tk, dsa_tq, dsa_tk, tt, tg = tiles
    bsz, t, _ = x.shape
    n = bsz * t
    x2 = x.reshape(n, D_MODEL)
    (aq, ak, av, bq, bk, bv, iq, ki2, akf, avf, bkf, bvf, ikf, iw, ga, gb) = _inproj(
        x2, mod3, g_pre_mix, w_in_b, w_gate_b, tm, t // tm)
    seq = lambda a: a.reshape(bsz, t, a.shape[-1])
    if past is None:
        q_off = 0
        ka, va, kb, vb, ki = seq(ak), seq(av), seq(bk), seq(bv), seq(ki2)
        s_len = t
    else:
        p_sbk, p_sbv, p_dk, p_dv, p_ki = past
        q_off = p_sbk.shape[1]
        s_len = q_off + t
        flat = lambda c: c.reshape(bsz, q_off, WIDTH).astype(BF16)
        ka = jnp.concatenate([flat(p_sbk), seq(ak)], axis=1)
        va = jnp.concatenate([flat(p_sbv), seq(av)], axis=1)
        kb = jnp.concatenate([flat(p_dk), seq(bk)], axis=1)
        vb = jnp.concatenate([flat(p_dv), seq(bv)], axis=1)
        pk = p_ki.astype(BF16)
        ki = jnp.concatenate([jnp.concatenate([pk, pk], axis=-1), seq(ki2)], axis=1)
    s_pad = -(-s_len // max(sb_tk, dsa_tk)) * max(sb_tk, dsa_tk)
    ka, va, kb, vb, ki = (_pad_keys(a, s_pad) for a in (ka, va, kb, vb, ki))
    n_sel = min(TOPK_MAX, s_len // 4)

    oa = _sb_attention(seq(aq), ka, va, sb_tq, sb_tk, q_off)
    ob = _dsa_attention(seq(bq), seq(iq), seq(iw), kb, vb, ki, dsa_tq, dsa_tk, q_off, n_sel)

    x1, h2, pq = _mix(oa.reshape(n, WIDTH), ob.reshape(n, WIDTH), ga, gb, x2, mod3, g_post_mix, g_pre_ffn,
                      wa_b, wb_b, wo_b, wq_b, tm, t // tm)
    eidx, gate = _peer_select(pq, subkeys_b, tt)
    eidx = eidx.reshape(PEER_SEL, n).T
    y = _peer_ffn(eidx, gate.reshape(PEER_SEL, n), h2, x1, mod3, g_post_ffn, peer_uv, tg, t)
    new = tuple(a.reshape(1, bsz, t, N_HEADS, HEAD_DIM) for a in (akf, avf, bkf, bvf)) + (
        ikf.reshape(1, bsz, t, HEAD_DIM),)
    return y.reshape(bsz, t, D_MODEL), new


def _tiles(t):
    if t % 256 == 0:
        return (256, 256, 256, 256, 256, 128, 128)
    return (t, t, 128, t, 128, 128, 128)


def kernel(x_prompt, x_sample, c_prompt, c_sample, cache_sb_k, cache_sb_v, cache_dsa_k, cache_dsa_v, cache_dsa_kidx, w_ada, b_ada, g_pre_mix, g_post_mix, g_pre_ffn, g_post_ffn, w_in, w_gate, w_branch_a, w_branch_b, w_out, w_peer_q, peer_subkeys, peer_u, peer_v):
    assert w_ada.shape[0] == 1, "one layer"
    n_p, n_s = c_prompt.shape[0], c_sample.shape[0]
    c_all = jnp.concatenate([c_prompt, c_sample], axis=0)
    rows = -(-c_all.shape[0] // 8) * 8
    c_all = jnp.pad(c_all, ((0, rows - c_all.shape[0]), (0, 0)))
    mod = _modulation(c_all, w_ada[0], b_ada[0]).reshape(rows, 6, D_MODEL)
    w_in_b = jnp.pad(w_in[0], ((0, 0), (0, W_IN_PAD - W_IN_COLS))).astype(BF16)
    weights = (g_pre_mix, g_post_mix, g_pre_ffn, g_post_ffn, w_in_b, w_gate[0].astype(BF16),
               w_branch_a[0].astype(BF16), w_branch_b[0].astype(BF16), w_out[0].astype(BF16),
               w_peer_q[0].astype(BF16), peer_subkeys[0].astype(BF16),
               jnp.concatenate([peer_u[0].reshape(-1, N_CHUNK, 1, LANES),
                                peer_v[0].reshape(-1, N_CHUNK, 1, LANES)], axis=1))
    yp, new_p = _layer(x_prompt, mod[:n_p], None, weights, _tiles(x_prompt.shape[1]))
    past = (cache_sb_k[0], cache_sb_v[0], cache_dsa_k[0], cache_dsa_v[0], cache_dsa_kidx[0])
    ys, new_s = _layer(x_sample, mod[n_p:n_p + n_s], past, weights, _tiles(x_sample.shape[1]))
    return (yp, ys) + new_p + new_s
```

```python
import functools
import math

import jax
import jax.numpy as jnp
import numpy as np
from jax import lax
from jax.experimental import pallas as pl
from jax.experimental.pallas import tpu as pltpu

F32 = jnp.float32
BF16 = jnp.bfloat16
I32 = jnp.int32
I16 = jnp.int16

D_MODEL = 1024
CHUNK = 64
EPS = 1e-6
N_HEADS = 8
HEAD_DIM = 64
WIDTH = N_HEADS * HEAD_DIM
N_PAIRS = N_HEADS // 2
SB_PAIRS_PER_STEP = 4
LANES = 128
N_CHUNK = D_MODEL // LANES
TOPK_MAX = 256
PEER_HEADS = 8
PEER_NKEYS = 128
PEER_DHALF = 128
PEER_TOPK = 16
PEER_SEL = PEER_HEADS * PEER_TOPK
PEER_SELECT_HEADS = 4
PEER_SLOTS = 8
PEER_AHEAD = 6
W_IN_COLS = 7 * WIDTH + HEAD_DIM + N_HEADS
W_IN_PAD = 7 * WIDTH + LANES
INT_MIN = -(2 ** 31)
HALF_BIAS = 2 ** 15
NEG_BIG = -1e30
DIST_MASKED = 1e30
LOG2E = 1.4426950408889634
VMEM_LIMIT = 56 * 1024 * 1024


def _params(sem):
    return pltpu.CompilerParams(dimension_semantics=sem, vmem_limit_bytes=VMEM_LIMIT)


def _dot(a, b):
    return jnp.dot(a, b, preferred_element_type=F32)


def _dot_nt(a, b):
    return lax.dot_general(a, b, (((1,), (1,)), ((), ())), preferred_element_type=F32)


def _rms(x, gain):
    return x * lax.rsqrt(jnp.mean(x * x, axis=-1, keepdims=True) + EPS) * gain


def _mod_kernel(c_ref, w_ref, b_ref, o_ref):
    c = c_ref[...]
    s = c * (1.0 / (1.0 + jnp.exp(-c)))
    o_ref[...] = jnp.dot(s, w_ref[...], preferred_element_type=F32,
                         precision=lax.Precision.HIGHEST) + b_ref[...]


def _modulation(c, w_ada, b_ada):
    n = c.shape[0]
    cols = w_ada.shape[1]
    tn = 1024
    return pl.pallas_call(
        _mod_kernel,
        out_shape=jax.ShapeDtypeStruct((n, cols), F32),
        grid=(cols // tn,),
        in_specs=[pl.BlockSpec((n, D_MODEL), lambda j: (0, 0)),
                  pl.BlockSpec((D_MODEL, tn), lambda j: (0, j)),
                  pl.BlockSpec((1, tn), lambda j: (0, j))],
        out_specs=pl.BlockSpec((n, tn), lambda j: (0, j)),
        compiler_params=_params(("arbitrary",)),
        name="mod",
    )(c, w_ada, b_ada.reshape(1, cols))


def _inproj_kernel(x_ref, mod_ref, g_ref, win_ref, wg_ref,
                   aq_ref, ak_ref, av_ref, bq_ref, bk_ref, bv_ref, iq_ref, ki2_ref,
                   akf_ref, avf_ref, bkf_ref, bvf_ref, ikf_ref, iw_ref, ga_ref, gb_ref):
    x = x_ref[...]
    mod = mod_ref[0]
    h = _rms(x, g_ref[...]) * (1.0 + mod[1:2, :]) + mod[0:1, :]
    hb = h.astype(BF16)
    qscale = HEAD_DIM ** -0.5

    def seg(i):
        return _dot(hb, win_ref[:, i * WIDTH:(i + 1) * WIDTH])

    aq_ref[...] = (seg(0) * (qscale * LOG2E)).astype(BF16)
    z = seg(1)
    akf_ref[...] = z
    ak_ref[...] = z.astype(BF16)
    z = seg(2)
    avf_ref[...] = z
    av_ref[...] = z.astype(BF16)
    bq_ref[...] = (seg(3) * (qscale * LOG2E)).astype(BF16)
    z = seg(4)
    bkf_ref[...] = z
    bk_ref[...] = z.astype(BF16)
    z = seg(5)
    bvf_ref[...] = z
    bv_ref[...] = z.astype(BF16)
    iq_ref[...] = (seg(6) * qscale).astype(BF16)
    tail = _dot(hb, win_ref[:, 7 * WIDTH:7 * WIDTH + LANES])
    ik = tail[:, :HEAD_DIM]
    ikf_ref[...] = ik
    ki2_ref[...] = jnp.concatenate([ik, ik], axis=-1).astype(BF16)
    iw_ref[...] = tail[:, HEAD_DIM:HEAD_DIM + N_HEADS] * (N_HEADS ** -0.5)
    zg = _dot(hb, wg_ref[...])
    gate = 1.0 / (1.0 + jnp.exp(-zg))
    ga_ref[...] = gate[:, :D_MODEL]
    gb_ref[...] = gate[:, D_MODEL:]


def _inproj(x2, mod3, g_pre, w_in_b, w_gate_b, tm, tiles_per_seq):
    n = x2.shape[0]
    row = lambda i: (i, 0)
    const = lambda i: (0, 0)
    bspec = lambda w: pl.BlockSpec((tm, w), row)
    outs = ([jax.ShapeDtypeStruct((n, WIDTH), BF16)] * 7
            + [jax.ShapeDtypeStruct((n, LANES), BF16)]
            + [jax.ShapeDtypeStruct((n, WIDTH), F32)] * 4
            + [jax.ShapeDtypeStruct((n, HEAD_DIM), F32),
               jax.ShapeDtypeStruct((n, N_HEADS), F32),
               jax.ShapeDtypeStruct((n, D_MODEL), F32),
               jax.ShapeDtypeStruct((n, D_MODEL), F32)])
    out_specs = ([bspec(WIDTH)] * 7 + [bspec(LANES)] + [bspec(WIDTH)] * 4
                 + [bspec(HEAD_DIM), bspec(N_HEADS), bspec(D_MODEL), bspec(D_MODEL)])
    return pl.pallas_call(
        _inproj_kernel,
        out_shape=outs,
        grid=(n // tm,),
        in_specs=[pl.BlockSpec((tm, D_MODEL), row),
                  pl.BlockSpec((1, 6, D_MODEL), lambda i: (i // tiles_per_seq, 0, 0)),
                  pl.BlockSpec((1, D_MODEL), const),
                  pl.BlockSpec((D_MODEL, W_IN_PAD), const, pipeline_mode=pl.Buffered(1)),
                  pl.BlockSpec((D_MODEL, 2 * D_MODEL), const, pipeline_mode=pl.Buffered(1))],
        out_specs=out_specs,
        compiler_params=_params(("parallel",)),
        name="inproj",
    )(x2, mod3, g_pre, w_in_b, w_gate_b)


def _sb_kernel(tab_ref, q_ref, k_ref, v_ref, tri_ref, o_ref, *scratch, tq, tk, q_off, pairs):
    s = pl.program_id(2)
    qi = tab_ref[0, s]
    kj = tab_ref[1, s]
    first = tab_ref[2, s]
    last = tab_ref[3, s]
    masked = tab_ref[4, s]
    n_heads = 2 * pairs
    acc_refs, run_refs = scratch[:n_heads], scratch[n_heads:]

    @pl.when(first == 1)
    def _():
        for r in scratch:
            r[...] = jnp.zeros_like(r)

    lo_lanes = lax.broadcasted_iota(I32, (tq, LANES), 1) < HEAD_DIM

    def step(use_mask):
        tri = tri_ref[...]
        if use_mask:
            qpos = q_off + qi * tq + lax.broadcasted_iota(I32, (tq, tk), 0)
            kpos = kj * tk + lax.broadcasted_iota(I32, (tq, tk), 1)
            causal = kpos < qpos
        z = []
        for p in range(pairs):
            q = q_ref[0, :, p * LANES:(p + 1) * LANES]
            k = k_ref[0, :, p * LANES:(p + 1) * LANES]
            zero = jnp.zeros_like(q)
            z += [_dot_nt(jnp.where(lo_lanes, q, zero), k), _dot_nt(jnp.where(lo_lanes, zero, q), k)]
        suf = []
        for h in range(n_heads):
            sp = jnp.maximum(z[h], 0.0) + jnp.log2(1.0 + jnp.exp2(-jnp.abs(z[h])))
            if use_mask:
                sp = jnp.where(causal, sp, 0.0)
            hi = sp.astype(BF16)
            lo = (sp - hi.astype(F32)).astype(BF16)
            suf.append(_dot(hi, tri) + _dot(lo, tri) + run_refs[h][:, 0:1])
        for h in range(n_heads):
            w = jnp.exp2(z[h] - suf[h])
            if use_mask:
                w = jnp.where(causal, w, 0.0)
            p = h // 2
            acc_refs[h][...] += _dot(w.astype(BF16), v_ref[0, :, p * LANES:(p + 1) * LANES])
            run_refs[h][...] = jnp.broadcast_to(suf[h][:, 0:1], (tq, LANES))

    @pl.when(masked == 1)
    def _():
        step(True)

    @pl.when(masked == 0)
    def _():
        step(False)

    @pl.when(last == 1)
    def _():
        for p in range(pairs):
            o_ref[0, :, p * LANES:(p + 1) * LANES] = jnp.where(
                lo_lanes, acc_refs[2 * p][...], acc_refs[2 * p + 1][...]).astype(o_ref.dtype)


def _sb_table(nq, nk, tq, tk, q_off):
    rows = []
    for qi in range(nq):
        q_lo = q_off + qi * tq
        q_hi = q_lo + tq - 1
        kjs = [kj for kj in range(nk) if kj * tk < q_hi]
        if not kjs:
            kjs = [0]
        kjs = kjs[::-1]
        for n, kj in enumerate(kjs):
            masked = int(kj * tk + tk - 1 >= q_lo)
            rows.append((qi, kj, int(n == 0), int(n == len(kjs) - 1), masked))
    return np.asarray(rows, dtype=np.int32).T.copy()


def _sb_attention(q, k, v, tq, tk, q_off):
    b, t, _ = q.shape
    s_len = k.shape[1]
    tab = _sb_table(t // tq, s_len // tk, tq, tk, q_off)
    tri = jnp.asarray(np.tril(np.ones((tk, tk), np.float32)), dtype=BF16)
    pairs = SB_PAIRS_PER_STEP
    width = pairs * LANES
    grid_spec = pltpu.PrefetchScalarGridSpec(
        num_scalar_prefetch=1,
        grid=(b, N_PAIRS // pairs, tab.shape[1]),
        in_specs=[pl.BlockSpec((1, tq, width), lambda bi, hp, s, tab: (bi, tab[0, s], hp)),
                  pl.BlockSpec((1, tk, width), lambda bi, hp, s, tab: (bi, tab[1, s], hp)),
                  pl.BlockSpec((1, tk, width), lambda bi, hp, s, tab: (bi, tab[1, s], hp)),
                  pl.BlockSpec((tk, tk), lambda bi, hp, s, tab: (0, 0))],
        out_specs=pl.BlockSpec((1, tq, width), lambda bi, hp, s, tab: (bi, tab[0, s], hp)),
        scratch_shapes=[pltpu.VMEM((tq, LANES), F32)] * (4 * pairs),
    )
    return pl.pallas_call(
        functools.partial(_sb_kernel, tq=tq, tk=tk, q_off=q_off, pairs=pairs),
        out_shape=jax.ShapeDtypeStruct((b, t, WIDTH), BF16),
        grid_spec=grid_spec,
        compiler_params=_params(("parallel", "parallel", "arbitrary")),
        name="sb_attn",
    )(jnp.asarray(tab), q, k, v, tri)


def _sortable(x):
    bits = pltpu.bitcast(x + 0.0, I32)
    return jnp.where(bits < 0, bits ^ 0x7FFFFFFF, bits)


def _alibi_slope2(h):
    return LOG2E * 2.0 ** (-8.0 * (h + 1) / N_HEADS)


def _dsa_kernel(q_ref, iq_ref, iwt_ref, k_ref, vt_ref, ki_ref, o_ref, keys_ref, dm_ref, hi_ref, lo_ref, qt_ref, iqt_ref,
                *stats,
                 tq, tk, q_off, n_kb_max, n_sel, idx_bits):
    m_refs, l_refs, acc_refs = stats[:N_HEADS], stats[N_HEADS:2 * N_HEADS], stats[2 * N_HEADS:]
    qi = pl.program_id(1)
    q_lo = q_off + qi * tq
    adm_end = ((q_lo + tq - 1) // CHUNK + 1) * CHUNK
    n_kb = jnp.minimum((adm_end + tk - 1) // tk, n_kb_max)

    lo_half = lax.broadcasted_iota(I32, (tq, LANES), 1) < HEAD_DIM
    qpos = q_lo + lax.broadcasted_iota(I32, (tk, tq), 1)
    krow = lax.broadcasted_iota(I32, (tk, tq), 0)

    def heads_of(x_ref):
        out = []
        for hp in range(N_PAIRS):
            pair = x_ref[0, :, hp * LANES:(hp + 1) * LANES]
            zero = jnp.zeros_like(pair)
            out += [jnp.where(lo_half, pair, zero), jnp.where(lo_half, zero, pair)]
        return out

    for src, dst in ((iq_ref, iqt_ref), (q_ref, qt_ref)):
        for h, xe in enumerate(heads_of(src)):
            dst[h] = xe.astype(F32).T.astype(BF16)

    def score_block(j, carry):
        ki = ki_ref[0, pl.ds(pl.multiple_of(j * tk, tk), tk), :]
        score = jnp.zeros((tk, tq), F32)
        for h in range(N_HEADS):
            score = score + iwt_ref[0, h:h + 1, :] * jnp.maximum(_dot(ki, iqt_ref[h]), 0.0)
        adm = ((j * tk + krow) // CHUNK) <= (qpos // CHUNK)
        key = jnp.where(adm, _sortable(score), INT_MIN)
        keys_ref[j] = key
        hi_ref[j] = (key >> 16).astype(I16)
        lo_ref[j] = ((key & 0xFFFF) - HALF_BIAS).astype(I16)
        return carry

    lax.fori_loop(0, n_kb, score_block, 0)

    def count_cols(pred_fn):
        def body(j, acc):
            hit = pred_fn(keys_ref[j], j).astype(I32)
            return acc + jnp.sum(hit.reshape(tk // 8, 8, tq), axis=0)
        acc = lax.fori_loop(0, n_kb, body, jnp.zeros((8, tq), I32))
        return jnp.sum(acc, axis=0, keepdims=True)

    def count_half(ref, pred_fn):
        def body(j, acc):
            hit = jnp.where(pred_fn(ref[j]), jnp.int16(1), jnp.int16(0))
            for r in range(tk // 16):
                acc = acc + hit[r * 16:(r + 1) * 16, :]
            return acc
        acc = lax.fori_loop(0, n_kb, body, jnp.zeros((16, tq), I16))
        return jnp.sum(acc.astype(I32), axis=0, keepdims=True)

    def search_half(ref, target):
        def bit_step(i, t):
            cand = t + jnp.left_shift(jnp.int32(1), 15 - i)
            cand16 = cand.astype(I16)
            cnt = count_half(ref, lambda kb: kb >= cand16)
            return jnp.where(cnt >= target, cand, t)
        return lax.fori_loop(0, 16, bit_step, jnp.full((1, tq), -HALF_BIAS, I32))

    t_hi = search_half(hi_ref, n_sel)
    t_hi16 = t_hi.astype(I16)
    n_above = count_half(hi_ref, lambda kb: kb > t_hi16)

    def keep_group(j, carry):
        lo_ref[j] = jnp.where(hi_ref[j] == t_hi16, lo_ref[j], jnp.int16(-HALF_BIAS))
        return carry

    lax.fori_loop(0, n_kb, keep_group, 0)
    t_lo = search_half(lo_ref, n_sel - n_above)
    thr = t_hi * (2 * HALF_BIAS) + (t_lo + HALF_BIAS)
    n_gt = count_cols(lambda kb, j: kb > thr)
    n_ge = count_cols(lambda kb, j: kb >= thr)
    need = n_sel - n_gt
    real = thr > INT_MIN
    tie = jnp.max(jnp.where(real & (n_ge > n_sel), 1, 0)) > 0

    def tie_cut():
        def idx_step(i, cut):
            cand = cut + jnp.left_shift(jnp.int32(1), idx_bits - 1 - i)
            cnt = count_cols(lambda kb, j: (kb == thr) & ((j * tk + krow) < cand))
            return jnp.where(cnt < need, cand, cut)
        return lax.fori_loop(0, idx_bits, idx_step, jnp.zeros((1, tq), I32))

    cut = lax.cond(tie, tie_cut, lambda: jnp.full((1, tq), n_kb_max * tk, I32))

    def mask_block(j, carry):
        kb = keys_ref[j]
        kpos = j * tk + krow
        sel = (kb > thr) | ((kb == thr) & real & (kpos <= cut))
        dm_ref[j] = jnp.where(sel, jnp.abs(qpos - kpos).astype(F32), DIST_MASKED)
        return carry

    lax.fori_loop(0, n_kb, mask_block, 0)

    for h in range(N_HEADS):
        m_refs[h][...] = jnp.full_like(m_refs[h], NEG_BIG)
        l_refs[h][...] = jnp.zeros_like(l_refs[h])
        acc_refs[h][...] = jnp.zeros_like(acc_refs[h])

    def attn_block(j, carry):
        start = pl.multiple_of(j * tk, tk)
        dm = dm_ref[j]

        def qk(h):
            return _dot(k_ref[0, pl.ds(start, tk), (h // 2) * LANES:(h // 2 + 1) * LANES], qt_ref[h])

        ahead = 4
        pending = [qk(h) for h in range(ahead)]
        for hp in range(N_PAIRS):
            for e in range(2):
                h = 2 * hp + e
                s_this = pending.pop(0)
                if h + ahead < N_HEADS:
                    pending.append(qk(h + ahead))
                t2 = s_this - _alibi_slope2(h) * dm
                m_old = m_refs[h][...]
                m_new = jnp.maximum(m_old, jnp.max(t2, axis=0, keepdims=True))
                alpha = jnp.exp2(m_old - m_new)
                p = jnp.exp2(t2 - m_new)
                l_refs[h][...] = alpha * l_refs[h][...] + jnp.sum(p, axis=0, keepdims=True)
                vt = vt_ref[0, j, hp * LANES + e * HEAD_DIM:hp * LANES + (e + 1) * HEAD_DIM, :]
                acc_refs[h][...] = alpha * acc_refs[h][...] + _dot(vt, p.astype(BF16))
                m_refs[h][...] = m_new
        return carry

    lax.fori_loop(0, n_kb, attn_block, 0)

    for hp in range(N_PAIRS):
        ot = jnp.concatenate([acc_refs[2 * hp][...] / l_refs[2 * hp][...],
                              acc_refs[2 * hp + 1][...] / l_refs[2 * hp + 1][...]], axis=0)
        o_ref[0, :, hp * LANES:(hp + 1) * LANES] = ot.T.astype(o_ref.dtype)


def _dsa_attention(q, iq, iw, k, v, ki2, tq, tk, q_off, n_sel):
    b, t, _ = q.shape
    s_len = k.shape[1]
    n_kb_max = s_len // tk
    iwt = jnp.swapaxes(iw, 1, 2)
    vt = jnp.swapaxes(v.reshape(b, n_kb_max, tk, WIDTH), 2, 3)
    qspec = lambda w: pl.BlockSpec((1, tq, w), lambda bi, i: (bi, i, 0))
    kspec = lambda w: pl.BlockSpec((1, s_len, w), lambda bi, i: (bi, 0, 0), pipeline_mode=pl.Buffered(1))
    return pl.pallas_call(
        functools.partial(_dsa_kernel, tq=tq, tk=tk, q_off=q_off, n_kb_max=n_kb_max, n_sel=n_sel,
                          idx_bits=max(1, int(math.ceil(math.log2(s_len + 1))))),
        out_shape=jax.ShapeDtypeStruct((b, t, WIDTH), BF16),
        grid=(b, t // tq),
        in_specs=[qspec(WIDTH), qspec(WIDTH),
                  pl.BlockSpec((1, N_HEADS, tq), lambda bi, i: (bi, 0, i)),
                  kspec(WIDTH),
                  pl.BlockSpec((1, n_kb_max, WIDTH, tk), lambda bi, i: (bi, 0, 0, 0), pipeline_mode=pl.Buffered(1)),
                  kspec(LANES)],
        out_specs=qspec(WIDTH),
        scratch_shapes=[pltpu.VMEM((n_kb_max, tk, tq), I32), pltpu.VMEM((n_kb_max, tk, tq), F32),
                        pltpu.VMEM((n_kb_max, tk, tq), I16), pltpu.VMEM((n_kb_max, tk, tq), I16),
                        pltpu.VMEM((N_HEADS, LANES, tq), BF16), pltpu.VMEM((N_HEADS, LANES, tq), BF16)]
                       + [pltpu.VMEM((1, tq), F32)] * (2 * N_HEADS) + [pltpu.VMEM((HEAD_DIM, tq), F32)] * N_HEADS,
        compiler_params=_params(("parallel", "arbitrary")),
        name="dsa_attn",
    )(q, iq, iwt, k, vt, ki2)


def _mix_kernel(oa_ref, ob_ref, ga_ref, gb_ref, x_ref, mod_ref, gpost_ref, gpre_ref,
                wa_ref, wb_ref, wo_ref, wq_ref, x1_ref, h2_ref, pq_ref):
    mod = mod_ref[0]
    mixed = ga_ref[...] * _dot(oa_ref[...], wa_ref[...]) + gb_ref[...] * _dot(ob_ref[...], wb_ref[...])
    y = _dot(mixed.astype(BF16), wo_ref[...])
    x1 = x_ref[...] + mod[2:3, :] * _rms(y, gpost_ref[...])
    x1_ref[...] = x1
    h2 = _rms(x1, gpre_ref[...]) * (1.0 + mod[4:5, :]) + mod[3:4, :]
    h2_ref[...] = h2
    pq_ref[...] = _dot(h2.astype(BF16), wq_ref[...]).astype(BF16)


def _mix(oa, ob, ga, gb, x2, mod3, g_post, g_pre, wa, wb, wo, wq, tm, tiles_per_seq):
    n = x2.shape[0]
    row = lambda i: (i, 0)
    const = lambda i: (0, 0)
    nq = wq.shape[1]
    wspec = lambda r, c: pl.BlockSpec((r, c), const, pipeline_mode=pl.Buffered(1))
    return pl.pallas_call(
        _mix_kernel,
        out_shape=[jax.ShapeDtypeStruct((n, D_MODEL), F32),
                   jax.ShapeDtypeStruct((n, D_MODEL), F32),
                   jax.ShapeDtypeStruct((n, nq), BF16)],
        grid=(n // tm,),
        in_specs=[pl.BlockSpec((tm, WIDTH), row), pl.BlockSpec((tm, WIDTH), row),
                  pl.BlockSpec((tm, D_MODEL), row), pl.BlockSpec((tm, D_MODEL), row),
                  pl.BlockSpec((tm, D_MODEL), row),
                  pl.BlockSpec((1, 6, D_MODEL), lambda i: (i // tiles_per_seq, 0, 0)),
                  pl.BlockSpec((1, D_MODEL), const), pl.BlockSpec((1, D_MODEL), const),
                  wspec(WIDTH, D_MODEL), wspec(WIDTH, D_MODEL), wspec(D_MODEL, D_MODEL),
                  wspec(D_MODEL, nq)],
        out_specs=[pl.BlockSpec((tm, D_MODEL), row), pl.BlockSpec((tm, D_MODEL), row),
                   pl.BlockSpec((tm, nq), row)],
        compiler_params=_params(("parallel",)),
        name="mix",
    )(oa, ob, ga, gb, x2, mod3, g_post, g_pre, wa, wb, wo, wq)


def _top16_many(arrays, payloads=None):
    state = list(arrays)
    pos = [lax.broadcasted_iota(I32, a.shape, 0) for a in arrays]
    vals = [[] for _ in arrays]
    picks = [[] for _ in arrays]
    for _ in range(PEER_TOPK):
        for c, s in enumerate(state):
            m = jnp.max(s, axis=0, keepdims=True)
            p = jnp.min(jnp.where(s == m, pos[c], s.shape[0]), axis=0, keepdims=True)
            hit = pos[c] == p
            vals[c].append(m)
            picks[c].append(p if payloads is None
                            else jnp.sum(jnp.where(hit, payloads[c], 0), axis=0, keepdims=True))
            state[c] = jnp.where(hit, -jnp.inf, s)
    return [(jnp.concatenate(v, axis=0), jnp.concatenate(p, axis=0)) for v, p in zip(vals, picks)]


def _peersel_kernel(pq_ref, sk_ref, idx_ref, g_ref, *, heads):
    tt = pq_ref.shape[0]
    scores = []
    for h in range(heads):
        for half in range(2):
            lo = (2 * h + half) * PEER_DHALF
            scores.append(_dot_nt(sk_ref[h, half], pq_ref[:, lo:lo + PEER_DHALF]))
    tops = _top16_many(scores)
    half_k = PEER_TOPK // 2
    row8 = lax.broadcasted_iota(I32, (half_k, tt), 0)
    cands, cidxs = [], []
    for h in range(heads):
        (v1, i1), (v2, i2) = tops[2 * h], tops[2 * h + 1]
        cand = [v1[0:1] + v2]
        cidx = [i1[0:1] * PEER_NKEYS + i2]
        for i in range(1, half_k):
            cand.append(jnp.where(row8 < PEER_TOPK // (i + 1), v1[i:i + 1] + v2[0:half_k], -jnp.inf))
            cidx.append(i1[i:i + 1] * PEER_NKEYS + i2[0:half_k])
        cand.append(v1[half_k:] + v2[0:1])
        cidx.append(i1[half_k:] * PEER_NKEYS + i2[0:1])
        cands.append(jnp.concatenate(cand, axis=0))
        cidxs.append(jnp.concatenate(cidx, axis=0))
    for h, (top, eidx) in enumerate(_top16_many(cands, cidxs)):
        ex = jnp.exp(top - top[0:1])
        g_ref[h] = ex / jnp.sum(ex, axis=0, keepdims=True)
        idx_ref[h] = eidx


def _peer_select(pq, subkeys_b, tt):
    n = pq.shape[0]
    heads = PEER_SELECT_HEADS
    return pl.pallas_call(
        functools.partial(_peersel_kernel, heads=heads),
        out_shape=[jax.ShapeDtypeStruct((PEER_HEADS, PEER_TOPK, n), I32),
                   jax.ShapeDtypeStruct((PEER_HEADS, PEER_TOPK, n), F32)],
        grid=(n // tt, PEER_HEADS // heads),
        in_specs=[pl.BlockSpec((tt, heads * 2 * PEER_DHALF), lambda i, h: (i, h)),
                  pl.BlockSpec((heads, 2, PEER_NKEYS, PEER_DHALF), lambda i, h: (h, 0, 0, 0))],
        out_specs=[pl.BlockSpec((heads, PEER_TOPK, tt), lambda i, h: (h, 0, i)),
                   pl.BlockSpec((heads, PEER_TOPK, tt), lambda i, h: (h, 0, i))],
        compiler_params=_params(("parallel", "arbitrary")),
        name="peer_select",
    )(pq, subkeys_b)


def _erf_gelu(a):
    return 0.5 * a * (1.0 + lax.erf(a * (2.0 ** -0.5)))


def _peerffn_kernel(idx_ref, idx_next_ref, gt_ref, h2_ref, x1_ref, mod_ref, gpost_ref, uv_hbm, o_ref,
                    *scratch, tg, seqs_per_tile):
    bufs, (f_ref, sem) = scratch[:PEER_SLOTS], scratch[PEER_SLOTS:]
    step, last_step = pl.program_id(0), pl.num_programs(0) - 1

    def start_token(t, slot, src=None):
        src = idx_ref if src is None else src
        for k in range(PEER_SEL):
            pltpu.make_async_copy(uv_hbm.at[src[t, k]], bufs[slot].at[:, pl.ds(k, 1), :],
                                  sem.at[slot]).start(priority=k % 2)

    def wait_token(slot):
        pltpu.make_async_copy(bufs[slot], bufs[slot], sem.at[slot]).wait()

    lane_tok = lax.broadcasted_iota(I32, (PEER_SEL, tg), 1)
    sub8 = lax.broadcasted_iota(I32, (8, D_MODEL), 0)

    def compute(t, slot):
        buf = bufs[slot]
        t8 = pl.multiple_of((t // 8) * 8, 8)
        h8 = h2_ref[pl.ds(t8, 8), :]
        hrow = jnp.sum(jnp.where(sub8 == t - t8, h8, 0.0), axis=0, keepdims=True)
        acc = None
        for c in range(N_CHUNK):
            term = buf[c] * hrow[:, c * LANES:(c + 1) * LANES]
            acc = term if acc is None else acc + term
        a = jnp.sum(acc, axis=1, keepdims=True)
        g = jnp.sum(jnp.where(lane_tok == t, gt_ref[...], 0.0), axis=1, keepdims=True)
        w = g * _erf_gelu(a)
        pieces = []
        for c in range(N_CHUNK):
            vc = buf[N_CHUNK + c]
            pieces.append(jnp.sum(vc * w, axis=0, keepdims=True))
        f_ref[pl.ds(t, 1), :] = jnp.concatenate(pieces, axis=1)

    @pl.when(step == 0)
    def _():
        for t in range(PEER_AHEAD):
            start_token(t, t)

    def group(i, carry):
        t0 = PEER_SLOTS * i
        for j in range(PEER_SLOTS):
            wait_token(j)
            start_token(t0 + j + PEER_AHEAD, (j + PEER_AHEAD) % PEER_SLOTS)
            compute(t0 + j, j)
        return carry

    n_groups = tg // PEER_SLOTS
    lax.fori_loop(0, n_groups - 1, group, 0)
    t0 = tg - PEER_SLOTS
    for j in range(PEER_SLOTS):
        wait_token(j)
        ahead = j + PEER_AHEAD
        if ahead < PEER_SLOTS:
            start_token(t0 + ahead, ahead)
        else:
            start_token(ahead - PEER_SLOTS, ahead - PEER_SLOTS, idx_next_ref)
        compute(t0 + j, j)

    @pl.when(step == last_step)
    def _():
        for t in range(PEER_AHEAD):
            wait_token(t)

    rows = tg // seqs_per_tile
    for s in range(seqs_per_tile):
        sl = slice(s * rows, (s + 1) * rows)
        o_ref[sl, :] = x1_ref[sl, :] + mod_ref[s][5:6, :] * _rms(f_ref[sl, :], gpost_ref[...])


def _peer_ffn(idx, gt, h2, x1, mod3, g_post, uv, tg, seq_len):
    n = h2.shape[0]
    row = lambda i: (i, 0)
    seqs_per_tile = max(1, tg // seq_len)
    tiles_per_seq = max(1, seq_len // tg)
    return pl.pallas_call(
        functools.partial(_peerffn_kernel, tg=tg, seqs_per_tile=seqs_per_tile),
        out_shape=jax.ShapeDtypeStruct((n, D_MODEL), F32),
        grid=(n // tg,),
        in_specs=[pl.BlockSpec((tg, PEER_SEL), row, memory_space=pltpu.SMEM),
                  pl.BlockSpec((tg, PEER_SEL), lambda i: (jnp.minimum(i + 1, n // tg - 1), 0), memory_space=pltpu.SMEM),
                  pl.BlockSpec((PEER_SEL, tg), lambda i: (0, i)),
                  pl.BlockSpec((tg, D_MODEL), row),
                  pl.BlockSpec((tg, D_MODEL), row),
                  pl.BlockSpec((seqs_per_tile, 6, D_MODEL), lambda i: (i // tiles_per_seq, 0, 0)),
                  pl.BlockSpec((1, D_MODEL), lambda i: (0, 0)),
                  pl.BlockSpec(memory_space=pl.ANY)],
        out_specs=pl.BlockSpec((tg, D_MODEL), row),
        scratch_shapes=[pltpu.VMEM((2 * N_CHUNK, PEER_SEL, LANES), F32)] * PEER_SLOTS
                       + [pltpu.VMEM((tg, D_MODEL), F32), pltpu.SemaphoreType.DMA((PEER_SLOTS,))],
        compiler_params=_params(("arbitrary",)),
        name="peer_ffn",
    )(idx, idx, gt, h2, x1, mod3, g_post, uv)


def _pad_keys(x, s_pad):
    return jnp.pad(x, ((0, 0), (0, s_pad - x.shape[1]), (0, 0)))


def _layer(x, mod3, past, weights, tiles):
    (g_pre_mix, g_post_mix, g_pre_ffn, g_post_ffn, w_in_b, w_gate_b, wa_b, wb_b, wo_b, wq_b,
     subkeys_b, peer_uv) = weights
    tm, sb_tq, sb_tk, dsa_tq, dsa_tk, tt, tg = tiles
    bsz, t, _ = x.shape
    n = bsz * t
    x2 = x.reshape(n, D_MODEL)
    (aq, ak, av, bq, bk, bv, iq, ki2, akf, avf, bkf, bvf, ikf, iw, ga, gb) = _inproj(
        x2, mod3, g_pre_mix, w_in_b, w_gate_b, tm, t // tm)
    seq = lambda a: a.reshape(bsz, t, a.shape[-1])
    if past is None:
        q_off = 0
        ka, va, kb, vb, ki = seq(ak), seq(av), seq(bk), seq(bv), seq(ki2)
        s_len = t
    else:
        p_sbk, p_sbv, p_dk, p_dv, p_ki = past
        q_off = p_sbk.shape[1]
        s_len = q_off + t
        flat = lambda c: c.reshape(bsz, q_off, WIDTH).astype(BF16)
        ka = jnp.concatenate([flat(p_sbk), seq(ak)], axis=1)
        va = jnp.concatenate([flat(p_sbv), seq(av)], axis=1)
        kb = jnp.concatenate([flat(p_dk), seq(bk)], axis=1)
        vb = jnp.concatenate([flat(p_dv), seq(bv)], axis=1)
        pk = p_ki.astype(BF16)
        ki = jnp.concatenate([jnp.concatenate([pk, pk], axis=-1), seq(ki2)], axis=1)
    s_pad = -(-s_len // max(sb_tk, dsa_tk)) * max(sb_tk, dsa_tk)
    ka, va, kb, vb, ki = (_pad_keys(a, s_pad) for a in (ka, va, kb, vb, ki))
    n_sel = min(TOPK_MAX, s_len // 4)

    oa = _sb_attention(seq(aq), ka, va, sb_tq, sb_tk, q_off)
    ob = _dsa_attention(seq(bq), seq(iq), seq(iw), kb, vb, ki, dsa_tq, dsa_tk, q_off, n_sel)

    x1, h2, pq = _mix(oa.reshape(n, WIDTH), ob.reshape(n, WIDTH), ga, gb, x2, mod3, g_post_mix, g_pre_ffn,
                      wa_b, wb_b, wo_b, wq_b, tm, t // tm)
    eidx, gate = _peer_select(pq, subkeys_b, tt)
    eidx = eidx.reshape(PEER_SEL, n).T
    y = _peer_ffn(eidx, gate.reshape(PEER_SEL, n), h2, x1, mod3, g_post_ffn, peer_uv, tg, t)
    new = tuple(a.reshape(1, bsz, t, N_HEADS, HEAD_DIM) for a in (akf, avf, bkf, bvf)) + (
        ikf.reshape(1, bsz, t, HEAD_DIM),)
    return y.reshape(bsz, t, D_MODEL), new


def _tiles(t):
    if t % 256 == 0:
        return (256, 256, 256, 256, 512, 128, 128)
    return (t, t, 128, t, 128, 128, 128)


def kernel(x_prompt, x_sample, c_prompt, c_sample, cache_sb_k, cache_sb_v, cache_dsa_k, cache_dsa_v, cache_dsa_kidx, w_ada, b_ada, g_pre_mix, g_post_mix, g_pre_ffn, g_post_ffn, w_in, w_gate, w_branch_a, w_branch_b, w_out, w_peer_q, peer_subkeys, peer_u, peer_v):
    assert w_ada.shape[0] == 1, "one layer"
    n_p, n_s = c_prompt.shape[0], c_sample.shape[0]
    c_all = jnp.concatenate([c_prompt, c_sample], axis=0)
    rows = -(-c_all.shape[0] // 8) * 8
    c_all = jnp.pad(c_all, ((0, rows - c_all.shape[0]), (0, 0)))
    mod = _modulation(c_all, w_ada[0], b_ada[0]).reshape(rows, 6, D_MODEL)
    w_in_b = jnp.pad(w_in[0], ((0, 0), (0, W_IN_PAD - W_IN_COLS))).astype(BF16)
    weights = (g_pre_mix, g_post_mix, g_pre_ffn, g_post_ffn, w_in_b, w_gate[0].astype(BF16),
               w_branch_a[0].astype(BF16), w_branch_b[0].astype(BF16), w_out[0].astype(BF16),
               w_peer_q[0].astype(BF16), peer_subkeys[0].astype(BF16),
               jnp.concatenate([peer_u[0].reshape(-1, N_CHUNK, 1, LANES),
                                peer_v[0].reshape(-1, N_CHUNK, 1, LANES)], axis=1))
    yp, new_p = _layer(x_prompt, mod[:n_p], None, weights, _tiles(x_prompt.shape[1]))
    past = (cache_sb_k[0], cache_sb_v[0], cache_dsa_k[0], cache_dsa_v[0], cache_dsa_kidx[0])
    ys, new_s = _layer(x_sample, mod[n_p:n_p + n_s], past, weights, _tiles(x_sample.shape[1]))
    return (yp, ys) + new_p + new_s
```

```python
import functools
import math

import jax
import jax.numpy as jnp
import numpy as np
from jax import lax
from jax.experimental import pallas as pl
from jax.experimental.pallas import tpu as pltpu

F32 = jnp.float32
BF16 = jnp.bfloat16
I32 = jnp.int32
I16 = jnp.int16

D_MODEL = 1024
CHUNK = 64
EPS = 1e-6
N_HEADS = 8
HEAD_DIM = 64
WIDTH = N_HEADS * HEAD_DIM
N_PAIRS = N_HEADS // 2
SB_PAIRS_PER_STEP = 4
LANES = 128
N_CHUNK = D_MODEL // LANES
TOPK_MAX = 256
PEER_HEADS = 8
PEER_NKEYS = 128
PEER_DHALF = 128
PEER_TOPK = 16
PEER_SEL = PEER_HEADS * PEER_TOPK
PEER_SELECT_PIECES = 4
PEER_SLOTS = 8
PEER_AHEAD = 6
W_IN_COLS = 7 * WIDTH + HEAD_DIM + N_HEADS
W_IN_PAD = 7 * WIDTH + LANES
INT_MIN = -(2 ** 31)
HALF_BIAS = 2 ** 15
NEG_BIG = -1e30
DIST_MASKED = 1e30
LOG2E = 1.4426950408889634
VMEM_LIMIT = 56 * 1024 * 1024


def _params(sem):
    return pltpu.CompilerParams(dimension_semantics=sem, vmem_limit_bytes=VMEM_LIMIT)


def _dot(a, b):
    return jnp.dot(a, b, preferred_element_type=F32)


def _dot_nt(a, b):
    return lax.dot_general(a, b, (((1,), (1,)), ((), ())), preferred_element_type=F32)


def _rms(x, gain):
    return x * lax.rsqrt(jnp.mean(x * x, axis=-1, keepdims=True) + EPS) * gain


def _mod_kernel(c_ref, w_ref, b_ref, o_ref):
    c = c_ref[...]
    s = c * (1.0 / (1.0 + jnp.exp(-c)))
    o_ref[...] = jnp.dot(s, w_ref[...], preferred_element_type=F32,
                         precision=lax.Precision.HIGHEST) + b_ref[...]


def _modulation(c, w_ada, b_ada):
    n = c.shape[0]
    cols = w_ada.shape[1]
    tn = 1024
    return pl.pallas_call(
        _mod_kernel,
        out_shape=jax.ShapeDtypeStruct((n, cols), F32),
        grid=(cols // tn,),
        in_specs=[pl.BlockSpec((n, D_MODEL), lambda j: (0, 0)),
                  pl.BlockSpec((D_MODEL, tn), lambda j: (0, j)),
                  pl.BlockSpec((1, tn), lambda j: (0, j))],
        out_specs=pl.BlockSpec((n, tn), lambda j: (0, j)),
        compiler_params=_params(("arbitrary",)),
        name="mod",
    )(c, w_ada, b_ada.reshape(1, cols))


def _inproj_kernel(x_ref, mod_ref, g_ref, win_ref, wg_ref,
                   aq_ref, ak_ref, av_ref, bq_ref, bk_ref, bv_ref, iq_ref, ki2_ref,
                   akf_ref, avf_ref, bkf_ref, bvf_ref, ikf_ref, iw_ref, ga_ref, gb_ref):
    x = x_ref[...]
    mod = mod_ref[0]
    h = _rms(x, g_ref[...]) * (1.0 + mod[1:2, :]) + mod[0:1, :]
    hb = h.astype(BF16)
    qscale = HEAD_DIM ** -0.5

    def seg(i):
        return _dot(hb, win_ref[:, i * WIDTH:(i + 1) * WIDTH])

    aq_ref[...] = (seg(0) * (qscale * LOG2E)).astype(BF16)
    z = seg(1)
    akf_ref[...] = z
    ak_ref[...] = z.astype(BF16)
    z = seg(2)
    avf_ref[...] = z
    av_ref[...] = z.astype(BF16)
    bq_ref[...] = (seg(3) * (qscale * LOG2E)).astype(BF16)
    z = seg(4)
    bkf_ref[...] = z
    bk_ref[...] = z.astype(BF16)
    z = seg(5)
    bvf_ref[...] = z
    bv_ref[...] = z.astype(BF16)
    iq_ref[...] = (seg(6) * qscale).astype(BF16)
    tail = _dot(hb, win_ref[:, 7 * WIDTH:7 * WIDTH + LANES])
    ik = tail[:, :HEAD_DIM]
    ikf_ref[...] = ik
    ki2_ref[...] = jnp.concatenate([ik, ik], axis=-1).astype(BF16)
    iw_ref[...] = tail[:, HEAD_DIM:HEAD_DIM + N_HEADS] * (N_HEADS ** -0.5)
    zg = _dot(hb, wg_ref[...])
    gate = 1.0 / (1.0 + jnp.exp(-zg))
    ga_ref[...] = gate[:, :D_MODEL]
    gb_ref[...] = gate[:, D_MODEL:]


def _inproj(x2, mod3, g_pre, w_in_b, w_gate_b, tm, tiles_per_seq):
    n = x2.shape[0]
    row = lambda i: (i, 0)
    const = lambda i: (0, 0)
    bspec = lambda w: pl.BlockSpec((tm, w), row)
    outs = ([jax.ShapeDtypeStruct((n, WIDTH), BF16)] * 7
            + [jax.ShapeDtypeStruct((n, LANES), BF16)]
            + [jax.ShapeDtypeStruct((n, WIDTH), F32)] * 4
            + [jax.ShapeDtypeStruct((n, HEAD_DIM), F32),
               jax.ShapeDtypeStruct((n, N_HEADS), F32),
               jax.ShapeDtypeStruct((n, D_MODEL), F32),
               jax.ShapeDtypeStruct((n, D_MODEL), F32)])
    out_specs = ([bspec(WIDTH)] * 7 + [bspec(LANES)] + [bspec(WIDTH)] * 4
                 + [bspec(HEAD_DIM), bspec(N_HEADS), bspec(D_MODEL), bspec(D_MODEL)])
    return pl.pallas_call(
        _inproj_kernel,
        out_shape=outs,
        grid=(n // tm,),
        in_specs=[pl.BlockSpec((tm, D_MODEL), row),
                  pl.BlockSpec((1, 6, D_MODEL), lambda i: (i // tiles_per_seq, 0, 0)),
                  pl.BlockSpec((1, D_MODEL), const),
                  pl.BlockSpec((D_MODEL, W_IN_PAD), const, pipeline_mode=pl.Buffered(1)),
                  pl.BlockSpec((D_MODEL, 2 * D_MODEL), const, pipeline_mode=pl.Buffered(1))],
        out_specs=out_specs,
        compiler_params=_params(("parallel",)),
        name="inproj",
    )(x2, mod3, g_pre, w_in_b, w_gate_b)


def _sb_kernel(tab_ref, q_ref, k_ref, v_ref, tri_ref, o_ref, *scratch, tq, tk, q_off, pairs):
    s = pl.program_id(2)
    qi = tab_ref[0, s]
    kj = tab_ref[1, s]
    first = tab_ref[2, s]
    last = tab_ref[3, s]
    masked = tab_ref[4, s]
    n_heads = 2 * pairs
    acc_refs, run_refs = scratch[:n_heads], scratch[n_heads:]

    @pl.when(first == 1)
    def _():
        for r in scratch:
            r[...] = jnp.zeros_like(r)

    lo_lanes = lax.broadcasted_iota(I32, (tq, LANES), 1) < HEAD_DIM

    def step(use_mask):
        tri = tri_ref[...]
        if use_mask:
            qpos = q_off + qi * tq + lax.broadcasted_iota(I32, (tq, tk), 0)
            kpos = kj * tk + lax.broadcasted_iota(I32, (tq, tk), 1)
            causal = kpos < qpos
        z = []
        for p in range(pairs):
            q = q_ref[0, :, p * LANES:(p + 1) * LANES]
            k = k_ref[0, :, p * LANES:(p + 1) * LANES]
            zero = jnp.zeros_like(q)
            z += [_dot_nt(jnp.where(lo_lanes, q, zero), k), _dot_nt(jnp.where(lo_lanes, zero, q), k)]
        suf = []
        for h in range(n_heads):
            sp = jnp.maximum(z[h], 0.0) + jnp.log2(1.0 + jnp.exp2(-jnp.abs(z[h])))
            if use_mask:
                sp = jnp.where(causal, sp, 0.0)
            hi = sp.astype(BF16)
            lo = (sp - hi.astype(F32)).astype(BF16)
            suf.append(_dot(hi, tri) + _dot(lo, tri) + run_refs[h][:, 0:1])
        for h in range(n_heads):
            w = jnp.exp2(z[h] - suf[h])
            if use_mask:
                w = jnp.where(causal, w, 0.0)
            p = h // 2
            acc_refs[h][...] += _dot(w.astype(BF16), v_ref[0, :, p * LANES:(p + 1) * LANES])
            run_refs[h][...] = jnp.broadcast_to(suf[h][:, 0:1], (tq, LANES))

    @pl.when(masked == 1)
    def _():
        step(True)

    @pl.when(masked == 0)
    def _():
        step(False)

    @pl.when(last == 1)
    def _():
        for p in range(pairs):
            o_ref[0, :, p * LANES:(p + 1) * LANES] = jnp.where(
                lo_lanes, acc_refs[2 * p][...], acc_refs[2 * p + 1][...]).astype(o_ref.dtype)


def _sb_table(nq, nk, tq, tk, q_off):
    rows = []
    for qi in range(nq):
        q_lo = q_off + qi * tq
        q_hi = q_lo + tq - 1
        kjs = [kj for kj in range(nk) if kj * tk < q_hi]
        if not kjs:
            kjs = [0]
        kjs = kjs[::-1]
        for n, kj in enumerate(kjs):
            masked = int(kj * tk + tk - 1 >= q_lo)
            rows.append((qi, kj, int(n == 0), int(n == len(kjs) - 1), masked))
    return np.asarray(rows, dtype=np.int32).T.copy()


def _sb_attention(q, k, v, tq, tk, q_off):
    b, t, _ = q.shape
    s_len = k.shape[1]
    tab = _sb_table(t // tq, s_len // tk, tq, tk, q_off)
    tri = jnp.asarray(np.tril(np.ones((tk, tk), np.float32)), dtype=BF16)
    pairs = SB_PAIRS_PER_STEP
    width = pairs * LANES
    grid_spec = pltpu.PrefetchScalarGridSpec(
        num_scalar_prefetch=1,
        grid=(b, N_PAIRS // pairs, tab.shape[1]),
        in_specs=[pl.BlockSpec((1, tq, width), lambda bi, hp, s, tab: (bi, tab[0, s], hp)),
                  pl.BlockSpec((1, tk, width), lambda bi, hp, s, tab: (bi, tab[1, s], hp)),
                  pl.BlockSpec((1, tk, width), lambda bi, hp, s, tab: (bi, tab[1, s], hp)),
                  pl.BlockSpec((tk, tk), lambda bi, hp, s, tab: (0, 0))],
        out_specs=pl.BlockSpec((1, tq, width), lambda bi, hp, s, tab: (bi, tab[0, s], hp)),
        scratch_shapes=[pltpu.VMEM((tq, LANES), F32)] * (4 * pairs),
    )
    return pl.pallas_call(
        functools.partial(_sb_kernel, tq=tq, tk=tk, q_off=q_off, pairs=pairs),
        out_shape=jax.ShapeDtypeStruct((b, t, WIDTH), BF16),
        grid_spec=grid_spec,
        compiler_params=_params(("parallel", "parallel", "arbitrary")),
        name="sb_attn",
    )(jnp.asarray(tab), q, k, v, tri)


def _sortable(x):
    bits = pltpu.bitcast(x + 0.0, I32)
    return jnp.where(bits < 0, bits ^ 0x7FFFFFFF, bits)


def _alibi_slope2(h):
    return LOG2E * 2.0 ** (-8.0 * (h + 1) / N_HEADS)


def _dsa_kernel(q_ref, iq_ref, iwt_ref, k_ref, vt_ref, ki_ref, o_ref, keys_ref, dm_ref, hi_ref, lo_ref, qt_ref, iqt_ref,
                *stats,
                 tq, tk, q_off, n_kb_max, n_sel, idx_bits):
    m_refs, l_refs, acc_refs = stats[:N_HEADS], stats[N_HEADS:2 * N_HEADS], stats[2 * N_HEADS:]
    qi = pl.program_id(1)
    q_lo = q_off + qi * tq
    adm_end = ((q_lo + tq - 1) // CHUNK + 1) * CHUNK
    n_kb = jnp.minimum((adm_end + tk - 1) // tk, n_kb_max)

    lo_half = lax.broadcasted_iota(I32, (tq, LANES), 1) < HEAD_DIM
    qpos = q_lo + lax.broadcasted_iota(I32, (tk, tq), 1)
    krow = lax.broadcasted_iota(I32, (tk, tq), 0)

    def heads_of(x_ref):
        out = []
        for hp in range(N_PAIRS):
            pair = x_ref[0, :, hp * LANES:(hp + 1) * LANES]
            zero = jnp.zeros_like(pair)
            out += [jnp.where(lo_half, pair, zero), jnp.where(lo_half, zero, pair)]
        return out

    for src, dst in ((iq_ref, iqt_ref), (q_ref, qt_ref)):
        for h, xe in enumerate(heads_of(src)):
            dst[h] = xe.astype(F32).T.astype(BF16)

    def score_block(j, carry):
        ki = ki_ref[0, pl.ds(pl.multiple_of(j * tk, tk), tk), :]
        score = jnp.zeros((tk, tq), F32)
        for h in range(N_HEADS):
            score = score + iwt_ref[0, h:h + 1, :] * jnp.maximum(_dot(ki, iqt_ref[h]), 0.0)
        adm = ((j * tk + krow) // CHUNK) <= (qpos // CHUNK)
        key = jnp.where(adm, _sortable(score), INT_MIN)
        keys_ref[j] = key
        hi_ref[j] = (key >> 16).astype(I16)
        lo_ref[j] = ((key & 0xFFFF) - HALF_BIAS).astype(I16)
        return carry

    lax.fori_loop(0, n_kb, score_block, 0)

    def count_cols(pred_fn):
        def body(j, acc):
            hit = pred_fn(keys_ref[j], j).astype(I32)
            return acc + jnp.sum(hit.reshape(tk // 8, 8, tq), axis=0)
        acc = lax.fori_loop(0, n_kb, body, jnp.zeros((8, tq), I32))
        return jnp.sum(acc, axis=0, keepdims=True)

    def count_half(ref, pred_fn):
        def body(j, acc):
            hit = jnp.where(pred_fn(ref[j]), jnp.int16(1), jnp.int16(0))
            for r in range(tk // 16):
                acc = acc + hit[r * 16:(r + 1) * 16, :]
            return acc
        acc = lax.fori_loop(0, n_kb, body, jnp.zeros((16, tq), I16))
        return jnp.sum(acc.astype(I32), axis=0, keepdims=True)

    def search_half(ref, target):
        def bit_step(i, t):
            cand = t + jnp.left_shift(jnp.int32(1), 15 - i)
            cand16 = cand.astype(I16)
            cnt = count_half(ref, lambda kb: kb >= cand16)
            return jnp.where(cnt >= target, cand, t)
        return lax.fori_loop(0, 16, bit_step, jnp.full((1, tq), -HALF_BIAS, I32))

    t_hi = search_half(hi_ref, n_sel)
    t_hi16 = t_hi.astype(I16)
    n_above = count_half(hi_ref, lambda kb: kb > t_hi16)

    def keep_group(j, carry):
        lo_ref[j] = jnp.where(hi_ref[j] == t_hi16, lo_ref[j], jnp.int16(-HALF_BIAS))
        return carry

    lax.fori_loop(0, n_kb, keep_group, 0)
    t_lo = search_half(lo_ref, n_sel - n_above)
    thr = t_hi * (2 * HALF_BIAS) + (t_lo + HALF_BIAS)
    n_gt = count_cols(lambda kb, j: kb > thr)
    n_ge = count_cols(lambda kb, j: kb >= thr)
    need = n_sel - n_gt
    real = thr > INT_MIN
    tie = jnp.max(jnp.where(real & (n_ge > n_sel), 1, 0)) > 0

    def tie_cut():
        def idx_step(i, cut):
            cand = cut + jnp.left_shift(jnp.int32(1), idx_bits - 1 - i)
            cnt = count_cols(lambda kb, j: (kb == thr) & ((j * tk + krow) < cand))
            return jnp.where(cnt < need, cand, cut)
        return lax.fori_loop(0, idx_bits, idx_step, jnp.zeros((1, tq), I32))

    cut = lax.cond(tie, tie_cut, lambda: jnp.full((1, tq), n_kb_max * tk, I32))

    def mask_block(j, carry):
        kb = keys_ref[j]
        kpos = j * tk + krow
        sel = (kb > thr) | ((kb == thr) & real & (kpos <= cut))
        dm_ref[j] = jnp.where(sel, jnp.abs(qpos - kpos).astype(F32), DIST_MASKED)
        return carry

    lax.fori_loop(0, n_kb, mask_block, 0)

    for h in range(N_HEADS):
        m_refs[h][...] = jnp.full_like(m_refs[h], NEG_BIG)
        l_refs[h][...] = jnp.zeros_like(l_refs[h])
        acc_refs[h][...] = jnp.zeros_like(acc_refs[h])

    def attn_block(j, carry):
        start = pl.multiple_of(j * tk, tk)
        dm = dm_ref[j]

        def qk(h):
            return _dot(k_ref[0, pl.ds(start, tk), (h // 2) * LANES:(h // 2 + 1) * LANES], qt_ref[h])

        ahead = 4
        pending = [qk(h) for h in range(ahead)]
        for hp in range(N_PAIRS):
            for e in range(2):
                h = 2 * hp + e
                s_this = pending.pop(0)
                if h + ahead < N_HEADS:
                    pending.append(qk(h + ahead))
                t2 = s_this - _alibi_slope2(h) * dm
                m_old = m_refs[h][...]
                m_new = jnp.maximum(m_old, jnp.max(t2, axis=0, keepdims=True))
                alpha = jnp.exp2(m_old - m_new)
                p = jnp.exp2(t2 - m_new)
                l_refs[h][...] = alpha * l_refs[h][...] + jnp.sum(p, axis=0, keepdims=True)
                vt = vt_ref[0, j, hp * LANES + e * HEAD_DIM:hp * LANES + (e + 1) * HEAD_DIM, :]
                acc_refs[h][...] = alpha * acc_refs[h][...] + _dot(vt, p.astype(BF16))
                m_refs[h][...] = m_new
        return carry

    lax.fori_loop(0, n_kb, attn_block, 0)

    for hp in range(N_PAIRS):
        ot = jnp.concatenate([acc_refs[2 * hp][...] / l_refs[2 * hp][...],
                              acc_refs[2 * hp + 1][...] / l_refs[2 * hp + 1][...]], axis=0)
        o_ref[0, :, hp * LANES:(hp + 1) * LANES] = ot.T.astype(o_ref.dtype)


def _dsa_attention(q, iq, iw, k, v, ki2, tq, tk, q_off, n_sel):
    b, t, _ = q.shape
    s_len = k.shape[1]
    n_kb_max = s_len // tk
    iwt = jnp.swapaxes(iw, 1, 2)
    vt = jnp.swapaxes(v.reshape(b, n_kb_max, tk, WIDTH), 2, 3)
    qspec = lambda w: pl.BlockSpec((1, tq, w), lambda bi, i: (bi, i, 0))
    kspec = lambda w: pl.BlockSpec((1, s_len, w), lambda bi, i: (bi, 0, 0), pipeline_mode=pl.Buffered(1))
    return pl.pallas_call(
        functools.partial(_dsa_kernel, tq=tq, tk=tk, q_off=q_off, n_kb_max=n_kb_max, n_sel=n_sel,
                          idx_bits=max(1, int(math.ceil(math.log2(s_len + 1))))),
        out_shape=jax.ShapeDtypeStruct((b, t, WIDTH), BF16),
        grid=(b, t // tq),
        in_specs=[qspec(WIDTH), qspec(WIDTH),
                  pl.BlockSpec((1, N_HEADS, tq), lambda bi, i: (bi, 0, i)),
                  kspec(WIDTH),
                  pl.BlockSpec((1, n_kb_max, WIDTH, tk), lambda bi, i: (bi, 0, 0, 0), pipeline_mode=pl.Buffered(1)),
                  kspec(LANES)],
        out_specs=qspec(WIDTH),
        scratch_shapes=[pltpu.VMEM((n_kb_max, tk, tq), I32), pltpu.VMEM((n_kb_max, tk, tq), F32),
                        pltpu.VMEM((n_kb_max, tk, tq), I16), pltpu.VMEM((n_kb_max, tk, tq), I16),
                        pltpu.VMEM((N_HEADS, LANES, tq), BF16), pltpu.VMEM((N_HEADS, LANES, tq), BF16)]
                       + [pltpu.VMEM((1, tq), F32)] * (2 * N_HEADS) + [pltpu.VMEM((HEAD_DIM, tq), F32)] * N_HEADS,
        compiler_params=_params(("parallel", "arbitrary")),
        name="dsa_attn",
    )(q, iq, iwt, k, vt, ki2)


def _mix_kernel(oa_ref, ob_ref, ga_ref, gb_ref, x_ref, mod_ref, gpost_ref, gpre_ref,
                wa_ref, wb_ref, wo_ref, wq_ref, x1_ref, h2_ref, pq_ref):
    mod = mod_ref[0]
    mixed = ga_ref[...] * _dot(oa_ref[...], wa_ref[...]) + gb_ref[...] * _dot(ob_ref[...], wb_ref[...])
    y = _dot(mixed.astype(BF16), wo_ref[...])
    x1 = x_ref[...] + mod[2:3, :] * _rms(y, gpost_ref[...])
    x1_ref[...] = x1
    h2 = _rms(x1, gpre_ref[...]) * (1.0 + mod[4:5, :]) + mod[3:4, :]
    h2_ref[...] = h2
    pq_ref[...] = _dot(h2.astype(BF16), wq_ref[...]).astype(BF16)


def _mix(oa, ob, ga, gb, x2, mod3, g_post, g_pre, wa, wb, wo, wq, tm, tiles_per_seq):
    n = x2.shape[0]
    row = lambda i: (i, 0)
    const = lambda i: (0, 0)
    nq = wq.shape[1]
    wspec = lambda r, c: pl.BlockSpec((r, c), const, pipeline_mode=pl.Buffered(1))
    return pl.pallas_call(
        _mix_kernel,
        out_shape=[jax.ShapeDtypeStruct((n, D_MODEL), F32),
                   jax.ShapeDtypeStruct((n, D_MODEL), F32),
                   jax.ShapeDtypeStruct((n, nq), BF16)],
        grid=(n // tm,),
        in_specs=[pl.BlockSpec((tm, WIDTH), row), pl.BlockSpec((tm, WIDTH), row),
                  pl.BlockSpec((tm, D_MODEL), row), pl.BlockSpec((tm, D_MODEL), row),
                  pl.BlockSpec((tm, D_MODEL), row),
                  pl.BlockSpec((1, 6, D_MODEL), lambda i: (i // tiles_per_seq, 0, 0)),
                  pl.BlockSpec((1, D_MODEL), const), pl.BlockSpec((1, D_MODEL), const),
                  wspec(WIDTH, D_MODEL), wspec(WIDTH, D_MODEL), wspec(D_MODEL, D_MODEL),
                  wspec(D_MODEL, nq)],
        out_specs=[pl.BlockSpec((tm, D_MODEL), row), pl.BlockSpec((tm, D_MODEL), row),
                   pl.BlockSpec((tm, nq), row)],
        compiler_params=_params(("parallel",)),
        name="mix",
    )(oa, ob, ga, gb, x2, mod3, g_post, g_pre, wa, wb, wo, wq)


def _top16_many(arrays, payloads=None):
    state = list(arrays)
    pos = [lax.broadcasted_iota(I32, a.shape, 0) for a in arrays]
    vals = [[] for _ in arrays]
    picks = [[] for _ in arrays]
    for _ in range(PEER_TOPK):
        for c, s in enumerate(state):
            m = jnp.max(s, axis=0, keepdims=True)
            p = jnp.min(jnp.where(s == m, pos[c], s.shape[0]), axis=0, keepdims=True)
            hit = pos[c] == p
            vals[c].append(m)
            picks[c].append(p if payloads is None
                            else jnp.sum(jnp.where(hit, payloads[c], 0), axis=0, keepdims=True))
            state[c] = jnp.where(hit, -jnp.inf, s)
    return [(jnp.concatenate(v, axis=0), jnp.concatenate(p, axis=0)) for v, p in zip(vals, picks)]


def _select_heads(pq_ref, sk_ref, heads):
    tt = pq_ref.shape[0]
    scores = []
    for h in heads:
        for half in range(2):
            lo = (2 * h + half) * PEER_DHALF
            scores.append(_dot_nt(sk_ref[h, half], pq_ref[:, lo:lo + PEER_DHALF]))
    tops = _top16_many(scores)
    half_k = PEER_TOPK // 2
    row8 = lax.broadcasted_iota(I32, (half_k, tt), 0)
    cands, cidxs = [], []
    for n in range(len(heads)):
        (v1, i1), (v2, i2) = tops[2 * n], tops[2 * n + 1]
        cand = [v1[0:1] + v2]
        cidx = [i1[0:1] * PEER_NKEYS + i2]
        for i in range(1, half_k):
            cand.append(jnp.where(row8 < PEER_TOPK // (i + 1), v1[i:i + 1] + v2[0:half_k], -jnp.inf))
            cidx.append(i1[i:i + 1] * PEER_NKEYS + i2[0:half_k])
        cand.append(v1[half_k:] + v2[0:1])
        cidx.append(i1[half_k:] * PEER_NKEYS + i2[0:1])
        cands.append(jnp.concatenate(cand, axis=0))
        cidxs.append(jnp.concatenate(cidx, axis=0))
    out = []
    for top, eidx in _top16_many(cands, cidxs):
        ex = jnp.exp(top - top[0:1])
        out.append((eidx, ex / jnp.sum(ex, axis=0, keepdims=True)))
    return out


def _erf_gelu(a):
    return 0.5 * a * (1.0 + lax.erf(a * (2.0 ** -0.5)))


def _peer_kernel(pq_ref, sk_ref, h2_ref, x1_ref, mod_ref, gpost_ref, uv_hbm, o_ref,
                 *scratch, tg, seqs_per_tile):
    bufs = scratch[:PEER_SLOTS]
    f_ref, sem, idx_sm, gt_vm, eidx_vm, eidx_t_vm, pub_sem = scratch[PEER_SLOTS:]
    s = pl.program_id(0)
    n_tiles = pl.num_programs(0) - 1
    par = lax.rem(s, 2)
    prev = 1 - par
    have_sel = s < n_tiles
    have_ffn = s >= 1

    def start_token(t, slot, which):
        for k in range(PEER_SEL):
            pltpu.make_async_copy(uv_hbm.at[idx_sm[which, t, k]], bufs[slot].at[:, pl.ds(k, 1), :],
                                  sem.at[slot]).start(priority=k % 2)

    def wait_token(slot):
        pltpu.make_async_copy(bufs[slot], bufs[slot], sem.at[slot]).wait()

    lane_tok = lax.broadcasted_iota(I32, (PEER_SEL, tg), 1)
    sub8 = lax.broadcasted_iota(I32, (8, D_MODEL), 0)

    def compute(t, slot):
        buf = bufs[slot]
        t8 = pl.multiple_of((t // 8) * 8, 8)
        h8 = h2_ref[pl.ds(t8, 8), :]
        hrow = jnp.sum(jnp.where(sub8 == t - t8, h8, 0.0), axis=0, keepdims=True)
        acc = None
        for c in range(N_CHUNK):
            term = buf[c] * hrow[:, c * LANES:(c + 1) * LANES]
            acc = term if acc is None else acc + term
        a = jnp.sum(acc, axis=1, keepdims=True)
        g = jnp.sum(jnp.where(lane_tok == t, gt_vm[prev], 0.0), axis=1, keepdims=True)
        w = g * _erf_gelu(a)
        pieces = []
        for c in range(N_CHUNK):
            vc = buf[N_CHUNK + c]
            pieces.append(jnp.sum(vc * w, axis=0, keepdims=True))
        f_ref[pl.ds(t, 1), :] = jnp.concatenate(pieces, axis=1)

    def group(i, carry):
        t0 = PEER_SLOTS * i
        for j in range(PEER_SLOTS):
            wait_token(j)
            start_token(t0 + j + PEER_AHEAD, (j + PEER_AHEAD) % PEER_SLOTS, prev)
            compute(t0 + j, j)
        return carry

    def mlp_groups(lo, hi):
        @pl.when(have_ffn)
        def _():
            lax.fori_loop(lo, hi, group, 0)

    def select_piece(n):
        heads_per = PEER_HEADS // PEER_SELECT_PIECES

        @pl.when(have_sel)
        def _():
            heads = list(range(n * heads_per, (n + 1) * heads_per))
            for h, (eidx, gate) in zip(heads, _select_heads(pq_ref, sk_ref, heads)):
                gt_vm[par, h * PEER_TOPK:(h + 1) * PEER_TOPK, :] = gate
                eidx_vm[h * PEER_TOPK:(h + 1) * PEER_TOPK, :] = eidx

    n_groups = tg // PEER_SLOTS
    cuts = [(n_groups - 1) * (n + 1) // PEER_SELECT_PIECES for n in range(PEER_SELECT_PIECES)]
    lo = 0
    for n, hi in enumerate(cuts):
        mlp_groups(lo, hi)
        select_piece(n)
        lo = hi

    @pl.when(have_sel)
    def _():
        eidx_t_vm[...] = eidx_vm[...].T
        cp = pltpu.make_async_copy(eidx_t_vm, idx_sm.at[par], pub_sem.at[0])
        cp.start()
        cp.wait()

    @pl.when(s == 0)
    def _():
        for t in range(PEER_AHEAD):
            start_token(t, t, par)

    @pl.when(have_ffn)
    def _():
        nxt = jnp.where(have_sel, par, prev)
        t0 = tg - PEER_SLOTS
        for j in range(PEER_SLOTS):
            wait_token(j)
            ahead = j + PEER_AHEAD
            if ahead < PEER_SLOTS:
                start_token(t0 + ahead, ahead, prev)
            else:
                start_token(ahead - PEER_SLOTS, ahead - PEER_SLOTS, nxt)
            compute(t0 + j, j)
        rows = tg // seqs_per_tile
        for q in range(seqs_per_tile):
            sl = slice(q * rows, (q + 1) * rows)
            o_ref[sl, :] = x1_ref[sl, :] + mod_ref[q][5:6, :] * _rms(f_ref[sl, :], gpost_ref[...])

    @pl.when(s == n_tiles)
    def _():
        for t in range(PEER_AHEAD):
            wait_token(t)


def _peer(pq, subkeys_b, h2, x1, mod3, g_post, uv, tg, seq_len):
    n = h2.shape[0]
    n_tiles = n // tg
    seqs_per_tile = max(1, tg // seq_len)
    tiles_per_seq = max(1, seq_len // tg)
    mlp_tile = lambda s: jnp.maximum(s - 1, 0)
    row = lambda s: (mlp_tile(s), 0)
    return pl.pallas_call(
        functools.partial(_peer_kernel, tg=tg, seqs_per_tile=seqs_per_tile),
        out_shape=jax.ShapeDtypeStruct((n, D_MODEL), F32),
        grid=(n_tiles + 1,),
        in_specs=[pl.BlockSpec((tg, PEER_HEADS * 2 * PEER_DHALF), lambda s: (jnp.minimum(s, n_tiles - 1), 0)),
                  pl.BlockSpec((PEER_HEADS, 2, PEER_NKEYS, PEER_DHALF), lambda s: (0, 0, 0, 0)),
                  pl.BlockSpec((tg, D_MODEL), row),
                  pl.BlockSpec((tg, D_MODEL), row),
                  pl.BlockSpec((seqs_per_tile, 6, D_MODEL), lambda s: (mlp_tile(s) // tiles_per_seq, 0, 0)),
                  pl.BlockSpec((1, D_MODEL), lambda s: (0, 0)),
                  pl.BlockSpec(memory_space=pl.ANY)],
        out_specs=pl.BlockSpec((tg, D_MODEL), row),
        scratch_shapes=[pltpu.VMEM((2 * N_CHUNK, PEER_SEL, LANES), F32)] * PEER_SLOTS
                       + [pltpu.VMEM((tg, D_MODEL), F32), pltpu.SemaphoreType.DMA((PEER_SLOTS,)),
                          pltpu.SMEM((2, tg, PEER_SEL), I32), pltpu.VMEM((2, PEER_SEL, tg), F32),
                          pltpu.VMEM((PEER_SEL, tg), I32), pltpu.VMEM((tg, PEER_SEL), I32),
                          pltpu.SemaphoreType.DMA((1,))],
        compiler_params=_params(("arbitrary",)),
        name="peer",
    )(pq, subkeys_b, h2, x1, mod3, g_post, uv)


def _pad_keys(x, s_pad):
    return jnp.pad(x, ((0, 0), (0, s_pad - x.shape[1]), (0, 0)))


def _layer(x, mod3, past, weights, tiles):
    (g_pre_mix, g_post_mix, g_pre_ffn, g_post_ffn, w_in_b, w_gate_b, wa_b, wb_b, wo_b, wq_b,
     subkeys_b, peer_uv) = weights
    tm, sb_tq, sb_tk, dsa_tq, dsa_tk, tt, tg = tiles
    bsz, t, _ = x.shape
    n = bsz * t
    x2 = x.reshape(n, D_MODEL)
    (aq, ak, av, bq, bk, bv, iq, ki2, akf, avf, bkf, bvf, ikf, iw, ga, gb) = _inproj(
        x2, mod3, g_pre_mix, w_in_b, w_gate_b, tm, t // tm)
    seq = lambda a: a.reshape(bsz, t, a.shape[-1])
    if past is None:
        q_off = 0
        ka, va, kb, vb, ki = seq(ak), seq(av), seq(bk), seq(bv), seq(ki2)
        s_len = t
    else:
        p_sbk, p_sbv, p_dk, p_dv, p_ki = past
        q_off = p_sbk.shape[1]
        s_len = q_off + t
        flat = lambda c: c.reshape(bsz, q_off, WIDTH).astype(BF16)
        ka = jnp.concatenate([flat(p_sbk), seq(ak)], axis=1)
        va = jnp.concatenate([flat(p_sbv), seq(av)], axis=1)
        kb = jnp.concatenate([flat(p_dk), seq(bk)], axis=1)
        vb = jnp.concatenate([flat(p_dv), seq(bv)], axis=1)
        pk = p_ki.astype(BF16)
        ki = jnp.concatenate([jnp.concatenate([pk, pk], axis=-1), seq(ki2)], axis=1)
    s_pad = -(-s_len // max(sb_tk, dsa_tk)) * max(sb_tk, dsa_tk)
    ka, va, kb, vb, ki = (_pad_keys(a, s_pad) for a in (ka, va, kb, vb, ki))
    n_sel = min(TOPK_MAX, s_len // 4)

    oa = _sb_attention(seq(aq), ka, va, sb_tq, sb_tk, q_off)
    ob = _dsa_attention(seq(bq), seq(iq), seq(iw), kb, vb, ki, dsa_tq, dsa_tk, q_off, n_sel)

    x1, h2, pq = _mix(oa.reshape(n, WIDTH), ob.reshape(n, WIDTH), ga, gb, x2, mod3, g_post_mix, g_pre_ffn,
                      wa_b, wb_b, wo_b, wq_b, tm, t // tm)
    y = _peer(pq, subkeys_b, h2, x1, mod3, g_post_ffn, peer_uv, tg, t)
    new = tuple(a.reshape(1, bsz, t, N_HEADS, HEAD_DIM) for a in (akf, avf, bkf, bvf)) + (
        ikf.reshape(1, bsz, t, HEAD_DIM),)
    return y.reshape(bsz, t, D_MODEL), new


def _tiles(t):
    if t % 256 == 0:
        return (256, 256, 256, 256, 512, 128, 128)
    return (t, t, 128, t, 128, 128, 128)


def kernel(x_prompt, x_sample, c_prompt, c_sample, cache_sb_k, cache_sb_v, cache_dsa_k, cache_dsa_v, cache_dsa_kidx, w_ada, b_ada, g_pre_mix, g_post_mix, g_pre_ffn, g_post_ffn, w_in, w_gate, w_branch_a, w_branch_b, w_out, w_peer_q, peer_subkeys, peer_u, peer_v):
    assert w_ada.shape[0] == 1, "one layer"
    n_p, n_s = c_prompt.shape[0], c_sample.shape[0]
    c_all = jnp.concatenate([c_prompt, c_sample], axis=0)
    rows = -(-c_all.shape[0] // 8) * 8
    c_all = jnp.pad(c_all, ((0, rows - c_all.shape[0]), (0, 0)))
    mod = _modulation(c_all, w_ada[0], b_ada[0]).reshape(rows, 6, D_MODEL)
    w_in_b = jnp.pad(w_in[0], ((0, 0), (0, W_IN_PAD - W_IN_COLS))).astype(BF16)
    weights = (g_pre_mix, g_post_mix, g_pre_ffn, g_post_ffn, w_in_b, w_gate[0].astype(BF16),
               w_branch_a[0].astype(BF16), w_branch_b[0].astype(BF16), w_out[0].astype(BF16),
               w_peer_q[0].astype(BF16), peer_subkeys[0].astype(BF16),
               jnp.concatenate([peer_u[0].reshape(-1, N_CHUNK, 1, LANES),
                                peer_v[0].reshape(-1, N_CHUNK, 1, LANES)], axis=1))
    yp, new_p = _layer(x_prompt, mod[:n_p], None, weights, _tiles(x_prompt.shape[1]))
    past = (cache_sb_k[0], cache_sb_v[0], cache_dsa_k[0], cache_dsa_v[0], cache_dsa_kidx[0])
    ys, new_s = _layer(x_sample, mod[n_p:n_p + n_s], past, weights, _tiles(x_sample.shape[1]))
    return (yp, ys) + new_p + new_s
```

```python
import functools
import math

import jax
import jax.numpy as jnp
import numpy as np
from jax import lax
from jax.experimental import pallas as pl
from jax.experimental.pallas import tpu as pltpu

F32 = jnp.float32
BF16 = jnp.bfloat16
I32 = jnp.int32
I16 = jnp.int16

D_MODEL = 1024
CHUNK = 64
EPS = 1e-6
N_HEADS = 8
HEAD_DIM = 64
WIDTH = N_HEADS * HEAD_DIM
N_PAIRS = N_HEADS // 2
SB_PAIRS_PER_STEP = 4
LANES = 128
N_CHUNK = D_MODEL // LANES
TOPK_MAX = 256
PEER_HEADS = 8
PEER_NKEYS = 128
PEER_DHALF = 128
PEER_TOPK = 16
PEER_SEL = PEER_HEADS * PEER_TOPK
PEER_SELECT_HEADS = 4
PEER_SLOTS = 8
PEER_AHEAD = 6
W_IN_COLS = 7 * WIDTH + HEAD_DIM + N_HEADS
W_IN_PAD = 7 * WIDTH + LANES
INT_MIN = -(2 ** 31)
HALF_BIAS = 2 ** 15
NEG_BIG = -1e30
DIST_MASKED = 1e30
LOG2E = 1.4426950408889634
VMEM_LIMIT = 56 * 1024 * 1024


def _params(sem):
    return pltpu.CompilerParams(dimension_semantics=sem, vmem_limit_bytes=VMEM_LIMIT)


def _dot(a, b):
    return jnp.dot(a, b, preferred_element_type=F32)


def _dot_nt(a, b):
    return lax.dot_general(a, b, (((1,), (1,)), ((), ())), preferred_element_type=F32)


def _rms(x, gain):
    return x * lax.rsqrt(jnp.mean(x * x, axis=-1, keepdims=True) + EPS) * gain


def _mod_kernel(c_ref, w_ref, b_ref, o_ref):
    c = c_ref[...]
    s = c * (1.0 / (1.0 + jnp.exp(-c)))
    o_ref[...] = jnp.dot(s, w_ref[...], preferred_element_type=F32,
                         precision=lax.Precision.HIGHEST) + b_ref[...]


def _modulation(c, w_ada, b_ada):
    n = c.shape[0]
    cols = w_ada.shape[1]
    tn = 1024
    return pl.pallas_call(
        _mod_kernel,
        out_shape=jax.ShapeDtypeStruct((n, cols), F32),
        grid=(cols // tn,),
        in_specs=[pl.BlockSpec((n, D_MODEL), lambda j: (0, 0)),
                  pl.BlockSpec((D_MODEL, tn), lambda j: (0, j)),
                  pl.BlockSpec((1, tn), lambda j: (0, j))],
        out_specs=pl.BlockSpec((n, tn), lambda j: (0, j)),
        compiler_params=_params(("arbitrary",)),
        name="mod",
    )(c, w_ada, b_ada.reshape(1, cols))


def _inproj_kernel(x_ref, mod_ref, g_ref, win_ref, wg_ref,
                   aq_ref, ak_ref, av_ref, bq_ref, bk_ref, bv_ref, iq_ref, ki2_ref,
                   akf_ref, avf_ref, bkf_ref, bvf_ref, ikf_ref, iw_ref, ga_ref, gb_ref):
    x = x_ref[...]
    mod = mod_ref[0]
    h = _rms(x, g_ref[...]) * (1.0 + mod[1:2, :]) + mod[0:1, :]
    hb = h.astype(BF16)
    qscale = HEAD_DIM ** -0.5

    def seg(i):
        return _dot(hb, win_ref[:, i * WIDTH:(i + 1) * WIDTH])

    aq_ref[...] = (seg(0) * (qscale * LOG2E)).astype(BF16)
    z = seg(1)
    akf_ref[...] = z
    ak_ref[...] = z.astype(BF16)
    z = seg(2)
    avf_ref[...] = z
    av_ref[...] = z.astype(BF16)
    bq_ref[...] = (seg(3) * (qscale * LOG2E)).astype(BF16)
    z = seg(4)
    bkf_ref[...] = z
    bk_ref[...] = z.astype(BF16)
    z = seg(5)
    bvf_ref[...] = z
    bv_ref[...] = z.astype(BF16)
    iq_ref[...] = (seg(6) * qscale).astype(BF16)
    tail = _dot(hb, win_ref[:, 7 * WIDTH:7 * WIDTH + LANES])
    ik = tail[:, :HEAD_DIM]
    ikf_ref[...] = ik
    ki2_ref[...] = jnp.concatenate([ik, ik], axis=-1).astype(BF16)
    iw_ref[...] = tail[:, HEAD_DIM:HEAD_DIM + N_HEADS] * (N_HEADS ** -0.5)
    zg = _dot(hb, wg_ref[...])
    gate = 1.0 / (1.0 + jnp.exp(-zg))
    ga_ref[...] = gate[:, :D_MODEL]
    gb_ref[...] = gate[:, D_MODEL:]


def _inproj(x2, mod3, g_pre, w_in_b, w_gate_b, tm, tiles_per_seq):
    n = x2.shape[0]
    row = lambda i: (i, 0)
    const = lambda i: (0, 0)
    bspec = lambda w: pl.BlockSpec((tm, w), row)
    outs = ([jax.ShapeDtypeStruct((n, WIDTH), BF16)] * 7
            + [jax.ShapeDtypeStruct((n, LANES), BF16)]
            + [jax.ShapeDtypeStruct((n, WIDTH), F32)] * 4
            + [jax.ShapeDtypeStruct((n, HEAD_DIM), F32),
               jax.ShapeDtypeStruct((n, N_HEADS), F32),
               jax.ShapeDtypeStruct((n, D_MODEL), F32),
               jax.ShapeDtypeStruct((n, D_MODEL), F32)])
    out_specs = ([bspec(WIDTH)] * 7 + [bspec(LANES)] + [bspec(WIDTH)] * 4
                 + [bspec(HEAD_DIM), bspec(N_HEADS), bspec(D_MODEL), bspec(D_MODEL)])
    return pl.pallas_call(
        _inproj_kernel,
        out_shape=outs,
        grid=(n // tm,),
        in_specs=[pl.BlockSpec((tm, D_MODEL), row),
                  pl.BlockSpec((1, 6, D_MODEL), lambda i: (i // tiles_per_seq, 0, 0)),
                  pl.BlockSpec((1, D_MODEL), const),
                  pl.BlockSpec((D_MODEL, W_IN_PAD), const, pipeline_mode=pl.Buffered(1)),
                  pl.BlockSpec((D_MODEL, 2 * D_MODEL), const, pipeline_mode=pl.Buffered(1))],
        out_specs=out_specs,
        compiler_params=_params(("parallel",)),
        name="inproj",
    )(x2, mod3, g_pre, w_in_b, w_gate_b)


def _sb_kernel(tab_ref, q_ref, k_ref, v_ref, tri_ref, o_ref, *scratch, tq, tk, q_off, pairs):
    s = pl.program_id(2)
    qi = tab_ref[0, s]
    kj = tab_ref[1, s]
    first = tab_ref[2, s]
    last = tab_ref[3, s]
    masked = tab_ref[4, s]
    n_heads = 2 * pairs
    acc_refs, run_refs = scratch[:n_heads], scratch[n_heads:]

    @pl.when(first == 1)
    def _():
        for r in scratch:
            r[...] = jnp.zeros_like(r)

    lo_lanes = lax.broadcasted_iota(I32, (tq, LANES), 1) < HEAD_DIM

    def step(use_mask):
        tri = tri_ref[...]
        if use_mask:
            qpos = q_off + qi * tq + lax.broadcasted_iota(I32, (tq, tk), 0)
            kpos = kj * tk + lax.broadcasted_iota(I32, (tq, tk), 1)
            causal = kpos < qpos
        z = []
        for p in range(pairs):
            q = q_ref[0, :, p * LANES:(p + 1) * LANES]
            k = k_ref[0, :, p * LANES:(p + 1) * LANES]
            zero = jnp.zeros_like(q)
            z += [_dot_nt(jnp.where(lo_lanes, q, zero), k), _dot_nt(jnp.where(lo_lanes, zero, q), k)]
        suf = []
        for h in range(n_heads):
            sp = jnp.maximum(z[h], 0.0) + jnp.log2(1.0 + jnp.exp2(-jnp.abs(z[h])))
            if use_mask:
                sp = jnp.where(causal, sp, 0.0)
            hi = sp.astype(BF16)
            lo = (sp - hi.astype(F32)).astype(BF16)
            suf.append(_dot(hi, tri) + _dot(lo, tri) + run_refs[h][:, 0:1])
        for h in range(n_heads):
            w = jnp.exp2(z[h] - suf[h])
            if use_mask:
                w = jnp.where(causal, w, 0.0)
            p = h // 2
            acc_refs[h][...] += _dot(w.astype(BF16), v_ref[0, :, p * LANES:(p + 1) * LANES])
            run_refs[h][...] = jnp.broadcast_to(suf[h][:, 0:1], (tq, LANES))

    @pl.when(masked == 1)
    def _():
        step(True)

    @pl.when(masked == 0)
    def _():
        step(False)

    @pl.when(last == 1)
    def _():
        for p in range(pairs):
            o_ref[0, :, p * LANES:(p + 1) * LANES] = jnp.where(
                lo_lanes, acc_refs[2 * p][...], acc_refs[2 * p + 1][...]).astype(o_ref.dtype)


def _sb_table(nq, nk, tq, tk, q_off):
    rows = []
    for qi in range(nq):
        q_lo = q_off + qi * tq
        q_hi = q_lo + tq - 1
        kjs = [kj for kj in range(nk) if kj * tk < q_hi]
        if not kjs:
            kjs = [0]
        kjs = kjs[::-1]
        for n, kj in enumerate(kjs):
            masked = int(kj * tk + tk - 1 >= q_lo)
            rows.append((qi, kj, int(n == 0), int(n == len(kjs) - 1), masked))
    return np.asarray(rows, dtype=np.int32).T.copy()


def _sb_attention(q, k, v, tq, tk, q_off):
    b, t, _ = q.shape
    s_len = k.shape[1]
    tab = _sb_table(t // tq, s_len // tk, tq, tk, q_off)
    tri = jnp.asarray(np.tril(np.ones((tk, tk), np.float32)), dtype=BF16)
    pairs = SB_PAIRS_PER_STEP
    width = pairs * LANES
    grid_spec = pltpu.PrefetchScalarGridSpec(
        num_scalar_prefetch=1,
        grid=(b, N_PAIRS // pairs, tab.shape[1]),
        in_specs=[pl.BlockSpec((1, tq, width), lambda bi, hp, s, tab: (bi, tab[0, s], hp)),
                  pl.BlockSpec((1, tk, width), lambda bi, hp, s, tab: (bi, tab[1, s], hp)),
                  pl.BlockSpec((1, tk, width), lambda bi, hp, s, tab: (bi, tab[1, s], hp)),
                  pl.BlockSpec((tk, tk), lambda bi, hp, s, tab: (0, 0))],
        out_specs=pl.BlockSpec((1, tq, width), lambda bi, hp, s, tab: (bi, tab[0, s], hp)),
        scratch_shapes=[pltpu.VMEM((tq, LANES), F32)] * (4 * pairs),
    )
    return pl.pallas_call(
        functools.partial(_sb_kernel, tq=tq, tk=tk, q_off=q_off, pairs=pairs),
        out_shape=jax.ShapeDtypeStruct((b, t, WIDTH), BF16),
        grid_spec=grid_spec,
        compiler_params=_params(("parallel", "parallel", "arbitrary")),
        name="sb_attn",
    )(jnp.asarray(tab), q, k, v, tri)


def _sortable(x):
    bits = pltpu.bitcast(x + 0.0, I32)
    return jnp.where(bits < 0, bits ^ 0x7FFFFFFF, bits)


def _alibi_slope2(h):
    return LOG2E * 2.0 ** (-8.0 * (h + 1) / N_HEADS)


def _dsa_kernel(q_ref, iq_ref, iwt_ref, k_ref, vt_ref, ki_ref, o_ref, keys_ref, dm_ref, hi_ref, lo_ref, qt_ref, iqt_ref,
                *stats,
                 tq, tk, q_off, n_kb_max, n_sel, idx_bits):
    m_refs, l_refs, acc_refs = stats[:N_HEADS], stats[N_HEADS:2 * N_HEADS], stats[2 * N_HEADS:]
    qi = pl.program_id(1)
    q_lo = q_off + qi * tq
    adm_end = ((q_lo + tq - 1) // CHUNK + 1) * CHUNK
    n_kb = jnp.minimum((adm_end + tk - 1) // tk, n_kb_max)

    lo_half = lax.broadcasted_iota(I32, (tq, LANES), 1) < HEAD_DIM
    qpos = q_lo + lax.broadcasted_iota(I32, (tk, tq), 1)
    krow = lax.broadcasted_iota(I32, (tk, tq), 0)

    def heads_of(x_ref):
        out = []
        for hp in range(N_PAIRS):
            pair = x_ref[0, :, hp * LANES:(hp + 1) * LANES]
            zero = jnp.zeros_like(pair)
            out += [jnp.where(lo_half, pair, zero), jnp.where(lo_half, zero, pair)]
        return out

    for src, dst in ((iq_ref, iqt_ref), (q_ref, qt_ref)):
        for h, xe in enumerate(heads_of(src)):
            dst[h] = xe.astype(F32).T.astype(BF16)

    def score_block(j, carry):
        ki = ki_ref[0, pl.ds(pl.multiple_of(j * tk, tk), tk), :]
        score = jnp.zeros((tk, tq), F32)
        for h in range(N_HEADS):
            score = score + iwt_ref[0, h:h + 1, :] * jnp.maximum(_dot(ki, iqt_ref[h]), 0.0)
        adm = ((j * tk + krow) // CHUNK) <= (qpos // CHUNK)
        key = jnp.where(adm, _sortable(score), INT_MIN)
        keys_ref[j] = key
        hi_ref[j] = (key >> 16).astype(I16)
        lo_ref[j] = ((key & 0xFFFF) - HALF_BIAS).astype(I16)
        return carry

    lax.fori_loop(0, n_kb, score_block, 0)

    def count_cols(pred_fn):
        def body(j, acc):
            hit = pred_fn(keys_ref[j], j).astype(I32)
            return acc + jnp.sum(hit.reshape(tk // 8, 8, tq), axis=0)
        acc = lax.fori_loop(0, n_kb, body, jnp.zeros((8, tq), I32))
        return jnp.sum(acc, axis=0, keepdims=True)

    def count_half(ref, pred_fn):
        def body(j, acc):
            hit = jnp.where(pred_fn(ref[j]), jnp.int16(1), jnp.int16(0))
            for r in range(tk // 16):
                acc = acc + hit[r * 16:(r + 1) * 16, :]
            return acc
        acc = lax.fori_loop(0, n_kb, body, jnp.zeros((16, tq), I16))
        return jnp.sum(acc.astype(I32), axis=0, keepdims=True)

    def search_half(ref, target):
        def bit_step(i, t):
            cand = t + jnp.left_shift(jnp.int32(1), 15 - i)
            cand16 = cand.astype(I16)
            cnt = count_half(ref, lambda kb: kb >= cand16)
            return jnp.where(cnt >= target, cand, t)
        return lax.fori_loop(0, 16, bit_step, jnp.full((1, tq), -HALF_BIAS, I32))

    t_hi = search_half(hi_ref, n_sel)
    t_hi16 = t_hi.astype(I16)
    n_above = count_half(hi_ref, lambda kb: kb > t_hi16)

    def keep_group(j, carry):
        lo_ref[j] = jnp.where(hi_ref[j] == t_hi16, lo_ref[j], jnp.int16(-HALF_BIAS))
        return carry

    lax.fori_loop(0, n_kb, keep_group, 0)
    t_lo = search_half(lo_ref, n_sel - n_above)
    thr = t_hi * (2 * HALF_BIAS) + (t_lo + HALF_BIAS)
    n_gt = count_cols(lambda kb, j: kb > thr)
    n_ge = count_cols(lambda kb, j: kb >= thr)
    need = n_sel - n_gt
    real = thr > INT_MIN
    tie = jnp.max(jnp.where(real & (n_ge > n_sel), 1, 0)) > 0

    def tie_cut():
        def idx_step(i, cut):
            cand = cut + jnp.left_shift(jnp.int32(1), idx_bits - 1 - i)
            cnt = count_cols(lambda kb, j: (kb == thr) & ((j * tk + krow) < cand))
            return jnp.where(cnt < need, cand, cut)
        return lax.fori_loop(0, idx_bits, idx_step, jnp.zeros((1, tq), I32))

    cut = lax.cond(tie, tie_cut, lambda: jnp.full((1, tq), n_kb_max * tk, I32))

    def mask_block(j, carry):
        kb = keys_ref[j]
        kpos = j * tk + krow
        sel = (kb > thr) | ((kb == thr) & real & (kpos <= cut))
        dm_ref[j] = jnp.where(sel, jnp.abs(qpos - kpos).astype(F32), DIST_MASKED)
        return carry

    lax.fori_loop(0, n_kb, mask_block, 0)

    for h in range(N_HEADS):
        m_refs[h][...] = jnp.full_like(m_refs[h], NEG_BIG)
        l_refs[h][...] = jnp.zeros_like(l_refs[h])
        acc_refs[h][...] = jnp.zeros_like(acc_refs[h])

    def attn_block(j, carry):
        start = pl.multiple_of(j * tk, tk)
        dm = dm_ref[j]

        def qk(h):
            return _dot(k_ref[0, pl.ds(start, tk), (h // 2) * LANES:(h // 2 + 1) * LANES], qt_ref[h])

        ahead = 4
        pending = [qk(h) for h in range(ahead)]
        for hp in range(N_PAIRS):
            for e in range(2):
                h = 2 * hp + e
                s_this = pending.pop(0)
                if h + ahead < N_HEADS:
                    pending.append(qk(h + ahead))
                t2 = s_this - _alibi_slope2(h) * dm
                m_old = m_refs[h][...]
                m_new = jnp.maximum(m_old, jnp.max(t2, axis=0, keepdims=True))
                alpha = jnp.exp2(m_old - m_new)
                p = jnp.exp2(t2 - m_new)
                l_refs[h][...] = alpha * l_refs[h][...] + jnp.sum(p, axis=0, keepdims=True)
                vt = vt_ref[0, j, hp * LANES + e * HEAD_DIM:hp * LANES + (e + 1) * HEAD_DIM, :]
                acc_refs[h][...] = alpha * acc_refs[h][...] + _dot(vt, p.astype(BF16))
                m_refs[h][...] = m_new
        return carry

    lax.fori_loop(0, n_kb, attn_block, 0)

    for hp in range(N_PAIRS):
        ot = jnp.concatenate([acc_refs[2 * hp][...] / l_refs[2 * hp][...],
                              acc_refs[2 * hp + 1][...] / l_refs[2 * hp + 1][...]], axis=0)
        o_ref[0, :, hp * LANES:(hp + 1) * LANES] = ot.T.astype(o_ref.dtype)


def _dsa_attention(q, iq, iw, k, v, ki2, tq, tk, q_off, n_sel):
    b, t, _ = q.shape
    s_len = k.shape[1]
    n_kb_max = s_len // tk
    iwt = jnp.swapaxes(iw, 1, 2)
    vt = jnp.swapaxes(v.reshape(b, n_kb_max, tk, WIDTH), 2, 3)
    qspec = lambda w: pl.BlockSpec((1, tq, w), lambda bi, i: (bi, i, 0))
    kspec = lambda w: pl.BlockSpec((1, s_len, w), lambda bi, i: (bi, 0, 0), pipeline_mode=pl.Buffered(1))
    return pl.pallas_call(
        functools.partial(_dsa_kernel, tq=tq, tk=tk, q_off=q_off, n_kb_max=n_kb_max, n_sel=n_sel,
                          idx_bits=max(1, int(math.ceil(math.log2(s_len + 1))))),
        out_shape=jax.ShapeDtypeStruct((b, t, WIDTH), BF16),
        grid=(b, t // tq),
        in_specs=[qspec(WIDTH), qspec(WIDTH),
                  pl.BlockSpec((1, N_HEADS, tq), lambda bi, i: (bi, 0, i)),
                  kspec(WIDTH),
                  pl.BlockSpec((1, n_kb_max, WIDTH, tk), lambda bi, i: (bi, 0, 0, 0), pipeline_mode=pl.Buffered(1)),
                  kspec(LANES)],
        out_specs=qspec(WIDTH),
        scratch_shapes=[pltpu.VMEM((n_kb_max, tk, tq), I32), pltpu.VMEM((n_kb_max, tk, tq), F32),
                        pltpu.VMEM((n_kb_max, tk, tq), I16), pltpu.VMEM((n_kb_max, tk, tq), I16),
                        pltpu.VMEM((N_HEADS, LANES, tq), BF16), pltpu.VMEM((N_HEADS, LANES, tq), BF16)]
                       + [pltpu.VMEM((1, tq), F32)] * (2 * N_HEADS) + [pltpu.VMEM((HEAD_DIM, tq), F32)] * N_HEADS,
        compiler_params=_params(("parallel", "arbitrary")),
        name="dsa_attn",
    )(q, iq, iwt, k, vt, ki2)


def _mix_kernel(oa_ref, ob_ref, ga_ref, gb_ref, x_ref, mod_ref, gpost_ref, gpre_ref,
                wa_ref, wb_ref, wo_ref, wq_ref, x1_ref, h2_ref, pq_ref):
    mod = mod_ref[0]
    mixed = ga_ref[...] * _dot(oa_ref[...], wa_ref[...]) + gb_ref[...] * _dot(ob_ref[...], wb_ref[...])
    y = _dot(mixed.astype(BF16), wo_ref[...])
    x1 = x_ref[...] + mod[2:3, :] * _rms(y, gpost_ref[...])
    x1_ref[...] = x1
    h2 = _rms(x1, gpre_ref[...]) * (1.0 + mod[4:5, :]) + mod[3:4, :]
    h2_ref[...] = h2
    pq_ref[...] = _dot(h2.astype(BF16), wq_ref[...]).astype(BF16)


def _mix(oa, ob, ga, gb, x2, mod3, g_post, g_pre, wa, wb, wo, wq, tm, tiles_per_seq):
    n = x2.shape[0]
    row = lambda i: (i, 0)
    const = lambda i: (0, 0)
    nq = wq.shape[1]
    wspec = lambda r, c: pl.BlockSpec((r, c), const, pipeline_mode=pl.Buffered(1))
    return pl.pallas_call(
        _mix_kernel,
        out_shape=[jax.ShapeDtypeStruct((n, D_MODEL), F32),
                   jax.ShapeDtypeStruct((n, D_MODEL), F32),
                   jax.ShapeDtypeStruct((n, nq), BF16)],
        grid=(n // tm,),
        in_specs=[pl.BlockSpec((tm, WIDTH), row), pl.BlockSpec((tm, WIDTH), row),
                  pl.BlockSpec((tm, D_MODEL), row), pl.BlockSpec((tm, D_MODEL), row),
                  pl.BlockSpec((tm, D_MODEL), row),
                  pl.BlockSpec((1, 6, D_MODEL), lambda i: (i // tiles_per_seq, 0, 0)),
                  pl.BlockSpec((1, D_MODEL), const), pl.BlockSpec((1, D_MODEL), const),
                  wspec(WIDTH, D_MODEL), wspec(WIDTH, D_MODEL), wspec(D_MODEL, D_MODEL),
                  wspec(D_MODEL, nq)],
        out_specs=[pl.BlockSpec((tm, D_MODEL), row), pl.BlockSpec((tm, D_MODEL), row),
                   pl.BlockSpec((tm, nq), row)],
        compiler_params=_params(("parallel",)),
        name="mix",
    )(oa, ob, ga, gb, x2, mod3, g_post, g_pre, wa, wb, wo, wq)


def _top16_many(arrays, payloads=None):
    state = list(arrays)
    pos = [lax.broadcasted_iota(I32, a.shape, 0) for a in arrays]
    vals = [[] for _ in arrays]
    picks = [[] for _ in arrays]
    for _ in range(PEER_TOPK):
        for c, s in enumerate(state):
            m = jnp.max(s, axis=0, keepdims=True)
            p = jnp.min(jnp.where(s == m, pos[c], s.shape[0]), axis=0, keepdims=True)
            hit = pos[c] == p
            vals[c].append(m)
            picks[c].append(p if payloads is None
                            else jnp.sum(jnp.where(hit, payloads[c], 0), axis=0, keepdims=True))
            state[c] = jnp.where(hit, -jnp.inf, s)
    return [(jnp.concatenate(v, axis=0), jnp.concatenate(p, axis=0)) for v, p in zip(vals, picks)]


def _peersel_kernel(pq_ref, sk_ref, idx_ref, g_ref, *, heads):
    tt = pq_ref.shape[0]
    scores = []
    for h in range(heads):
        for half in range(2):
            lo = (2 * h + half) * PEER_DHALF
            scores.append(_dot_nt(sk_ref[h, half], pq_ref[:, lo:lo + PEER_DHALF]))
    tops = _top16_many(scores)
    half_k = PEER_TOPK // 2
    row8 = lax.broadcasted_iota(I32, (half_k, tt), 0)
    cands, cidxs = [], []
    for h in range(heads):
        (v1, i1), (v2, i2) = tops[2 * h], tops[2 * h + 1]
        cand = [v1[0:1] + v2]
        cidx = [i1[0:1] * PEER_NKEYS + i2]
        for i in range(1, half_k):
            cand.append(jnp.where(row8 < PEER_TOPK // (i + 1), v1[i:i + 1] + v2[0:half_k], -jnp.inf))
            cidx.append(i1[i:i + 1] * PEER_NKEYS + i2[0:half_k])
        cand.append(v1[half_k:] + v2[0:1])
        cidx.append(i1[half_k:] * PEER_NKEYS + i2[0:1])
        cands.append(jnp.concatenate(cand, axis=0))
        cidxs.append(jnp.concatenate(cidx, axis=0))
    for h, (top, eidx) in enumerate(_top16_many(cands, cidxs)):
        ex = jnp.exp(top - top[0:1])
        g_ref[h] = ex / jnp.sum(ex, axis=0, keepdims=True)
        idx_ref[h] = eidx


def _peer_select(pq, subkeys_b, tt):
    n = pq.shape[0]
    heads = PEER_SELECT_HEADS
    return pl.pallas_call(
        functools.partial(_peersel_kernel, heads=heads),
        out_shape=[jax.ShapeDtypeStruct((PEER_HEADS, PEER_TOPK, n), I32),
                   jax.ShapeDtypeStruct((PEER_HEADS, PEER_TOPK, n), F32)],
        grid=(n // tt, PEER_HEADS // heads),
        in_specs=[pl.BlockSpec((tt, heads * 2 * PEER_DHALF), lambda i, h: (i, h)),
                  pl.BlockSpec((heads, 2, PEER_NKEYS, PEER_DHALF), lambda i, h: (h, 0, 0, 0))],
        out_specs=[pl.BlockSpec((heads, PEER_TOPK, tt), lambda i, h: (h, 0, i)),
                   pl.BlockSpec((heads, PEER_TOPK, tt), lambda i, h: (h, 0, i))],
        compiler_params=_params(("parallel", "arbitrary")),
        name="peer_select",
    )(pq, subkeys_b)


def _erf_gelu(a):
    return 0.5 * a * (1.0 + lax.erf(a * (2.0 ** -0.5)))


def _peerffn_kernel(idx_ref, idx_next_ref, gt_ref, h2_ref, x1_ref, mod_ref, gpost_ref, uv_hbm, o_ref,
                    *scratch, tg, seqs_per_tile):
    bufs, (f_ref, sem) = scratch[:PEER_SLOTS], scratch[PEER_SLOTS:]
    step, last_step = pl.program_id(0), pl.num_programs(0) - 1

    def start_token(t, slot, src=None):
        src = idx_ref if src is None else src
        for k in range(PEER_SEL):
            pltpu.make_async_copy(uv_hbm.at[src[t, k]], bufs[slot].at[k],
                                  sem.at[slot]).start(priority=k % 2)

    def wait_token(slot):
        pltpu.make_async_copy(bufs[slot], bufs[slot], sem.at[slot]).wait()

    lane_tok = lax.broadcasted_iota(I32, (PEER_SEL, tg), 1)
    sub8 = lax.broadcasted_iota(I32, (8, D_MODEL), 0)

    row = lax.broadcasted_iota(I32, (8, LANES), 0)
    low_half, mid_pairs, odd_rows = row < 4, (row % 4) >= 2, (row % 2) == 1

    def fold_group(tiles):
        t_ = [tiles[i] for i in (3, 7, 1, 5, 2, 6, 0, 4)]
        r = [x + pltpu.roll(x, 4, 0) for x in t_]
        y = [jnp.where(low_half, r[2 * m], r[2 * m + 1]) for m in range(4)]
        y = [x + pltpu.roll(x, 2, 0) for x in y]
        z = [jnp.where(mid_pairs, y[2 * n], pltpu.roll(y[2 * n + 1], 6, 0)) for n in range(2)]
        z = [x + pltpu.roll(x, 1, 0) for x in z]
        return jnp.where(odd_rows, z[0], pltpu.roll(z[1], 7, 0))

    def compute(t, slot):
        buf = bufs[slot]
        t8 = pl.multiple_of((t // 8) * 8, 8)
        h8 = h2_ref[pl.ds(t8, 8), :]
        hrow = jnp.sum(jnp.where(sub8 == t - t8, h8, 0.0), axis=0, keepdims=True)
        htile = jnp.concatenate([hrow[:, c * LANES:(c + 1) * LANES] for c in range(N_CHUNK)], axis=0)
        part = jnp.concatenate(
            [fold_group([buf[8 * grp + i, 0:N_CHUNK, :] * htile for i in range(8)]) for grp in range(PEER_SEL // 8)],
            axis=0)
        a = jnp.sum(part, axis=1, keepdims=True)
        g = jnp.sum(jnp.where(lane_tok == t, gt_ref[...], 0.0), axis=1, keepdims=True)
        w = jnp.broadcast_to(g * _erf_gelu(a), (PEER_SEL, LANES))
        accs = [None] * 4
        for k in range(PEER_SEL):
            term = buf[k, N_CHUNK:, :] * jnp.broadcast_to(w[k:k + 1, :], (N_CHUNK, LANES))
            accs[k % 4] = term if accs[k % 4] is None else accs[k % 4] + term
        ftile = (accs[0] + accs[1]) + (accs[2] + accs[3])
        f_ref[pl.ds(t, 1), :] = jnp.concatenate([ftile[c:c + 1, :] for c in range(N_CHUNK)], axis=1)

    @pl.when(step == 0)
    def _():
        for t in range(PEER_AHEAD):
            start_token(t, t)

    def group(i, carry):
        t0 = PEER_SLOTS * i
        for j in range(PEER_SLOTS):
            wait_token(j)
            start_token(t0 + j + PEER_AHEAD, (j + PEER_AHEAD) % PEER_SLOTS)
            compute(t0 + j, j)
        return carry

    n_groups = tg // PEER_SLOTS
    lax.fori_loop(0, n_groups - 1, group, 0)
    t0 = tg - PEER_SLOTS
    for j in range(PEER_SLOTS):
        wait_token(j)
        ahead = j + PEER_AHEAD
        if ahead < PEER_SLOTS:
            start_token(t0 + ahead, ahead)
        else:
            start_token(ahead - PEER_SLOTS, ahead - PEER_SLOTS, idx_next_ref)
        compute(t0 + j, j)

    @pl.when(step == last_step)
    def _():
        for t in range(PEER_AHEAD):
            wait_token(t)

    rows = tg // seqs_per_tile
    for s in range(seqs_per_tile):
        sl = slice(s * rows, (s + 1) * rows)
        o_ref[sl, :] = x1_ref[sl, :] + mod_ref[s][5:6, :] * _rms(f_ref[sl, :], gpost_ref[...])


def _peer_ffn(idx, gt, h2, x1, mod3, g_post, uv, tg, seq_len):
    n = h2.shape[0]
    row = lambda i: (i, 0)
    seqs_per_tile = max(1, tg // seq_len)
    tiles_per_seq = max(1, seq_len // tg)
    return pl.pallas_call(
        functools.partial(_peerffn_kernel, tg=tg, seqs_per_tile=seqs_per_tile),
        out_shape=jax.ShapeDtypeStruct((n, D_MODEL), F32),
        grid=(n // tg,),
        in_specs=[pl.BlockSpec((tg, PEER_SEL), row, memory_space=pltpu.SMEM),
                  pl.BlockSpec((tg, PEER_SEL), lambda i: (jnp.minimum(i + 1, n // tg - 1), 0), memory_space=pltpu.SMEM),
                  pl.BlockSpec((PEER_SEL, tg), lambda i: (0, i)),
                  pl.BlockSpec((tg, D_MODEL), row),
                  pl.BlockSpec((tg, D_MODEL), row),
                  pl.BlockSpec((seqs_per_tile, 6, D_MODEL), lambda i: (i // tiles_per_seq, 0, 0)),
                  pl.BlockSpec((1, D_MODEL), lambda i: (0, 0)),
                  pl.BlockSpec(memory_space=pl.ANY)],
        out_specs=pl.BlockSpec((tg, D_MODEL), row),
        scratch_shapes=[pltpu.VMEM((PEER_SEL, 2 * N_CHUNK, LANES), F32)] * PEER_SLOTS
                       + [pltpu.VMEM((tg, D_MODEL), F32), pltpu.SemaphoreType.DMA((PEER_SLOTS,))],
        compiler_params=_params(("arbitrary",)),
        name="peer_ffn",
    )(idx, idx, gt, h2, x1, mod3, g_post, uv)


def _pad_keys(x, s_pad):
    return jnp.pad(x, ((0, 0), (0, s_pad - x.shape[1]), (0, 0)))


def _layer(x, mod3, past, weights, tiles):
    (g_pre_mix, g_post_mix, g_pre_ffn, g_post_ffn, w_in_b, w_gate_b, wa_b, wb_b, wo_b, wq_b,
     subkeys_b, peer_uv) = weights
    tm, sb_tq, sb_tk, dsa_tq, dsa_tk, tt, tg = tiles
    bsz, t, _ = x.shape
    n = bsz * t
    x2 = x.reshape(n, D_MODEL)
    (aq, ak, av, bq, bk, bv, iq, ki2, akf, avf, bkf, bvf, ikf, iw, ga, gb) = _inproj(
        x2, mod3, g_pre_mix, w_in_b, w_gate_b, tm, t // tm)
    seq = lambda a: a.reshape(bsz, t, a.shape[-1])
    if past is None:
        q_off = 0
        ka, va, kb, vb, ki = seq(ak), seq(av), seq(bk), seq(bv), seq(ki2)
        s_len = t
    else:
        p_sbk, p_sbv, p_dk, p_dv, p_ki = past
        q_off = p_sbk.shape[1]
        s_len = q_off + t
        flat = lambda c: c.reshape(bsz, q_off, WIDTH).astype(BF16)
        ka = jnp.concatenate([flat(p_sbk), seq(ak)], axis=1)
        va = jnp.concatenate([flat(p_sbv), seq(av)], axis=1)
        kb = jnp.concatenate([flat(p_dk), seq(bk)], axis=1)
        vb = jnp.concatenate([flat(p_dv), seq(bv)], axis=1)
        pk = p_ki.astype(BF16)
        ki = jnp.concatenate([jnp.concatenate([pk, pk], axis=-1), seq(ki2)], axis=1)
    s_pad = -(-s_len // max(sb_tk, dsa_tk)) * max(sb_tk, dsa_tk)
    ka, va, kb, vb, ki = (_pad_keys(a, s_pad) for a in (ka, va, kb, vb, ki))
    n_sel = min(TOPK_MAX, s_len // 4)

    oa = _sb_attention(seq(aq), ka, va, sb_tq, sb_tk, q_off)
    ob = _dsa_attention(seq(bq), seq(iq), seq(iw), kb, vb, ki, dsa_tq, dsa_tk, q_off, n_sel)

    x1, h2, pq = _mix(oa.reshape(n, WIDTH), ob.reshape(n, WIDTH), ga, gb, x2, mod3, g_post_mix, g_pre_ffn,
                      wa_b, wb_b, wo_b, wq_b, tm, t // tm)
    eidx, gate = _peer_select(pq, subkeys_b, tt)
    eidx = eidx.reshape(PEER_SEL, n).T
    y = _peer_ffn(eidx, gate.reshape(PEER_SEL, n), h2, x1, mod3, g_post_ffn, peer_uv, tg, t)
    new = tuple(a.reshape(1, bsz, t, N_HEADS, HEAD_DIM) for a in (akf, avf, bkf, bvf)) + (
        ikf.reshape(1, bsz, t, HEAD_DIM),)
    return y.reshape(bsz, t, D_MODEL), new


def _tiles(t):
    if t % 256 == 0:
        return (256, 256, 256, 256, 512, 128, 128)
    return (t, t, 128, t, 128, 128, 128)


def kernel(x_prompt, x_sample, c_prompt, c_sample, cache_sb_k, cache_sb_v, cache_dsa_k, cache_dsa_v, cache_dsa_kidx, w_ada, b_ada, g_pre_mix, g_post_mix, g_pre_ffn, g_post_ffn, w_in, w_gate, w_branch_a, w_branch_b, w_out, w_peer_q, peer_subkeys, peer_u, peer_v):
    assert w_ada.shape[0] == 1, "one layer"
    n_p, n_s = c_prompt.shape[0], c_sample.shape[0]
    c_all = jnp.concatenate([c_prompt, c_sample], axis=0)
    rows = -(-c_all.shape[0] // 8) * 8
    c_all = jnp.pad(c_all, ((0, rows - c_all.shape[0]), (0, 0)))
    mod = _modulation(c_all, w_ada[0], b_ada[0]).reshape(rows, 6, D_MODEL)
    w_in_b = jnp.pad(w_in[0], ((0, 0), (0, W_IN_PAD - W_IN_COLS))).astype(BF16)
    weights = (g_pre_mix, g_post_mix, g_pre_ffn, g_post_ffn, w_in_b, w_gate[0].astype(BF16),
               w_branch_a[0].astype(BF16), w_branch_b[0].astype(BF16), w_out[0].astype(BF16),
               w_peer_q[0].astype(BF16), peer_subkeys[0].astype(BF16),
               jnp.concatenate([peer_u[0].reshape(-1, N_CHUNK, LANES),
                                peer_v[0].reshape(-1, N_CHUNK, LANES)], axis=1))
    yp, new_p = _layer(x_prompt, mod[:n_p], None, weights, _tiles(x_prompt.shape[1]))
    past = (cache_sb_k[0], cache_sb_v[0], cache_dsa_k[0], cache_dsa_v[0], cache_dsa_kidx[0])
    ys, new_s = _layer(x_sample, mod[n_p:n_p + n_s], past, weights, _tiles(x_sample.shape[1]))
    return (yp, ys) + new_p + new_s
```

```python
import functools
import math

import jax
import jax.numpy as jnp
import numpy as np
from jax import lax
from jax.experimental import pallas as pl
from jax.experimental.pallas import tpu as pltpu

F32 = jnp.float32
BF16 = jnp.bfloat16
I32 = jnp.int32
I16 = jnp.int16

D_MODEL = 1024
CHUNK = 64
EPS = 1e-6
N_HEADS = 8
HEAD_DIM = 64
WIDTH = N_HEADS * HEAD_DIM
N_PAIRS = N_HEADS // 2
SB_PAIRS_PER_STEP = 4
LANES = 128
N_CHUNK = D_MODEL // LANES
TOPK_MAX = 256
PEER_HEADS = 8
PEER_NKEYS = 128
PEER_DHALF = 128
PEER_TOPK = 16
PEER_SEL = PEER_HEADS * PEER_TOPK
PEER_SELECT_HEADS = 4
PEER_SLOTS = 8
PEER_AHEAD = 6
W_IN_COLS = 7 * WIDTH + HEAD_DIM + N_HEADS
W_IN_PAD = 7 * WIDTH + LANES
INT_MIN = -(2 ** 31)
HALF_BIAS = 2 ** 15
NEG_BIG = -1e30
DIST_MASKED = 1e30
LOG2E = 1.4426950408889634
SOFTPLUS_LINEAR = 64.0
VMEM_LIMIT = 56 * 1024 * 1024


def _params(sem):
    return pltpu.CompilerParams(dimension_semantics=sem, vmem_limit_bytes=VMEM_LIMIT)


def _dot(a, b):
    return jnp.dot(a, b, preferred_element_type=F32)


def _dot_nt(a, b):
    return lax.dot_general(a, b, (((1,), (1,)), ((), ())), preferred_element_type=F32)


def _rms(x, gain):
    return x * lax.rsqrt(jnp.mean(x * x, axis=-1, keepdims=True) + EPS) * gain


def _mod_kernel(c_ref, w_ref, b_ref, o_ref):
    c = c_ref[...]
    s = c * (1.0 / (1.0 + jnp.exp(-c)))
    o_ref[...] = jnp.dot(s, w_ref[...], preferred_element_type=F32,
                         precision=lax.Precision.HIGHEST) + b_ref[...]


def _modulation(c, w_ada, b_ada):
    n = c.shape[0]
    cols = w_ada.shape[1]
    tn = 1024
    return pl.pallas_call(
        _mod_kernel,
        out_shape=jax.ShapeDtypeStruct((n, cols), F32),
        grid=(cols // tn,),
        in_specs=[pl.BlockSpec((n, D_MODEL), lambda j: (0, 0)),
                  pl.BlockSpec((D_MODEL, tn), lambda j: (0, j)),
                  pl.BlockSpec((1, tn), lambda j: (0, j))],
        out_specs=pl.BlockSpec((n, tn), lambda j: (0, j)),
        compiler_params=_params(("arbitrary",)),
        name="mod",
    )(c, w_ada, b_ada.reshape(1, cols))


def _inproj_kernel(x_ref, mod_ref, g_ref, win_ref, wg_ref,
                   aq_ref, ak_ref, av_ref, bq_ref, bk_ref, bv_ref, iq_ref, ki2_ref,
                   akf_ref, avf_ref, bkf_ref, bvf_ref, ikf_ref, iw_ref, ga_ref, gb_ref):
    x = x_ref[...]
    mod = mod_ref[0]
    h = _rms(x, g_ref[...]) * (1.0 + mod[1:2, :]) + mod[0:1, :]
    hb = h.astype(BF16)
    qscale = HEAD_DIM ** -0.5

    def seg(i):
        return _dot(hb, win_ref[:, i * WIDTH:(i + 1) * WIDTH])

    aq_ref[...] = (seg(0) * (qscale * LOG2E)).astype(BF16)
    z = seg(1)
    akf_ref[...] = z
    ak_ref[...] = z.astype(BF16)
    z = seg(2)
    avf_ref[...] = z
    av_ref[...] = z.astype(BF16)
    bq_ref[...] = (seg(3) * (qscale * LOG2E)).astype(BF16)
    z = seg(4)
    bkf_ref[...] = z
    bk_ref[...] = z.astype(BF16)
    z = seg(5)
    bvf_ref[...] = z
    bv_ref[...] = z.astype(BF16)
    iq_ref[...] = (seg(6) * qscale).astype(BF16)
    tail = _dot(hb, win_ref[:, 7 * WIDTH:7 * WIDTH + LANES])
    ik = tail[:, :HEAD_DIM]
    ikf_ref[...] = ik
    ki2_ref[...] = jnp.concatenate([ik, ik], axis=-1).astype(BF16)
    iw_ref[...] = tail[:, HEAD_DIM:HEAD_DIM + N_HEADS] * (N_HEADS ** -0.5)
    zg = _dot(hb, wg_ref[...])
    gate = 1.0 / (1.0 + jnp.exp(-zg))
    ga_ref[...] = gate[:, :D_MODEL]
    gb_ref[...] = gate[:, D_MODEL:]


def _inproj(x2, mod3, g_pre, w_in_b, w_gate_b, tm, tiles_per_seq):
    n = x2.shape[0]
    row = lambda i: (i, 0)
    const = lambda i: (0, 0)
    bspec = lambda w: pl.BlockSpec((tm, w), row)
    outs = ([jax.ShapeDtypeStruct((n, WIDTH), BF16)] * 7
            + [jax.ShapeDtypeStruct((n, LANES), BF16)]
            + [jax.ShapeDtypeStruct((n, WIDTH), F32)] * 4
            + [jax.ShapeDtypeStruct((n, HEAD_DIM), F32),
               jax.ShapeDtypeStruct((n, N_HEADS), F32),
               jax.ShapeDtypeStruct((n, D_MODEL), F32),
               jax.ShapeDtypeStruct((n, D_MODEL), F32)])
    out_specs = ([bspec(WIDTH)] * 7 + [bspec(LANES)] + [bspec(WIDTH)] * 4
                 + [bspec(HEAD_DIM), bspec(N_HEADS), bspec(D_MODEL), bspec(D_MODEL)])
    return pl.pallas_call(
        _inproj_kernel,
        out_shape=outs,
        grid=(n // tm,),
        in_specs=[pl.BlockSpec((tm, D_MODEL), row),
                  pl.BlockSpec((1, 6, D_MODEL), lambda i: (i // tiles_per_seq, 0, 0)),
                  pl.BlockSpec((1, D_MODEL), const),
                  pl.BlockSpec((D_MODEL, W_IN_PAD), const, pipeline_mode=pl.Buffered(1)),
                  pl.BlockSpec((D_MODEL, 2 * D_MODEL), const, pipeline_mode=pl.Buffered(1))],
        out_specs=out_specs,
        compiler_params=_params(("parallel",)),
        name="inproj",
    )(x2, mod3, g_pre, w_in_b, w_gate_b)


def _sb_kernel(tab_ref, q_ref, k_ref, v_ref, tri_ref, o_ref, *scratch, tq, tk, q_off, pairs):
    s = pl.program_id(2)
    qi = tab_ref[0, s]
    kj = tab_ref[1, s]
    first = tab_ref[2, s]
    last = tab_ref[3, s]
    masked = tab_ref[4, s]
    n_heads = 2 * pairs
    acc_refs, run_refs = scratch[:n_heads], scratch[n_heads:]

    @pl.when(first == 1)
    def _():
        for r in scratch:
            r[...] = jnp.zeros_like(r)

    lo_lanes = lax.broadcasted_iota(I32, (tq, LANES), 1) < HEAD_DIM

    def step(use_mask):
        tri = tri_ref[...]
        if use_mask:
            qpos = q_off + qi * tq + lax.broadcasted_iota(I32, (tq, tk), 0)
            kpos = kj * tk + lax.broadcasted_iota(I32, (tq, tk), 1)
            causal = kpos < qpos
        z = []
        for p in range(pairs):
            q = q_ref[0, :, p * LANES:(p + 1) * LANES]
            k = k_ref[0, :, p * LANES:(p + 1) * LANES]
            zero = jnp.zeros_like(q)
            z += [_dot_nt(jnp.where(lo_lanes, q, zero), k), _dot_nt(jnp.where(lo_lanes, zero, q), k)]
        suf = []
        for h in range(n_heads):
            sp = jnp.maximum(z[h], jnp.log2(1.0 + jnp.exp2(jnp.minimum(z[h], SOFTPLUS_LINEAR))))
            if use_mask:
                sp = jnp.where(causal, sp, 0.0)
            hi = sp.astype(BF16)
            lo = (sp - hi.astype(F32)).astype(BF16)
            suf.append(_dot(hi, tri) + _dot(lo, tri) + run_refs[h][:, 0:1])
        for h in range(n_heads):
            w = jnp.exp2(z[h] - suf[h])
            if use_mask:
                w = jnp.where(causal, w, 0.0)
            p = h // 2
            acc_refs[h][...] += _dot(w.astype(BF16), v_ref[0, :, p * LANES:(p + 1) * LANES])
            run_refs[h][...] = jnp.broadcast_to(suf[h][:, 0:1], (tq, LANES))

    @pl.when(masked == 1)
    def _():
        step(True)

    @pl.when(masked == 0)
    def _():
        step(False)

    @pl.when(last == 1)
    def _():
        for p in range(pairs):
            o_ref[0, :, p * LANES:(p + 1) * LANES] = jnp.where(
                lo_lanes, acc_refs[2 * p][...], acc_refs[2 * p + 1][...]).astype(o_ref.dtype)


def _sb_table(nq, nk, tq, tk, q_off):
    rows = []
    for qi in range(nq):
        q_lo = q_off + qi * tq
        q_hi = q_lo + tq - 1
        kjs = [kj for kj in range(nk) if kj * tk < q_hi]
        if not kjs:
            kjs = [0]
        kjs = kjs[::-1]
        for n, kj in enumerate(kjs):
            masked = int(kj * tk + tk - 1 >= q_lo)
            rows.append((qi, kj, int(n == 0), int(n == len(kjs) - 1), masked))
    return np.asarray(rows, dtype=np.int32).T.copy()


def _sb_attention(q, k, v, tq, tk, q_off):
    b, t, _ = q.shape
    s_len = k.shape[1]
    tab = _sb_table(t // tq, s_len // tk, tq, tk, q_off)
    tri = jnp.asarray(np.tril(np.ones((tk, tk), np.float32)), dtype=BF16)
    pairs = SB_PAIRS_PER_STEP
    width = pairs * LANES
    grid_spec = pltpu.PrefetchScalarGridSpec(
        num_scalar_prefetch=1,
        grid=(b, N_PAIRS // pairs, tab.shape[1]),
        in_specs=[pl.BlockSpec((1, tq, width), lambda bi, hp, s, tab: (bi, tab[0, s], hp)),
                  pl.BlockSpec((1, tk, width), lambda bi, hp, s, tab: (bi, tab[1, s], hp)),
                  pl.BlockSpec((1, tk, width), lambda bi, hp, s, tab: (bi, tab[1, s], hp)),
                  pl.BlockSpec((tk, tk), lambda bi, hp, s, tab: (0, 0))],
        out_specs=pl.BlockSpec((1, tq, width), lambda bi, hp, s, tab: (bi, tab[0, s], hp)),
        scratch_shapes=[pltpu.VMEM((tq, LANES), F32)] * (4 * pairs),
    )
    return pl.pallas_call(
        functools.partial(_sb_kernel, tq=tq, tk=tk, q_off=q_off, pairs=pairs),
        out_shape=jax.ShapeDtypeStruct((b, t, WIDTH), BF16),
        grid_spec=grid_spec,
        compiler_params=_params(("parallel", "parallel", "arbitrary")),
        name="sb_attn",
    )(jnp.asarray(tab), q, k, v, tri)


def _sortable(x):
    bits = pltpu.bitcast(x + 0.0, I32)
    return jnp.where(bits < 0, bits ^ 0x7FFFFFFF, bits)


def _alibi_slope2(h):
    return LOG2E * 2.0 ** (-8.0 * (h + 1) / N_HEADS)


def _dsa_kernel(q_ref, iq_ref, iwt_ref, k_ref, vt_ref, ki_ref, o_ref, keys_ref, dm_ref, hi_ref, lo_ref, qt_ref, iqt_ref,
                *stats,
                 tq, tk, q_off, n_kb_max, n_sel, idx_bits):
    m_refs, l_refs, acc_refs = stats[:N_HEADS], stats[N_HEADS:2 * N_HEADS], stats[2 * N_HEADS:]
    qi = pl.program_id(1)
    q_lo = q_off + qi * tq
    adm_end = ((q_lo + tq - 1) // CHUNK + 1) * CHUNK
    n_kb = jnp.minimum((adm_end + tk - 1) // tk, n_kb_max)

    lo_half = lax.broadcasted_iota(I32, (tq, LANES), 1) < HEAD_DIM
    qpos = q_lo + lax.broadcasted_iota(I32, (tk, tq), 1)
    krow = lax.broadcasted_iota(I32, (tk, tq), 0)

    def heads_of(x_ref):
        out = []
        for hp in range(N_PAIRS):
            pair = x_ref[0, :, hp * LANES:(hp + 1) * LANES]
            zero = jnp.zeros_like(pair)
            out += [jnp.where(lo_half, pair, zero), jnp.where(lo_half, zero, pair)]
        return out

    for src, dst in ((iq_ref, iqt_ref), (q_ref, qt_ref)):
        for h, xe in enumerate(heads_of(src)):
            dst[h] = xe.astype(F32).T.astype(BF16)

    def score_block(j, carry):
        ki = ki_ref[0, pl.ds(pl.multiple_of(j * tk, tk), tk), :]
        score = jnp.zeros((tk, tq), F32)
        for h in range(N_HEADS):
            score = score + iwt_ref[0, h:h + 1, :] * jnp.maximum(_dot(ki, iqt_ref[h]), 0.0)
        adm = ((j * tk + krow) // CHUNK) <= (qpos // CHUNK)
        key = jnp.where(adm, _sortable(score), INT_MIN)
        keys_ref[j] = key
        hi_ref[j] = (key >> 16).astype(I16)
        lo_ref[j] = ((key & 0xFFFF) - HALF_BIAS).astype(I16)
        return carry

    lax.fori_loop(0, n_kb, score_block, 0)

    def count_cols(pred_fn):
        def body(j, acc):
            hit = pred_fn(keys_ref[j], j).astype(I32)
            return acc + jnp.sum(hit.reshape(tk // 8, 8, tq), axis=0)
        acc = lax.fori_loop(0, n_kb, body, jnp.zeros((8, tq), I32))
        return jnp.sum(acc, axis=0, keepdims=True)

    def count_half(ref, pred_fn):
        def body(j, acc):
            hit = jnp.where(pred_fn(ref[j]), jnp.int16(1), jnp.int16(0))
            for r in range(tk // 16):
                acc = acc + hit[r * 16:(r + 1) * 16, :]
            return acc
        acc = lax.fori_loop(0, n_kb, body, jnp.zeros((16, tq), I16))
        return jnp.sum(acc.astype(I32), axis=0, keepdims=True)

    def search_half(ref, target):
        def bit_step(i, t):
            cand = t + jnp.left_shift(jnp.int32(1), 15 - i)
            cand16 = cand.astype(I16)
            cnt = count_half(ref, lambda kb: kb >= cand16)
            return jnp.where(cnt >= target, cand, t)
        return lax.fori_loop(0, 16, bit_step, jnp.full((1, tq), -HALF_BIAS, I32))

    t_hi = search_half(hi_ref, n_sel)
    t_hi16 = t_hi.astype(I16)
    n_above = count_half(hi_ref, lambda kb: kb > t_hi16)

    def keep_group(j, carry):
        lo_ref[j] = jnp.where(hi_ref[j] == t_hi16, lo_ref[j], jnp.int16(-HALF_BIAS))
        return carry

    lax.fori_loop(0, n_kb, keep_group, 0)
    t_lo = search_half(lo_ref, n_sel - n_above)
    thr = t_hi * (2 * HALF_BIAS) + (t_lo + HALF_BIAS)
    n_gt = count_cols(lambda kb, j: kb > thr)
    n_ge = count_cols(lambda kb, j: kb >= thr)
    need = n_sel - n_gt
    real = thr > INT_MIN
    tie = jnp.max(jnp.where(real & (n_ge > n_sel), 1, 0)) > 0

    def tie_cut():
        def idx_step(i, cut):
            cand = cut + jnp.left_shift(jnp.int32(1), idx_bits - 1 - i)
            cnt = count_cols(lambda kb, j: (kb == thr) & ((j * tk + krow) < cand))
            return jnp.where(cnt < need, cand, cut)
        return lax.fori_loop(0, idx_bits, idx_step, jnp.zeros((1, tq), I32))

    cut = lax.cond(tie, tie_cut, lambda: jnp.full((1, tq), n_kb_max * tk, I32))

    def mask_block(j, carry):
        kb = keys_ref[j]
        kpos = j * tk + krow
        sel = (kb > thr) | ((kb == thr) & real & (kpos <= cut))
        dm_ref[j] = jnp.where(sel, jnp.abs(qpos - kpos).astype(F32), DIST_MASKED)
        return carry

    lax.fori_loop(0, n_kb, mask_block, 0)

    for h in range(N_HEADS):
        m_refs[h][...] = jnp.full_like(m_refs[h], NEG_BIG)
        l_refs[h][...] = jnp.zeros_like(l_refs[h])
        acc_refs[h][...] = jnp.zeros_like(acc_refs[h])

    def attn_block(j, carry):
        start = pl.multiple_of(j * tk, tk)
        dm = dm_ref[j]

        def qk(h):
            return _dot(k_ref[0, pl.ds(start, tk), (h // 2) * LANES:(h // 2 + 1) * LANES], qt_ref[h])

        ahead = 4
        pending = [qk(h) for h in range(ahead)]
        for hp in range(N_PAIRS):
            for e in range(2):
                h = 2 * hp + e
                s_this = pending.pop(0)
                if h + ahead < N_HEADS:
                    pending.append(qk(h + ahead))
                t2 = s_this - _alibi_slope2(h) * dm
                m_old = m_refs[h][...]
                m_new = jnp.maximum(m_old, jnp.max(t2, axis=0, keepdims=True))
                alpha = jnp.exp2(m_old - m_new)
                p = jnp.exp2(t2 - m_new)
                l_refs[h][...] = alpha * l_refs[h][...] + jnp.sum(p, axis=0, keepdims=True)
                vt = vt_ref[0, j, hp * LANES + e * HEAD_DIM:hp * LANES + (e + 1) * HEAD_DIM, :]
                acc_refs[h][...] = alpha * acc_refs[h][...] + _dot(vt, p.astype(BF16))
                m_refs[h][...] = m_new
        return carry

    lax.fori_loop(0, n_kb, attn_block, 0)

    for hp in range(N_PAIRS):
        ot = jnp.concatenate([acc_refs[2 * hp][...] / l_refs[2 * hp][...],
                              acc_refs[2 * hp + 1][...] / l_refs[2 * hp + 1][...]], axis=0)
        o_ref[0, :, hp * LANES:(hp + 1) * LANES] = ot.T.astype(o_ref.dtype)


def _dsa_attention(q, iq, iw, k, v, ki2, tq, tk, q_off, n_sel):
    b, t, _ = q.shape
    s_len = k.shape[1]
    n_kb_max = s_len // tk
    iwt = jnp.swapaxes(iw, 1, 2)
    vt = jnp.swapaxes(v.reshape(b, n_kb_max, tk, WIDTH), 2, 3)
    qspec = lambda w: pl.BlockSpec((1, tq, w), lambda bi, i: (bi, i, 0))
    kspec = lambda w: pl.BlockSpec((1, s_len, w), lambda bi, i: (bi, 0, 0), pipeline_mode=pl.Buffered(1))
    return pl.pallas_call(
        functools.partial(_dsa_kernel, tq=tq, tk=tk, q_off=q_off, n_kb_max=n_kb_max, n_sel=n_sel,
                          idx_bits=max(1, int(math.ceil(math.log2(s_len + 1))))),
        out_shape=jax.ShapeDtypeStruct((b, t, WIDTH), BF16),
        grid=(b, t // tq),
        in_specs=[qspec(WIDTH), qspec(WIDTH),
                  pl.BlockSpec((1, N_HEADS, tq), lambda bi, i: (bi, 0, i)),
                  kspec(WIDTH),
                  pl.BlockSpec((1, n_kb_max, WIDTH, tk), lambda bi, i: (bi, 0, 0, 0), pipeline_mode=pl.Buffered(1)),
                  kspec(LANES)],
        out_specs=qspec(WIDTH),
        scratch_shapes=[pltpu.VMEM((n_kb_max, tk, tq), I32), pltpu.VMEM((n_kb_max, tk, tq), F32),
                        pltpu.VMEM((n_kb_max, tk, tq), I16), pltpu.VMEM((n_kb_max, tk, tq), I16),
                        pltpu.VMEM((N_HEADS, LANES, tq), BF16), pltpu.VMEM((N_HEADS, LANES, tq), BF16)]
                       + [pltpu.VMEM((1, tq), F32)] * (2 * N_HEADS) + [pltpu.VMEM((HEAD_DIM, tq), F32)] * N_HEADS,
        compiler_params=_params(("parallel", "arbitrary")),
        name="dsa_attn",
    )(q, iq, iwt, k, vt, ki2)


def _mix_kernel(oa_ref, ob_ref, ga_ref, gb_ref, x_ref, mod_ref, gpost_ref, gpre_ref,
                wa_ref, wb_ref, wo_ref, wq_ref, x1_ref, h2_ref, pq_ref):
    mod = mod_ref[0]
    mixed = ga_ref[...] * _dot(oa_ref[...], wa_ref[...]) + gb_ref[...] * _dot(ob_ref[...], wb_ref[...])
    y = _dot(mixed.astype(BF16), wo_ref[...])
    x1 = x_ref[...] + mod[2:3, :] * _rms(y, gpost_ref[...])
    x1_ref[...] = x1
    h2 = _rms(x1, gpre_ref[...]) * (1.0 + mod[4:5, :]) + mod[3:4, :]
    h2_ref[...] = h2
    pq_ref[...] = _dot(h2.astype(BF16), wq_ref[...]).astype(BF16)


def _mix(oa, ob, ga, gb, x2, mod3, g_post, g_pre, wa, wb, wo, wq, tm, tiles_per_seq):
    n = x2.shape[0]
    row = lambda i: (i, 0)
    const = lambda i: (0, 0)
    nq = wq.shape[1]
    wspec = lambda r, c: pl.BlockSpec((r, c), const, pipeline_mode=pl.Buffered(1))
    return pl.pallas_call(
        _mix_kernel,
        out_shape=[jax.ShapeDtypeStruct((n, D_MODEL), F32),
                   jax.ShapeDtypeStruct((n, D_MODEL), F32),
                   jax.ShapeDtypeStruct((n, nq), BF16)],
        grid=(n // tm,),
        in_specs=[pl.BlockSpec((tm, WIDTH), row), pl.BlockSpec((tm, WIDTH), row),
                  pl.BlockSpec((tm, D_MODEL), row), pl.BlockSpec((tm, D_MODEL), row),
                  pl.BlockSpec((tm, D_MODEL), row),
                  pl.BlockSpec((1, 6, D_MODEL), lambda i: (i // tiles_per_seq, 0, 0)),
                  pl.BlockSpec((1, D_MODEL), const), pl.BlockSpec((1, D_MODEL), const),
                  wspec(WIDTH, D_MODEL), wspec(WIDTH, D_MODEL), wspec(D_MODEL, D_MODEL),
                  wspec(D_MODEL, nq)],
        out_specs=[pl.BlockSpec((tm, D_MODEL), row), pl.BlockSpec((tm, D_MODEL), row),
                   pl.BlockSpec((tm, nq), row)],
        compiler_params=_params(("parallel",)),
        name="mix",
    )(oa, ob, ga, gb, x2, mod3, g_post, g_pre, wa, wb, wo, wq)


def _top16_many(arrays, payloads=None):
    state = list(arrays)
    pos = [lax.broadcasted_iota(I32, a.shape, 0) for a in arrays]
    vals = [[] for _ in arrays]
    picks = [[] for _ in arrays]
    for _ in range(PEER_TOPK):
        for c, s in enumerate(state):
            m = jnp.max(s, axis=0, keepdims=True)
            p = jnp.min(jnp.where(s == m, pos[c], s.shape[0]), axis=0, keepdims=True)
            hit = pos[c] == p
            vals[c].append(m)
            picks[c].append(p if payloads is None
                            else jnp.sum(jnp.where(hit, payloads[c], 0), axis=0, keepdims=True))
            state[c] = jnp.where(hit, -jnp.inf, s)
    return [(jnp.concatenate(v, axis=0), jnp.concatenate(p, axis=0)) for v, p in zip(vals, picks)]


def _peersel_kernel(pq_ref, sk_ref, idx_ref, g_ref, *, heads):
    tt = pq_ref.shape[0]
    scores = []
    for h in range(heads):
        for half in range(2):
            lo = (2 * h + half) * PEER_DHALF
            scores.append(_dot_nt(sk_ref[h, half], pq_ref[:, lo:lo + PEER_DHALF]))
    tops = _top16_many(scores)
    half_k = PEER_TOPK // 2
    row8 = lax.broadcasted_iota(I32, (half_k, tt), 0)
    cands, cidxs = [], []
    for h in range(heads):
        (v1, i1), (v2, i2) = tops[2 * h], tops[2 * h + 1]
        cand = [v1[0:1] + v2]
        cidx = [i1[0:1] * PEER_NKEYS + i2]
        for i in range(1, half_k):
            cand.append(jnp.where(row8 < PEER_TOPK // (i + 1), v1[i:i + 1] + v2[0:half_k], -jnp.inf))
            cidx.append(i1[i:i + 1] * PEER_NKEYS + i2[0:half_k])
        cand.append(v1[half_k:] + v2[0:1])
        cidx.append(i1[half_k:] * PEER_NKEYS + i2[0:1])
        cands.append(jnp.concatenate(cand, axis=0))
        cidxs.append(jnp.concatenate(cidx, axis=0))
    for h, (top, eidx) in enumerate(_top16_many(cands, cidxs)):
        ex = jnp.exp(top - top[0:1])
        g_ref[h] = ex / jnp.sum(ex, axis=0, keepdims=True)
        idx_ref[h] = eidx


def _peer_select(pq, subkeys_b, tt):
    n = pq.shape[0]
    heads = PEER_SELECT_HEADS
    return pl.pallas_call(
        functools.partial(_peersel_kernel, heads=heads),
        out_shape=[jax.ShapeDtypeStruct((PEER_HEADS, PEER_TOPK, n), I32),
                   jax.ShapeDtypeStruct((PEER_HEADS, PEER_TOPK, n), F32)],
        grid=(n // tt, PEER_HEADS // heads),
        in_specs=[pl.BlockSpec((tt, heads * 2 * PEER_DHALF), lambda i, h: (i, h)),
                  pl.BlockSpec((heads, 2, PEER_NKEYS, PEER_DHALF), lambda i, h: (h, 0, 0, 0))],
        out_specs=[pl.BlockSpec((heads, PEER_TOPK, tt), lambda i, h: (h, 0, i)),
                   pl.BlockSpec((heads, PEER_TOPK, tt), lambda i, h: (h, 0, i))],
        compiler_params=_params(("parallel", "arbitrary")),
        name="peer_select",
    )(pq, subkeys_b)


def _erf_gelu(a):
    return 0.5 * a * (1.0 + lax.erf(a * (2.0 ** -0.5)))


def _peerffn_kernel(idx_ref, idx_next_ref, gt_ref, h2_ref, x1_ref, mod_ref, gpost_ref, uv_hbm, o_ref,
                    *scratch, tg, seqs_per_tile):
    bufs, (f_ref, sem) = scratch[:PEER_SLOTS], scratch[PEER_SLOTS:]
    step, last_step = pl.program_id(0), pl.num_programs(0) - 1

    def start_token(t, slot, src=None):
        src = idx_ref if src is None else src
        for k in range(PEER_SEL):
            pltpu.make_async_copy(uv_hbm.at[src[t, k]], bufs[slot].at[k],
                                  sem.at[slot]).start(priority=k % 2)

    def wait_token(slot):
        pltpu.make_async_copy(bufs[slot], bufs[slot], sem.at[slot]).wait()

    lane_tok = lax.broadcasted_iota(I32, (PEER_SEL, tg), 1)
    sub8 = lax.broadcasted_iota(I32, (8, D_MODEL), 0)

    row = lax.broadcasted_iota(I32, (8, LANES), 0)
    low_half, mid_pairs, odd_rows = row < 4, (row % 4) >= 2, (row % 2) == 1

    def fold_group(tiles):
        t_ = [tiles[i] for i in (3, 7, 1, 5, 2, 6, 0, 4)]
        r = [x + pltpu.roll(x, 4, 0) for x in t_]
        y = [jnp.where(low_half, r[2 * m], r[2 * m + 1]) for m in range(4)]
        y = [x + pltpu.roll(x, 2, 0) for x in y]
        z = [jnp.where(mid_pairs, y[2 * n], pltpu.roll(y[2 * n + 1], 6, 0)) for n in range(2)]
        z = [x + pltpu.roll(x, 1, 0) for x in z]
        return jnp.where(odd_rows, z[0], pltpu.roll(z[1], 7, 0))

    def compute(t, slot):
        buf = bufs[slot]
        t8 = pl.multiple_of((t // 8) * 8, 8)
        h8 = h2_ref[pl.ds(t8, 8), :]
        hrow = jnp.sum(jnp.where(sub8 == t - t8, h8, 0.0), axis=0, keepdims=True)
        htile = jnp.concatenate([hrow[:, c * LANES:(c + 1) * LANES] for c in range(N_CHUNK)], axis=0)
        part = jnp.concatenate(
            [fold_group([buf[8 * grp + i, 0:N_CHUNK, :] * htile for i in range(8)]) for grp in range(PEER_SEL // 8)],
            axis=0)
        a = jnp.sum(part, axis=1, keepdims=True)
        g = jnp.sum(jnp.where(lane_tok == t, gt_ref[...], 0.0), axis=1, keepdims=True)
        w = jnp.broadcast_to(g * _erf_gelu(a), (PEER_SEL, LANES))
        accs = [None] * 4
        for k in range(PEER_SEL):
            term = buf[k, N_CHUNK:, :] * jnp.broadcast_to(w[k:k + 1, :], (N_CHUNK, LANES))
            accs[k % 4] = term if accs[k % 4] is None else accs[k % 4] + term
        ftile = (accs[0] + accs[1]) + (accs[2] + accs[3])
        f_ref[pl.ds(t, 1), :] = jnp.concatenate([ftile[c:c + 1, :] for c in range(N_CHUNK)], axis=1)

    @pl.when(step == 0)
    def _():
        for t in range(PEER_AHEAD):
            start_token(t, t)

    def group(i, carry):
        t0 = PEER_SLOTS * i
        for j in range(PEER_SLOTS):
            wait_token(j)
            start_token(t0 + j + PEER_AHEAD, (j + PEER_AHEAD) % PEER_SLOTS)
            compute(t0 + j, j)
        return carry

    n_groups = tg // PEER_SLOTS
    lax.fori_loop(0, n_groups - 1, group, 0)
    t0 = tg - PEER_SLOTS
    for j in range(PEER_SLOTS):
        wait_token(j)
        ahead = j + PEER_AHEAD
        if ahead < PEER_SLOTS:
            start_token(t0 + ahead, ahead)
        else:
            start_token(ahead - PEER_SLOTS, ahead - PEER_SLOTS, idx_next_ref)
        compute(t0 + j, j)

    @pl.when(step == last_step)
    def _():
        for t in range(PEER_AHEAD):
            wait_token(t)

    rows = tg // seqs_per_tile
    for s in range(seqs_per_tile):
        sl = slice(s * rows, (s + 1) * rows)
        o_ref[sl, :] = x1_ref[sl, :] + mod_ref[s][5:6, :] * _rms(f_ref[sl, :], gpost_ref[...])


def _peer_ffn(idx, gt, h2, x1, mod3, g_post, uv, tg, seq_len):
    n = h2.shape[0]
    row = lambda i: (i, 0)
    seqs_per_tile = max(1, tg // seq_len)
    tiles_per_seq = max(1, seq_len // tg)
    return pl.pallas_call(
        functools.partial(_peerffn_kernel, tg=tg, seqs_per_tile=seqs_per_tile),
        out_shape=jax.ShapeDtypeStruct((n, D_MODEL), F32),
        grid=(n // tg,),
        in_specs=[pl.BlockSpec((tg, PEER_SEL), row, memory_space=pltpu.SMEM),
                  pl.BlockSpec((tg, PEER_SEL), lambda i: (jnp.minimum(i + 1, n // tg - 1), 0), memory_space=pltpu.SMEM),
                  pl.BlockSpec((PEER_SEL, tg), lambda i: (0, i)),
                  pl.BlockSpec((tg, D_MODEL), row),
                  pl.BlockSpec((tg, D_MODEL), row),
                  pl.BlockSpec((seqs_per_tile, 6, D_MODEL), lambda i: (i // tiles_per_seq, 0, 0)),
                  pl.BlockSpec((1, D_MODEL), lambda i: (0, 0)),
                  pl.BlockSpec(memory_space=pl.ANY)],
        out_specs=pl.BlockSpec((tg, D_MODEL), row),
        scratch_shapes=[pltpu.VMEM((PEER_SEL, 2 * N_CHUNK, LANES), F32)] * PEER_SLOTS
                       + [pltpu.VMEM((tg, D_MODEL), F32), pltpu.SemaphoreType.DMA((PEER_SLOTS,))],
        compiler_params=_params(("arbitrary",)),
        name="peer_ffn",
    )(idx, idx, gt, h2, x1, mod3, g_post, uv)


def _pad_keys(x, s_pad):
    return jnp.pad(x, ((0, 0), (0, s_pad - x.shape[1]), (0, 0)))


def _layer(x, mod3, past, weights, tiles):
    (g_pre_mix, g_post_mix, g_pre_ffn, g_post_ffn, w_in_b, w_gate_b, wa_b, wb_b, wo_b, wq_b,
     subkeys_b, peer_uv) = weights
    tm, sb_tq, sb_tk, dsa_tq, dsa_tk, tt, tg = tiles
    bsz, t, _ = x.shape
    n = bsz * t
    x2 = x.reshape(n, D_MODEL)
    (aq, ak, av, bq, bk, bv, iq, ki2, akf, avf, bkf, bvf, ikf, iw, ga, gb) = _inproj(
        x2, mod3, g_pre_mix, w_in_b, w_gate_b, tm, t // tm)
    seq = lambda a: a.reshape(bsz, t, a.shape[-1])
    if past is None:
        q_off = 0
        ka, va, kb, vb, ki = seq(ak), seq(av), seq(bk), seq(bv), seq(ki2)
        s_len = t
    else:
        p_sbk, p_sbv, p_dk, p_dv, p_ki = past
        q_off = p_sbk.shape[1]
        s_len = q_off + t
        flat = lambda c: c.reshape(bsz, q_off, WIDTH).astype(BF16)
        ka = jnp.concatenate([flat(p_sbk), seq(ak)], axis=1)
        va = jnp.concatenate([flat(p_sbv), seq(av)], axis=1)
        kb = jnp.concatenate([flat(p_dk), seq(bk)], axis=1)
        vb = jnp.concatenate([flat(p_dv), seq(bv)], axis=1)
        pk = p_ki.astype(BF16)
        ki = jnp.concatenate([jnp.concatenate([pk, pk], axis=-1), seq(ki2)], axis=1)
    s_pad = -(-s_len // max(sb_tk, dsa_tk)) * max(sb_tk, dsa_tk)
    ka, va, kb, vb, ki = (_pad_keys(a, s_pad) for a in (ka, va, kb, vb, ki))
    n_sel = min(TOPK_MAX, s_len // 4)

    oa = _sb_attention(seq(aq), ka, va, sb_tq, sb_tk, q_off)
    ob = _dsa_attention(seq(bq), seq(iq), seq(iw), kb, vb, ki, dsa_tq, dsa_tk, q_off, n_sel)

    x1, h2, pq = _mix(oa.reshape(n, WIDTH), ob.reshape(n, WIDTH), ga, gb, x2, mod3, g_post_mix, g_pre_ffn,
                      wa_b, wb_b, wo_b, wq_b, tm, t // tm)
    eidx, gate = _peer_select(pq, subkeys_b, tt)
    eidx = eidx.reshape(PEER_SEL, n).T
    y = _peer_ffn(eidx, gate.reshape(PEER_SEL, n), h2, x1, mod3, g_post_ffn, peer_uv, tg, t)
    new = tuple(a.reshape(1, bsz, t, N_HEADS, HEAD_DIM) for a in (akf, avf, bkf, bvf)) + (
        ikf.reshape(1, bsz, t, HEAD_DIM),)
    return y.reshape(bsz, t, D_MODEL), new


def _tiles(t):
    if t % 256 == 0:
        return (256, 256, 256, 256, 512, 128, 128)
    return (t, t, 128, t, 128, 128, 128)


def kernel(x_prompt, x_sample, c_prompt, c_sample, cache_sb_k, cache_sb_v, cache_dsa_k, cache_dsa_v, cache_dsa_kidx, w_ada, b_ada, g_pre_mix, g_post_mix, g_pre_ffn, g_post_ffn, w_in, w_gate, w_branch_a, w_branch_b, w_out, w_peer_q, peer_subkeys, peer_u, peer_v):
    assert w_ada.shape[0] == 1, "one layer"
    n_p, n_s = c_prompt.shape[0], c_sample.shape[0]
    c_all = jnp.concatenate([c_prompt, c_sample], axis=0)
    rows = -(-c_all.shape[0] // 8) * 8
    c_all = jnp.pad(c_all, ((0, rows - c_all.shape[0]), (0, 0)))
    mod = _modulation(c_all, w_ada[0], b_ada[0]).reshape(rows, 6, D_MODEL)
    w_in_b = jnp.pad(w_in[0], ((0, 0), (0, W_IN_PAD - W_IN_COLS))).astype(BF16)
    weights = (g_pre_mix, g_post_mix, g_pre_ffn, g_post_ffn, w_in_b, w_gate[0].astype(BF16),
               w_branch_a[0].astype(BF16), w_branch_b[0].astype(BF16), w_out[0].astype(BF16),
               w_peer_q[0].astype(BF16), peer_subkeys[0].astype(BF16),
               jnp.concatenate([peer_u[0].reshape(-1, N_CHUNK, LANES),
                                peer_v[0].reshape(-1, N_CHUNK, LANES)], axis=1))
    yp, new_p = _layer(x_prompt, mod[:n_p], None, weights, _tiles(x_prompt.shape[1]))
    past = (cache_sb_k[0], cache_sb_v[0], cache_dsa_k[0], cache_dsa_v[0], cache_dsa_kidx[0])
    ys, new_s = _layer(x_sample, mod[n_p:n_p + n_s], past, weights, _tiles(x_sample.shape[1]))
    return (yp, ys) + new_p + new_s
```

```python
import functools
import math

import jax
import jax.numpy as jnp
import numpy as np
from jax import lax
from jax.experimental import pallas as pl
from jax.experimental.pallas import tpu as pltpu

F32 = jnp.float32
BF16 = jnp.bfloat16
I32 = jnp.int32
I16 = jnp.int16

D_MODEL = 1024
CHUNK = 64
EPS = 1e-6
N_HEADS = 8
HEAD_DIM = 64
WIDTH = N_HEADS * HEAD_DIM
N_PAIRS = N_HEADS // 2
SB_PAIRS_PER_STEP = 4
LANES = 128
N_CHUNK = D_MODEL // LANES
TOPK_MAX = 256
PEER_HEADS = 8
PEER_NKEYS = 128
PEER_DHALF = 128
PEER_TOPK = 16
PEER_SEL = PEER_HEADS * PEER_TOPK
PEER_SELECT_HEADS = 4
PEER_SLOTS = 8
PEER_AHEAD = 6
W_IN_COLS = 7 * WIDTH + HEAD_DIM + N_HEADS
W_IN_PAD = 7 * WIDTH + LANES
INT_MIN = -(2 ** 31)
HALF_BIAS = 2 ** 15
NEG_BIG = -1e30
DIST_MASKED = 1e30
LOG2E = 1.4426950408889634
SOFTPLUS_LINEAR = 64.0
VMEM_LIMIT = 56 * 1024 * 1024


def _params(sem):
    return pltpu.CompilerParams(dimension_semantics=sem, vmem_limit_bytes=VMEM_LIMIT)


def _dot(a, b):
    return jnp.dot(a, b, preferred_element_type=F32)


def _dot_nt(a, b):
    return lax.dot_general(a, b, (((1,), (1,)), ((), ())), preferred_element_type=F32)


def _rms(x, gain):
    return x * lax.rsqrt(jnp.mean(x * x, axis=-1, keepdims=True) + EPS) * gain


def _mod_kernel(c_ref, w_ref, b_ref, o_ref):
    c = c_ref[...]
    s = c * (1.0 / (1.0 + jnp.exp(-c)))
    o_ref[...] = jnp.dot(s, w_ref[...], preferred_element_type=F32,
                         precision=lax.Precision.HIGHEST) + b_ref[...]


def _modulation(c, w_ada, b_ada):
    n = c.shape[0]
    cols = w_ada.shape[1]
    tn = 1024
    return pl.pallas_call(
        _mod_kernel,
        out_shape=jax.ShapeDtypeStruct((n, cols), F32),
        grid=(cols // tn,),
        in_specs=[pl.BlockSpec((n, D_MODEL), lambda j: (0, 0)),
                  pl.BlockSpec((D_MODEL, tn), lambda j: (0, j)),
                  pl.BlockSpec((1, tn), lambda j: (0, j))],
        out_specs=pl.BlockSpec((n, tn), lambda j: (0, j)),
        compiler_params=_params(("arbitrary",)),
        name="mod",
    )(c, w_ada, b_ada.reshape(1, cols))


def _inproj_kernel(x_ref, mod_ref, g_ref, win_ref, wg_ref,
                   aq_ref, ak_ref, av_ref, bq_ref, bk_ref, bv_ref, iq_ref, ki2_ref,
                   akf_ref, avf_ref, bkf_ref, bvf_ref, ikf_ref, iw_ref, ga_ref, gb_ref):
    x = x_ref[...]
    mod = mod_ref[0]
    h = _rms(x, g_ref[...]) * (1.0 + mod[1:2, :]) + mod[0:1, :]
    hb = h.astype(BF16)
    qscale = HEAD_DIM ** -0.5

    def seg(i):
        return _dot(hb, win_ref[:, i * WIDTH:(i + 1) * WIDTH])

    aq_ref[...] = (seg(0) * (qscale * LOG2E)).astype(BF16)
    z = seg(1)
    akf_ref[...] = z
    ak_ref[...] = z.astype(BF16)
    z = seg(2)
    avf_ref[...] = z
    av_ref[...] = z.astype(BF16)
    bq_ref[...] = (seg(3) * (qscale * LOG2E)).astype(BF16)
    z = seg(4)
    bkf_ref[...] = z
    bk_ref[...] = z.astype(BF16)
    z = seg(5)
    bvf_ref[...] = z
    bv_ref[...] = z.astype(BF16)
    iq_ref[...] = (seg(6) * qscale).astype(BF16)
    tail = _dot(hb, win_ref[:, 7 * WIDTH:7 * WIDTH + LANES])
    ik = tail[:, :HEAD_DIM]
    ikf_ref[...] = ik
    ki2_ref[...] = jnp.concatenate([ik, ik], axis=-1).astype(BF16)
    iw_ref[...] = tail[:, HEAD_DIM:HEAD_DIM + N_HEADS] * (N_HEADS ** -0.5)
    zg = _dot(hb, wg_ref[...])
    gate = 1.0 / (1.0 + jnp.exp(-zg))
    ga_ref[...] = gate[:, :D_MODEL]
    gb_ref[...] = gate[:, D_MODEL:]


def _inproj(x2, mod3, g_pre, w_in_b, w_gate_b, tm, tiles_per_seq):
    n = x2.shape[0]
    row = lambda i: (i, 0)
    const = lambda i: (0, 0)
    bspec = lambda w: pl.BlockSpec((tm, w), row)
    outs = ([jax.ShapeDtypeStruct((n, WIDTH), BF16)] * 7
            + [jax.ShapeDtypeStruct((n, LANES), BF16)]
            + [jax.ShapeDtypeStruct((n, WIDTH), F32)] * 4
            + [jax.ShapeDtypeStruct((n, HEAD_DIM), F32),
               jax.ShapeDtypeStruct((n, N_HEADS), F32),
               jax.ShapeDtypeStruct((n, D_MODEL), F32),
               jax.ShapeDtypeStruct((n, D_MODEL), F32)])
    out_specs = ([bspec(WIDTH)] * 7 + [bspec(LANES)] + [bspec(WIDTH)] * 4
                 + [bspec(HEAD_DIM), bspec(N_HEADS), bspec(D_MODEL), bspec(D_MODEL)])
    return pl.pallas_call(
        _inproj_kernel,
        out_shape=outs,
        grid=(n // tm,),
        in_specs=[pl.BlockSpec((tm, D_MODEL), row),
                  pl.BlockSpec((1, 6, D_MODEL), lambda i: (i // tiles_per_seq, 0, 0)),
                  pl.BlockSpec((1, D_MODEL), const),
                  pl.BlockSpec((D_MODEL, W_IN_PAD), const, pipeline_mode=pl.Buffered(1)),
                  pl.BlockSpec((D_MODEL, 2 * D_MODEL), const, pipeline_mode=pl.Buffered(1))],
        out_specs=out_specs,
        compiler_params=_params(("parallel",)),
        name="inproj",
    )(x2, mod3, g_pre, w_in_b, w_gate_b)


def _sb_kernel(tab_ref, q_ref, k_ref, v_ref, tri_ref, o_ref, *scratch, tq, tk, q_off, pairs):
    s = pl.program_id(2)
    qi = tab_ref[0, s]
    kj = tab_ref[1, s]
    first = tab_ref[2, s]
    last = tab_ref[3, s]
    masked = tab_ref[4, s]
    n_heads = 2 * pairs
    acc_refs, run_refs = scratch[:n_heads], scratch[n_heads:]

    @pl.when(first == 1)
    def _():
        for r in scratch:
            r[...] = jnp.zeros_like(r)

    lo_lanes = lax.broadcasted_iota(I32, (tq, LANES), 1) < HEAD_DIM

    def step(use_mask):
        tri = tri_ref[...]
        if use_mask:
            qpos = q_off + qi * tq + lax.broadcasted_iota(I32, (tq, tk), 0)
            kpos = kj * tk + lax.broadcasted_iota(I32, (tq, tk), 1)
            causal = kpos < qpos
        z = []
        for p in range(pairs):
            q = q_ref[0, :, p * LANES:(p + 1) * LANES]
            k = k_ref[0, :, p * LANES:(p + 1) * LANES]
            zero = jnp.zeros_like(q)
            z += [_dot_nt(jnp.where(lo_lanes, q, zero), k), _dot_nt(jnp.where(lo_lanes, zero, q), k)]
        suf = []
        for h in range(n_heads):
            sp = jnp.maximum(z[h], jnp.log2(1.0 + jnp.exp2(jnp.minimum(z[h], SOFTPLUS_LINEAR))))
            if use_mask:
                sp = jnp.where(causal, sp, 0.0)
            hi = sp.astype(BF16)
            lo = (sp - hi.astype(F32)).astype(BF16)
            suf.append(_dot(hi, tri) + _dot(lo, tri) + run_refs[h][:, 0:1])
        for h in range(n_heads):
            w = jnp.exp2(z[h] - suf[h])
            if use_mask:
                w = jnp.where(causal, w, 0.0)
            p = h // 2
            acc_refs[h][...] += _dot(w.astype(BF16), v_ref[0, :, p * LANES:(p + 1) * LANES])
            run_refs[h][...] = jnp.broadcast_to(suf[h][:, 0:1], (tq, LANES))

    @pl.when(masked == 1)
    def _():
        step(True)

    @pl.when(masked == 0)
    def _():
        step(False)

    @pl.when(last == 1)
    def _():
        for p in range(pairs):
            o_ref[0, :, p * LANES:(p + 1) * LANES] = jnp.where(
                lo_lanes, acc_refs[2 * p][...], acc_refs[2 * p + 1][...]).astype(o_ref.dtype)


def _sb_table(nq, nk, tq, tk, q_off):
    rows = []
    for qi in range(nq):
        q_lo = q_off + qi * tq
        q_hi = q_lo + tq - 1
        kjs = [kj for kj in range(nk) if kj * tk < q_hi]
        if not kjs:
            kjs = [0]
        kjs = kjs[::-1]
        for n, kj in enumerate(kjs):
            masked = int(kj * tk + tk - 1 >= q_lo)
            rows.append((qi, kj, int(n == 0), int(n == len(kjs) - 1), masked))
    return np.asarray(rows, dtype=np.int32).T.copy()


def _sb_attention(q, k, v, tq, tk, q_off):
    b, t, _ = q.shape
    s_len = k.shape[1]
    tab = _sb_table(t // tq, s_len // tk, tq, tk, q_off)
    tri = jnp.asarray(np.tril(np.ones((tk, tk), np.float32)), dtype=BF16)
    pairs = SB_PAIRS_PER_STEP
    width = pairs * LANES
    grid_spec = pltpu.PrefetchScalarGridSpec(
        num_scalar_prefetch=1,
        grid=(b, N_PAIRS // pairs, tab.shape[1]),
        in_specs=[pl.BlockSpec((1, tq, width), lambda bi, hp, s, tab: (bi, tab[0, s], hp)),
                  pl.BlockSpec((1, tk, width), lambda bi, hp, s, tab: (bi, tab[1, s], hp)),
                  pl.BlockSpec((1, tk, width), lambda bi, hp, s, tab: (bi, tab[1, s], hp)),
                  pl.BlockSpec((tk, tk), lambda bi, hp, s, tab: (0, 0))],
        out_specs=pl.BlockSpec((1, tq, width), lambda bi, hp, s, tab: (bi, tab[0, s], hp)),
        scratch_shapes=[pltpu.VMEM((tq, LANES), F32)] * (4 * pairs),
    )
    return pl.pallas_call(
        functools.partial(_sb_kernel, tq=tq, tk=tk, q_off=q_off, pairs=pairs),
        out_shape=jax.ShapeDtypeStruct((b, t, WIDTH), BF16),
        grid_spec=grid_spec,
        compiler_params=_params(("parallel", "parallel", "arbitrary")),
        name="sb_attn",
    )(jnp.asarray(tab), q, k, v, tri)


def _sortable(x):
    bits = pltpu.bitcast(x + 0.0, I32)
    return jnp.where(bits < 0, bits ^ 0x7FFFFFFF, bits)


def _alibi_slope2(h):
    return LOG2E * 2.0 ** (-8.0 * (h + 1) / N_HEADS)


def _dsa_kernel(q_ref, iq_ref, iwt_ref, k_ref, vt_ref, ki_ref, o_ref, keys_ref, dm_ref, hi_ref, lo_ref, qt_ref, iqt_ref,
                *stats,
                 tq, tk, q_off, n_kb_max, n_sel, idx_bits):
    m_refs, l_refs, acc_refs = stats[:N_HEADS], stats[N_HEADS:2 * N_HEADS], stats[2 * N_HEADS:]
    qi = pl.program_id(1)
    q_lo = q_off + qi * tq
    adm_end = ((q_lo + tq - 1) // CHUNK + 1) * CHUNK
    n_kb = jnp.minimum((adm_end + tk - 1) // tk, n_kb_max)

    lo_half = lax.broadcasted_iota(I32, (tq, LANES), 1) < HEAD_DIM
    qpos = q_lo + lax.broadcasted_iota(I32, (tk, tq), 1)
    krow = lax.broadcasted_iota(I32, (tk, tq), 0)

    def heads_of(x_ref):
        out = []
        for hp in range(N_PAIRS):
            pair = x_ref[0, :, hp * LANES:(hp + 1) * LANES]
            zero = jnp.zeros_like(pair)
            out += [jnp.where(lo_half, pair, zero), jnp.where(lo_half, zero, pair)]
        return out

    for src, dst in ((iq_ref, iqt_ref), (q_ref, qt_ref)):
        for h, xe in enumerate(heads_of(src)):
            dst[h] = xe.astype(F32).T.astype(BF16)

    def score_block(j, carry):
        ki = ki_ref[0, pl.ds(pl.multiple_of(j * tk, tk), tk), :]
        score = jnp.zeros((tk, tq), F32)
        for h in range(N_HEADS):
            score = score + iwt_ref[0, h:h + 1, :] * jnp.maximum(_dot(ki, iqt_ref[h]), 0.0)
        adm = ((j * tk + krow) // CHUNK) <= (qpos // CHUNK)
        key = jnp.where(adm, _sortable(score), INT_MIN)
        keys_ref[j] = key
        hi_ref[j] = (key >> 16).astype(I16)
        lo_ref[j] = ((key & 0xFFFF) - HALF_BIAS).astype(I16)
        return carry

    lax.fori_loop(0, n_kb, score_block, 0)

    def count_cols(pred_fn):
        def body(j, acc):
            hit = pred_fn(keys_ref[j], j).astype(I32)
            return acc + jnp.sum(hit.reshape(tk // 8, 8, tq), axis=0)
        acc = lax.fori_loop(0, n_kb, body, jnp.zeros((8, tq), I32))
        return jnp.sum(acc, axis=0, keepdims=True)

    def count_half(ref, pred_fn):
        def body(j, acc):
            hit = jnp.where(pred_fn(ref[j]), jnp.int16(1), jnp.int16(0))
            for r in range(tk // 16):
                acc = acc + hit[r * 16:(r + 1) * 16, :]
            return acc
        acc = lax.fori_loop(0, n_kb, body, jnp.zeros((16, tq), I16))
        return jnp.sum(acc.astype(I32), axis=0, keepdims=True)

    def search_half(ref, target):
        def bit_step(i, t):
            cand = t + jnp.left_shift(jnp.int32(1), 15 - i)
            cand16 = cand.astype(I16)
            cnt = count_half(ref, lambda kb: kb >= cand16)
            return jnp.where(cnt >= target, cand, t)
        return lax.fori_loop(0, 16, bit_step, jnp.full((1, tq), -HALF_BIAS, I32))

    t_hi = search_half(hi_ref, n_sel)
    t_hi16 = t_hi.astype(I16)
    n_above = count_half(hi_ref, lambda kb: kb > t_hi16)

    def keep_group(j, carry):
        lo_ref[j] = jnp.where(hi_ref[j] == t_hi16, lo_ref[j], jnp.int16(-HALF_BIAS))
        return carry

    lax.fori_loop(0, n_kb, keep_group, 0)
    t_lo = search_half(lo_ref, n_sel - n_above)
    thr = t_hi * (2 * HALF_BIAS) + (t_lo + HALF_BIAS)
    n_gt = count_cols(lambda kb, j: kb > thr)
    n_ge = count_cols(lambda kb, j: kb >= thr)
    need = n_sel - n_gt
    real = thr > INT_MIN
    tie = jnp.max(jnp.where(real & (n_ge > n_sel), 1, 0)) > 0

    def tie_cut():
        def idx_step(i, cut):
            cand = cut + jnp.left_shift(jnp.int32(1), idx_bits - 1 - i)
            cnt = count_cols(lambda kb, j: (kb == thr) & ((j * tk + krow) < cand))
            return jnp.where(cnt < need, cand, cut)
        return lax.fori_loop(0, idx_bits, idx_step, jnp.zeros((1, tq), I32))

    cut = lax.cond(tie, tie_cut, lambda: jnp.full((1, tq), n_kb_max * tk, I32))

    def mask_block(j, carry):
        kb = keys_ref[j]
        kpos = j * tk + krow
        sel = (kb > thr) | ((kb == thr) & real & (kpos <= cut))
        dm_ref[j] = jnp.where(sel, jnp.abs(qpos - kpos).astype(F32), DIST_MASKED)
        return carry

    lax.fori_loop(0, n_kb, mask_block, 0)

    for h in range(N_HEADS):
        m_refs[h][...] = jnp.full_like(m_refs[h], NEG_BIG)
        l_refs[h][...] = jnp.zeros_like(l_refs[h])
        acc_refs[h][...] = jnp.zeros_like(acc_refs[h])

    def attn_block(j, carry):
        start = pl.multiple_of(j * tk, tk)
        dm = dm_ref[j]

        def qk(h):
            return _dot(k_ref[0, pl.ds(start, tk), (h // 2) * LANES:(h // 2 + 1) * LANES], qt_ref[h])

        ahead = 4
        pending = [qk(h) for h in range(ahead)]
        for hp in range(N_PAIRS):
            for e in range(2):
                h = 2 * hp + e
                s_this = pending.pop(0)
                if h + ahead < N_HEADS:
                    pending.append(qk(h + ahead))
                t2 = s_this - _alibi_slope2(h) * dm
                m_old = m_refs[h][...]
                m_new = jnp.maximum(m_old, jnp.max(t2, axis=0, keepdims=True))
                alpha = jnp.exp2(m_old - m_new)
                p = jnp.exp2(t2 - m_new)
                l_refs[h][...] = alpha * l_refs[h][...] + jnp.sum(p, axis=0, keepdims=True)
                vt = vt_ref[0, j, hp * LANES + e * HEAD_DIM:hp * LANES + (e + 1) * HEAD_DIM, :]
                acc_refs[h][...] = alpha * acc_refs[h][...] + _dot(vt, p.astype(BF16))
                m_refs[h][...] = m_new
        return carry

    lax.fori_loop(0, n_kb, attn_block, 0)

    for hp in range(N_PAIRS):
        ot = jnp.concatenate([acc_refs[2 * hp][...] / l_refs[2 * hp][...],
                              acc_refs[2 * hp + 1][...] / l_refs[2 * hp + 1][...]], axis=0)
        o_ref[0, :, hp * LANES:(hp + 1) * LANES] = ot.T.astype(o_ref.dtype)


def _dsa_attention(q, iq, iw, k, v, ki2, tq, tk, q_off, n_sel):
    b, t, _ = q.shape
    s_len = k.shape[1]
    n_kb_max = s_len // tk
    iwt = jnp.swapaxes(iw, 1, 2)
    vt = jnp.swapaxes(v.reshape(b, n_kb_max, tk, WIDTH), 2, 3)
    qspec = lambda w: pl.BlockSpec((1, tq, w), lambda bi, i: (bi, i, 0))
    kspec = lambda w: pl.BlockSpec((1, s_len, w), lambda bi, i: (bi, 0, 0), pipeline_mode=pl.Buffered(1))
    return pl.pallas_call(
        functools.partial(_dsa_kernel, tq=tq, tk=tk, q_off=q_off, n_kb_max=n_kb_max, n_sel=n_sel,
                          idx_bits=max(1, int(math.ceil(math.log2(s_len + 1))))),
        out_shape=jax.ShapeDtypeStruct((b, t, WIDTH), BF16),
        grid=(b, t // tq),
        in_specs=[qspec(WIDTH), qspec(WIDTH),
                  pl.BlockSpec((1, N_HEADS, tq), lambda bi, i: (bi, 0, i)),
                  kspec(WIDTH),
                  pl.BlockSpec((1, n_kb_max, WIDTH, tk), lambda bi, i: (bi, 0, 0, 0), pipeline_mode=pl.Buffered(1)),
                  kspec(LANES)],
        out_specs=qspec(WIDTH),
        scratch_shapes=[pltpu.VMEM((n_kb_max, tk, tq), I32), pltpu.VMEM((n_kb_max, tk, tq), F32),
                        pltpu.VMEM((n_kb_max, tk, tq), I16), pltpu.VMEM((n_kb_max, tk, tq), I16),
                        pltpu.VMEM((N_HEADS, LANES, tq), BF16), pltpu.VMEM((N_HEADS, LANES, tq), BF16)]
                       + [pltpu.VMEM((1, tq), F32)] * (2 * N_HEADS) + [pltpu.VMEM((HEAD_DIM, tq), F32)] * N_HEADS,
        compiler_params=_params(("parallel", "arbitrary")),
        name="dsa_attn",
    )(q, iq, iwt, k, vt, ki2)


def _mix_kernel(oa_ref, ob_ref, ga_ref, gb_ref, x_ref, mod_ref, gpost_ref, gpre_ref,
                wa_ref, wb_ref, wo_ref, wq_ref, x1_ref, h2_ref, pq_ref):
    mod = mod_ref[0]
    mixed = ga_ref[...] * _dot(oa_ref[...], wa_ref[...]) + gb_ref[...] * _dot(ob_ref[...], wb_ref[...])
    y = _dot(mixed.astype(BF16), wo_ref[...])
    x1 = x_ref[...] + mod[2:3, :] * _rms(y, gpost_ref[...])
    x1_ref[...] = x1
    h2 = _rms(x1, gpre_ref[...]) * (1.0 + mod[4:5, :]) + mod[3:4, :]
    h2_ref[...] = h2
    pq_ref[...] = _dot(h2.astype(BF16), wq_ref[...]).astype(BF16)


def _mix(oa, ob, ga, gb, x2, mod3, g_post, g_pre, wa, wb, wo, wq, tm, tiles_per_seq):
    n = x2.shape[0]
    row = lambda i: (i, 0)
    const = lambda i: (0, 0)
    nq = wq.shape[1]
    wspec = lambda r, c: pl.BlockSpec((r, c), const, pipeline_mode=pl.Buffered(1))
    return pl.pallas_call(
        _mix_kernel,
        out_shape=[jax.ShapeDtypeStruct((n, D_MODEL), F32),
                   jax.ShapeDtypeStruct((n, D_MODEL), F32),
                   jax.ShapeDtypeStruct((n, nq), BF16)],
        grid=(n // tm,),
        in_specs=[pl.BlockSpec((tm, WIDTH), row), pl.BlockSpec((tm, WIDTH), row),
                  pl.BlockSpec((tm, D_MODEL), row), pl.BlockSpec((tm, D_MODEL), row),
                  pl.BlockSpec((tm, D_MODEL), row),
                  pl.BlockSpec((1, 6, D_MODEL), lambda i: (i // tiles_per_seq, 0, 0)),
                  pl.BlockSpec((1, D_MODEL), const), pl.BlockSpec((1, D_MODEL), const),
                  wspec(WIDTH, D_MODEL), wspec(WIDTH, D_MODEL), wspec(D_MODEL, D_MODEL),
                  wspec(D_MODEL, nq)],
        out_specs=[pl.BlockSpec((tm, D_MODEL), row), pl.BlockSpec((tm, D_MODEL), row),
                   pl.BlockSpec((tm, nq), row)],
        compiler_params=_params(("parallel",)),
        name="mix",
    )(oa, ob, ga, gb, x2, mod3, g_post, g_pre, wa, wb, wo, wq)


def _top16_many(arrays, payloads=None):
    state = list(arrays)
    pos = [lax.broadcasted_iota(I32, a.shape, 0) for a in arrays]
    vals = [[] for _ in arrays]
    picks = [[] for _ in arrays]
    for _ in range(PEER_TOPK):
        for c, s in enumerate(state):
            m = jnp.max(s, axis=0, keepdims=True)
            p = jnp.min(jnp.where(s == m, pos[c], s.shape[0]), axis=0, keepdims=True)
            hit = pos[c] == p
            vals[c].append(m)
            picks[c].append(p if payloads is None
                            else jnp.sum(jnp.where(hit, payloads[c], 0), axis=0, keepdims=True))
            state[c] = jnp.where(hit, -jnp.inf, s)
    return [(jnp.concatenate(v, axis=0), jnp.concatenate(p, axis=0)) for v, p in zip(vals, picks)]


def _peersel_kernel(pq_ref, sk_ref, idx_ref, g_ref, *, heads):
    tt = pq_ref.shape[0]
    scores = []
    for h in range(heads):
        for half in range(2):
            lo = (2 * h + half) * PEER_DHALF
            scores.append(_dot_nt(sk_ref[h, half], pq_ref[:, lo:lo + PEER_DHALF]))
    tops = _top16_many(scores)
    half_k = PEER_TOPK // 2
    row8 = lax.broadcasted_iota(I32, (half_k, tt), 0)
    cands, cidxs = [], []
    for h in range(heads):
        (v1, i1), (v2, i2) = tops[2 * h], tops[2 * h + 1]
        cand = [v1[0:1] + v2]
        cidx = [i1[0:1] * PEER_NKEYS + i2]
        for i in range(1, half_k):
            cand.append(jnp.where(row8 < PEER_TOPK // (i + 1), v1[i:i + 1] + v2[0:half_k], -jnp.inf))
            cidx.append(i1[i:i + 1] * PEER_NKEYS + i2[0:half_k])
        cand.append(v1[half_k:] + v2[0:1])
        cidx.append(i1[half_k:] * PEER_NKEYS + i2[0:1])
        cands.append(jnp.concatenate(cand, axis=0))
        cidxs.append(jnp.concatenate(cidx, axis=0))
    for h, (top, eidx) in enumerate(_top16_many(cands, cidxs)):
        ex = jnp.exp(top - top[0:1])
        g_ref[h] = ex / jnp.sum(ex, axis=0, keepdims=True)
        idx_ref[h] = eidx


def _peer_select(pq, subkeys_b, tt):
    n = pq.shape[0]
    heads = PEER_SELECT_HEADS
    return pl.pallas_call(
        functools.partial(_peersel_kernel, heads=heads),
        out_shape=[jax.ShapeDtypeStruct((PEER_HEADS, PEER_TOPK, n), I32),
                   jax.ShapeDtypeStruct((PEER_HEADS, PEER_TOPK, n), F32)],
        grid=(n // tt, PEER_HEADS // heads),
        in_specs=[pl.BlockSpec((tt, heads * 2 * PEER_DHALF), lambda i, h: (i, h)),
                  pl.BlockSpec((heads, 2, PEER_NKEYS, PEER_DHALF), lambda i, h: (h, 0, 0, 0))],
        out_specs=[pl.BlockSpec((heads, PEER_TOPK, tt), lambda i, h: (h, 0, i)),
                   pl.BlockSpec((heads, PEER_TOPK, tt), lambda i, h: (h, 0, i))],
        compiler_params=_params(("parallel", "arbitrary")),
        name="peer_select",
    )(pq, subkeys_b)


def _erf_gelu(a):
    return 0.5 * a * (1.0 + lax.erf(a * (2.0 ** -0.5)))


def _peerffn_kernel(idx_ref, idx_next_ref, gt_ref, h2_ref, x1_ref, mod_ref, gpost_ref, uv_hbm, o_ref,
                    *scratch, tg, seqs_per_tile):
    bufs, (f_ref, sem) = scratch[:PEER_SLOTS], scratch[PEER_SLOTS:]
    step, last_step = pl.program_id(0), pl.num_programs(0) - 1

    def start_token(t, slot, src=None):
        src = idx_ref if src is None else src
        for k in range(PEER_SEL):
            pltpu.make_async_copy(uv_hbm.at[src[t, k]], bufs[slot].at[k],
                                  sem.at[slot]).start(priority=k % 2)

    def wait_token(slot):
        pltpu.make_async_copy(bufs[slot], bufs[slot], sem.at[slot]).wait()

    lane_tok = lax.broadcasted_iota(I32, (PEER_SEL, tg), 1)
    sub8 = lax.broadcasted_iota(I32, (8, D_MODEL), 0)

    row = lax.broadcasted_iota(I32, (8, LANES), 0)
    low_half, mid_pairs, odd_rows = row < 4, (row % 4) >= 2, (row % 2) == 1

    def fold_group(tiles):
        t_ = [tiles[i] for i in (3, 7, 1, 5, 2, 6, 0, 4)]
        r = [x + pltpu.roll(x, 4, 0) for x in t_]
        y = [jnp.where(low_half, r[2 * m], r[2 * m + 1]) for m in range(4)]
        y = [x + pltpu.roll(x, 2, 0) for x in y]
        z = [jnp.where(mid_pairs, y[2 * n], pltpu.roll(y[2 * n + 1], 6, 0)) for n in range(2)]
        z = [x + pltpu.roll(x, 1, 0) for x in z]
        return jnp.where(odd_rows, z[0], pltpu.roll(z[1], 7, 0))

    def compute(t, slot):
        buf = bufs[slot]
        t8 = pl.multiple_of((t // 8) * 8, 8)
        h8 = h2_ref[pl.ds(t8, 8), :]
        hrow = jnp.sum(jnp.where(sub8 == t - t8, h8, 0.0), axis=0, keepdims=True)
        htile = jnp.concatenate([hrow[:, c * LANES:(c + 1) * LANES] for c in range(N_CHUNK)], axis=0)
        part = jnp.concatenate(
            [fold_group([buf[8 * grp + i, 0:N_CHUNK, :] * htile for i in range(8)]) for grp in range(PEER_SEL // 8)],
            axis=0)
        a = jnp.sum(part, axis=1, keepdims=True)
        g = jnp.sum(jnp.where(lane_tok == t, gt_ref[...], 0.0), axis=1, keepdims=True)
        w = jnp.broadcast_to(g * _erf_gelu(a), (PEER_SEL, LANES))
        accs = [None] * 4
        for k in range(PEER_SEL):
            term = buf[k, N_CHUNK:, :] * jnp.broadcast_to(w[k:k + 1, :], (N_CHUNK, LANES))
            accs[k % 4] = term if accs[k % 4] is None else accs[k % 4] + term
        ftile = (accs[0] + accs[1]) + (accs[2] + accs[3])
        f_ref[pl.ds(t, 1), :] = jnp.concatenate([ftile[c:c + 1, :] for c in range(N_CHUNK)], axis=1)

    @pl.when(step == 0)
    def _():
        for t in range(PEER_AHEAD):
            start_token(t, t)

    def group(i, carry):
        t0 = PEER_SLOTS * i
        for j in range(PEER_SLOTS):
            wait_token(j)
            start_token(t0 + j + PEER_AHEAD, (j + PEER_AHEAD) % PEER_SLOTS)
            compute(t0 + j, j)
        return carry

    n_groups = tg // PEER_SLOTS
    lax.fori_loop(0, n_groups - 1, group, 0)
    t0 = tg - PEER_SLOTS
    for j in range(PEER_SLOTS):
        wait_token(j)
        ahead = j + PEER_AHEAD
        if ahead < PEER_SLOTS:
            start_token(t0 + ahead, ahead)
        else:
            start_token(ahead - PEER_SLOTS, ahead - PEER_SLOTS, idx_next_ref)
        compute(t0 + j, j)

    @pl.when(step == last_step)
    def _():
        for t in range(PEER_AHEAD):
            wait_token(t)

    rows = tg // seqs_per_tile
    for s in range(seqs_per_tile):
        sl = slice(s * rows, (s + 1) * rows)
        o_ref[sl, :] = x1_ref[sl, :] + mod_ref[s][5:6, :] * _rms(f_ref[sl, :], gpost_ref[...])


def _peer_ffn(idx, gt, h2, x1, mod3, g_post, uv, tg, seq_len):
    n = h2.shape[0]
    row = lambda i: (i, 0)
    seqs_per_tile = max(1, tg // seq_len)
    tiles_per_seq = max(1, seq_len // tg)
    return pl.pallas_call(
        functools.partial(_peerffn_kernel, tg=tg, seqs_per_tile=seqs_per_tile),
        out_shape=jax.ShapeDtypeStruct((n, D_MODEL), F32),
        grid=(n // tg,),
        in_specs=[pl.BlockSpec((tg, PEER_SEL), row, memory_space=pltpu.SMEM),
                  pl.BlockSpec((tg, PEER_SEL), lambda i: (jnp.minimum(i + 1, n // tg - 1), 0), memory_space=pltpu.SMEM),
                  pl.BlockSpec((PEER_SEL, tg), lambda i: (0, i)),
                  pl.BlockSpec((tg, D_MODEL), row),
                  pl.BlockSpec((tg, D_MODEL), row),
                  pl.BlockSpec((seqs_per_tile, 6, D_MODEL), lambda i: (i // tiles_per_seq, 0, 0)),
                  pl.BlockSpec((1, D_MODEL), lambda i: (0, 0)),
                  pl.BlockSpec(memory_space=pl.ANY)],
        out_specs=pl.BlockSpec((tg, D_MODEL), row),
        scratch_shapes=[pltpu.VMEM((PEER_SEL, 2 * N_CHUNK, LANES), F32)] * PEER_SLOTS
                       + [pltpu.VMEM((tg, D_MODEL), F32), pltpu.SemaphoreType.DMA((PEER_SLOTS,))],
        compiler_params=_params(("arbitrary",)),
        name="peer_ffn",
    )(idx, idx, gt, h2, x1, mod3, g_post, uv)


def _flat_kernel(x_ref, o_ref):
    x = x_ref[0]
    o_ref[0] = jnp.concatenate([x[:, h, :] for h in range(N_HEADS)], axis=-1).astype(BF16)


def _flatten_cache(c):
    b, s_len = c.shape[0], c.shape[1]
    rows = 256 if s_len % 256 == 0 else CHUNK
    return pl.pallas_call(
        _flat_kernel,
        out_shape=jax.ShapeDtypeStruct((b, s_len, WIDTH), BF16),
        grid=(b, s_len // rows),
        in_specs=[pl.BlockSpec((1, rows, N_HEADS, HEAD_DIM), lambda i, j: (i, j, 0, 0))],
        out_specs=pl.BlockSpec((1, rows, WIDTH), lambda i, j: (i, j, 0)),
        compiler_params=_params(("parallel", "parallel")),
        name="flatten_cache",
    )(c)


def _pad_keys(x, s_pad):
    return jnp.pad(x, ((0, 0), (0, s_pad - x.shape[1]), (0, 0)))


def _layer(x, mod3, past, weights, tiles):
    (g_pre_mix, g_post_mix, g_pre_ffn, g_post_ffn, w_in_b, w_gate_b, wa_b, wb_b, wo_b, wq_b,
     subkeys_b, peer_uv) = weights
    tm, sb_tq, sb_tk, dsa_tq, dsa_tk, tt, tg = tiles
    bsz, t, _ = x.shape
    n = bsz * t
    x2 = x.reshape(n, D_MODEL)
    (aq, ak, av, bq, bk, bv, iq, ki2, akf, avf, bkf, bvf, ikf, iw, ga, gb) = _inproj(
        x2, mod3, g_pre_mix, w_in_b, w_gate_b, tm, t // tm)
    seq = lambda a: a.reshape(bsz, t, a.shape[-1])
    if past is None:
        q_off = 0
        ka, va, kb, vb, ki = seq(ak), seq(av), seq(bk), seq(bv), seq(ki2)
        s_len = t
    else:
        p_sbk, p_sbv, p_dk, p_dv, p_ki = past
        q_off = p_sbk.shape[1]
        s_len = q_off + t
        ka = jnp.concatenate([_flatten_cache(p_sbk), seq(ak)], axis=1)
        va = jnp.concatenate([_flatten_cache(p_sbv), seq(av)], axis=1)
        kb = jnp.concatenate([_flatten_cache(p_dk), seq(bk)], axis=1)
        vb = jnp.concatenate([_flatten_cache(p_dv), seq(bv)], axis=1)
        pk = p_ki.astype(BF16)
        ki = jnp.concatenate([jnp.concatenate([pk, pk], axis=-1), seq(ki2)], axis=1)
    s_pad = -(-s_len // max(sb_tk, dsa_tk)) * max(sb_tk, dsa_tk)
    ka, va, kb, vb, ki = (_pad_keys(a, s_pad) for a in (ka, va, kb, vb, ki))
    n_sel = min(TOPK_MAX, s_len // 4)

    oa = _sb_attention(seq(aq), ka, va, sb_tq, sb_tk, q_off)
    ob = _dsa_attention(seq(bq), seq(iq), seq(iw), kb, vb, ki, dsa_tq, dsa_tk, q_off, n_sel)

    x1, h2, pq = _mix(oa.reshape(n, WIDTH), ob.reshape(n, WIDTH), ga, gb, x2, mod3, g_post_mix, g_pre_ffn,
                      wa_b, wb_b, wo_b, wq_b, tm, t // tm)
    eidx, gate = _peer_select(pq, subkeys_b, tt)
    eidx = eidx.reshape(PEER_SEL, n).T
    y = _peer_ffn(eidx, gate.reshape(PEER_SEL, n), h2, x1, mod3, g_post_ffn, peer_uv, tg, t)
    new = tuple(a.reshape(1, bsz, t, N_HEADS, HEAD_DIM) for a in (akf, avf, bkf, bvf)) + (
        ikf.reshape(1, bsz, t, HEAD_DIM),)
    return y.reshape(bsz, t, D_MODEL), new


def _tiles(t):
    if t % 256 == 0:
        return (256, 256, 256, 256, 512, 128, 128)
    return (t, t, 128, t, 128, 128, 128)


def kernel(x_prompt, x_sample, c_prompt, c_sample, cache_sb_k, cache_sb_v, cache_dsa_k, cache_dsa_v, cache_dsa_kidx, w_ada, b_ada, g_pre_mix, g_post_mix, g_pre_ffn, g_post_ffn, w_in, w_gate, w_branch_a, w_branch_b, w_out, w_peer_q, peer_subkeys, peer_u, peer_v):
    assert w_ada.shape[0] == 1, "one layer"
    n_p, n_s = c_prompt.shape[0], c_sample.shape[0]
    c_all = jnp.concatenate([c_prompt, c_sample], axis=0)
    rows = -(-c_all.shape[0] // 8) * 8
    c_all = jnp.pad(c_all, ((0, rows - c_all.shape[0]), (0, 0)))
    mod = _modulation(c_all, w_ada[0], b_ada[0]).reshape(rows, 6, D_MODEL)
    w_in_b = jnp.pad(w_in[0], ((0, 0), (0, W_IN_PAD - W_IN_COLS))).astype(BF16)
    weights = (g_pre_mix, g_post_mix, g_pre_ffn, g_post_ffn, w_in_b, w_gate[0].astype(BF16),
               w_branch_a[0].astype(BF16), w_branch_b[0].astype(BF16), w_out[0].astype(BF16),
               w_peer_q[0].astype(BF16), peer_subkeys[0].astype(BF16),
               jnp.concatenate([peer_u[0].reshape(-1, N_CHUNK, LANES),
                                peer_v[0].reshape(-1, N_CHUNK, LANES)], axis=1))
    yp, new_p = _layer(x_prompt, mod[:n_p], None, weights, _tiles(x_prompt.shape[1]))
    past = (cache_sb_k[0], cache_sb_v[0], cache_dsa_k[0], cache_dsa_v[0], cache_dsa_kidx[0])
    ys, new_s = _layer(x_sample, mod[n_p:n_p + n_s], past, weights, _tiles(x_sample.shape[1]))
    return (yp, ys) + new_p + new_s
```

```python
import functools
import math

import jax
import jax.numpy as jnp
import numpy as np
from jax import lax
from jax.experimental import pallas as pl
from jax.experimental.pallas import tpu as pltpu

F32 = jnp.float32
BF16 = jnp.bfloat16
I32 = jnp.int32
I16 = jnp.int16

D_MODEL = 1024
CHUNK = 64
EPS = 1e-6
N_HEADS = 8
HEAD_DIM = 64
WIDTH = N_HEADS * HEAD_DIM
N_PAIRS = N_HEADS // 2
SB_PAIRS_PER_STEP = 4
LANES = 128
N_CHUNK = D_MODEL // LANES
TOPK_MAX = 256
PEER_HEADS = 8
PEER_NKEYS = 128
PEER_DHALF = 128
PEER_TOPK = 16
PEER_SEL = PEER_HEADS * PEER_TOPK
PEER_SELECT_HEADS = 4
PEER_SLOTS = 8
PEER_AHEAD = 6
W_IN_COLS = 7 * WIDTH + HEAD_DIM + N_HEADS
W_IN_PAD = 7 * WIDTH + LANES
INT_MIN = -(2 ** 31)
HALF_BIAS = 2 ** 15
NEG_BIG = -1e30
DIST_MASKED = 1e30
LOG2E = 1.4426950408889634
SOFTPLUS_LINEAR = 64.0
VMEM_LIMIT = 56 * 1024 * 1024


def _params(sem):
    return pltpu.CompilerParams(dimension_semantics=sem, vmem_limit_bytes=VMEM_LIMIT)


def _dot(a, b):
    return jnp.dot(a, b, preferred_element_type=F32)


def _dot_nt(a, b):
    return lax.dot_general(a, b, (((1,), (1,)), ((), ())), preferred_element_type=F32)


def _rms(x, gain):
    return x * lax.rsqrt(jnp.mean(x * x, axis=-1, keepdims=True) + EPS) * gain


def _mod_kernel(c_ref, w_ref, b_ref, o_ref):
    c = c_ref[...]
    s = c * (1.0 / (1.0 + jnp.exp(-c)))
    o_ref[...] = jnp.dot(s, w_ref[...], preferred_element_type=F32,
                         precision=lax.Precision.HIGHEST) + b_ref[...]


def _modulation(c, w_ada, b_ada):
    n = c.shape[0]
    cols = w_ada.shape[1]
    tn = 1024
    return pl.pallas_call(
        _mod_kernel,
        out_shape=jax.ShapeDtypeStruct((n, cols), F32),
        grid=(cols // tn,),
        in_specs=[pl.BlockSpec((n, D_MODEL), lambda j: (0, 0)),
                  pl.BlockSpec((D_MODEL, tn), lambda j: (0, j)),
                  pl.BlockSpec((1, tn), lambda j: (0, j))],
        out_specs=pl.BlockSpec((n, tn), lambda j: (0, j)),
        compiler_params=_params(("arbitrary",)),
        name="mod",
    )(c, w_ada, b_ada.reshape(1, cols))


def _inproj_kernel(x_ref, mod_ref, g_ref, win_ref, wg_ref,
                   aq_ref, ak_ref, av_ref, bq_ref, bk_ref, bv_ref, iq_ref, ki2_ref,
                   akf_ref, avf_ref, bkf_ref, bvf_ref, ikf_ref, iw_ref, ga_ref, gb_ref):
    x = x_ref[...]
    mod = mod_ref[0]
    h = _rms(x, g_ref[...]) * (1.0 + mod[1:2, :]) + mod[0:1, :]
    hb = h.astype(BF16)
    qscale = HEAD_DIM ** -0.5

    def seg(i):
        return _dot(hb, win_ref[:, i * WIDTH:(i + 1) * WIDTH])

    aq_ref[...] = (seg(0) * (qscale * LOG2E)).astype(BF16)
    z = seg(1)
    akf_ref[...] = z
    ak_ref[...] = z.astype(BF16)
    z = seg(2)
    avf_ref[...] = z
    av_ref[...] = z.astype(BF16)
    bq_ref[...] = (seg(3) * (qscale * LOG2E)).astype(BF16)
    z = seg(4)
    bkf_ref[...] = z
    bk_ref[...] = z.astype(BF16)
    z = seg(5)
    bvf_ref[...] = z
    bv_ref[...] = z.astype(BF16)
    iq_ref[...] = (seg(6) * qscale).astype(BF16)
    tail = _dot(hb, win_ref[:, 7 * WIDTH:7 * WIDTH + LANES])
    ik = tail[:, :HEAD_DIM]
    ikf_ref[...] = ik
    ki2_ref[...] = jnp.concatenate([ik, ik], axis=-1).astype(BF16)
    iw_ref[...] = tail[:, HEAD_DIM:HEAD_DIM + N_HEADS] * (N_HEADS ** -0.5)
    zg = _dot(hb, wg_ref[...])
    gate = 1.0 / (1.0 + jnp.exp(-zg))
    ga_ref[...] = gate[:, :D_MODEL]
    gb_ref[...] = gate[:, D_MODEL:]


def _inproj(x2, mod3, g_pre, w_in_b, w_gate_b, tm, tiles_per_seq):
    n = x2.shape[0]
    row = lambda i: (i, 0)
    const = lambda i: (0, 0)
    bspec = lambda w: pl.BlockSpec((tm, w), row)
    outs = ([jax.ShapeDtypeStruct((n, WIDTH), BF16)] * 7
            + [jax.ShapeDtypeStruct((n, LANES), BF16)]
            + [jax.ShapeDtypeStruct((n, WIDTH), F32)] * 4
            + [jax.ShapeDtypeStruct((n, HEAD_DIM), F32),
               jax.ShapeDtypeStruct((n, N_HEADS), F32),
               jax.ShapeDtypeStruct((n, D_MODEL), F32),
               jax.ShapeDtypeStruct((n, D_MODEL), F32)])
    out_specs = ([bspec(WIDTH)] * 7 + [bspec(LANES)] + [bspec(WIDTH)] * 4
                 + [bspec(HEAD_DIM), bspec(N_HEADS), bspec(D_MODEL), bspec(D_MODEL)])
    return pl.pallas_call(
        _inproj_kernel,
        out_shape=outs,
        grid=(n // tm,),
        in_specs=[pl.BlockSpec((tm, D_MODEL), row),
                  pl.BlockSpec((1, 6, D_MODEL), lambda i: (i // tiles_per_seq, 0, 0)),
                  pl.BlockSpec((1, D_MODEL), const),
                  pl.BlockSpec((D_MODEL, W_IN_PAD), const, pipeline_mode=pl.Buffered(1)),
                  pl.BlockSpec((D_MODEL, 2 * D_MODEL), const, pipeline_mode=pl.Buffered(1))],
        out_specs=out_specs,
        compiler_params=_params(("parallel",)),
        name="inproj",
    )(x2, mod3, g_pre, w_in_b, w_gate_b)


def _sb_kernel(tab_ref, q_ref, k_ref, v_ref, tri_ref, o_ref, *scratch, tq, tk, q_off, pairs):
    s = pl.program_id(2)
    qi = tab_ref[0, s]
    kj = tab_ref[1, s]
    first = tab_ref[2, s]
    last = tab_ref[3, s]
    masked = tab_ref[4, s]
    n_heads = 2 * pairs
    acc_refs, run_refs = scratch[:n_heads], scratch[n_heads:]

    @pl.when(first == 1)
    def _():
        for r in scratch:
            r[...] = jnp.zeros_like(r)

    lo_lanes = lax.broadcasted_iota(I32, (tq, LANES), 1) < HEAD_DIM

    def step(use_mask):
        tri = tri_ref[...]
        if use_mask:
            qpos = q_off + qi * tq + lax.broadcasted_iota(I32, (tq, tk), 0)
            kpos = kj * tk + lax.broadcasted_iota(I32, (tq, tk), 1)
            causal = kpos < qpos
        z = []
        for p in range(pairs):
            q = q_ref[0, :, p * LANES:(p + 1) * LANES]
            k = k_ref[0, :, p * LANES:(p + 1) * LANES]
            zero = jnp.zeros_like(q)
            z += [_dot_nt(jnp.where(lo_lanes, q, zero), k), _dot_nt(jnp.where(lo_lanes, zero, q), k)]
        suf = []
        for h in range(n_heads):
            sp = jnp.maximum(z[h], jnp.log2(1.0 + jnp.exp2(jnp.minimum(z[h], SOFTPLUS_LINEAR))))
            if use_mask:
                sp = jnp.where(causal, sp, 0.0)
            hi = sp.astype(BF16)
            lo = (sp - hi.astype(F32)).astype(BF16)
            suf.append(_dot(hi, tri) + _dot(lo, tri) + run_refs[h][:, 0:1])
        for h in range(n_heads):
            w = jnp.exp2(z[h] - suf[h])
            if use_mask:
                w = jnp.where(causal, w, 0.0)
            p = h // 2
            acc_refs[h][...] += _dot(w.astype(BF16), v_ref[0, :, p * LANES:(p + 1) * LANES])
            run_refs[h][...] = jnp.broadcast_to(suf[h][:, 0:1], (tq, LANES))

    @pl.when(masked == 1)
    def _():
        step(True)

    @pl.when(masked == 0)
    def _():
        step(False)

    @pl.when(last == 1)
    def _():
        for p in range(pairs):
            o_ref[0, :, p * LANES:(p + 1) * LANES] = jnp.where(
                lo_lanes, acc_refs[2 * p][...], acc_refs[2 * p + 1][...]).astype(o_ref.dtype)


def _sb_table(nq, nk, tq, tk, q_off):
    rows = []
    for qi in range(nq):
        q_lo = q_off + qi * tq
        q_hi = q_lo + tq - 1
        kjs = [kj for kj in range(nk) if kj * tk < q_hi]
        if not kjs:
            kjs = [0]
        kjs = kjs[::-1]
        for n, kj in enumerate(kjs):
            masked = int(kj * tk + tk - 1 >= q_lo)
            rows.append((qi, kj, int(n == 0), int(n == len(kjs) - 1), masked))
    return np.asarray(rows, dtype=np.int32).T.copy()


def _sb_attention(q, k, v, tq, tk, q_off):
    b, t, _ = q.shape
    s_len = k.shape[1]
    tab = _sb_table(t // tq, s_len // tk, tq, tk, q_off)
    tri = jnp.asarray(np.tril(np.ones((tk, tk), np.float32)), dtype=BF16)
    pairs = SB_PAIRS_PER_STEP
    width = pairs * LANES
    grid_spec = pltpu.PrefetchScalarGridSpec(
        num_scalar_prefetch=1,
        grid=(b, N_PAIRS // pairs, tab.shape[1]),
        in_specs=[pl.BlockSpec((1, tq, width), lambda bi, hp, s, tab: (bi, tab[0, s], hp)),
                  pl.BlockSpec((1, tk, width), lambda bi, hp, s, tab: (bi, tab[1, s], hp)),
                  pl.BlockSpec((1, tk, width), lambda bi, hp, s, tab: (bi, tab[1, s], hp)),
                  pl.BlockSpec((tk, tk), lambda bi, hp, s, tab: (0, 0))],
        out_specs=pl.BlockSpec((1, tq, width), lambda bi, hp, s, tab: (bi, tab[0, s], hp)),
        scratch_shapes=[pltpu.VMEM((tq, LANES), F32)] * (4 * pairs),
    )
    return pl.pallas_call(
        functools.partial(_sb_kernel, tq=tq, tk=tk, q_off=q_off, pairs=pairs),
        out_shape=jax.ShapeDtypeStruct((b, t, WIDTH), BF16),
        grid_spec=grid_spec,
        compiler_params=_params(("parallel", "parallel", "arbitrary")),
        name="sb_attn",
    )(jnp.asarray(tab), q, k, v, tri)


def _sortable(x):
    bits = pltpu.bitcast(x + 0.0, I32)
    return jnp.where(bits < 0, bits ^ 0x7FFFFFFF, bits)


def _alibi_slope2(h):
    return LOG2E * 2.0 ** (-8.0 * (h + 1) / N_HEADS)


def _dsa_kernel(q_ref, iq_ref, iwt_ref, k_ref, vt_ref, ki_ref, o_ref, keys_ref, dm_ref, hi_ref, lo_ref, qt_ref, iqt_ref,
                *stats,
                 tq, tk, q_off, n_kb_max, n_sel, idx_bits):
    m_refs, l_refs, acc_refs = stats[:N_HEADS], stats[N_HEADS:2 * N_HEADS], stats[2 * N_HEADS:]
    qi = pl.program_id(1)
    q_lo = q_off + qi * tq
    adm_end = ((q_lo + tq - 1) // CHUNK + 1) * CHUNK
    n_kb = jnp.minimum((adm_end + tk - 1) // tk, n_kb_max)

    lo_half = lax.broadcasted_iota(I32, (tq, LANES), 1) < HEAD_DIM
    qpos = q_lo + lax.broadcasted_iota(I32, (tk, tq), 1)
    krow = lax.broadcasted_iota(I32, (tk, tq), 0)

    def heads_of(x_ref):
        out = []
        for hp in range(N_PAIRS):
            pair = x_ref[0, :, hp * LANES:(hp + 1) * LANES]
            zero = jnp.zeros_like(pair)
            out += [jnp.where(lo_half, pair, zero), jnp.where(lo_half, zero, pair)]
        return out

    for src, dst in ((iq_ref, iqt_ref), (q_ref, qt_ref)):
        for h, xe in enumerate(heads_of(src)):
            dst[h] = xe.astype(F32).T.astype(BF16)

    def score_block(j, carry):
        ki = ki_ref[0, pl.ds(pl.multiple_of(j * tk, tk), tk), :]
        score = jnp.zeros((tk, tq), F32)
        for h in range(N_HEADS):
            score = score + iwt_ref[0, h:h + 1, :] * jnp.maximum(_dot(ki, iqt_ref[h]), 0.0)
        adm = ((j * tk + krow) // CHUNK) <= (qpos // CHUNK)
        key = jnp.where(adm, _sortable(score), INT_MIN)
        keys_ref[j] = key
        hi_ref[j] = (key >> 16).astype(I16)
        lo_ref[j] = ((key & 0xFFFF) - HALF_BIAS).astype(I16)
        return carry

    lax.fori_loop(0, n_kb, score_block, 0)

    def count_cols(pred_fn):
        def body(j, acc):
            hit = pred_fn(keys_ref[j], j).astype(I32)
            return acc + jnp.sum(hit.reshape(tk // 8, 8, tq), axis=0)
        acc = lax.fori_loop(0, n_kb, body, jnp.zeros((8, tq), I32))
        return jnp.sum(acc, axis=0, keepdims=True)

    def count_half(ref, pred_fn):
        def body(j, acc):
            hit = jnp.where(pred_fn(ref[j]), jnp.int16(1), jnp.int16(0))
            for r in range(tk // 16):
                acc = acc + hit[r * 16:(r + 1) * 16, :]
            return acc
        acc = lax.fori_loop(0, n_kb, body, jnp.zeros((16, tq), I16))
        return jnp.sum(acc.astype(I32), axis=0, keepdims=True)

    def search_half(ref, target):
        def bit_step(i, t):
            cand = t + jnp.left_shift(jnp.int32(1), 15 - i)
            cand16 = cand.astype(I16)
            cnt = count_half(ref, lambda kb: kb >= cand16)
            return jnp.where(cnt >= target, cand, t)
        return lax.fori_loop(0, 16, bit_step, jnp.full((1, tq), -HALF_BIAS, I32))

    t_hi = search_half(hi_ref, n_sel)
    t_hi16 = t_hi.astype(I16)
    n_above = count_half(hi_ref, lambda kb: kb > t_hi16)

    def keep_group(j, carry):
        lo_ref[j] = jnp.where(hi_ref[j] == t_hi16, lo_ref[j], jnp.int16(-HALF_BIAS))
        return carry

    lax.fori_loop(0, n_kb, keep_group, 0)
    t_lo = search_half(lo_ref, n_sel - n_above)
    thr = t_hi * (2 * HALF_BIAS) + (t_lo + HALF_BIAS)
    n_gt = count_cols(lambda kb, j: kb > thr)
    n_ge = count_cols(lambda kb, j: kb >= thr)
    need = n_sel - n_gt
    real = thr > INT_MIN
    tie = jnp.max(jnp.where(real & (n_ge > n_sel), 1, 0)) > 0

    def tie_cut():
        def idx_step(i, cut):
            cand = cut + jnp.left_shift(jnp.int32(1), idx_bits - 1 - i)
            cnt = count_cols(lambda kb, j: (kb == thr) & ((j * tk + krow) < cand))
            return jnp.where(cnt < need, cand, cut)
        return lax.fori_loop(0, idx_bits, idx_step, jnp.zeros((1, tq), I32))

    cut = lax.cond(tie, tie_cut, lambda: jnp.full((1, tq), n_kb_max * tk, I32))

    def mask_block(j, carry):
        kb = keys_ref[j]
        kpos = j * tk + krow
        sel = (kb > thr) | ((kb == thr) & real & (kpos <= cut))
        dm_ref[j] = jnp.where(sel, jnp.abs(qpos - kpos).astype(F32), DIST_MASKED)
        return carry

    lax.fori_loop(0, n_kb, mask_block, 0)

    for h in range(N_HEADS):
        m_refs[h][...] = jnp.full_like(m_refs[h], NEG_BIG)
        l_refs[h][...] = jnp.zeros_like(l_refs[h])
        acc_refs[h][...] = jnp.zeros_like(acc_refs[h])

    def attn_block(j, carry):
        start = pl.multiple_of(j * tk, tk)
        dm = dm_ref[j]

        def qk(h):
            return _dot(k_ref[0, pl.ds(start, tk), (h // 2) * LANES:(h // 2 + 1) * LANES], qt_ref[h])

        ahead = 4
        pending = [qk(h) for h in range(ahead)]
        for hp in range(N_PAIRS):
            for e in range(2):
                h = 2 * hp + e
                s_this = pending.pop(0)
                if h + ahead < N_HEADS:
                    pending.append(qk(h + ahead))
                t2 = s_this - _alibi_slope2(h) * dm
                m_old = m_refs[h][...]
                m_new = jnp.maximum(m_old, jnp.max(t2, axis=0, keepdims=True))
                alpha = jnp.exp2(m_old - m_new)
                p = jnp.exp2(t2 - m_new)
                l_refs[h][...] = alpha * l_refs[h][...] + jnp.sum(p, axis=0, keepdims=True)
                vt = vt_ref[0, j, hp * LANES + e * HEAD_DIM:hp * LANES + (e + 1) * HEAD_DIM, :]
                acc_refs[h][...] = alpha * acc_refs[h][...] + _dot(vt, p.astype(BF16))
                m_refs[h][...] = m_new
        return carry

    lax.fori_loop(0, n_kb, attn_block, 0)

    for hp in range(N_PAIRS):
        ot = jnp.concatenate([acc_refs[2 * hp][...] / l_refs[2 * hp][...],
                              acc_refs[2 * hp + 1][...] / l_refs[2 * hp + 1][...]], axis=0)
        o_ref[0, :, hp * LANES:(hp + 1) * LANES] = ot.T.astype(o_ref.dtype)


def _dsa_attention(q, iq, iw, k, v, ki2, tq, tk, q_off, n_sel):
    b, t, _ = q.shape
    s_len = k.shape[1]
    n_kb_max = s_len // tk
    iwt = jnp.swapaxes(iw, 1, 2)
    vt = jnp.swapaxes(v.reshape(b, n_kb_max, tk, WIDTH), 2, 3)
    qspec = lambda w: pl.BlockSpec((1, tq, w), lambda bi, i: (bi, i, 0))
    kspec = lambda w: pl.BlockSpec((1, s_len, w), lambda bi, i: (bi, 0, 0), pipeline_mode=pl.Buffered(1))
    return pl.pallas_call(
        functools.partial(_dsa_kernel, tq=tq, tk=tk, q_off=q_off, n_kb_max=n_kb_max, n_sel=n_sel,
                          idx_bits=max(1, int(math.ceil(math.log2(s_len + 1))))),
        out_shape=jax.ShapeDtypeStruct((b, t, WIDTH), BF16),
        grid=(b, t // tq),
        in_specs=[qspec(WIDTH), qspec(WIDTH),
                  pl.BlockSpec((1, N_HEADS, tq), lambda bi, i: (bi, 0, i)),
                  kspec(WIDTH),
                  pl.BlockSpec((1, n_kb_max, WIDTH, tk), lambda bi, i: (bi, 0, 0, 0), pipeline_mode=pl.Buffered(1)),
                  kspec(LANES)],
        out_specs=qspec(WIDTH),
        scratch_shapes=[pltpu.VMEM((n_kb_max, tk, tq), I32), pltpu.VMEM((n_kb_max, tk, tq), F32),
                        pltpu.VMEM((n_kb_max, tk, tq), I16), pltpu.VMEM((n_kb_max, tk, tq), I16),
                        pltpu.VMEM((N_HEADS, LANES, tq), BF16), pltpu.VMEM((N_HEADS, LANES, tq), BF16)]
                       + [pltpu.VMEM((1, tq), F32)] * (2 * N_HEADS) + [pltpu.VMEM((HEAD_DIM, tq), F32)] * N_HEADS,
        compiler_params=_params(("parallel", "arbitrary")),
        name="dsa_attn",
    )(q, iq, iwt, k, vt, ki2)


def _mix_kernel(oa_ref, ob_ref, ga_ref, gb_ref, x_ref, mod_ref, gpost_ref, gpre_ref,
                wa_ref, wb_ref, wo_ref, wq_ref, x1_ref, h2_ref, pq_ref):
    mod = mod_ref[0]
    mixed = ga_ref[...] * _dot(oa_ref[...], wa_ref[...]) + gb_ref[...] * _dot(ob_ref[...], wb_ref[...])
    y = _dot(mixed.astype(BF16), wo_ref[...])
    x1 = x_ref[...] + mod[2:3, :] * _rms(y, gpost_ref[...])
    x1_ref[...] = x1
    h2 = _rms(x1, gpre_ref[...]) * (1.0 + mod[4:5, :]) + mod[3:4, :]
    h2_ref[...] = h2
    pq_ref[...] = _dot(h2.astype(BF16), wq_ref[...]).astype(BF16)


def _mix(oa, ob, ga, gb, x2, mod3, g_post, g_pre, wa, wb, wo, wq, tm, tiles_per_seq):
    n = x2.shape[0]
    row = lambda i: (i, 0)
    const = lambda i: (0, 0)
    nq = wq.shape[1]
    wspec = lambda r, c: pl.BlockSpec((r, c), const, pipeline_mode=pl.Buffered(1))
    return pl.pallas_call(
        _mix_kernel,
        out_shape=[jax.ShapeDtypeStruct((n, D_MODEL), F32),
                   jax.ShapeDtypeStruct((n, D_MODEL), F32),
                   jax.ShapeDtypeStruct((n, nq), BF16)],
        grid=(n // tm,),
        in_specs=[pl.BlockSpec((tm, WIDTH), row), pl.BlockSpec((tm, WIDTH), row),
                  pl.BlockSpec((tm, D_MODEL), row), pl.BlockSpec((tm, D_MODEL), row),
                  pl.BlockSpec((tm, D_MODEL), row),
                  pl.BlockSpec((1, 6, D_MODEL), lambda i: (i // tiles_per_seq, 0, 0)),
                  pl.BlockSpec((1, D_MODEL), const), pl.BlockSpec((1, D_MODEL), const),
                  wspec(WIDTH, D_MODEL), wspec(WIDTH, D_MODEL), wspec(D_MODEL, D_MODEL),
                  wspec(D_MODEL, nq)],
        out_specs=[pl.BlockSpec((tm, D_MODEL), row), pl.BlockSpec((tm, D_MODEL), row),
                   pl.BlockSpec((tm, nq), row)],
        compiler_params=_params(("parallel",)),
        name="mix",
    )(oa, ob, ga, gb, x2, mod3, g_post, g_pre, wa, wb, wo, wq)


def _top16_many(arrays, payloads=None):
    state = list(arrays)
    pos = [lax.broadcasted_iota(I32, a.shape, 0) for a in arrays]
    vals = [[] for _ in arrays]
    picks = [[] for _ in arrays]
    for _ in range(PEER_TOPK):
        for c, s in enumerate(state):
            m = jnp.max(s, axis=0, keepdims=True)
            p = jnp.min(jnp.where(s == m, pos[c], s.shape[0]), axis=0, keepdims=True)
            hit = pos[c] == p
            vals[c].append(m)
            picks[c].append(p if payloads is None
                            else jnp.sum(jnp.where(hit, payloads[c], 0), axis=0, keepdims=True))
            state[c] = jnp.where(hit, -jnp.inf, s)
    return [(jnp.concatenate(v, axis=0), jnp.concatenate(p, axis=0)) for v, p in zip(vals, picks)]


def _peersel_kernel(pq_ref, sk_ref, idx_ref, g_ref, *, heads):
    tt = pq_ref.shape[0]
    scores = []
    for h in range(heads):
        for half in range(2):
            lo = (2 * h + half) * PEER_DHALF
            scores.append(_dot_nt(sk_ref[h, half], pq_ref[:, lo:lo + PEER_DHALF]))
    tops = _top16_many(scores)
    half_k = PEER_TOPK // 2
    row8 = lax.broadcasted_iota(I32, (half_k, tt), 0)
    cands, cidxs = [], []
    for h in range(heads):
        (v1, i1), (v2, i2) = tops[2 * h], tops[2 * h + 1]
        cand = [v1[0:1] + v2]
        cidx = [i1[0:1] * PEER_NKEYS + i2]
        for i in range(1, half_k):
            cand.append(jnp.where(row8 < PEER_TOPK // (i + 1), v1[i:i + 1] + v2[0:half_k], -jnp.inf))
            cidx.append(i1[i:i + 1] * PEER_NKEYS + i2[0:half_k])
        cand.append(v1[half_k:] + v2[0:1])
        cidx.append(i1[half_k:] * PEER_NKEYS + i2[0:1])
        cands.append(jnp.concatenate(cand, axis=0))
        cidxs.append(jnp.concatenate(cidx, axis=0))
    for h, (top, eidx) in enumerate(_top16_many(cands, cidxs)):
        ex = jnp.exp(top - top[0:1])
        g_ref[h] = ex / jnp.sum(ex, axis=0, keepdims=True)
        idx_ref[h] = eidx


def _peer_select(pq, subkeys_b, tt):
    n = pq.shape[0]
    heads = PEER_SELECT_HEADS
    return pl.pallas_call(
        functools.partial(_peersel_kernel, heads=heads),
        out_shape=[jax.ShapeDtypeStruct((PEER_HEADS, PEER_TOPK, n), I32),
                   jax.ShapeDtypeStruct((PEER_HEADS, PEER_TOPK, n), F32)],
        grid=(n // tt, PEER_HEADS // heads),
        in_specs=[pl.BlockSpec((tt, heads * 2 * PEER_DHALF), lambda i, h: (i, h)),
                  pl.BlockSpec((heads, 2, PEER_NKEYS, PEER_DHALF), lambda i, h: (h, 0, 0, 0))],
        out_specs=[pl.BlockSpec((heads, PEER_TOPK, tt), lambda i, h: (h, 0, i)),
                   pl.BlockSpec((heads, PEER_TOPK, tt), lambda i, h: (h, 0, i))],
        compiler_params=_params(("parallel", "arbitrary")),
        name="peer_select",
    )(pq, subkeys_b)


def _erf_gelu(a):
    return 0.5 * a * (1.0 + lax.erf(a * (2.0 ** -0.5)))


def _peerffn_kernel(idx_ref, idx_next_ref, gt_ref, h2_ref, x1_ref, mod_ref, gpost_ref, uv_hbm, o_ref,
                    *scratch, tg, seqs_per_tile):
    bufs, (f_ref, sem) = scratch[:PEER_SLOTS], scratch[PEER_SLOTS:]
    step, last_step = pl.program_id(0), pl.num_programs(0) - 1

    def start_token(t, slot, src=None):
        src = idx_ref if src is None else src
        for k in range(PEER_SEL):
            pltpu.make_async_copy(uv_hbm.at[src[t, k]], bufs[slot].at[k],
                                  sem.at[slot]).start(priority=k % 2)

    def wait_token(slot):
        pltpu.make_async_copy(bufs[slot], bufs[slot], sem.at[slot]).wait()

    lane_tok = lax.broadcasted_iota(I32, (PEER_SEL, tg), 1)
    sub8 = lax.broadcasted_iota(I32, (8, D_MODEL), 0)

    row = lax.broadcasted_iota(I32, (8, LANES), 0)
    low_half, mid_pairs, odd_rows = row < 4, (row % 4) >= 2, (row % 2) == 1

    def fold_group(tiles):
        t_ = [tiles[i] for i in (3, 7, 1, 5, 2, 6, 0, 4)]
        r = [x + pltpu.roll(x, 4, 0) for x in t_]
        y = [jnp.where(low_half, r[2 * m], r[2 * m + 1]) for m in range(4)]
        y = [x + pltpu.roll(x, 2, 0) for x in y]
        z = [jnp.where(mid_pairs, y[2 * n], pltpu.roll(y[2 * n + 1], 6, 0)) for n in range(2)]
        z = [x + pltpu.roll(x, 1, 0) for x in z]
        return jnp.where(odd_rows, z[0], pltpu.roll(z[1], 7, 0))

    def compute(t, slot):
        buf = bufs[slot]
        t8 = pl.multiple_of((t // 8) * 8, 8)
        h8 = h2_ref[pl.ds(t8, 8), :]
        hrow = jnp.sum(jnp.where(sub8 == t - t8, h8, 0.0), axis=0, keepdims=True)
        htile = jnp.concatenate([hrow[:, c * LANES:(c + 1) * LANES] for c in range(N_CHUNK)], axis=0)
        part = jnp.concatenate(
            [fold_group([buf[8 * grp + i, 0:N_CHUNK, :] * htile for i in range(8)]) for grp in range(PEER_SEL // 8)],
            axis=0)
        a = jnp.sum(part, axis=1, keepdims=True)
        g = jnp.sum(jnp.where(lane_tok == t, gt_ref[...], 0.0), axis=1, keepdims=True)
        w = jnp.broadcast_to(g * _erf_gelu(a), (PEER_SEL, LANES))
        accs = [None] * 4
        for k in range(PEER_SEL):
            term = buf[k, N_CHUNK:, :] * jnp.broadcast_to(w[k:k + 1, :], (N_CHUNK, LANES))
            accs[k % 4] = term if accs[k % 4] is None else accs[k % 4] + term
        ftile = (accs[0] + accs[1]) + (accs[2] + accs[3])
        f_ref[pl.ds(t, 1), :] = jnp.concatenate([ftile[c:c + 1, :] for c in range(N_CHUNK)], axis=1)

    @pl.when(step == 0)
    def _():
        for t in range(PEER_AHEAD):
            start_token(t, t)

    def group(i, carry):
        t0 = PEER_SLOTS * i
        for j in range(PEER_SLOTS):
            wait_token(j)
            start_token(t0 + j + PEER_AHEAD, (j + PEER_AHEAD) % PEER_SLOTS)
            compute(t0 + j, j)
        return carry

    n_groups = tg // PEER_SLOTS
    lax.fori_loop(0, n_groups - 1, group, 0)
    t0 = tg - PEER_SLOTS
    for j in range(PEER_SLOTS):
        wait_token(j)
        ahead = j + PEER_AHEAD
        if ahead < PEER_SLOTS:
            start_token(t0 + ahead, ahead)
        else:
            start_token(ahead - PEER_SLOTS, ahead - PEER_SLOTS, idx_next_ref)
        compute(t0 + j, j)

    @pl.when(step == last_step)
    def _():
        for t in range(PEER_AHEAD):
            wait_token(t)

    rows = tg // seqs_per_tile
    for s in range(seqs_per_tile):
        sl = slice(s * rows, (s + 1) * rows)
        o_ref[sl, :] = x1_ref[sl, :] + mod_ref[s][5:6, :] * _rms(f_ref[sl, :], gpost_ref[...])


def _peer_ffn(idx, gt, h2, x1, mod3, g_post, uv, tg, seq_len):
    n = h2.shape[0]
    row = lambda i: (i, 0)
    seqs_per_tile = max(1, tg // seq_len)
    tiles_per_seq = max(1, seq_len // tg)
    return pl.pallas_call(
        functools.partial(_peerffn_kernel, tg=tg, seqs_per_tile=seqs_per_tile),
        out_shape=jax.ShapeDtypeStruct((n, D_MODEL), F32),
        grid=(n // tg,),
        in_specs=[pl.BlockSpec((tg, PEER_SEL), row, memory_space=pltpu.SMEM),
                  pl.BlockSpec((tg, PEER_SEL), lambda i: (jnp.minimum(i + 1, n // tg - 1), 0), memory_space=pltpu.SMEM),
                  pl.BlockSpec((PEER_SEL, tg), lambda i: (0, i)),
                  pl.BlockSpec((tg, D_MODEL), row),
                  pl.BlockSpec((tg, D_MODEL), row),
                  pl.BlockSpec((seqs_per_tile, 6, D_MODEL), lambda i: (i // tiles_per_seq, 0, 0)),
                  pl.BlockSpec((1, D_MODEL), lambda i: (0, 0)),
                  pl.BlockSpec(memory_space=pl.ANY)],
        out_specs=pl.BlockSpec((tg, D_MODEL), row),
        scratch_shapes=[pltpu.VMEM((PEER_SEL, 2 * N_CHUNK, LANES), F32)] * PEER_SLOTS
                       + [pltpu.VMEM((tg, D_MODEL), F32), pltpu.SemaphoreType.DMA((PEER_SLOTS,))],
        compiler_params=_params(("arbitrary",)),
        name="peer_ffn",
    )(idx, idx, gt, h2, x1, mod3, g_post, uv)


def _pad_keys(x, s_pad):
    return jnp.pad(x, ((0, 0), (0, s_pad - x.shape[1]), (0, 0)))


def _layer(x, mod3, past, weights, tiles):
    (g_pre_mix, g_post_mix, g_pre_ffn, g_post_ffn, w_in_b, w_gate_b, wa_b, wb_b, wo_b, wq_b,
     subkeys_b, peer_uv) = weights
    tm, sb_tq, sb_tk, dsa_tq, dsa_tk, tt, tg = tiles
    bsz, t, _ = x.shape
    n = bsz * t
    x2 = x.reshape(n, D_MODEL)
    (aq, ak, av, bq, bk, bv, iq, ki2, akf, avf, bkf, bvf, ikf, iw, ga, gb) = _inproj(
        x2, mod3, g_pre_mix, w_in_b, w_gate_b, tm, t // tm)
    seq = lambda a: a.reshape(bsz, t, a.shape[-1])
    if past is None:
        q_off = 0
        ka, va, kb, vb, ki = seq(ak), seq(av), seq(bk), seq(bv), seq(ki2)
        s_len = t
    else:
        p_sbk, p_sbv, p_dk, p_dv, p_ki = past
        q_off = p_sbk.shape[1]
        s_len = q_off + t
        flat = lambda c: c.reshape(bsz, q_off, WIDTH).astype(BF16)
        ka = jnp.concatenate([flat(p_sbk), seq(ak)], axis=1)
        va = jnp.concatenate([flat(p_sbv), seq(av)], axis=1)
        kb = jnp.concatenate([flat(p_dk), seq(bk)], axis=1)
        vb = jnp.concatenate([flat(p_dv), seq(bv)], axis=1)
        pk = p_ki.astype(BF16)
        ki = jnp.concatenate([jnp.concatenate([pk, pk], axis=-1), seq(ki2)], axis=1)
    s_pad = -(-s_len // max(sb_tk, dsa_tk)) * max(sb_tk, dsa_tk)
    ka, va, kb, vb, ki = (_pad_keys(a, s_pad) for a in (ka, va, kb, vb, ki))
    n_sel = min(TOPK_MAX, s_len // 4)

    oa = _sb_attention(seq(aq), ka, va, sb_tq, sb_tk, q_off)
    ob = _dsa_attention(seq(bq), seq(iq), seq(iw), kb, vb, ki, dsa_tq, dsa_tk, q_off, n_sel)

    x1, h2, pq = _mix(oa.reshape(n, WIDTH), ob.reshape(n, WIDTH), ga, gb, x2, mod3, g_post_mix, g_pre_ffn,
                      wa_b, wb_b, wo_b, wq_b, tm, t // tm)
    eidx, gate = _peer_select(pq, subkeys_b, tt)
    eidx = eidx.reshape(PEER_SEL, n).T
    y = _peer_ffn(eidx, gate.reshape(PEER_SEL, n), h2, x1, mod3, g_post_ffn, peer_uv, tg, t)
    new = tuple(a.reshape(1, bsz, t, N_HEADS, HEAD_DIM) for a in (akf, avf, bkf, bvf)) + (
        ikf.reshape(1, bsz, t, HEAD_DIM),)
    return y.reshape(bsz, t, D_MODEL), new


def _tiles(t):
    if t % 256 == 0:
        return (256, 512 if t % 512 == 0 else 256, 256, 256, 512, 128, 128)
    return (t, t, 128, t, 128, 128, 128)


def kernel(x_prompt, x_sample, c_prompt, c_sample, cache_sb_k, cache_sb_v, cache_dsa_k, cache_dsa_v, cache_dsa_kidx, w_ada, b_ada, g_pre_mix, g_post_mix, g_pre_ffn, g_post_ffn, w_in, w_gate, w_branch_a, w_branch_b, w_out, w_peer_q, peer_subkeys, peer_u, peer_v):
    assert w_ada.shape[0] == 1, "one layer"
    n_p, n_s = c_prompt.shape[0], c_sample.shape[0]
    c_all = jnp.concatenate([c_prompt, c_sample], axis=0)
    rows = -(-c_all.shape[0] // 8) * 8
    c_all = jnp.pad(c_all, ((0, rows - c_all.shape[0]), (0, 0)))
    mod = _modulation(c_all, w_ada[0], b_ada[0]).reshape(rows, 6, D_MODEL)
    w_in_b = jnp.pad(w_in[0], ((0, 0), (0, W_IN_PAD - W_IN_COLS))).astype(BF16)
    weights = (g_pre_mix, g_post_mix, g_pre_ffn, g_post_ffn, w_in_b, w_gate[0].astype(BF16),
               w_branch_a[0].astype(BF16), w_branch_b[0].astype(BF16), w_out[0].astype(BF16),
               w_peer_q[0].astype(BF16), peer_subkeys[0].astype(BF16),
               jnp.concatenate([peer_u[0].reshape(-1, N_CHUNK, LANES),
                                peer_v[0].reshape(-1, N_CHUNK, LANES)], axis=1))
    yp, new_p = _layer(x_prompt, mod[:n_p], None, weights, _tiles(x_prompt.shape[1]))
    past = (cache_sb_k[0], cache_sb_v[0], cache_dsa_k[0], cache_dsa_v[0], cache_dsa_kidx[0])
    ys, new_s = _layer(x_sample, mod[n_p:n_p + n_s], past, weights, _tiles(x_sample.shape[1]))
    return (yp, ys) + new_p + new_s
```

```python
import functools
import math

import jax
import jax.numpy as jnp
import numpy as np
from jax import lax
from jax.experimental import pallas as pl
from jax.experimental.pallas import tpu as pltpu

F32 = jnp.float32
BF16 = jnp.bfloat16
I32 = jnp.int32
I16 = jnp.int16

D_MODEL = 1024
CHUNK = 64
EPS = 1e-6
N_HEADS = 8
HEAD_DIM = 64
WIDTH = N_HEADS * HEAD_DIM
N_PAIRS = N_HEADS // 2
SB_PAIRS_PER_STEP = 4
LANES = 128
N_CHUNK = D_MODEL // LANES
TOPK_MAX = 256
PEER_HEADS = 8
PEER_NKEYS = 128
PEER_DHALF = 128
PEER_TOPK = 16
PEER_SEL = PEER_HEADS * PEER_TOPK
PEER_SELECT_HEADS = 4
PEER_SLOTS = 8
PEER_AHEAD = 6
W_IN_COLS = 7 * WIDTH + HEAD_DIM + N_HEADS
W_IN_PAD = 7 * WIDTH + LANES
INT_MIN = -(2 ** 31)
HALF_BIAS = 2 ** 15
NEG_BIG = -1e30
DIST_MASKED = 1e30
LOG2E = 1.4426950408889634
SOFTPLUS_LINEAR = 64.0
VMEM_LIMIT = 56 * 1024 * 1024


def _params(sem):
    return pltpu.CompilerParams(dimension_semantics=sem, vmem_limit_bytes=VMEM_LIMIT)


def _dot(a, b):
    return jnp.dot(a, b, preferred_element_type=F32)


def _dot_nt(a, b):
    return lax.dot_general(a, b, (((1,), (1,)), ((), ())), preferred_element_type=F32)


def _rms(x, gain):
    return x * lax.rsqrt(jnp.mean(x * x, axis=-1, keepdims=True) + EPS) * gain


def _mod_kernel(c_ref, w_ref, b_ref, o_ref):
    c = c_ref[...]
    s = c * (1.0 / (1.0 + jnp.exp(-c)))
    o_ref[...] = jnp.dot(s, w_ref[...], preferred_element_type=F32,
                         precision=lax.Precision.HIGHEST) + b_ref[...]


def _modulation(c, w_ada, b_ada):
    n = c.shape[0]
    cols = w_ada.shape[1]
    tn = 1024
    return pl.pallas_call(
        _mod_kernel,
        out_shape=jax.ShapeDtypeStruct((n, cols), F32),
        grid=(cols // tn,),
        in_specs=[pl.BlockSpec((n, D_MODEL), lambda j: (0, 0)),
                  pl.BlockSpec((D_MODEL, tn), lambda j: (0, j)),
                  pl.BlockSpec((1, tn), lambda j: (0, j))],
        out_specs=pl.BlockSpec((n, tn), lambda j: (0, j)),
        compiler_params=_params(("arbitrary",)),
        name="mod",
    )(c, w_ada, b_ada.reshape(1, cols))


def _inproj_kernel(x_ref, mod_ref, g_ref, win_ref, wg_ref,
                   aq_ref, ak_ref, av_ref, bq_ref, bk_ref, bv_ref, iq_ref, ki2_ref,
                   akf_ref, avf_ref, bkf_ref, bvf_ref, ikf_ref, iw_ref, ga_ref, gb_ref):
    x = x_ref[...]
    mod = mod_ref[0]
    h = _rms(x, g_ref[...]) * (1.0 + mod[1:2, :]) + mod[0:1, :]
    hb = h.astype(BF16)
    qscale = HEAD_DIM ** -0.5

    def seg(i):
        return _dot(hb, win_ref[:, i * WIDTH:(i + 1) * WIDTH])

    aq_ref[...] = (seg(0) * (qscale * LOG2E)).astype(BF16)
    z = seg(1)
    akf_ref[...] = z
    ak_ref[...] = z.astype(BF16)
    z = seg(2)
    avf_ref[...] = z
    av_ref[...] = z.astype(BF16)
    bq_ref[...] = (seg(3) * (qscale * LOG2E)).astype(BF16)
    z = seg(4)
    bkf_ref[...] = z
    bk_ref[...] = z.astype(BF16)
    z = seg(5)
    bvf_ref[...] = z
    bv_ref[...] = z.astype(BF16)
    iq_ref[...] = (seg(6) * qscale).astype(BF16)
    tail = _dot(hb, win_ref[:, 7 * WIDTH:7 * WIDTH + LANES])
    ik = tail[:, :HEAD_DIM]
    ikf_ref[...] = ik
    ki2_ref[...] = jnp.concatenate([ik, ik], axis=-1).astype(BF16)
    iw_ref[...] = tail[:, HEAD_DIM:HEAD_DIM + N_HEADS] * (N_HEADS ** -0.5)
    zg = _dot(hb, wg_ref[...])
    gate = 1.0 / (1.0 + jnp.exp(-zg))
    ga_ref[...] = gate[:, :D_MODEL]
    gb_ref[...] = gate[:, D_MODEL:]


def _inproj(x2, mod3, g_pre, w_in_b, w_gate_b, tm, tiles_per_seq):
    n = x2.shape[0]
    row = lambda i: (i, 0)
    const = lambda i: (0, 0)
    bspec = lambda w: pl.BlockSpec((tm, w), row)
    outs = ([jax.ShapeDtypeStruct((n, WIDTH), BF16)] * 7
            + [jax.ShapeDtypeStruct((n, LANES), BF16)]
            + [jax.ShapeDtypeStruct((n, WIDTH), F32)] * 4
            + [jax.ShapeDtypeStruct((n, HEAD_DIM), F32),
               jax.ShapeDtypeStruct((n, N_HEADS), F32),
               jax.ShapeDtypeStruct((n, D_MODEL), F32),
               jax.ShapeDtypeStruct((n, D_MODEL), F32)])
    out_specs = ([bspec(WIDTH)] * 7 + [bspec(LANES)] + [bspec(WIDTH)] * 4
                 + [bspec(HEAD_DIM), bspec(N_HEADS), bspec(D_MODEL), bspec(D_MODEL)])
    return pl.pallas_call(
        _inproj_kernel,
        out_shape=outs,
        grid=(n // tm,),
        in_specs=[pl.BlockSpec((tm, D_MODEL), row),
                  pl.BlockSpec((1, 6, D_MODEL), lambda i: (i // tiles_per_seq, 0, 0)),
                  pl.BlockSpec((1, D_MODEL), const),
                  pl.BlockSpec((D_MODEL, W_IN_PAD), const, pipeline_mode=pl.Buffered(1)),
                  pl.BlockSpec((D_MODEL, 2 * D_MODEL), const, pipeline_mode=pl.Buffered(1))],
        out_specs=out_specs,
        compiler_params=_params(("parallel",)),
        name="inproj",
    )(x2, mod3, g_pre, w_in_b, w_gate_b)


def _sb_kernel(tab_ref, q_ref, k_ref, v_ref, tri_ref, o_ref, *scratch, tq, tk, q_off, pairs):
    s = pl.program_id(2)
    qi = tab_ref[0, s]
    kj = tab_ref[1, s]
    first = tab_ref[2, s]
    last = tab_ref[3, s]
    masked = tab_ref[4, s]
    n_heads = 2 * pairs
    acc_refs, run_refs = scratch[:n_heads], scratch[n_heads:]

    @pl.when(first == 1)
    def _():
        for r in scratch:
            r[...] = jnp.zeros_like(r)

    lo_lanes = lax.broadcasted_iota(I32, (tq, LANES), 1) < HEAD_DIM

    def step(use_mask):
        tri = tri_ref[...]
        if use_mask:
            qpos = q_off + qi * tq + lax.broadcasted_iota(I32, (tq, tk), 0)
            kpos = kj * tk + lax.broadcasted_iota(I32, (tq, tk), 1)
            causal = kpos < qpos
        z = []
        for p in range(pairs):
            q = q_ref[0, :, p * LANES:(p + 1) * LANES]
            k = k_ref[0, :, p * LANES:(p + 1) * LANES]
            zero = jnp.zeros_like(q)
            z += [_dot_nt(jnp.where(lo_lanes, q, zero), k), _dot_nt(jnp.where(lo_lanes, zero, q), k)]
        suf = []
        for h in range(n_heads):
            sp = jnp.maximum(z[h], jnp.log2(1.0 + jnp.exp2(jnp.minimum(z[h], SOFTPLUS_LINEAR))))
            if use_mask:
                sp = jnp.where(causal, sp, 0.0)
            hi = sp.astype(BF16)
            lo = (sp - hi.astype(F32)).astype(BF16)
            suf.append(_dot(hi, tri) + _dot(lo, tri) + run_refs[h][:, 0:1])
        for h in range(n_heads):
            w = jnp.exp2(z[h] - suf[h])
            if use_mask:
                w = jnp.where(causal, w, 0.0)
            p = h // 2
            acc_refs[h][...] += _dot(w.astype(BF16), v_ref[0, :, p * LANES:(p + 1) * LANES])
            run_refs[h][...] = jnp.broadcast_to(suf[h][:, 0:1], (tq, LANES))

    @pl.when(masked == 1)
    def _():
        step(True)

    @pl.when(masked == 0)
    def _():
        step(False)

    @pl.when(last == 1)
    def _():
        for p in range(pairs):
            o_ref[0, :, p * LANES:(p + 1) * LANES] = jnp.where(
                lo_lanes, acc_refs[2 * p][...], acc_refs[2 * p + 1][...]).astype(o_ref.dtype)


def _sb_table(nq, nk, tq, tk, q_off):
    rows = []
    for qi in range(nq):
        q_lo = q_off + qi * tq
        q_hi = q_lo + tq - 1
        kjs = [kj for kj in range(nk) if kj * tk < q_hi]
        if not kjs:
            kjs = [0]
        kjs = kjs[::-1]
        for n, kj in enumerate(kjs):
            masked = int(kj * tk + tk - 1 >= q_lo)
            rows.append((qi, kj, int(n == 0), int(n == len(kjs) - 1), masked))
    return np.asarray(rows, dtype=np.int32).T.copy()


def _sb_attention(q, k, v, tq, tk, q_off):
    b, t, _ = q.shape
    s_len = k.shape[1]
    tab = _sb_table(t // tq, s_len // tk, tq, tk, q_off)
    tri = jnp.asarray(np.tril(np.ones((tk, tk), np.float32)), dtype=BF16)
    pairs = SB_PAIRS_PER_STEP
    width = pairs * LANES
    grid_spec = pltpu.PrefetchScalarGridSpec(
        num_scalar_prefetch=1,
        grid=(b, N_PAIRS // pairs, tab.shape[1]),
        in_specs=[pl.BlockSpec((1, tq, width), lambda bi, hp, s, tab: (bi, tab[0, s], hp)),
                  pl.BlockSpec((1, tk, width), lambda bi, hp, s, tab: (bi, tab[1, s], hp)),
                  pl.BlockSpec((1, tk, width), lambda bi, hp, s, tab: (bi, tab[1, s], hp)),
                  pl.BlockSpec((tk, tk), lambda bi, hp, s, tab: (0, 0))],
        out_specs=pl.BlockSpec((1, tq, width), lambda bi, hp, s, tab: (bi, tab[0, s], hp)),
        scratch_shapes=[pltpu.VMEM((tq, LANES), F32)] * (4 * pairs),
    )
    return pl.pallas_call(
        functools.partial(_sb_kernel, tq=tq, tk=tk, q_off=q_off, pairs=pairs),
        out_shape=jax.ShapeDtypeStruct((b, t, WIDTH), BF16),
        grid_spec=grid_spec,
        compiler_params=_params(("parallel", "parallel", "arbitrary")),
        name="sb_attn",
    )(jnp.asarray(tab), q, k, v, tri)


def _sortable(x):
    bits = pltpu.bitcast(x + 0.0, I32)
    return jnp.where(bits < 0, bits ^ 0x7FFFFFFF, bits)


def _alibi_slope2(h):
    return LOG2E * 2.0 ** (-8.0 * (h + 1) / N_HEADS)


def _dsa_kernel(q_ref, iq_ref, iwt_ref, k_ref, vt_ref, ki_ref, o_ref, keys_ref, dm_ref, hi_ref, lo_ref, qt_ref, iqt_ref,
                *stats,
                 tq, tk, q_off, n_kb_max, n_sel, idx_bits):
    m_refs, l_refs, acc_refs = stats[:N_HEADS], stats[N_HEADS:2 * N_HEADS], stats[2 * N_HEADS:]
    qi = pl.program_id(1)
    q_lo = q_off + qi * tq
    adm_end = ((q_lo + tq - 1) // CHUNK + 1) * CHUNK
    n_kb = jnp.minimum((adm_end + tk - 1) // tk, n_kb_max)

    lo_half = lax.broadcasted_iota(I32, (tq, LANES), 1) < HEAD_DIM
    qpos = q_lo + lax.broadcasted_iota(I32, (tk, tq), 1)
    krow = lax.broadcasted_iota(I32, (tk, tq), 0)

    def heads_of(x_ref):
        out = []
        for hp in range(N_PAIRS):
            pair = x_ref[0, :, hp * LANES:(hp + 1) * LANES]
            zero = jnp.zeros_like(pair)
            out += [jnp.where(lo_half, pair, zero), jnp.where(lo_half, zero, pair)]
        return out

    for src, dst in ((iq_ref, iqt_ref), (q_ref, qt_ref)):
        for h, xe in enumerate(heads_of(src)):
            dst[h] = xe.astype(F32).T.astype(BF16)

    def score_block(j, carry):
        ki = ki_ref[0, pl.ds(pl.multiple_of(j * tk, tk), tk), :]
        score = jnp.zeros((tk, tq), F32)
        for h in range(N_HEADS):
            score = score + iwt_ref[0, h:h + 1, :] * jnp.maximum(_dot(ki, iqt_ref[h]), 0.0)
        adm = ((j * tk + krow) // CHUNK) <= (qpos // CHUNK)
        key = jnp.where(adm, _sortable(score), INT_MIN)
        keys_ref[j] = key
        hi_ref[j] = (key >> 16).astype(I16)
        lo_ref[j] = ((key & 0xFFFF) - HALF_BIAS).astype(I16)
        return carry

    lax.fori_loop(0, n_kb, score_block, 0)

    def count_cols(pred_fn):
        def body(j, acc):
            hit = pred_fn(keys_ref[j], j).astype(I32)
            return acc + jnp.sum(hit.reshape(tk // 8, 8, tq), axis=0)
        acc = lax.fori_loop(0, n_kb, body, jnp.zeros((8, tq), I32))
        return jnp.sum(acc, axis=0, keepdims=True)

    def count_half(ref, pred_fn):
        def body(j, acc):
            hit = jnp.where(pred_fn(ref[j]), jnp.int16(1), jnp.int16(0))
            for r in range(tk // 16):
                acc = acc + hit[r * 16:(r + 1) * 16, :]
            return acc
        acc = lax.fori_loop(0, n_kb, body, jnp.zeros((16, tq), I16))
        return jnp.sum(acc.astype(I32), axis=0, keepdims=True)

    def search_half(ref, target):
        def bit_step(i, t):
            cand = t + jnp.left_shift(jnp.int32(1), 15 - i)
            cand16 = cand.astype(I16)
            cnt = count_half(ref, lambda kb: kb >= cand16)
            return jnp.where(cnt >= target, cand, t)
        return lax.fori_loop(0, 16, bit_step, jnp.full((1, tq), -HALF_BIAS, I32))

    t_hi = search_half(hi_ref, n_sel)
    t_hi16 = t_hi.astype(I16)
    n_above = count_half(hi_ref, lambda kb: kb > t_hi16)

    def keep_group(j, carry):
        lo_ref[j] = jnp.where(hi_ref[j] == t_hi16, lo_ref[j], jnp.int16(-HALF_BIAS))
        return carry

    lax.fori_loop(0, n_kb, keep_group, 0)
    t_lo = search_half(lo_ref, n_sel - n_above)
    thr = t_hi * (2 * HALF_BIAS) + (t_lo + HALF_BIAS)
    n_gt = count_cols(lambda kb, j: kb > thr)
    n_ge = count_cols(lambda kb, j: kb >= thr)
    need = n_sel - n_gt
    real = thr > INT_MIN
    tie = jnp.max(jnp.where(real & (n_ge > n_sel), 1, 0)) > 0

    def tie_cut():
        def idx_step(i, cut):
            cand = cut + jnp.left_shift(jnp.int32(1), idx_bits - 1 - i)
            cnt = count_cols(lambda kb, j: (kb == thr) & ((j * tk + krow) < cand))
            return jnp.where(cnt < need, cand, cut)
        return lax.fori_loop(0, idx_bits, idx_step, jnp.zeros((1, tq), I32))

    @pl.when(tie)
    def _():
        cut = tie_cut()

        def mask_block(j, carry):
            kb = keys_ref[j]
            kpos = j * tk + krow
            sel = (kb > thr) | ((kb == thr) & real & (kpos <= cut))
            dm_ref[j] = jnp.where(sel, jnp.abs(qpos - kpos).astype(F32), DIST_MASKED)
            return carry

        lax.fori_loop(0, n_kb, mask_block, 0)

    @pl.when(jnp.logical_not(tie))
    def _():
        lowest = jnp.where(real, thr, INT_MIN + 1)

        def mask_block(j, carry):
            kpos = j * tk + krow
            dm_ref[j] = jnp.where(keys_ref[j] >= lowest, jnp.abs(qpos - kpos).astype(F32), DIST_MASKED)
            return carry

        lax.fori_loop(0, n_kb, mask_block, 0)

    for h in range(N_HEADS):
        m_refs[h][...] = jnp.full_like(m_refs[h], NEG_BIG)
        l_refs[h][...] = jnp.zeros_like(l_refs[h])
        acc_refs[h][...] = jnp.zeros_like(acc_refs[h])

    def attn_block(j, carry):
        start = pl.multiple_of(j * tk, tk)
        dm = dm_ref[j]

        def qk(h):
            return _dot(k_ref[0, pl.ds(start, tk), (h // 2) * LANES:(h // 2 + 1) * LANES], qt_ref[h])

        ahead = 4
        pending = [qk(h) for h in range(ahead)]
        for hp in range(N_PAIRS):
            for e in range(2):
                h = 2 * hp + e
                s_this = pending.pop(0)
                if h + ahead < N_HEADS:
                    pending.append(qk(h + ahead))
                t2 = s_this - _alibi_slope2(h) * dm
                m_old = m_refs[h][...]
                m_new = jnp.maximum(m_old, jnp.max(t2, axis=0, keepdims=True))
                alpha = jnp.exp2(m_old - m_new)
                p = jnp.exp2(t2 - m_new)
                l_refs[h][...] = alpha * l_refs[h][...] + jnp.sum(p, axis=0, keepdims=True)
                vt = vt_ref[0, j, hp * LANES + e * HEAD_DIM:hp * LANES + (e + 1) * HEAD_DIM, :]
                acc_refs[h][...] = alpha * acc_refs[h][...] + _dot(vt, p.astype(BF16))
                m_refs[h][...] = m_new
        return carry

    lax.fori_loop(0, n_kb, attn_block, 0)

    for hp in range(N_PAIRS):
        ot = jnp.concatenate([acc_refs[2 * hp][...] / l_refs[2 * hp][...],
                              acc_refs[2 * hp + 1][...] / l_refs[2 * hp + 1][...]], axis=0)
        o_ref[0, :, hp * LANES:(hp + 1) * LANES] = ot.T.astype(o_ref.dtype)


def _dsa_attention(q, iq, iw, k, v, ki2, tq, tk, q_off, n_sel):
    b, t, _ = q.shape
    s_len = k.shape[1]
    n_kb_max = s_len // tk
    iwt = jnp.swapaxes(iw, 1, 2)
    vt = jnp.swapaxes(v.reshape(b, n_kb_max, tk, WIDTH), 2, 3)
    qspec = lambda w: pl.BlockSpec((1, tq, w), lambda bi, i: (bi, i, 0))
    kspec = lambda w: pl.BlockSpec((1, s_len, w), lambda bi, i: (bi, 0, 0), pipeline_mode=pl.Buffered(1))
    return pl.pallas_call(
        functools.partial(_dsa_kernel, tq=tq, tk=tk, q_off=q_off, n_kb_max=n_kb_max, n_sel=n_sel,
                          idx_bits=max(1, int(math.ceil(math.log2(s_len + 1))))),
        out_shape=jax.ShapeDtypeStruct((b, t, WIDTH), BF16),
        grid=(b, t // tq),
        in_specs=[qspec(WIDTH), qspec(WIDTH),
                  pl.BlockSpec((1, N_HEADS, tq), lambda bi, i: (bi, 0, i)),
                  kspec(WIDTH),
                  pl.BlockSpec((1, n_kb_max, WIDTH, tk), lambda bi, i: (bi, 0, 0, 0), pipeline_mode=pl.Buffered(1)),
                  kspec(LANES)],
        out_specs=qspec(WIDTH),
        scratch_shapes=[pltpu.VMEM((n_kb_max, tk, tq), I32), pltpu.VMEM((n_kb_max, tk, tq), F32),
                        pltpu.VMEM((n_kb_max, tk, tq), I16), pltpu.VMEM((n_kb_max, tk, tq), I16),
                        pltpu.VMEM((N_HEADS, LANES, tq), BF16), pltpu.VMEM((N_HEADS, LANES, tq), BF16)]
                       + [pltpu.VMEM((1, tq), F32)] * (2 * N_HEADS) + [pltpu.VMEM((HEAD_DIM, tq), F32)] * N_HEADS,
        compiler_params=_params(("parallel", "arbitrary")),
        name="dsa_attn",
    )(q, iq, iwt, k, vt, ki2)


def _mix_kernel(oa_ref, ob_ref, ga_ref, gb_ref, x_ref, mod_ref, gpost_ref, gpre_ref,
                wa_ref, wb_ref, wo_ref, wq_ref, x1_ref, h2_ref, pq_ref):
    mod = mod_ref[0]
    mixed = ga_ref[...] * _dot(oa_ref[...], wa_ref[...]) + gb_ref[...] * _dot(ob_ref[...], wb_ref[...])
    y = _dot(mixed.astype(BF16), wo_ref[...])
    x1 = x_ref[...] + mod[2:3, :] * _rms(y, gpost_ref[...])
    x1_ref[...] = x1
    h2 = _rms(x1, gpre_ref[...]) * (1.0 + mod[4:5, :]) + mod[3:4, :]
    h2_ref[...] = h2
    pq_ref[...] = _dot(h2.astype(BF16), wq_ref[...]).astype(BF16)


def _mix(oa, ob, ga, gb, x2, mod3, g_post, g_pre, wa, wb, wo, wq, tm, tiles_per_seq):
    n = x2.shape[0]
    row = lambda i: (i, 0)
    const = lambda i: (0, 0)
    nq = wq.shape[1]
    wspec = lambda r, c: pl.BlockSpec((r, c), const, pipeline_mode=pl.Buffered(1))
    return pl.pallas_call(
        _mix_kernel,
        out_shape=[jax.ShapeDtypeStruct((n, D_MODEL), F32),
                   jax.ShapeDtypeStruct((n, D_MODEL), F32),
                   jax.ShapeDtypeStruct((n, nq), BF16)],
        grid=(n // tm,),
        in_specs=[pl.BlockSpec((tm, WIDTH), row), pl.BlockSpec((tm, WIDTH), row),
                  pl.BlockSpec((tm, D_MODEL), row), pl.BlockSpec((tm, D_MODEL), row),
                  pl.BlockSpec((tm, D_MODEL), row),
                  pl.BlockSpec((1, 6, D_MODEL), lambda i: (i // tiles_per_seq, 0, 0)),
                  pl.BlockSpec((1, D_MODEL), const), pl.BlockSpec((1, D_MODEL), const),
                  wspec(WIDTH, D_MODEL), wspec(WIDTH, D_MODEL), wspec(D_MODEL, D_MODEL),
                  wspec(D_MODEL, nq)],
        out_specs=[pl.BlockSpec((tm, D_MODEL), row), pl.BlockSpec((tm, D_MODEL), row),
                   pl.BlockSpec((tm, nq), row)],
        compiler_params=_params(("parallel",)),
        name="mix",
    )(oa, ob, ga, gb, x2, mod3, g_post, g_pre, wa, wb, wo, wq)


def _top16_many(arrays, payloads=None):
    state = list(arrays)
    pos = [lax.broadcasted_iota(I32, a.shape, 0) for a in arrays]
    vals = [[] for _ in arrays]
    picks = [[] for _ in arrays]
    for _ in range(PEER_TOPK):
        for c, s in enumerate(state):
            m = jnp.max(s, axis=0, keepdims=True)
            p = jnp.min(jnp.where(s == m, pos[c], s.shape[0]), axis=0, keepdims=True)
            hit = pos[c] == p
            vals[c].append(m)
            picks[c].append(p if payloads is None
                            else jnp.sum(jnp.where(hit, payloads[c], 0), axis=0, keepdims=True))
            state[c] = jnp.where(hit, -jnp.inf, s)
    return [(jnp.concatenate(v, axis=0), jnp.concatenate(p, axis=0)) for v, p in zip(vals, picks)]


def _peersel_kernel(pq_ref, sk_ref, idx_ref, g_ref, *, heads):
    tt = pq_ref.shape[0]
    scores = []
    for h in range(heads):
        for half in range(2):
            lo = (2 * h + half) * PEER_DHALF
            scores.append(_dot_nt(sk_ref[h, half], pq_ref[:, lo:lo + PEER_DHALF]))
    tops = _top16_many(scores)
    half_k = PEER_TOPK // 2
    row8 = lax.broadcasted_iota(I32, (half_k, tt), 0)
    cands, cidxs = [], []
    for h in range(heads):
        (v1, i1), (v2, i2) = tops[2 * h], tops[2 * h + 1]
        cand = [v1[0:1] + v2]
        cidx = [i1[0:1] * PEER_NKEYS + i2]
        for i in range(1, half_k):
            cand.append(jnp.where(row8 < PEER_TOPK // (i + 1), v1[i:i + 1] + v2[0:half_k], -jnp.inf))
            cidx.append(i1[i:i + 1] * PEER_NKEYS + i2[0:half_k])
        cand.append(v1[half_k:] + v2[0:1])
        cidx.append(i1[half_k:] * PEER_NKEYS + i2[0:1])
        cands.append(jnp.concatenate(cand, axis=0))
        cidxs.append(jnp.concatenate(cidx, axis=0))
    for h, (top, eidx) in enumerate(_top16_many(cands, cidxs)):
        ex = jnp.exp(top - top[0:1])
        g_ref[h] = ex / jnp.sum(ex, axis=0, keepdims=True)
        idx_ref[h] = eidx


def _peer_select(pq, subkeys_b, tt):
    n = pq.shape[0]
    heads = PEER_SELECT_HEADS
    return pl.pallas_call(
        functools.partial(_peersel_kernel, heads=heads),
        out_shape=[jax.ShapeDtypeStruct((PEER_HEADS, PEER_TOPK, n), I32),
                   jax.ShapeDtypeStruct((PEER_HEADS, PEER_TOPK, n), F32)],
        grid=(n // tt, PEER_HEADS // heads),
        in_specs=[pl.BlockSpec((tt, heads * 2 * PEER_DHALF), lambda i, h: (i, h)),
                  pl.BlockSpec((heads, 2, PEER_NKEYS, PEER_DHALF), lambda i, h: (h, 0, 0, 0))],
        out_specs=[pl.BlockSpec((heads, PEER_TOPK, tt), lambda i, h: (h, 0, i)),
                   pl.BlockSpec((heads, PEER_TOPK, tt), lambda i, h: (h, 0, i))],
        compiler_params=_params(("parallel", "arbitrary")),
        name="peer_select",
    )(pq, subkeys_b)


def _erf_gelu(a):
    return 0.5 * a * (1.0 + lax.erf(a * (2.0 ** -0.5)))


def _peerffn_kernel(idx_ref, idx_next_ref, gt_ref, h2_ref, x1_ref, mod_ref, gpost_ref, uv_hbm, o_ref,
                    *scratch, tg, seqs_per_tile):
    bufs, (f_ref, sem) = scratch[:PEER_SLOTS], scratch[PEER_SLOTS:]
    step, last_step = pl.program_id(0), pl.num_programs(0) - 1

    def start_token(t, slot, src=None):
        src = idx_ref if src is None else src
        for k in range(PEER_SEL):
            pltpu.make_async_copy(uv_hbm.at[src[t, k]], bufs[slot].at[k],
                                  sem.at[slot]).start(priority=k % 2)

    def wait_token(slot):
        pltpu.make_async_copy(bufs[slot], bufs[slot], sem.at[slot]).wait()

    lane_tok = lax.broadcasted_iota(I32, (PEER_SEL, tg), 1)
    sub8 = lax.broadcasted_iota(I32, (8, D_MODEL), 0)

    row = lax.broadcasted_iota(I32, (8, LANES), 0)
    low_half, mid_pairs, odd_rows = row < 4, (row % 4) >= 2, (row % 2) == 1

    def fold_group(tiles):
        t_ = [tiles[i] for i in (3, 7, 1, 5, 2, 6, 0, 4)]
        r = [x + pltpu.roll(x, 4, 0) for x in t_]
        y = [jnp.where(low_half, r[2 * m], r[2 * m + 1]) for m in range(4)]
        y = [x + pltpu.roll(x, 2, 0) for x in y]
        z = [jnp.where(mid_pairs, y[2 * n], pltpu.roll(y[2 * n + 1], 6, 0)) for n in range(2)]
        z = [x + pltpu.roll(x, 1, 0) for x in z]
        return jnp.where(odd_rows, z[0], pltpu.roll(z[1], 7, 0))

    def compute(t, slot):
        buf = bufs[slot]
        t8 = pl.multiple_of((t // 8) * 8, 8)
        h8 = h2_ref[pl.ds(t8, 8), :]
        hrow = jnp.sum(jnp.where(sub8 == t - t8, h8, 0.0), axis=0, keepdims=True)
        htile = jnp.concatenate([hrow[:, c * LANES:(c + 1) * LANES] for c in range(N_CHUNK)], axis=0)
        part = jnp.concatenate(
            [fold_group([buf[8 * grp + i, 0:N_CHUNK, :] * htile for i in range(8)]) for grp in range(PEER_SEL // 8)],
            axis=0)
        a = jnp.sum(part, axis=1, keepdims=True)
        g = jnp.sum(jnp.where(lane_tok == t, gt_ref[...], 0.0), axis=1, keepdims=True)
        w = jnp.broadcast_to(g * _erf_gelu(a), (PEER_SEL, LANES))
        accs = [None] * 4
        for k in range(PEER_SEL):
            term = buf[k, N_CHUNK:, :] * jnp.broadcast_to(w[k:k + 1, :], (N_CHUNK, LANES))
            accs[k % 4] = term if accs[k % 4] is None else accs[k % 4] + term
        ftile = (accs[0] + accs[1]) + (accs[2] + accs[3])
        f_ref[pl.ds(t, 1), :] = jnp.concatenate([ftile[c:c + 1, :] for c in range(N_CHUNK)], axis=1)

    @pl.when(step == 0)
    def _():
        for t in range(PEER_AHEAD):
            start_token(t, t)

    def group(i, carry):
        t0 = PEER_SLOTS * i
        for j in range(PEER_SLOTS):
            wait_token(j)
            start_token(t0 + j + PEER_AHEAD, (j + PEER_AHEAD) % PEER_SLOTS)
            compute(t0 + j, j)
        return carry

    n_groups = tg // PEER_SLOTS
    lax.fori_loop(0, n_groups - 1, group, 0)
    t0 = tg - PEER_SLOTS
    for j in range(PEER_SLOTS):
        wait_token(j)
        ahead = j + PEER_AHEAD
        if ahead < PEER_SLOTS:
            start_token(t0 + ahead, ahead)
        else:
            start_token(ahead - PEER_SLOTS, ahead - PEER_SLOTS, idx_next_ref)
        compute(t0 + j, j)

    @pl.when(step == last_step)
    def _():
        for t in range(PEER_AHEAD):
            wait_token(t)

    rows = tg // seqs_per_tile
    for s in range(seqs_per_tile):
        sl = slice(s * rows, (s + 1) * rows)
        o_ref[sl, :] = x1_ref[sl, :] + mod_ref[s][5:6, :] * _rms(f_ref[sl, :], gpost_ref[...])


def _peer_ffn(idx, gt, h2, x1, mod3, g_post, uv, tg, seq_len):
    n = h2.shape[0]
    row = lambda i: (i, 0)
    seqs_per_tile = max(1, tg // seq_len)
    tiles_per_seq = max(1, seq_len // tg)
    return pl.pallas_call(
        functools.partial(_peerffn_kernel, tg=tg, seqs_per_tile=seqs_per_tile),
        out_shape=jax.ShapeDtypeStruct((n, D_MODEL), F32),
        grid=(n // tg,),
        in_specs=[pl.BlockSpec((tg, PEER_SEL), row, memory_space=pltpu.SMEM),
                  pl.BlockSpec((tg, PEER_SEL), lambda i: (jnp.minimum(i + 1, n // tg - 1), 0), memory_space=pltpu.SMEM),
                  pl.BlockSpec((PEER_SEL, tg), lambda i: (0, i)),
                  pl.BlockSpec((tg, D_MODEL), row),
                  pl.BlockSpec((tg, D_MODEL), row),
                  pl.BlockSpec((seqs_per_tile, 6, D_MODEL), lambda i: (i // tiles_per_seq, 0, 0)),
                  pl.BlockSpec((1, D_MODEL), lambda i: (0, 0)),
                  pl.BlockSpec(memory_space=pl.ANY)],
        out_specs=pl.BlockSpec((tg, D_MODEL), row),
        scratch_shapes=[pltpu.VMEM((PEER_SEL, 2 * N_CHUNK, LANES), F32)] * PEER_SLOTS
                       + [pltpu.VMEM((tg, D_MODEL), F32), pltpu.SemaphoreType.DMA((PEER_SLOTS,))],
        compiler_params=_params(("arbitrary",)),
        name="peer_ffn",
    )(idx, idx, gt, h2, x1, mod3, g_post, uv)


def _pad_keys(x, s_pad):
    return jnp.pad(x, ((0, 0), (0, s_pad - x.shape[1]), (0, 0)))


def _layer(x, mod3, past, weights, tiles):
    (g_pre_mix, g_post_mix, g_pre_ffn, g_post_ffn, w_in_b, w_gate_b, wa_b, wb_b, wo_b, wq_b,
     subkeys_b, peer_uv) = weights
    tm, sb_tq, sb_tk, dsa_tq, dsa_tk, tt, tg = tiles
    bsz, t, _ = x.shape
    n = bsz * t
    x2 = x.reshape(n, D_MODEL)
    (aq, ak, av, bq, bk, bv, iq, ki2, akf, avf, bkf, bvf, ikf, iw, ga, gb) = _inproj(
        x2, mod3, g_pre_mix, w_in_b, w_gate_b, tm, t // tm)
    seq = lambda a: a.reshape(bsz, t, a.shape[-1])
    if past is None:
        q_off = 0
        ka, va, kb, vb, ki = seq(ak), seq(av), seq(bk), seq(bv), seq(ki2)
        s_len = t
    else:
        p_sbk, p_sbv, p_dk, p_dv, p_ki = past
        q_off = p_sbk.shape[1]
        s_len = q_off + t
        flat = lambda c: c.reshape(bsz, q_off, WIDTH).astype(BF16)
        ka = jnp.concatenate([flat(p_sbk), seq(ak)], axis=1)
        va = jnp.concatenate([flat(p_sbv), seq(av)], axis=1)
        kb = jnp.concatenate([flat(p_dk), seq(bk)], axis=1)
        vb = jnp.concatenate([flat(p_dv), seq(bv)], axis=1)
        pk = p_ki.astype(BF16)
        ki = jnp.concatenate([jnp.concatenate([pk, pk], axis=-1), seq(ki2)], axis=1)
    s_pad = -(-s_len // max(sb_tk, dsa_tk)) * max(sb_tk, dsa_tk)
    ka, va, kb, vb, ki = (_pad_keys(a, s_pad) for a in (ka, va, kb, vb, ki))
    n_sel = min(TOPK_MAX, s_len // 4)

    oa = _sb_attention(seq(aq), ka, va, sb_tq, sb_tk, q_off)
    ob = _dsa_attention(seq(bq), seq(iq), seq(iw), kb, vb, ki, dsa_tq, dsa_tk, q_off, n_sel)

    x1, h2, pq = _mix(oa.reshape(n, WIDTH), ob.reshape(n, WIDTH), ga, gb, x2, mod3, g_post_mix, g_pre_ffn,
                      wa_b, wb_b, wo_b, wq_b, tm, t // tm)
    eidx, gate = _peer_select(pq, subkeys_b, tt)
    eidx = eidx.reshape(PEER_SEL, n).T
    y = _peer_ffn(eidx, gate.reshape(PEER_SEL, n), h2, x1, mod3, g_post_ffn, peer_uv, tg, t)
    new = tuple(a.reshape(1, bsz, t, N_HEADS, HEAD_DIM) for a in (akf, avf, bkf, bvf)) + (
        ikf.reshape(1, bsz, t, HEAD_DIM),)
    return y.reshape(bsz, t, D_MODEL), new


def _tiles(t):
    if t % 256 == 0:
        return (256, 512 if t % 512 == 0 else 256, 256, 256, 512, 128, 128)
    return (t, t, 128, t, 128, 128, 128)


def kernel(x_prompt, x_sample, c_prompt, c_sample, cache_sb_k, cache_sb_v, cache_dsa_k, cache_dsa_v, cache_dsa_kidx, w_ada, b_ada, g_pre_mix, g_post_mix, g_pre_ffn, g_post_ffn, w_in, w_gate, w_branch_a, w_branch_b, w_out, w_peer_q, peer_subkeys, peer_u, peer_v):
    assert w_ada.shape[0] == 1, "one layer"
    n_p, n_s = c_prompt.shape[0], c_sample.shape[0]
    c_all = jnp.concatenate([c_prompt, c_sample], axis=0)
    rows = -(-c_all.shape[0] // 8) * 8
    c_all = jnp.pad(c_all, ((0, rows - c_all.shape[0]), (0, 0)))
    mod = _modulation(c_all, w_ada[0], b_ada[0]).reshape(rows, 6, D_MODEL)
    w_in_b = jnp.pad(w_in[0], ((0, 0), (0, W_IN_PAD - W_IN_COLS))).astype(BF16)
    weights = (g_pre_mix, g_post_mix, g_pre_ffn, g_post_ffn, w_in_b, w_gate[0].astype(BF16),
               w_branch_a[0].astype(BF16), w_branch_b[0].astype(BF16), w_out[0].astype(BF16),
               w_peer_q[0].astype(BF16), peer_subkeys[0].astype(BF16),
               jnp.concatenate([peer_u[0].reshape(-1, N_CHUNK, LANES),
                                peer_v[0].reshape(-1, N_CHUNK, LANES)], axis=1))
    yp, new_p = _layer(x_prompt, mod[:n_p], None, weights, _tiles(x_prompt.shape[1]))
    past = (cache_sb_k[0], cache_sb_v[0], cache_dsa_k[0], cache_dsa_v[0], cache_dsa_kidx[0])
    ys, new_s = _layer(x_sample, mod[n_p:n_p + n_s], past, weights, _tiles(x_sample.shape[1]))
    return (yp, ys) + new_p + new_s
```
